```python
import jax, jax.numpy as jnp
from jax import lax
import numpy as np

D_MODEL = 2048
BATCH = 16
SEQ = 2048
DEPTH = 4

GLA_HEADS = 4
GLA_DK = D_MODEL // 2
GLA_DV = D_MODEL
GLA_HEAD_K = GLA_DK // GLA_HEADS
GLA_HEAD_V = GLA_DV // GLA_HEADS
GLA_GATE_RANK = 16
GLA_GATE_NORMALIZER = 16.0
GLA_CHUNK = 64
GLA_SUBCHUNK = 16
GLA_SPLITS = (GLA_DK, 2 * GLA_DK, 2 * GLA_DK + GLA_DV, 2 * GLA_DK + 2 * GLA_DV)
GLA_IN = 2 * GLA_DK + 2 * GLA_DV + GLA_GATE_RANK

DIL_PATTERNS = ((128, 1), (512, 4), (2048, 16))
DIL_GROUPS = len(DIL_PATTERNS)
DIL_HEADS = 8
DIL_HEAD_DIM = 128
DIL_WIDTH = DIL_HEADS * DIL_HEAD_DIM
DIL_BLOCK = 128
DIL_IN = 3 * DIL_GROUPS * DIL_WIDTH

D_FF = 5504
CONV_WIDTH = 3

N_GLA_LAYERS = (DEPTH + 1) // 2
N_DIL_LAYERS = DEPTH // 2

DEEPNORM_ALPHA = (2 * DEPTH) ** 0.25
DEEPNORM_BETA = (8 * DEPTH) ** -0.25
LN_EPS = 1e-5
RMS_EPS = 1e-6

kernel_name = "hybrid_gla_dilated_convffn_deepnorm"


def layer_norm(x, g, b):
    xf = x.astype(jnp.float32)
    mu = xf.mean(-1, keepdims=True)
    var = jnp.square(xf - mu).mean(-1, keepdims=True)
    return ((xf - mu) * lax.rsqrt(var + LN_EPS)).astype(x.dtype) * g + b


def gla_chunk_step(state, inp):
    q, k, v, g = inp
    Bb, H, C, dk = q.shape
    c = GLA_SUBCHUNK
    n = C // c
    b = jnp.cumsum(g, axis=2)
    b_last = b[:, :, -1:, :]
    o_inter = jnp.einsum('bhcd,bhde->bhce', q * jnp.exp(b), state)
    qs = q.reshape(Bb, H, n, c, dk)
    ks = k.reshape(Bb, H, n, c, dk)
    vs = v.reshape(Bb, H, n, c, -1)
    bs = b.reshape(Bb, H, n, c, dk)
    b_ref = jnp.concatenate([jnp.zeros_like(bs[:, :, :1, -1]), bs[:, :, :-1, -1]], axis=2)
    q_ref = qs * jnp.exp(bs - b_ref[:, :, :, None, :])
    earlier = jnp.arange(C)[None, :] < (jnp.arange(n) * c)[:, None]
    k_exp = jnp.where(earlier[None, None, :, :, None],
                      b_ref[:, :, :, None, :] - b[:, :, None, :, :], -jnp.inf)
    k_ref = k[:, :, None] * jnp.exp(k_exp)
    a_inter = jnp.einsum('bhsid,bhsjd->bhsij', q_ref, k_ref)
    causal = jnp.tril(jnp.ones((c, c), dtype=bool))
    d_exp = jnp.where(causal[:, :, None],
                      bs[:, :, :, :, None, :] - bs[:, :, :, None, :, :], -jnp.inf)
    a_intra = jnp.einsum('bhsid,bhsjd,bhsijd->bhsij', qs, ks, jnp.exp(d_exp))
    o_intra = (jnp.einsum('bhsij,bhje->bhsie', a_inter, v)
               + jnp.einsum('bhsij,bhsje->bhsie', a_intra, vs)).reshape(Bb, H, C, -1)
    new_state = (jnp.exp(b_last[:, :, 0, :])[..., None] * state
                 + jnp.einsum('bhcd,bhce->bhde', k * jnp.exp(b_last - b), v))
    return new_state, o_inter + o_intra


def gla_mixer(x, w_in, w_gate_up, gate_bias, norm_g, w_out):
    B, S, _ = x.shape
    H, dk, dv, C = GLA_HEADS, GLA_HEAD_K, GLA_HEAD_V, GLA_CHUNK
    proj = x @ w_in
    q, k, v, r, g_low = jnp.split(proj, list(GLA_SPLITS), axis=-1)
    log_gate = jax.nn.log_sigmoid((g_low @ w_gate_up + gate_bias).astype(jnp.float32)) / GLA_GATE_NORMALIZER
    q = q.astype(jnp.float32) * dk ** -0.5

    def to_chunks(t, d):
        return t.astype(jnp.float32).reshape(B, S // C, C, H, d).transpose(1, 0, 3, 2, 4)

    xs = (to_chunks(q, dk), to_chunks(k, dk), to_chunks(v, dv), to_chunks(log_gate, dk))
    state0 = jnp.zeros((B, H, dk, dv), jnp.float32)
    _, o = lax.scan(gla_chunk_step, state0, xs)
    o = o.transpose(1, 0, 3, 2, 4).reshape(B, S, H, dv)
    o = o * lax.rsqrt(jnp.mean(jnp.square(o), -1, keepdims=True) + RMS_EPS) * norm_g
    o = o.reshape(B, S, GLA_DV).astype(x.dtype) * jax.nn.silu(r)
    return o @ w_out


def banded_attention(q, k, v, steps):
    N, L, H, dh = q.shape
    P = DIL_BLOCK
    nb = -(-L // P)
    Lp = nb * P
    pad = ((0, 0), (0, Lp - L), (0, 0), (0, 0))
    q, k, v = (jnp.pad(t, pad).reshape(N, nb, P, H, dh) for t in (q, k, v))

    def with_prev(t):
        prev = jnp.concatenate([jnp.zeros_like(t[:, :1]), t[:, :-1]], axis=1)
        return jnp.concatenate([prev, t], axis=2)

    kw, vw = with_prev(k), with_prev(v)
    s = jnp.einsum('nbqhd,nbkhd->nbhqk', q, kw).astype(jnp.float32) * dh ** -0.5
    qi = jnp.arange(P)[:, None] + P
    kj = jnp.arange(2 * P)[None, :]
    dist = qi - kj
    band = (dist >= 0) & (dist <= steps)
    real_key = (jnp.arange(nb)[:, None, None] > 0) | (kj[None] >= P)
    mask = band[None] & real_key
    s = jnp.where(mask[None, :, None], s, -jnp.inf)
    m = s.max(-1, keepdims=True)
    p = jnp.exp(s - m)
    l = p.sum(-1, keepdims=True)
    o = jnp.einsum('nbhqk,nbkhd->nbqhd', (p / l).astype(v.dtype), vw)
    lse = (m + jnp.log(l))[..., 0]
    o = o.reshape(N, Lp, H, dh)[:, :L]
    lse = lse.transpose(0, 1, 3, 2).reshape(N, Lp, H)[:, :L]
    return o, lse


def dilated_group(q, k, v, window, dilation):
    B, S, H, dh = q.shape
    L = S // dilation

    def to_strided(t):
        return t.reshape(B, L, dilation, H, dh).transpose(0, 2, 1, 3, 4).reshape(B * dilation, L, H, dh)

    o, lse = banded_attention(to_strided(q), to_strided(k), to_strided(v), window // dilation)
    o = o.reshape(B, dilation, L, H, dh).transpose(0, 2, 1, 3, 4).reshape(B, S, H, dh)
    lse = lse.reshape(B, dilation, L, H).transpose(0, 2, 1, 3).reshape(B, S, H)
    return o, lse


def dilated_mixer(x, w_in, w_out):
    B, S, _ = x.shape
    proj = (x @ w_in).reshape(B, S, DIL_GROUPS, 3, DIL_HEADS, DIL_HEAD_DIM)
    outs, lses = [], []
    for gi, (window, dilation) in enumerate(DIL_PATTERNS):
        o, lse = dilated_group(proj[:, :, gi, 0], proj[:, :, gi, 1], proj[:, :, gi, 2], window, dilation)
        outs.append(o)
        lses.append(lse)
    wts = jax.nn.softmax(jnp.stack(lses, 0), axis=0)
    o = jnp.einsum('gbsh,gbshd->bshd', wts.astype(x.dtype), jnp.stack(outs, 0))
    return o.reshape(B, S, DIL_WIDTH) @ w_out


def conv_ffn(x, w_up, conv_w, conv_b, w_down):
    S = x.shape[1]
    h = x @ w_up
    hp = jnp.pad(h, ((0, 0), (CONV_WIDTH - 1, 0), (0, 0)))
    h = sum((conv_w[j] * hp[:, j:j + S] for j in range(CONV_WIDTH)), conv_b)
    gate, up = jnp.split(h, 2, axis=-1)
    return (jax.nn.silu(gate) * up) @ w_down


def _fwd_setup_inputs(seed: int = 0) -> dict:
    key = jax.random.key(seed)
    ks = jax.random.split(key, 16)
    nrm = lambda k, shape, scale: jax.random.normal(k, shape, jnp.float32) * scale
    return {
        "x": nrm(ks[0], (BATCH, SEQ, D_MODEL), 1.0),
        "gla_w_in": nrm(ks[1], (N_GLA_LAYERS, D_MODEL, GLA_IN), D_MODEL ** -0.5),
        "gla_w_gate_up": nrm(ks[2], (N_GLA_LAYERS, GLA_GATE_RANK, GLA_DK), GLA_GATE_RANK ** -0.5),
        "gla_gate_bias": nrm(ks[3], (N_GLA_LAYERS, GLA_DK), 0.1),
        "gla_norm_g": 1.0 + nrm(ks[4], (N_GLA_LAYERS, GLA_HEAD_V), 0.02),
        "gla_w_out": nrm(ks[5], (N_GLA_LAYERS, GLA_DV, D_MODEL), GLA_DV ** -0.5 * DEEPNORM_BETA),
        "dil_w_in": nrm(ks[6], (N_DIL_LAYERS, D_MODEL, DIL_IN), D_MODEL ** -0.5),
        "dil_w_out": nrm(ks[7], (N_DIL_LAYERS, DIL_WIDTH, D_MODEL), DIL_WIDTH ** -0.5 * DEEPNORM_BETA),
        "ffn_w_up": nrm(ks[8], (DEPTH, D_MODEL, 2 * D_FF), D_MODEL ** -0.5),
        "ffn_conv_w": nrm(ks[9], (DEPTH, CONV_WIDTH, 2 * D_FF), CONV_WIDTH ** -0.5),
        "ffn_conv_b": nrm(ks[10], (DEPTH, 2 * D_FF), 0.02),
        "ffn_w_down": nrm(ks[11], (DEPTH, D_FF, D_MODEL), D_FF ** -0.5 * DEEPNORM_BETA),
        "ln_g": 1.0 + nrm(ks[12], (DEPTH, 2, D_MODEL), 0.02),
        "ln_b": nrm(ks[13], (DEPTH, 2, D_MODEL), 0.02),
    }


def _fwd_reference(x, gla_w_in, gla_w_gate_up, gla_gate_bias, gla_norm_g, gla_w_out,
              dil_w_in, dil_w_out, ffn_w_up, ffn_conv_w, ffn_conv_b, ffn_w_down, ln_g, ln_b):
    for i in range(DEPTH):
        j = i // 2
        if i % 2 == 0:
            mix = gla_mixer(x, gla_w_in[j], gla_w_gate_up[j], gla_gate_bias[j], gla_norm_g[j], gla_w_out[j])
        else:
            mix = dilated_mixer(x, dil_w_in[j], dil_w_out[j])
        x = layer_norm(DEEPNORM_ALPHA * x + mix, ln_g[i, 0], ln_b[i, 0])
        ffn = conv_ffn(x, ffn_w_up[i], ffn_conv_w[i], ffn_conv_b[i], ffn_w_down[i])
        x = layer_norm(DEEPNORM_ALPHA * x + ffn, ln_g[i, 1], ln_b[i, 1])
    return x


import jax as _jax
import jax.numpy as _jnp

TWIN_FORMAT = 'train_step'
FWD_PARAMS = ['x', 'gla_w_in', 'gla_w_gate_up', 'gla_gate_bias', 'gla_norm_g', 'gla_w_out', 'dil_w_in', 'dil_w_out', 'ffn_w_up', 'ffn_conv_w', 'ffn_conv_b', 'ffn_w_down', 'ln_g', 'ln_b']
TWIN_WEIGHTS = ['gla_w_in', 'gla_w_gate_up', 'gla_gate_bias', 'gla_norm_g', 'gla_w_out', 'dil_w_in', 'dil_w_out', 'ffn_w_up', 'ffn_conv_w', 'ffn_conv_b', 'ffn_w_down', 'ln_g', 'ln_b']
TWIN_DIFF_INPUT = 'x'
TWIN_INPUTS = ['x', 'gla_w_in', 'gla_w_gate_up', 'gla_gate_bias', 'gla_norm_g', 'gla_w_out', 'dil_w_in', 'dil_w_out', 'ffn_w_up', 'ffn_conv_w', 'ffn_conv_b', 'ffn_w_down', 'ln_g', 'ln_b', 'loss_target', 'm_gla_w_in', 'm_gla_w_gate_up', 'm_gla_gate_bias', 'm_gla_norm_g', 'm_gla_w_out', 'm_dil_w_in', 'm_dil_w_out', 'm_ffn_w_up', 'm_ffn_conv_w', 'm_ffn_conv_b', 'm_ffn_w_down', 'm_ln_g', 'm_ln_b', 'v_gla_w_in', 'v_gla_w_gate_up', 'v_gla_gate_bias', 'v_gla_norm_g', 'v_gla_w_out', 'v_dil_w_in', 'v_dil_w_out', 'v_ffn_w_up', 'v_ffn_conv_w', 'v_ffn_conv_b', 'v_ffn_w_down', 'v_ln_g', 'v_ln_b']
TWIN_OUTPUTS = ['loss', 'grad_x', 'grad_gla_w_in', 'grad_gla_w_gate_up', 'grad_gla_gate_bias', 'grad_gla_norm_g', 'grad_gla_w_out', 'grad_dil_w_in', 'grad_dil_w_out', 'grad_ffn_w_up', 'grad_ffn_conv_w', 'grad_ffn_conv_b', 'grad_ffn_w_down', 'grad_ln_g', 'grad_ln_b', 'delta_gla_w_in', 'delta_gla_w_gate_up', 'delta_gla_gate_bias', 'delta_gla_norm_g', 'delta_gla_w_out', 'delta_dil_w_in', 'delta_dil_w_out', 'delta_ffn_w_up', 'delta_ffn_conv_w', 'delta_ffn_conv_b', 'delta_ffn_w_down', 'delta_ln_g', 'delta_ln_b', 'new_m_gla_w_in', 'new_m_gla_w_gate_up', 'new_m_gla_gate_bias', 'new_m_gla_norm_g', 'new_m_gla_w_out', 'new_m_dil_w_in', 'new_m_dil_w_out', 'new_m_ffn_w_up', 'new_m_ffn_conv_w', 'new_m_ffn_conv_b', 'new_m_ffn_w_down', 'new_m_ln_g', 'new_m_ln_b', 'new_v_gla_w_in', 'new_v_gla_w_gate_up', 'new_v_gla_gate_bias', 'new_v_gla_norm_g', 'new_v_gla_w_out', 'new_v_dil_w_in', 'new_v_dil_w_out', 'new_v_ffn_w_up', 'new_v_ffn_conv_w', 'new_v_ffn_conv_b', 'new_v_ffn_w_down', 'new_v_ln_g', 'new_v_ln_b']
TWIN_LEAF_KINDS = {'loss': 'loss', 'grad_x': 'grad_x', 'grad_gla_w_in': 'grad_w', 'grad_gla_w_gate_up': 'grad_w', 'grad_gla_gate_bias': 'grad_w', 'grad_gla_norm_g': 'grad_w', 'grad_gla_w_out': 'grad_w', 'grad_dil_w_in': 'grad_w', 'grad_dil_w_out': 'grad_w', 'grad_ffn_w_up': 'grad_w', 'grad_ffn_conv_w': 'grad_w', 'grad_ffn_conv_b': 'grad_w', 'grad_ffn_w_down': 'grad_w', 'grad_ln_g': 'grad_w', 'grad_ln_b': 'grad_w', 'delta_gla_w_in': 'delta_w', 'delta_gla_w_gate_up': 'delta_w', 'delta_gla_gate_bias': 'delta_w', 'delta_gla_norm_g': 'delta_w', 'delta_gla_w_out': 'delta_w', 'delta_dil_w_in': 'delta_w', 'delta_dil_w_out': 'delta_w', 'delta_ffn_w_up': 'delta_w', 'delta_ffn_conv_w': 'delta_w', 'delta_ffn_conv_b': 'delta_w', 'delta_ffn_w_down': 'delta_w', 'delta_ln_g': 'delta_w', 'delta_ln_b': 'delta_w', 'new_m_gla_w_in': 'new_m', 'new_m_gla_w_gate_up': 'new_m', 'new_m_gla_gate_bias': 'new_m', 'new_m_gla_norm_g': 'new_m', 'new_m_gla_w_out': 'new_m', 'new_m_dil_w_in': 'new_m', 'new_m_dil_w_out': 'new_m', 'new_m_ffn_w_up': 'new_m', 'new_m_ffn_conv_w': 'new_m', 'new_m_ffn_conv_b': 'new_m', 'new_m_ffn_w_down': 'new_m', 'new_m_ln_g': 'new_m', 'new_m_ln_b': 'new_m', 'new_v_gla_w_in': 'new_v', 'new_v_gla_w_gate_up': 'new_v', 'new_v_gla_gate_bias': 'new_v', 'new_v_gla_norm_g': 'new_v', 'new_v_gla_w_out': 'new_v', 'new_v_dil_w_in': 'new_v', 'new_v_dil_w_out': 'new_v', 'new_v_ffn_w_up': 'new_v', 'new_v_ffn_conv_w': 'new_v', 'new_v_ffn_conv_b': 'new_v', 'new_v_ffn_w_down': 'new_v', 'new_v_ln_g': 'new_v', 'new_v_ln_b': 'new_v'}


def _forward(args):
    return _fwd_reference(*[args[k] for k in FWD_PARAMS])


def _output_shape():
    out = _jax.eval_shape(lambda: _forward(_fwd_setup_inputs(0)))
    return out.shape, out.dtype

N_MICROBATCH = 1
ADAM_LR = 0.001
ADAM_B1 = 0.9
ADAM_B2 = 0.999
ADAM_EPS = 1e-08
ADAM_WD = 0.01
ADAM_STEP = 10
PER_EXAMPLE_BATCH_AXIS = {'x': 0, 'loss_target': 0}
SHARED_INPUTS = []
_WEIGHT_DTYPES = {'gla_w_in': _jnp.float32, 'gla_w_gate_up': _jnp.float32, 'gla_gate_bias': _jnp.float32, 'gla_norm_g': _jnp.float32, 'gla_w_out': _jnp.float32, 'dil_w_in': _jnp.float32, 'dil_w_out': _jnp.float32, 'ffn_w_up': _jnp.float32, 'ffn_conv_w': _jnp.float32, 'ffn_conv_b': _jnp.float32, 'ffn_w_down': _jnp.float32, 'ln_g': _jnp.float32, 'ln_b': _jnp.float32}
MOMENT_SCALE = {'gla_w_in': 1.713240e-02, 'gla_w_gate_up': 2.290419e-03, 'gla_gate_bias': 9.544394e-03, 'gla_norm_g': 3.062331e-02, 'gla_w_out': 3.286119e-02, 'dil_w_in': 3.394798e-03, 'dil_w_out': 1.109020e-02, 'ffn_w_up': 8.554181e-03, 'ffn_conv_w': 8.584623e-03, 'ffn_conv_b': 9.296745e-03, 'ffn_w_down': 3.285038e-02, 'ln_g': 5.686343e+00, 'ln_b': 3.294005e-01}


def _to_microbatches(a, axis):
    t = _jnp.moveaxis(a, axis, 0)
    t = t.reshape((N_MICROBATCH, t.shape[0] // N_MICROBATCH) + t.shape[1:])
    return _jnp.moveaxis(t, 1, axis + 1)


def setup_inputs(seed: int = 0) -> dict:
    inp = _fwd_setup_inputs(seed)
    key = _jax.random.fold_in(_jax.random.key(seed), 7919)
    shape, _ = _output_shape()
    out = dict(inp)
    out["loss_target"] = _jax.random.normal(_jax.random.fold_in(key, 0), shape, _jnp.float32)
    for i, name in enumerate(TWIN_WEIGHTS):
        w = inp[name].astype(_jnp.float32)
        if MOMENT_SCALE is None:
            s = _jnp.sqrt(_jnp.mean(_jnp.square(w)) + 1e-30)
        else:
            s = MOMENT_SCALE[name]
        km, kv = _jax.random.split(_jax.random.fold_in(key, i + 1))
        out[name] = w
        out["m_" + name] = s * _jax.random.normal(km, w.shape, _jnp.float32)
        out["v_" + name] = (s * s) * _jax.random.uniform(kv, w.shape, _jnp.float32, 0.5, 1.5)
    if N_MICROBATCH > 1:
        for name, axis in PER_EXAMPLE_BATCH_AXIS.items():
            out[name] = _to_microbatches(out[name], axis)
    return {'x': out['x'], 'gla_w_in': out['gla_w_in'], 'gla_w_gate_up': out['gla_w_gate_up'], 'gla_gate_bias': out['gla_gate_bias'], 'gla_norm_g': out['gla_norm_g'], 'gla_w_out': out['gla_w_out'], 'dil_w_in': out['dil_w_in'], 'dil_w_out': out['dil_w_out'], 'ffn_w_up': out['ffn_w_up'], 'ffn_conv_w': out['ffn_conv_w'], 'ffn_conv_b': out['ffn_conv_b'], 'ffn_w_down': out['ffn_w_down'], 'ln_g': out['ln_g'], 'ln_b': out['ln_b'], 'loss_target': out['loss_target'], 'm_gla_w_in': out['m_gla_w_in'], 'm_gla_w_gate_up': out['m_gla_w_gate_up'], 'm_gla_gate_bias': out['m_gla_gate_bias'], 'm_gla_norm_g': out['m_gla_norm_g'], 'm_gla_w_out': out['m_gla_w_out'], 'm_dil_w_in': out['m_dil_w_in'], 'm_dil_w_out': out['m_dil_w_out'], 'm_ffn_w_up': out['m_ffn_w_up'], 'm_ffn_conv_w': out['m_ffn_conv_w'], 'm_ffn_conv_b': out['m_ffn_conv_b'], 'm_ffn_w_down': out['m_ffn_w_down'], 'm_ln_g': out['m_ln_g'], 'm_ln_b': out['m_ln_b'], 'v_gla_w_in': out['v_gla_w_in'], 'v_gla_w_gate_up': out['v_gla_w_gate_up'], 'v_gla_gate_bias': out['v_gla_gate_bias'], 'v_gla_norm_g': out['v_gla_norm_g'], 'v_gla_w_out': out['v_gla_w_out'], 'v_dil_w_in': out['v_dil_w_in'], 'v_dil_w_out': out['v_dil_w_out'], 'v_ffn_w_up': out['v_ffn_w_up'], 'v_ffn_conv_w': out['v_ffn_conv_w'], 'v_ffn_conv_b': out['v_ffn_conv_b'], 'v_ffn_w_down': out['v_ffn_w_down'], 'v_ln_g': out['v_ln_g'], 'v_ln_b': out['v_ln_b']}


def _loss(weights, diff, rest, loss_target):
    with _jax.named_scope("forward"):
        args = {**rest, TWIN_DIFF_INPUT: diff, **{k: w.astype(_WEIGHT_DTYPES[k]) for k, w in weights.items()}}
        y = _forward(args)
    with _jax.named_scope("loss_head"):
        err = _jnp.square(y.astype(_jnp.float32) - loss_target)
        return 0.5 * _jnp.sum(_jnp.mean(err, axis=-1)) if err.ndim else 0.5 * err


def _adamw(w, g, m, v):
    m = ADAM_B1 * m + (1.0 - ADAM_B1) * g
    v = ADAM_B2 * v + (1.0 - ADAM_B2) * _jnp.square(g)
    m_hat = m / (1.0 - ADAM_B1 ** ADAM_STEP)
    v_hat = v / (1.0 - ADAM_B2 ** ADAM_STEP)
    delta = -ADAM_LR * (m_hat / (_jnp.sqrt(v_hat) + ADAM_EPS) + ADAM_WD * w)
    return delta, m, v


def reference(x, gla_w_in, gla_w_gate_up, gla_gate_bias, gla_norm_g, gla_w_out, dil_w_in, dil_w_out, ffn_w_up, ffn_conv_w, ffn_conv_b, ffn_w_down, ln_g, ln_b, loss_target, m_gla_w_in, m_gla_w_gate_up, m_gla_gate_bias, m_gla_norm_g, m_gla_w_out, m_dil_w_in, m_dil_w_out, m_ffn_w_up, m_ffn_conv_w, m_ffn_conv_b, m_ffn_w_down, m_ln_g, m_ln_b, v_gla_w_in, v_gla_w_gate_up, v_gla_gate_bias, v_gla_norm_g, v_gla_w_out, v_dil_w_in, v_dil_w_out, v_ffn_w_up, v_ffn_conv_w, v_ffn_conv_b, v_ffn_w_down, v_ln_g, v_ln_b):
    given = dict(x=x, gla_w_in=gla_w_in, gla_w_gate_up=gla_w_gate_up, gla_gate_bias=gla_gate_bias, gla_norm_g=gla_norm_g, gla_w_out=gla_w_out, dil_w_in=dil_w_in, dil_w_out=dil_w_out, ffn_w_up=ffn_w_up, ffn_conv_w=ffn_conv_w, ffn_conv_b=ffn_conv_b, ffn_w_down=ffn_w_down, ln_g=ln_g, ln_b=ln_b, loss_target=loss_target, m_gla_w_in=m_gla_w_in, m_gla_w_gate_up=m_gla_w_gate_up, m_gla_gate_bias=m_gla_gate_bias, m_gla_norm_g=m_gla_norm_g, m_gla_w_out=m_gla_w_out, m_dil_w_in=m_dil_w_in, m_dil_w_out=m_dil_w_out, m_ffn_w_up=m_ffn_w_up, m_ffn_conv_w=m_ffn_conv_w, m_ffn_conv_b=m_ffn_conv_b, m_ffn_w_down=m_ffn_w_down, m_ln_g=m_ln_g, m_ln_b=m_ln_b, v_gla_w_in=v_gla_w_in, v_gla_w_gate_up=v_gla_w_gate_up, v_gla_gate_bias=v_gla_gate_bias, v_gla_norm_g=v_gla_norm_g, v_gla_w_out=v_gla_w_out, v_dil_w_in=v_dil_w_in, v_dil_w_out=v_dil_w_out, v_ffn_w_up=v_ffn_w_up, v_ffn_conv_w=v_ffn_conv_w, v_ffn_conv_b=v_ffn_conv_b, v_ffn_w_down=v_ffn_w_down, v_ln_g=v_ln_g, v_ln_b=v_ln_b)
    weights = {n: given[n] for n in TWIN_WEIGHTS}
    shared = {n: given[n] for n in SHARED_INPUTS}
    per_example = {n: given[n] for n in ['x']}
    grad_fn = _jax.value_and_grad(_loss, argnums=(0, 1))

    def one_microbatch(ex, loss_target):
        ex = dict(ex)
        diff = ex.pop(TWIN_DIFF_INPUT)
        return grad_fn(weights, diff, {**shared, **ex}, loss_target)

    if N_MICROBATCH == 1:
        loss, (grad_w, grad_x) = one_microbatch(per_example, given["loss_target"])
    else:
        def body(carry, xs):
            loss_sum, grad_sum = carry
            l_k, (gw_k, gx_k) = one_microbatch(xs[0], xs[1])
            with _jax.named_scope("update"):
                return (loss_sum + l_k, _jax.tree.map(_jnp.add, grad_sum, gw_k)), gx_k

        init = (_jnp.zeros((), _jnp.float32), _jax.tree.map(_jnp.zeros_like, weights))
        (loss, grad_w), grad_x = _jax.lax.scan(body, init, (per_example, given["loss_target"]))
    with _jax.named_scope("update"):
        delta_w, new_m, new_v = {}, {}, {}
        for n in TWIN_WEIGHTS:
            delta_w[n], new_m[n], new_v[n] = _adamw(weights[n], grad_w[n], given["m_" + n], given["v_" + n])
    return (loss, grad_x, *[grad_w[n] for n in TWIN_WEIGHTS], *[delta_w[n] for n in TWIN_WEIGHTS],
            *[new_m[n] for n in TWIN_WEIGHTS], *[new_v[n] for n in TWIN_WEIGHTS])
```

```python
import functools
import math

import jax
import jax.numpy as jnp
from jax import lax
from jax.experimental import pallas as pl
from jax.experimental.pallas import tpu as pltpu

F32 = jnp.float32
BF16 = jnp.bfloat16

D_MODEL = 2048
SEQ = 2048
DEPTH = 4
N_DEV = 8
GLA_HEADS = 4
GLA_DK = 1024
GLA_DV = 2048
GLA_HEAD_K = 256
GLA_HEAD_V = 512
GLA_GATE_RANK = 16
GLA_GATE_NORMALIZER = 16.0
GLA_CHUNK = 64
GLA_MAIN = 2 * GLA_DK + 2 * GLA_DV
GLA_IN = GLA_MAIN + GLA_GATE_RANK
DIL_PATTERNS = ((128, 1), (512, 4), (2048, 16))
DIL_HEADS = 8
DIL_HEAD_DIM = 128
DIL_WIDTH = DIL_HEADS * DIL_HEAD_DIM
DIL_BLOCK = 128
DIL_IN = 3 * len(DIL_PATTERNS) * DIL_WIDTH
D_FF = 5504
DEEPNORM_ALPHA = (2 * DEPTH) ** 0.25
LN_EPS = 1e-5
RMS_EPS = 1e-6
ADAM_LR = 0.001
ADAM_B1 = 0.9
ADAM_B2 = 0.999
ADAM_EPS = 1e-08
ADAM_WD = 0.01
ADAM_STEP = 10

LANES = 128
SUBLANES = 8
VMEM_LIMIT_BYTES = 56 * 1024 * 1024

FF_SHARD = 2 * D_FF // N_DEV
FF_SHARD_PAD = 1408
FF_HALF_PAD = 4 * FF_SHARD_PAD
GLOW_PAD = LANES


def _cparams(dims=None):
    return pltpu.CompilerParams(dimension_semantics=dims, vmem_limit_bytes=VMEM_LIMIT_BYTES)


def _matmul(a, b, *, ta=False, tb=False, tm, tn, tk, out_dtype=F32, res=None, res_scale=1.0, name):
    m, k = (a.shape[1], a.shape[0]) if ta else a.shape
    n = b.shape[0] if tb else b.shape[1]
    assert (b.shape[1] if tb else b.shape[0]) == k
    assert m % tm == 0 and n % tn == 0 and k % tk == 0, (m, n, k, tm, tn, tk)
    nk = k // tk
    a_spec = pl.BlockSpec((tk, tm), lambda i, j, kk: (kk, i)) if ta else pl.BlockSpec((tm, tk), lambda i, j, kk: (i, kk))
    b_spec = pl.BlockSpec((tn, tk), lambda i, j, kk: (j, kk)) if tb else pl.BlockSpec((tk, tn), lambda i, j, kk: (kk, j))
    o_spec = pl.BlockSpec((tm, tn), lambda i, j, kk: (i, j))
    dims = (((0 if ta else 1,), (1 if tb else 0,)), ((), ()))
    has_res = res is not None

    def body(*refs):
        a_ref, b_ref = refs[0], refs[1]
        res_ref = refs[2] if has_res else None
        o_ref = refs[3] if has_res else refs[2]
        acc_ref = refs[-1] if nk > 1 else None
        p = lax.dot_general(a_ref[...], b_ref[...], dims, preferred_element_type=F32)

        def finish(acc):
            if has_res:
                acc = acc + res_scale * res_ref[...]
            o_ref[...] = acc.astype(o_ref.dtype)

        if nk == 1:
            finish(p)
        else:
            kk = pl.program_id(2)

            @pl.when(kk == 0)
            def _():
                acc_ref[...] = p

            @pl.when(kk > 0)
            def _():
                acc_ref[...] += p

            @pl.when(kk == nk - 1)
            def _():
                finish(acc_ref[...])

    in_specs = [a_spec, b_spec] + ([o_spec] if has_res else [])
    args = (a, b) + ((res,) if has_res else ())
    return pl.pallas_call(
        body,
        name=name,
        grid=(m // tm, n // tn, nk),
        in_specs=in_specs,
        out_specs=o_spec,
        out_shape=jax.ShapeDtypeStruct((m, n), out_dtype),
        scratch_shapes=[pltpu.VMEM((tm, tn), F32)] if nk > 1 else [],
        compiler_params=_cparams(("parallel", "parallel", "arbitrary")),
    )(*args)


LN_ROWS = 256


def _ln_fwd(x, f, g, b, *, name):
    t, d = x.shape

    def body(x_ref, f_ref, g_ref, b_ref, y_ref, yb_ref, xh_ref, rs_ref):
        z = DEEPNORM_ALPHA * x_ref[...] + f_ref[...]
        mu = jnp.mean(z, axis=-1, keepdims=True)
        zc = z - mu
        var = jnp.mean(zc * zc, axis=-1, keepdims=True)
        rstd = lax.rsqrt(var + LN_EPS)
        xh = zc * rstd
        y = xh * g_ref[...] + b_ref[...]
        y_ref[...] = y
        yb_ref[...] = y.astype(BF16)
        xh_ref[...] = xh
        rs_ref[...] = rstd

    row = pl.BlockSpec((LN_ROWS, d), lambda i: (i, 0))
    vec = pl.BlockSpec((1, d), lambda i: (0, 0))
    return pl.pallas_call(
        body,
        name=name,
        grid=(t // LN_ROWS,),
        in_specs=[row, row, vec, vec],
        out_specs=[row, row, row, pl.BlockSpec((LN_ROWS, 1), lambda i: (i, 0))],
        out_shape=[jax.ShapeDtypeStruct((t, d), F32), jax.ShapeDtypeStruct((t, d), BF16),
                   jax.ShapeDtypeStruct((t, d), F32), jax.ShapeDtypeStruct((t, 1), F32)],
        compiler_params=_cparams(("parallel",)),
    )(x, f, g, b)


def _ln_bwd(dy, xhat, rstd, g, *, name):
    t, d = dy.shape

    def body(dy_ref, xh_ref, rs_ref, g_ref, dz_ref, dzb_ref, dg_ref, db_ref):
        dyv = dy_ref[...]
        xh = xh_ref[...]
        dyg = dyv * g_ref[...]
        m1 = jnp.mean(dyg, axis=-1, keepdims=True)
        m2 = jnp.mean(dyg * xh, axis=-1, keepdims=True)
        dz = rs_ref[...] * (dyg - m1 - xh * m2)
        dz_ref[...] = dz
        dzb_ref[...] = dz.astype(BF16)
        dg_part = jnp.sum(dyv * xh, axis=0, keepdims=True)
        db_part = jnp.sum(dyv, axis=0, keepdims=True)

        @pl.when(pl.program_id(0) == 0)
        def _():
            dg_ref[...] = dg_part
            db_ref[...] = db_part

        @pl.when(pl.program_id(0) > 0)
        def _():
            dg_ref[...] += dg_part
            db_ref[...] += db_part

    row = pl.BlockSpec((LN_ROWS, d), lambda i: (i, 0))
    vec = pl.BlockSpec((1, d), lambda i: (0, 0))
    return pl.pallas_call(
        body,
        name=name,
        grid=(t // LN_ROWS,),
        in_specs=[row, row, pl.BlockSpec((LN_ROWS, 1), lambda i: (i, 0)), vec],
        out_specs=[row, row, vec, vec],
        out_shape=[jax.ShapeDtypeStruct((t, d), F32), jax.ShapeDtypeStruct((t, d), BF16),
                   jax.ShapeDtypeStruct((1, d), F32), jax.ShapeDtypeStruct((1, d), F32)],
        compiler_params=_cparams(("arbitrary",)),
    )(dy, xhat, rstd, g)


def _loss_fwd_bwd(y, target, *, name):
    t, d = y.shape

    def body(y_ref, t_ref, loss_ref, dy_ref):
        err = y_ref[...] - t_ref[...]
        dy_ref[...] = err * (1.0 / d)
        part = 0.5 * jnp.sum(jnp.mean(err * err, axis=-1, keepdims=True), axis=0, keepdims=True)

        @pl.when(pl.program_id(0) == 0)
        def _():
            loss_ref[...] = part

        @pl.when(pl.program_id(0) > 0)
        def _():
            loss_ref[...] += part

    row = pl.BlockSpec((LN_ROWS, d), lambda i: (i, 0))
    return pl.pallas_call(
        body,
        name=name,
        grid=(t // LN_ROWS,),
        in_specs=[row, row],
        out_specs=[pl.BlockSpec((1, 1), lambda i: (0, 0)), row],
        out_shape=[jax.ShapeDtypeStruct((1, 1), F32), jax.ShapeDtypeStruct((t, d), F32)],
        compiler_params=_cparams(("arbitrary",)),
    )(y, target)


FFN_COLS = 256


def _shift_rows(h, s):
    rows = lax.broadcasted_iota(jnp.int32, h.shape, 0)
    return jnp.where(rows >= s, pltpu.roll(h, s, 0), 0.0)


def _shift_rows_up(h, s):
    n = h.shape[0]
    rows = lax.broadcasted_iota(jnp.int32, h.shape, 0)
    return jnp.where(rows < n - s, pltpu.roll(h, n - s, 0), 0.0)


def _causal_conv(h, w, b):
    return w[0:1, :] * _shift_rows(h, 2) + w[1:2, :] * _shift_rows(h, 1) + w[2:3, :] * h + b


def _sigmoid(x):
    return 1.0 / (1.0 + jnp.exp(-x))


def _convgate_fwd(hg, hu, wg, wu, bg, bu, *, name):
    t, n = hg.shape
    nb = t // SEQ

    def body(hg_ref, hu_ref, wg_ref, wu_ref, bg_ref, bu_ref, a_ref):
        gate = _causal_conv(hg_ref[...], wg_ref[...], bg_ref[...])
        up = _causal_conv(hu_ref[...], wu_ref[...], bu_ref[...])
        a_ref[...] = (gate * _sigmoid(gate) * up).astype(BF16)

    blk = pl.BlockSpec((SEQ, FFN_COLS), lambda s, j: (s, j))
    wsp = pl.BlockSpec((3, FFN_COLS), lambda s, j: (0, j))
    bsp = pl.BlockSpec((1, FFN_COLS), lambda s, j: (0, j))
    return pl.pallas_call(
        body,
        name=name,
        grid=(nb, n // FFN_COLS),
        in_specs=[blk, blk, wsp, wsp, bsp, bsp],
        out_specs=blk,
        out_shape=jax.ShapeDtypeStruct((t, n), BF16),
        compiler_params=_cparams(("parallel", "parallel")),
    )(hg, hu, wg, wu, bg, bu)


def _convgate_bwd(hg, hu, dact, wg, wu, bg, bu, *, name):
    t, n = hg.shape
    nb = t // SEQ

    def body(hg_ref, hu_ref, da_ref, wg_ref, wu_ref, bg_ref, bu_ref,
             dhg_ref, dhu_ref, dwg_ref, dwu_ref, dbg_ref, dbu_ref):
        hgv, huv = hg_ref[...], hu_ref[...]
        wgv, wuv = wg_ref[...], wu_ref[...]
        gate = _causal_conv(hgv, wgv, bg_ref[...])
        up = _causal_conv(huv, wuv, bu_ref[...])
        sg = _sigmoid(gate)
        da = da_ref[...]
        dgate = da * up * (sg * (1.0 + gate * (1.0 - sg)))
        dup = da * (gate * sg)

        def conv_bwd(dc, h, w, dh_ref, dw_ref, db_ref):
            dh = w[2:3, :] * dc + w[1:2, :] * _shift_rows_up(dc, 1) + w[0:1, :] * _shift_rows_up(dc, 2)
            dh_ref[...] = dh.astype(BF16)
            dws = [jnp.sum(dc * _shift_rows(h, 2), axis=0, keepdims=True),
                   jnp.sum(dc * _shift_rows(h, 1), axis=0, keepdims=True),
                   jnp.sum(dc * h, axis=0, keepdims=True)]
            db = jnp.sum(dc, axis=0, keepdims=True)

            @pl.when(pl.program_id(1) == 0)
            def _():
                for r in range(3):
                    dw_ref[r:r + 1, :] = dws[r]
                db_ref[...] = db

            @pl.when(pl.program_id(1) > 0)
            def _():
                for r in range(3):
                    dw_ref[r:r + 1, :] += dws[r]
                db_ref[...] += db

        conv_bwd(dgate, hgv, wgv, dhg_ref, dwg_ref, dbg_ref)
        conv_bwd(dup, huv, wuv, dhu_ref, dwu_ref, dbu_ref)

    blk = pl.BlockSpec((SEQ, FFN_COLS), lambda j, s: (s, j))
    wsp = pl.BlockSpec((3, FFN_COLS), lambda j, s: (0, j))
    bsp = pl.BlockSpec((1, FFN_COLS), lambda j, s: (0, j))
    return pl.pallas_call(
        body,
        name=name,
        grid=(n // FFN_COLS, nb),
        in_specs=[blk, blk, blk, wsp, wsp, bsp, bsp],
        out_specs=[blk, blk, wsp, wsp, bsp, bsp],
        out_shape=[jax.ShapeDtypeStruct((t, n), BF16), jax.ShapeDtypeStruct((t, n), BF16),
                   jax.ShapeDtypeStruct((3, n), F32), jax.ShapeDtypeStruct((3, n), F32),
                   jax.ShapeDtypeStruct((1, n), F32), jax.ShapeDtypeStruct((1, n), F32)],
        compiler_params=_cparams(("parallel", "arbitrary")),
    )(hg, hu, dact, wg, wu, bg, bu)


GLA_Q_SCALE = GLA_HEAD_K ** -0.5
GLA_NC = SEQ // GLA_CHUNK
_NT = (((1,), (1,)), ((), ()))
_TN = (((0,), (0,)), ((), ()))


def _cumsum_rows(g):
    n = g.shape[0]
    rows = lax.broadcasted_iota(jnp.int32, g.shape, 0)
    s = 1
    while s < n:
        g = g + jnp.where(rows >= s, pltpu.roll(g, s, 0), 0.0)
        s *= 2
    return g


def _suffix_sum_rows(x):
    n = x.shape[0]
    rows = lax.broadcasted_iota(jnp.int32, x.shape, 0)
    s = 1
    while s < n:
        x = x + jnp.where(rows < n - s, pltpu.roll(x, n - s, 0), 0.0)
        s *= 2
    return x


def _gla_log_gate(gl_ref, wgu_ref, bias_ref):
    pre = jnp.dot(gl_ref[...].astype(BF16), wgu_ref[...], preferred_element_type=F32) + bias_ref[...]
    log_sig = jnp.minimum(pre, 0.0) - jnp.log(1.0 + jnp.exp(-jnp.abs(pre)))
    return pre, log_sig * (1.0 / GLA_GATE_NORMALIZER)


def _pair_rows(j):
    return (j // SUBLANES) * SUBLANES


def _gla_pair_fwd(q_ref, k_ref, b_scr, a_scr, h):
    c = GLA_CHUNK
    kc = pl.ds(h * GLA_HEAD_K, GLA_HEAD_K)
    a_scr[...] = jnp.zeros(a_scr.shape, F32)
    lane = lax.broadcasted_iota(jnp.int32, (1, c), 1)
    for j in range(c):
        r0 = _pair_rows(j)
        rs = pl.ds(r0, c - r0)
        rows = lax.broadcasted_iota(jnp.int32, (c - r0, 1), 0) + r0
        e = jnp.exp(jnp.minimum(b_scr[rs, kc] - b_scr[pl.ds(j, 1), kc], 0.0))
        w = (q_ref[rs, kc] * GLA_Q_SCALE) * k_ref[pl.ds(j, 1), kc] * e
        col = jnp.where(rows >= j, jnp.sum(w, axis=-1, keepdims=True), 0.0)
        a_scr[rs, :] += col * (lane == j).astype(F32)


def _gla_fwd(proj, wgu, bias, ng, *, name):
    t = proj.shape[0]
    nb, nc, c = t // SEQ, GLA_NC, GLA_CHUNK

    def body(q_ref, k_ref, v_ref, r_ref, gl_ref, wgu_ref, bias_ref, ng_ref,
             y_ref, o_ref, a_ref, st_ref, state, b_scr, a_scr):
        @pl.when(pl.program_id(1) == 0)
        def _():
            state[...] = jnp.zeros(state.shape, F32)

        _, g = _gla_log_gate(gl_ref, wgu_ref, bias_ref)
        b_scr[...] = _cumsum_rows(g)
        for h in range(GLA_HEADS):
            kc = pl.ds(h * GLA_HEAD_K, GLA_HEAD_K)
            vc = pl.ds(h * GLA_HEAD_V, GLA_HEAD_V)
            qh = q_ref[:, kc] * GLA_Q_SCALE
            kh = k_ref[:, kc]
            vh = v_ref[:, vc].astype(BF16)
            bh = b_scr[:, kc]
            blast = b_scr[pl.ds(c - 1, 1), kc]
            st = state[h]
            st_ref[h] = st
            o_inter = lax.dot_general((qh * jnp.exp(bh)).astype(BF16), st.astype(BF16), _NT, preferred_element_type=F32)
            _gla_pair_fwd(q_ref, k_ref, b_scr, a_scr, h)
            a = a_scr[...]
            a_ref[h] = a
            o = o_inter + jnp.dot(a.astype(BF16), vh, preferred_element_type=F32)
            kd = (kh * jnp.exp(blast - bh)).astype(BF16)
            state[h] = st * jnp.exp(blast) + lax.dot_general(vh, kd, _TN, preferred_element_type=F32)
            o_ref[:, vc] = o
            rs = lax.rsqrt(jnp.mean(o * o, axis=-1, keepdims=True) + RMS_EPS)
            rh = r_ref[:, vc]
            y_ref[:, vc] = ((o * rs * ng_ref[...]) * (rh * _sigmoid(rh))).astype(BF16)

    def tok(width, col):
        return pl.BlockSpec((c, width), lambda b, i: (b * nc + i, col))

    whole = lambda shape: pl.BlockSpec(shape, lambda b, i: (0,) * len(shape))
    return pl.pallas_call(
        body,
        name=name,
        grid=(nb, nc),
        in_specs=[tok(GLA_DK, 0), tok(GLA_DK, 1), tok(GLA_DV, 1), tok(GLA_DV, 2), tok(GLOW_PAD, GLA_MAIN // GLOW_PAD),
                  whole((GLOW_PAD, GLA_DK)), whole((1, GLA_DK)), whole((1, GLA_HEAD_V))],
        out_specs=[tok(GLA_DV, 0), tok(GLA_DV, 0),
                   pl.BlockSpec((GLA_HEADS, c, c), lambda b, i: (0, b * nc + i, 0)),
                   pl.BlockSpec((None, GLA_HEADS, GLA_HEAD_V, GLA_HEAD_K), lambda b, i: (b * nc + i, 0, 0, 0))],
        out_shape=[jax.ShapeDtypeStruct((t, GLA_DV), BF16), jax.ShapeDtypeStruct((t, GLA_DV), F32),
                   jax.ShapeDtypeStruct((GLA_HEADS, t, c), F32),
                   jax.ShapeDtypeStruct((t // c, GLA_HEADS, GLA_HEAD_V, GLA_HEAD_K), F32)],
        scratch_shapes=[pltpu.VMEM((GLA_HEADS, GLA_HEAD_V, GLA_HEAD_K), F32), pltpu.VMEM((c, GLA_DK), F32),
                        pltpu.VMEM((c, c), F32)],
        compiler_params=_cparams(("parallel", "arbitrary")),
    )(proj, proj, proj, proj, proj, wgu, bias, ng)


def _gla_pair_bwd(q_ref, k_ref, b_scr, da_scr, dq_scr, dk_scr, h):
    c = GLA_CHUNK
    kc = pl.ds(h * GLA_HEAD_K, GLA_HEAD_K)
    lane = lax.broadcasted_iota(jnp.int32, (1, c), 1)
    for j in range(c):
        r0 = _pair_rows(j)
        rs = pl.ds(r0, c - r0)
        rows = lax.broadcasted_iota(jnp.int32, (c - r0, 1), 0) + r0
        e = jnp.exp(jnp.minimum(b_scr[rs, kc] - b_scr[pl.ds(j, 1), kc], 0.0))
        dacol = jnp.sum(jnp.where(lane == j, da_scr[rs, :], 0.0), axis=-1, keepdims=True)
        t1 = jnp.where(rows >= j, dacol, 0.0) * e
        dq_scr[rs, kc] += t1 * k_ref[pl.ds(j, 1), kc]
        dk_scr[pl.ds(j, 1), kc] += jnp.sum(t1 * (q_ref[rs, kc] * GLA_Q_SCALE), axis=0, keepdims=True)


def _gla_bwd(proj, wgu, bias, ng, o, a, states, dy, *, name):
    t = proj.shape[0]
    nb, nc, c = t // SEQ, GLA_NC, GLA_CHUNK

    def body(q_ref, k_ref, v_ref, r_ref, gl_ref, wgu_ref, bias_ref, ng_ref, o_ref, a_ref, stp_ref, stn_ref, dy_ref,
             dq_ref, dk_ref, dv_ref, dr_ref, dgl_ref, dwgu_ref, dbias_ref, dng_ref,
             dstate, b_scr, da_scr, dq_scr, dk_scr, dg_scr):
        first = jnp.logical_and(pl.program_id(0) == 0, pl.program_id(1) == 0)

        @pl.when(first)
        def _():
            dwgu_ref[...] = jnp.zeros(dwgu_ref.shape, F32)
            dbias_ref[...] = jnp.zeros(dbias_ref.shape, F32)
            dng_ref[...] = jnp.zeros(dng_ref.shape, F32)

        @pl.when(pl.program_id(1) == 0)
        def _():
            dstate[...] = jnp.zeros(dstate.shape, F32)

        pre, g = _gla_log_gate(gl_ref, wgu_ref, bias_ref)
        b_scr[...] = _cumsum_rows(g)
        ngv = ng_ref[...]
        tri = lax.broadcasted_iota(jnp.int32, (c, c), 0) >= lax.broadcasted_iota(jnp.int32, (c, c), 1)
        for h in range(GLA_HEADS):
            kc = pl.ds(h * GLA_HEAD_K, GLA_HEAD_K)
            vc = pl.ds(h * GLA_HEAD_V, GLA_HEAD_V)
            oh = o_ref[:, vc]
            rh = r_ref[:, vc]
            dyh = dy_ref[:, vc]
            rs = lax.rsqrt(jnp.mean(oh * oh, axis=-1, keepdims=True) + RMS_EPS)
            u = oh * rs
            sg = _sigmoid(rh)
            sr = rh * sg
            dr_ref[:, vc] = (dyh * (u * ngv) * (sg * (1.0 + rh * (1.0 - sg)))).astype(BF16)
            dng_ref[...] += jnp.sum(dyh * sr * u, axis=0, keepdims=True)
            du = dyh * sr * ngv
            do = (rs * (du - u * jnp.mean(du * u, axis=-1, keepdims=True))).astype(BF16)
            qh = q_ref[:, kc] * GLA_Q_SCALE
            kh = k_ref[:, kc]
            vh = v_ref[:, vc].astype(BF16)
            bh = b_scr[:, kc]
            blast = b_scr[pl.ds(c - 1, 1), kc]
            eb = jnp.exp(bh)
            ek = jnp.exp(blast - bh)
            dst = dstate[h]
            dst_b = dst.astype(BF16)
            dg_carry = jnp.sum(dst * stn_ref[h], axis=0, keepdims=True)
            da = lax.dot_general(do, vh, _NT, preferred_element_type=F32)
            da_scr[...] = jnp.where(tri, da, 0.0)
            dv = lax.dot_general(a_ref[h].astype(BF16), do, _TN, preferred_element_type=F32)
            dv = dv + lax.dot_general((kh * ek).astype(BF16), dst_b, _NT, preferred_element_type=F32)
            dv_ref[:, vc] = dv.astype(BF16)
            dq_scr[:, kc] = jnp.dot(do, stp_ref[h].astype(BF16), preferred_element_type=F32) * eb
            dk_scr[:, kc] = jnp.dot(vh, dst_b, preferred_element_type=F32) * ek
            _gla_pair_bwd(q_ref, k_ref, b_scr, da_scr, dq_scr, dk_scr, h)
            dq = dq_scr[:, kc]
            dk = dk_scr[:, kc]
            dg_scr[:, kc] = _suffix_sum_rows(qh * dq - kh * dk) + dg_carry
            dstate[h] = dst * jnp.exp(blast) + lax.dot_general(do, (qh * eb).astype(BF16), _TN, preferred_element_type=F32)
        dq_ref[...] = (dq_scr[...] * GLA_Q_SCALE).astype(BF16)
        dk_ref[...] = dk_scr[...].astype(BF16)
        dpre = dg_scr[...] * ((1.0 - _sigmoid(pre)) * (1.0 / GLA_GATE_NORMALIZER))
        dpre_b = dpre.astype(BF16)
        dbias_ref[...] += jnp.sum(dpre, axis=0, keepdims=True)
        dwgu_ref[...] += lax.dot_general(gl_ref[...].astype(BF16), dpre_b, _TN, preferred_element_type=F32)
        dgl_ref[...] = lax.dot_general(dpre_b, wgu_ref[...], _NT, preferred_element_type=F32).astype(BF16)

    def chunk(b, i):
        return b * nc + (nc - 1 - i)

    def tok(width, col):
        return pl.BlockSpec((c, width), lambda b, i: (chunk(b, i), col))

    whole = lambda shape: pl.BlockSpec(shape, lambda b, i: (0,) * len(shape))
    st_shape = (None, GLA_HEADS, GLA_HEAD_V, GLA_HEAD_K)
    return pl.pallas_call(
        body,
        name=name,
        grid=(nb, nc),
        in_specs=[tok(GLA_DK, 0), tok(GLA_DK, 1), tok(GLA_DV, 1), tok(GLA_DV, 2), tok(GLOW_PAD, GLA_MAIN // GLOW_PAD),
                  whole((GLOW_PAD, GLA_DK)), whole((1, GLA_DK)), whole((1, GLA_HEAD_V)),
                  tok(GLA_DV, 0),
                  pl.BlockSpec((GLA_HEADS, c, c), lambda b, i: (0, chunk(b, i), 0)),
                  pl.BlockSpec(st_shape, lambda b, i: (chunk(b, i), 0, 0, 0)),
                  pl.BlockSpec(st_shape, lambda b, i: (b * nc + jnp.minimum(nc - i, nc - 1), 0, 0, 0)),
                  tok(GLA_DV, 0)],
        out_specs=[tok(GLA_DK, 0), tok(GLA_DK, 0), tok(GLA_DV, 0), tok(GLA_DV, 0), tok(GLOW_PAD, 0),
                   whole((GLOW_PAD, GLA_DK)), whole((1, GLA_DK)), whole((1, GLA_HEAD_V))],
        out_shape=[jax.ShapeDtypeStruct((t, GLA_DK), BF16), jax.ShapeDtypeStruct((t, GLA_DK), BF16),
                   jax.ShapeDtypeStruct((t, GLA_DV), BF16), jax.ShapeDtypeStruct((t, GLA_DV), BF16),
                   jax.ShapeDtypeStruct((t, GLOW_PAD), BF16),
                   jax.ShapeDtypeStruct((GLOW_PAD, GLA_DK), F32), jax.ShapeDtypeStruct((1, GLA_DK), F32),
                   jax.ShapeDtypeStruct((1, GLA_HEAD_V), F32)],
        scratch_shapes=[pltpu.VMEM((GLA_HEADS, GLA_HEAD_V, GLA_HEAD_K), F32), pltpu.VMEM((c, GLA_DK), F32),
                        pltpu.VMEM((c, c), F32), pltpu.VMEM((c, GLA_DK), F32), pltpu.VMEM((c, GLA_DK), F32),
                        pltpu.VMEM((c, GLA_DK), F32)],
        compiler_params=_cparams(("arbitrary", "arbitrary")),
    )(proj, proj, proj, proj, proj, wgu, bias, ng, o, a, states, states, dy)


DIL_STEPS = DIL_BLOCK
DIL_SCALE = DIL_HEAD_DIM ** -0.5
DIL_COLS = DIL_IN // DIL_WIDTH


def _dil_mask(i):
    rowi = lax.broadcasted_iota(jnp.int32, (DIL_BLOCK, 2 * DIL_BLOCK), 0)
    colj = lax.broadcasted_iota(jnp.int32, (DIL_BLOCK, 2 * DIL_BLOCK), 1)
    dist = rowi + DIL_BLOCK - colj
    band = jnp.logical_and(dist >= 0, dist <= DIL_STEPS)
    return jnp.logical_and(band, jnp.logical_or(i > 0, colj >= DIL_BLOCK))


def _dil_geometry(t, gi):
    _, d = DIL_PATTERNS[gi]
    length = SEQ // d
    return d, length, length // DIL_BLOCK, t // SEQ


def _dil_in_specs(gi, d, nq, clamp):
    def spec(j, prev):
        def index(b, r, i):
            ic = clamp(i)
            return (b * nq + (jnp.maximum(ic - 1, 0) if prev else ic), r * DIL_COLS + gi * 3 + j)
        return pl.BlockSpec((DIL_BLOCK, DIL_WIDTH), index)
    return [spec(0, False), spec(1, False), spec(1, True), spec(2, False), spec(2, True)]


def _dil_fwd(proj, gi, *, name):
    t = proj.shape[0]
    d, length, nq, nb = _dil_geometry(t, gi)
    pv = proj.reshape(nb * length, d * DIL_IN)

    def body(q_ref, kc_ref, kp_ref, vc_ref, vp_ref, o_ref, lse_ref):
        mask = _dil_mask(pl.program_id(2))
        for h in range(DIL_HEADS):
            hc = pl.ds(h * DIL_HEAD_DIM, DIL_HEAD_DIM)
            qh = q_ref[:, hc].astype(BF16)
            kcat = jnp.concatenate([kp_ref[:, hc], kc_ref[:, hc]], axis=0).astype(BF16)
            vcat = jnp.concatenate([vp_ref[:, hc], vc_ref[:, hc]], axis=0).astype(BF16)
            s = lax.dot_general(qh, kcat, _NT, preferred_element_type=F32) * DIL_SCALE
            s = jnp.where(mask, s, -jnp.inf)
            m = jnp.max(s, axis=-1, keepdims=True)
            p = jnp.exp(s - m)
            l = jnp.sum(p, axis=-1, keepdims=True)
            o_ref[:, hc] = jnp.dot((p / l).astype(BF16), vcat, preferred_element_type=F32)
            lse_ref[:, hc] = jnp.broadcast_to(m + jnp.log(l), (DIL_BLOCK, DIL_HEAD_DIM))

    out_spec = pl.BlockSpec((DIL_BLOCK, DIL_WIDTH), lambda b, r, i: (b * nq + i, r))
    o, lse = pl.pallas_call(
        body,
        name=name,
        grid=(nb, d, nq),
        in_specs=_dil_in_specs(gi, d, nq, lambda i: i),
        out_specs=[out_spec, out_spec],
        out_shape=[jax.ShapeDtypeStruct((nb * length, d * DIL_WIDTH), F32)] * 2,
        compiler_params=_cparams(("parallel", "parallel", "parallel")),
    )(pv, pv, pv, pv, pv)
    return o.reshape(t, DIL_WIDTH), lse.reshape(t, DIL_WIDTH)


def _dil_bwd(proj, gi, lse, do, delta, *, name):
    t = proj.shape[0]
    d, length, nq, nb = _dil_geometry(t, gi)
    pv = proj.reshape(nb * length, d * DIL_IN)
    view = lambda a: a.reshape(nb * length, d * DIL_WIDTH)

    def body(q_ref, kc_ref, kp_ref, vc_ref, vp_ref, lse_ref, do_ref, dl_ref, dq_ref, dk_ref, dv_ref, ck, cv):
        i = pl.program_id(2)

        @pl.when(i < nq)
        def _():
            mask = _dil_mask(i)
            for h in range(DIL_HEADS):
                hc = pl.ds(h * DIL_HEAD_DIM, DIL_HEAD_DIM)
                h1 = pl.ds(h * DIL_HEAD_DIM, 1)
                qh = q_ref[:, hc].astype(BF16)
                kcat = jnp.concatenate([kp_ref[:, hc], kc_ref[:, hc]], axis=0).astype(BF16)
                vcat = jnp.concatenate([vp_ref[:, hc], vc_ref[:, hc]], axis=0).astype(BF16)
                doh = do_ref[:, hc]
                s = lax.dot_general(qh, kcat, _NT, preferred_element_type=F32) * DIL_SCALE
                p = jnp.exp(jnp.where(mask, s, -jnp.inf) - lse_ref[:, h1])
                dp = lax.dot_general(doh, vcat, _NT, preferred_element_type=F32)
                ds = (p * (dp + dl_ref[:, h1]) * DIL_SCALE).astype(BF16)
                dq_ref[:, hc] = jnp.dot(ds, kcat, preferred_element_type=F32).astype(BF16)
                dkcat = lax.dot_general(ds, qh, _TN, preferred_element_type=F32)
                dvcat = lax.dot_general(p.astype(BF16), doh, _TN, preferred_element_type=F32)

                @pl.when(i > 0)
                def _():
                    dk_ref[:, hc] = (ck[:, hc] + dkcat[:DIL_BLOCK]).astype(BF16)
                    dv_ref[:, hc] = (cv[:, hc] + dvcat[:DIL_BLOCK]).astype(BF16)

                ck[:, hc] = dkcat[DIL_BLOCK:]
                cv[:, hc] = dvcat[DIL_BLOCK:]

        @pl.when(i == nq)
        def _():
            dk_ref[...] = ck[...].astype(BF16)
            dv_ref[...] = cv[...].astype(BF16)

    clamp = lambda i: jnp.minimum(i, nq - 1)
    cur = pl.BlockSpec((DIL_BLOCK, DIL_WIDTH), lambda b, r, i: (b * nq + clamp(i), r))
    done = pl.BlockSpec((DIL_BLOCK, DIL_WIDTH), lambda b, r, i: (b * nq + jnp.maximum(i - 1, 0), r))
    shape = jax.ShapeDtypeStruct((nb * length, d * DIL_WIDTH), BF16)
    dq, dk, dv = pl.pallas_call(
        body,
        name=name,
        grid=(nb, d, nq + 1),
        in_specs=_dil_in_specs(gi, d, nq, clamp) + [cur, cur, cur],
        out_specs=[cur, done, done],
        out_shape=[shape, shape, shape],
        scratch_shapes=[pltpu.VMEM((DIL_BLOCK, DIL_WIDTH), F32), pltpu.VMEM((DIL_BLOCK, DIL_WIDTH), F32)],
        compiler_params=_cparams(("parallel", "parallel", "arbitrary")),
    )(pv, pv, pv, pv, pv, view(lse), view(do), view(delta))
    return dq.reshape(t, DIL_WIDTH), dk.reshape(t, DIL_WIDTH), dv.reshape(t, DIL_WIDTH)


MIX_ROWS = 256


def _head_rowsum(x):
    parts = []
    for h in range(DIL_HEADS):
        s = jnp.sum(x[:, h * DIL_HEAD_DIM:(h + 1) * DIL_HEAD_DIM], axis=-1, keepdims=True)
        parts.append(jnp.broadcast_to(s, (x.shape[0], DIL_HEAD_DIM)))
    return jnp.concatenate(parts, axis=-1)


def _mix_weights(lse_refs):
    ls = [r[...] for r in lse_refs]
    m = jnp.maximum(jnp.maximum(ls[0], ls[1]), ls[2])
    es = [jnp.exp(l - m) for l in ls]
    inv = 1.0 / (es[0] + es[1] + es[2])
    return [e * inv for e in es]


def _dil_mix_fwd(os_, lses, *, name):
    t = os_[0].shape[0]

    def body(o0, o1, o2, l0, l1, l2, out_ref):
        w = _mix_weights((l0, l1, l2))
        out_ref[...] = (w[0] * o0[...] + w[1] * o1[...] + w[2] * o2[...]).astype(BF16)

    row = pl.BlockSpec((MIX_ROWS, DIL_WIDTH), lambda i: (i, 0))
    return pl.pallas_call(
        body, name=name, grid=(t // MIX_ROWS,), in_specs=[row] * 6, out_specs=row,
        out_shape=jax.ShapeDtypeStruct((t, DIL_WIDTH), BF16), compiler_params=_cparams(("parallel",)),
    )(*os_, *lses)


def _dil_mix_bwd(os_, lses, dout, *, name):
    t = os_[0].shape[0]

    def body(o0, o1, o2, l0, l1, l2, d_ref, do0, do1, do2, dl0, dl1, dl2):
        w = _mix_weights((l0, l1, l2))
        dv = d_ref[...]
        mix = w[0] * o0[...] + w[1] * o1[...] + w[2] * o2[...]
        bar = _head_rowsum(dv * mix)
        for wg, do_ref, dl_ref in zip(w, (do0, do1, do2), (dl0, dl1, dl2)):
            do_ref[...] = (wg * dv).astype(BF16)
            dl_ref[...] = -wg * bar

    row = pl.BlockSpec((MIX_ROWS, DIL_WIDTH), lambda i: (i, 0))
    outs = pl.pallas_call(
        body, name=name, grid=(t // MIX_ROWS,), in_specs=[row] * 7, out_specs=[row] * 6,
        out_shape=[jax.ShapeDtypeStruct((t, DIL_WIDTH), BF16)] * 3 + [jax.ShapeDtypeStruct((t, DIL_WIDTH), F32)] * 3,
        compiler_params=_cparams(("parallel",)),
    )(*os_, *lses, dout)
    return outs[:3], outs[3:]


_MESH = pl.DeviceIdType.MESH
_ANY = pl.BlockSpec(memory_space=pl.ANY)


def _position():
    return lax.axis_index("x"), lax.axis_index("y"), lax.axis_index("c")


def _all_gather(shard, *, name):
    r, c = shard.shape

    def body(x_ref, out_ref, send_sems, recv_sems, local_sem):
        x, y, cc = _position()
        me, sibling = (x, y, cc), (x, y, 1 - cc)
        chips = [(1 - x, y), (x, 1 - y), (1 - x, 1 - y)]

        def slot(px, py, pc):
            return out_ref.at[4 * px + 2 * py + pc]

        def copy(k, block, to, src=None):
            return pltpu.make_async_remote_copy(
                src_ref=slot(*block) if src is None else src, dst_ref=slot(*block),
                send_sem=send_sems.at[k], recv_sem=recv_sems.at[k], device_id=to, device_id_type=_MESH)

        mine = pltpu.make_async_copy(x_ref, slot(*me), local_sem)
        mine.start()
        first = [copy(0, me, sibling, src=x_ref)]
        first += [copy(1 + j, me, (*chip, cc), src=x_ref) for j, chip in enumerate(chips)]
        for cp in first:
            cp.start()
        passed = [copy(4 + j, (*chip, cc), sibling) for j, chip in enumerate(chips)]
        for j, chip in enumerate(chips):
            copy(1 + j, (*chip, cc), me).wait_recv()
            passed[j].start()
        copy(0, sibling, me).wait_recv()
        for j, chip in enumerate(chips):
            copy(4 + j, (*chip, 1 - cc), me).wait_recv()
        for cp in first + passed:
            cp.wait_send()
        mine.wait()

    return pl.pallas_call(
        body,
        name=name,
        out_shape=jax.ShapeDtypeStruct((N_DEV, r, c), shard.dtype),
        in_specs=[_ANY],
        out_specs=_ANY,
        scratch_shapes=[pltpu.SemaphoreType.DMA((7,)), pltpu.SemaphoreType.DMA((7,)), pltpu.SemaphoreType.DMA],
    )(shard)


def _exchange_sibling(gs, *, name):
    n = len(gs)

    def body(*refs):
        g_refs, out_refs = refs[:n], refs[n:2 * n]
        send_sems, recv_sems = refs[2 * n], refs[2 * n + 1]
        x, y, cc = _position()
        copies = [pltpu.make_async_remote_copy(
            src_ref=g_refs[k].at[1 - cc], dst_ref=out_refs[k], send_sem=send_sems.at[k], recv_sem=recv_sems.at[k],
            device_id=(x, y, 1 - cc), device_id_type=_MESH) for k in range(n)]
        for cp in copies:
            cp.start()
        for cp in copies:
            cp.wait()

    return pl.pallas_call(
        body,
        name=name,
        out_shape=[jax.ShapeDtypeStruct(g.shape[1:], g.dtype) for g in gs],
        in_specs=[_ANY] * n,
        out_specs=[_ANY] * n,
        scratch_shapes=[pltpu.SemaphoreType.DMA((n,)), pltpu.SemaphoreType.DMA((n,))],
    )(*gs)


def _exchange_chips(ps, *, name):
    n = len(ps)

    def body(*refs):
        p_refs, out_refs = refs[:n], refs[n:2 * n]
        send_sems, recv_sems = refs[2 * n], refs[2 * n + 1]
        x, y, cc = _position()
        copies = []
        for w in range(n):
            for k in (1, 2, 3):
                px = 1 - x if k >> 1 else x
                py = 1 - y if k & 1 else y
                copies.append(pltpu.make_async_remote_copy(
                    src_ref=p_refs[w].at[2 * px + py], dst_ref=out_refs[w].at[k - 1],
                    send_sem=send_sems.at[3 * w + k - 1], recv_sem=recv_sems.at[3 * w + k - 1],
                    device_id=(px, py, cc), device_id_type=_MESH))
        for cp in copies:
            cp.start()
        for cp in copies:
            cp.wait()

    return pl.pallas_call(
        body,
        name=name,
        out_shape=[jax.ShapeDtypeStruct((3,) + p.shape[1:], p.dtype) for p in ps],
        in_specs=[_ANY] * n,
        out_specs=[_ANY] * n,
        scratch_shapes=[pltpu.SemaphoreType.DMA((3 * n,)), pltpu.SemaphoreType.DMA((3 * n,))],
    )(*ps)


def _add_sibling(g, r1, parity, *, tr, tc, name):
    _, nchip, r, c = g.shape

    def body(par_ref, g_ref, r_ref, o_ref):
        o_ref[...] = g_ref[...] + r_ref[...]

    grid_spec = pltpu.PrefetchScalarGridSpec(
        num_scalar_prefetch=1,
        grid=(nchip, r // tr, c // tc),
        in_specs=[pl.BlockSpec((None, None, tr, tc), lambda k, i, j, par: (par[0], k, i, j)),
                  pl.BlockSpec((None, tr, tc), lambda k, i, j, par: (k, i, j))],
        out_specs=pl.BlockSpec((None, tr, tc), lambda k, i, j, par: (k, i, j)),
    )
    return pl.pallas_call(
        body, name=name, grid_spec=grid_spec, out_shape=jax.ShapeDtypeStruct((nchip, r, c), F32),
        compiler_params=_cparams(("parallel", "parallel", "parallel")),
    )(parity, g, r1)


def _adamw_math(g, w, m, v):
    m = ADAM_B1 * m + (1.0 - ADAM_B1) * g
    v = ADAM_B2 * v + (1.0 - ADAM_B2) * (g * g)
    m_hat = m / (1.0 - ADAM_B1 ** ADAM_STEP)
    v_hat = v / (1.0 - ADAM_B2 ** ADAM_STEP)
    delta = -ADAM_LR * (m_hat / (jnp.sqrt(v_hat) + ADAM_EPS) + ADAM_WD * w)
    return delta, m, v


def _adamw_big(p, r2, chip, w, m, v, prev, layer, *, tr, tc, name):
    _, r, c = w.shape

    def body(chip_ref, p_ref, r2_ref, w_ref, m_ref, v_ref, a0, a1, a2, a3, g_ref, d_ref, mo_ref, vo_ref):
        g = ((p_ref[...] + r2_ref[0]) + r2_ref[1]) + r2_ref[2]
        delta, mn, vn = _adamw_math(g, w_ref[...], m_ref[...], v_ref[...])
        g_ref[...] = g
        d_ref[...] = delta
        mo_ref[...] = mn
        vo_ref[...] = vn

    lay = pl.BlockSpec((None, tr, tc), lambda i, j, ch: (layer, i, j))
    grid_spec = pltpu.PrefetchScalarGridSpec(
        num_scalar_prefetch=1,
        grid=(r // tr, c // tc),
        in_specs=[pl.BlockSpec((None, tr, tc), lambda i, j, ch: (ch[0], i, j)),
                  pl.BlockSpec((3, tr, tc), lambda i, j, ch: (0, i, j)),
                  lay, lay, lay, _ANY, _ANY, _ANY, _ANY],
        out_specs=[lay, lay, lay, lay],
    )
    return pl.pallas_call(
        body, name=name, grid_spec=grid_spec, out_shape=[jax.ShapeDtypeStruct(w.shape, F32)] * 4,
        input_output_aliases={6: 0, 7: 1, 8: 2, 9: 3},
        compiler_params=_cparams(("parallel", "parallel")),
    )(chip, p, r2, w, m, v, *prev)


SMALL_COLS = 1024


def _sum_gathered(parts, *, name):
    _, r, c = parts.shape

    def body(p_ref, o_ref):
        acc = p_ref[0]
        for k in range(1, N_DEV):
            acc = acc + p_ref[k]
        o_ref[...] = acc

    return pl.pallas_call(
        body, name=name, grid=(1,), in_specs=[pl.BlockSpec((N_DEV, r, c), lambda i: (0, 0, 0))],
        out_specs=pl.BlockSpec((r, c), lambda i: (0, 0)), out_shape=jax.ShapeDtypeStruct((r, c), F32),
        compiler_params=_cparams(("arbitrary",)),
    )(parts)


def _adamw_small(g, w, m, v, *, name):
    r, c = g.shape

    def body(g_ref, w_ref, m_ref, v_ref, d_ref, mo_ref, vo_ref):
        delta, mn, vn = _adamw_math(g_ref[...], w_ref[...], m_ref[...], v_ref[...])
        d_ref[...] = delta
        mo_ref[...] = mn
        vo_ref[...] = vn

    spec = pl.BlockSpec((r, c), lambda i: (0, 0))
    return pl.pallas_call(
        body, name=name, grid=(1,), in_specs=[spec] * 4, out_specs=[spec] * 3,
        out_shape=[jax.ShapeDtypeStruct((r, c), F32)] * 3, compiler_params=_cparams(("arbitrary",)),
    )(g, w, m, v)


AG_COLS = 1024
WEIGHT_NAMES = ("gla_w_in", "gla_w_gate_up", "gla_gate_bias", "gla_norm_g", "gla_w_out", "dil_w_in", "dil_w_out",
                "ffn_w_up", "ffn_conv_w", "ffn_conv_b", "ffn_w_down", "ln_g", "ln_b")
ADAM_TILES = {"gla_w_in": (256, 770), "gla_w_out": (128, 2048), "dil_w_in": (256, 1152), "dil_w_out": (512, 256),
              "ffn_w_up": (128, 1376), "ffn_w_down": (344, 1024)}


def _layer_pieces(l, w):
    j = l // 2
    if l % 2 == 0:
        mixer = [("w_in", w["gla_w_in"][j], False), ("w_gate_up", w["gla_w_gate_up"][j], False),
                 ("w_out", w["gla_w_out"][j], False)]
    else:
        mixer = [("w_in", w["dil_w_in"][j], False), ("w_out", w["dil_w_out"][j], False)]
    return mixer + [("w_up", w["ffn_w_up"][l], False), ("w_down", w["ffn_w_down"][l], False),
                    ("conv_w", w["ffn_conv_w"][l], True), ("ln_g", w["ln_g"][l], True), ("ln_b", w["ln_b"][l], True)]


def _piece_len(a, bits):
    return math.prod(a.shape) * (2 if bits else 1)


def _gather_layer(l, w, rows):
    pieces = _layer_pieces(l, w)
    flat = [(lax.bitcast_convert_type(a, BF16) if bits else a.astype(BF16)).reshape(-1) for _, a, bits in pieces]
    used = sum(f.shape[0] for f in flat)
    buf = jnp.concatenate(flat + [jnp.zeros((rows * AG_COLS - used,), BF16)]).reshape(rows, AG_COLS)
    got = _all_gather(buf, name=f"gather_l{l}").reshape(N_DEV, rows * AG_COLS)
    out, off = {}, 0
    for key, a, bits in pieces:
        n = _piece_len(a, bits)
        part = got[:, off:off + n]
        off += n
        if bits:
            out[key] = lax.bitcast_convert_type(part.reshape((N_DEV,) + a.shape + (2,)), F32)
        else:
            out[key] = part.reshape((N_DEV,) + a.shape)
    return out


def _pad_last(a, to):
    return jnp.pad(a, [(0, 0)] * (a.ndim - 1) + [(0, to - a.shape[-1])])


def _ff_cols(blocks):
    r = blocks.shape[1]
    p = _pad_last(blocks, FF_SHARD_PAD).reshape(2, 4, r, FF_SHARD_PAD).transpose(0, 2, 1, 3).reshape(2, r, FF_HALF_PAD)
    return p[0], p[1]


def _ff_cols_back(a):
    r = a.shape[0]
    return a.reshape(r, 4, FF_SHARD_PAD)[:, :, :FF_SHARD].reshape(r, D_FF)


def _layer_weights(l, g, w):
    out = {}
    if l % 2 == 0:
        win = g["w_in"].transpose(1, 0, 2).reshape(D_MODEL, GLA_IN)
        out["w_in"] = _pad_last(win, GLA_MAIN + GLOW_PAD)
        wgu = g["w_gate_up"].transpose(1, 0, 2).reshape(GLA_GATE_RANK, GLA_DK)
        out["w_gate_up"] = jnp.pad(wgu, ((0, GLOW_PAD - GLA_GATE_RANK), (0, 0)))
        out["w_out"] = g["w_out"].reshape(GLA_DV, D_MODEL)
        out["gate_bias"] = w["gla_gate_bias"][l // 2].reshape(1, GLA_DK)
        out["norm_g"] = w["gla_norm_g"][l // 2].reshape(1, GLA_HEAD_V)
    else:
        out["w_in"] = g["w_in"].transpose(1, 0, 2).reshape(D_MODEL, DIL_IN)
        out["w_out"] = g["w_out"].transpose(1, 0, 2).reshape(DIL_WIDTH, D_MODEL)
    out["w_gate"], out["w_up"] = _ff_cols(g["w_up"])
    wd = g["w_down"].reshape(4, FF_SHARD, D_MODEL)
    out["w_down"] = jnp.pad(wd, ((0, 0), (0, FF_SHARD_PAD - FF_SHARD), (0, 0))).reshape(FF_HALF_PAD, D_MODEL)
    out["conv_w_gate"], out["conv_w_up"] = _ff_cols(g["conv_w"])
    cb = _pad_last(w["ffn_conv_b"][l].reshape(N_DEV, FF_SHARD), FF_SHARD_PAD).reshape(2, 1, FF_HALF_PAD)
    out["conv_b_gate"], out["conv_b_up"] = cb[0], cb[1]
    out["ln_g"] = g["ln_g"].transpose(1, 0, 2).reshape(2, 1, D_MODEL)
    out["ln_b"] = g["ln_b"].transpose(1, 0, 2).reshape(2, 1, D_MODEL)
    return out


def _by_parity_chip(blocks):
    return blocks.reshape((4, 2) + blocks.shape[1:]).transpose(1, 0, 2, 3)


def _col_blocks(dw, width):
    r = dw.shape[0]
    return dw.reshape(r, N_DEV, width).transpose(1, 0, 2)


def _ffn_fwd(l, yb, lw):
    hg = _matmul(yb, lw["w_gate"], tm=512, tn=2816, tk=D_MODEL, name=f"l{l}_ffn_gate")
    hu = _matmul(yb, lw["w_up"], tm=512, tn=2816, tk=D_MODEL, name=f"l{l}_ffn_up")
    act = _convgate_fwd(hg, hu, lw["conv_w_gate"], lw["conv_w_up"], lw["conv_b_gate"], lw["conv_b_up"], name=f"l{l}_convgate")
    ffn = _matmul(act, lw["w_down"], tm=1024, tn=1024, tk=2816, name=f"l{l}_ffn_down")
    return ffn, (hg, hu, act)


def _ffn_bwd(l, yb, dz, dzb, lw, saved):
    hg, hu, act = saved
    dact = _matmul(dzb, lw["w_down"], tb=True, tm=512, tn=2816, tk=D_MODEL, name=f"l{l}_ffn_dact")
    d_down = _matmul(act, dzb, ta=True, tm=1408, tn=1024, tk=2048, name=f"l{l}_ffn_dwdown")
    dhg, dhu, dcwg, dcwu, dcbg, dcbu = _convgate_bwd(
        hg, hu, dact, lw["conv_w_gate"], lw["conv_w_up"], lw["conv_b_gate"], lw["conv_b_up"], name=f"l{l}_convgate_bwd")
    d_gate = _matmul(yb, dhg, ta=True, tm=1024, tn=1408, tk=2048, name=f"l{l}_ffn_dwgate")
    d_up = _matmul(yb, dhu, ta=True, tm=1024, tn=1408, tk=2048, name=f"l{l}_ffn_dwup")
    dy = _matmul(dhg, lw["w_gate"], tb=True, tm=1024, tn=1024, tk=2816, res=dz, res_scale=DEEPNORM_ALPHA, name=f"l{l}_ffn_dy_gate")
    dy = _matmul(dhu, lw["w_up"], tb=True, tm=1024, tn=1024, tk=2816, res=dy, name=f"l{l}_ffn_dy_up")
    cols = lambda a: a.reshape(a.shape[0], 4, FF_SHARD_PAD)[:, :, :FF_SHARD].transpose(1, 0, 2)
    big = {"ffn_w_up": _by_parity_chip(jnp.concatenate([cols(d_gate), cols(d_up)], axis=0)),
           "ffn_w_down": _by_parity_chip(d_down.reshape(4, FF_SHARD_PAD, D_MODEL)[:, :FF_SHARD].reshape(N_DEV, FF_SHARD // 2, D_MODEL))}
    small = {"ffn_conv_w": jnp.concatenate([_ff_cols_back(dcwg), _ff_cols_back(dcwu)], axis=-1),
             "ffn_conv_b": jnp.concatenate([_ff_cols_back(dcbg), _ff_cols_back(dcbu)], axis=-1)[0]}
    return dy, big, small


def _gla_layer_fwd(l, hb, lw):
    proj = _matmul(hb, lw["w_in"], tm=1024, tn=896, tk=D_MODEL, name=f"l{l}_gla_proj")
    y, o, a, st = _gla_fwd(proj, lw["w_gate_up"], lw["gate_bias"], lw["norm_g"], name=f"l{l}_gla")
    mix = _matmul(y, lw["w_out"], tm=1024, tn=1024, tk=GLA_DV, name=f"l{l}_gla_out")
    return mix, (proj, y, o, a, st)


def _gla_layer_bwd(l, hb, dz, dzb, lw, saved):
    proj, y, o, a, st = saved
    dy = _matmul(dzb, lw["w_out"], tb=True, tm=1024, tn=1024, tk=D_MODEL, name=f"l{l}_gla_dy")
    d_out = _matmul(y, dzb, ta=True, tm=1024, tn=1024, tk=2048, name=f"l{l}_gla_dwout")
    dq, dk, dv, dr, dgl, dwgu, dbias, dng = _gla_bwd(proj, lw["w_gate_up"], lw["gate_bias"], lw["norm_g"], o, a, st, dy, name=f"l{l}_gla_bwd")
    dproj = jnp.concatenate([dq, dk, dv, dr, dgl], axis=-1)
    d_in = _matmul(hb, dproj, ta=True, tm=1024, tn=896, tk=2048, name=f"l{l}_gla_dwin")
    dx = _matmul(dproj, lw["w_in"], tb=True, tm=1024, tn=1024, tk=896, res=dz, res_scale=DEEPNORM_ALPHA, name=f"l{l}_gla_dx")
    big = {"gla_w_in": _by_parity_chip(_col_blocks(d_in[:, :GLA_IN], GLA_IN // N_DEV)),
           "gla_w_out": _by_parity_chip(d_out.reshape(N_DEV, GLA_DV // N_DEV, D_MODEL))}
    small = {"gla_w_gate_up": dwgu[:GLA_GATE_RANK], "gla_gate_bias": dbias[0], "gla_norm_g": dng[0]}
    return dx, big, small


def _dil_layer_fwd(l, hb, lw):
    proj = _matmul(hb, lw["w_in"], tm=1024, tn=1024, tk=D_MODEL, name=f"l{l}_dil_proj")
    os_, lses = [], []
    for gi in range(len(DIL_PATTERNS)):
        o, lse = _dil_fwd(proj, gi, name=f"l{l}_dil_attn{gi}")
        os_.append(o)
        lses.append(lse)
    omix = _dil_mix_fwd(os_, lses, name=f"l{l}_dil_mix")
    mix = _matmul(omix, lw["w_out"], tm=1024, tn=1024, tk=DIL_WIDTH, name=f"l{l}_dil_out")
    return mix, (proj, os_, lses, omix)


def _dil_layer_bwd(l, hb, dz, dzb, lw, saved):
    proj, os_, lses, omix = saved
    dout = _matmul(dzb, lw["w_out"], tb=True, tm=1024, tn=1024, tk=D_MODEL, name=f"l{l}_dil_dy")
    d_out = _matmul(omix, dzb, ta=True, tm=1024, tn=1024, tk=2048, name=f"l{l}_dil_dwout")
    dos, dls = _dil_mix_bwd(os_, lses, dout, name=f"l{l}_dil_mix_bwd")
    parts = []
    for gi in range(len(DIL_PATTERNS)):
        parts += list(_dil_bwd(proj, gi, lses[gi], dos[gi], dls[gi], name=f"l{l}_dil_attn_bwd{gi}"))
    dproj = jnp.concatenate(parts, axis=-1)
    d_in = _matmul(hb, dproj, ta=True, tm=1024, tn=1024, tk=2048, name=f"l{l}_dil_dwin")
    dx = _matmul(dproj, lw["w_in"], tb=True, tm=1024, tn=1024, tk=2304, res=dz, res_scale=DEEPNORM_ALPHA, name=f"l{l}_dil_dx")
    big = {"dil_w_in": _by_parity_chip(_col_blocks(d_in, DIL_IN // N_DEV)),
           "dil_w_out": _by_parity_chip(_col_blocks(d_out, D_MODEL // N_DEV))}
    return dx, big, {}


def _pack_rows(arrays, rows):
    flat = [a.reshape(-1) for a in arrays]
    used = sum(f.shape[0] for f in flat)
    return jnp.concatenate(flat + [jnp.zeros((rows * SMALL_COLS - used,), F32)]).reshape(rows, SMALL_COLS)


def _unpack_rows(packed, shapes):
    flat, out, off = packed.reshape(-1), [], 0
    for s in shapes:
        n = math.prod(s)
        out.append(flat[off:off + n].reshape(s))
        off += n
    return out


def _rows_for(shapes):
    n = sum(math.prod(s) for s in shapes)
    return -(-n // (SMALL_COLS * SUBLANES)) * SUBLANES


def kernel(x, gla_w_in, gla_w_gate_up, gla_gate_bias, gla_norm_g, gla_w_out, dil_w_in, dil_w_out, ffn_w_up, ffn_conv_w, ffn_conv_b, ffn_w_down, ln_g, ln_b, loss_target, m_gla_w_in, m_gla_w_gate_up, m_gla_gate_bias, m_gla_norm_g, m_gla_w_out, m_dil_w_in, m_dil_w_out, m_ffn_w_up, m_ffn_conv_w, m_ffn_conv_b, m_ffn_w_down, m_ln_g, m_ln_b, v_gla_w_in, v_gla_w_gate_up, v_gla_gate_bias, v_gla_norm_g, v_gla_w_out, v_dil_w_in, v_dil_w_out, v_ffn_w_up, v_ffn_conv_w, v_ffn_conv_b, v_ffn_w_down, v_ln_g, v_ln_b):
    w = dict(zip(WEIGHT_NAMES, (gla_w_in, gla_w_gate_up, gla_gate_bias, gla_norm_g, gla_w_out, dil_w_in, dil_w_out,
                                ffn_w_up, ffn_conv_w, ffn_conv_b, ffn_w_down, ln_g, ln_b)))
    mom = dict(zip(WEIGHT_NAMES, (m_gla_w_in, m_gla_w_gate_up, m_gla_gate_bias, m_gla_norm_g, m_gla_w_out, m_dil_w_in,
                                  m_dil_w_out, m_ffn_w_up, m_ffn_conv_w, m_ffn_conv_b, m_ffn_w_down, m_ln_g, m_ln_b)))
    var = dict(zip(WEIGHT_NAMES, (v_gla_w_in, v_gla_w_gate_up, v_gla_gate_bias, v_gla_norm_g, v_gla_w_out, v_dil_w_in,
                                  v_dil_w_out, v_ffn_w_up, v_ffn_conv_w, v_ffn_conv_b, v_ffn_w_down, v_ln_g, v_ln_b)))
    xi, yi, ci = _position()
    dev = 4 * xi + 2 * yi + ci
    parity = ci.astype(jnp.int32).reshape(1)
    chip = (2 * xi + yi).astype(jnp.int32).reshape(1)
    t = x.shape[0] * x.shape[1]
    h = x.reshape(t, D_MODEL)
    hb = h.astype(BF16)
    target = loss_target.reshape(t, D_MODEL)

    ag_rows = max(-(-sum(_piece_len(a, bits) for _, a, bits in _layer_pieces(l, w)) // (AG_COLS * 16)) * 16
                  for l in range(DEPTH))
    lws = [_layer_weights(l, _gather_layer(l, w, ag_rows), w) for l in range(DEPTH)]

    saved = []
    for l in range(DEPTH):
        lw = lws[l]
        mix, mixer_saved = (_gla_layer_fwd if l % 2 == 0 else _dil_layer_fwd)(l, hb, lw)
        y1, y1b, xh1, rs1 = _ln_fwd(h, mix, lw["ln_g"][0], lw["ln_b"][0], name=f"l{l}_ln1")
        ffn, ffn_saved = _ffn_fwd(l, y1b, lw)
        y2, y2b, xh2, rs2 = _ln_fwd(y1, ffn, lw["ln_g"][1], lw["ln_b"][1], name=f"l{l}_ln2")
        saved.append((hb, mixer_saved, y1b, xh1, rs1, ffn_saved, xh2, rs2))
        h, hb = y2, y2b
    loss_local, dy = _loss_fwd_bwd(h, target, name="loss")
    loss = lax.psum(loss_local[0, 0], ("x", "y", "c"))

    big_names = tuple(ADAM_TILES)
    results = {n: [lax.empty(w[n].shape, F32) for _ in range(4)] for n in big_names}
    small_grads = {n: [None] * w[n].shape[0] for n in WEIGHT_NAMES if n not in big_names}
    for l in reversed(range(DEPTH)):
        lw = lws[l]
        hb_in, mixer_saved, y1b, xh1, rs1, ffn_saved, xh2, rs2 = saved[l]
        j = l // 2
        dz2, dz2b, dg2, db2 = _ln_bwd(dy, xh2, rs2, lw["ln_g"][1], name=f"l{l}_ln2_bwd")
        dy1, big_ffn, small_ffn = _ffn_bwd(l, y1b, dz2, dz2b, lw, ffn_saved)
        dz1, dz1b, dg1, db1 = _ln_bwd(dy1, xh1, rs1, lw["ln_g"][0], name=f"l{l}_ln1_bwd")
        dy, big_mix, small_mix = (_gla_layer_bwd if l % 2 == 0 else _dil_layer_bwd)(l, hb_in, dz1, dz1b, lw, mixer_saved)
        small_grads["ln_g"][l] = jnp.concatenate([dg1, dg2], axis=0)
        small_grads["ln_b"][l] = jnp.concatenate([db1, db2], axis=0)
        for n, g in small_ffn.items():
            small_grads[n][l] = g
        for n, g in small_mix.items():
            small_grads[n][j] = g
        layer_of = {n: (l if n.startswith("ffn") else j) for n in (*big_ffn, *big_mix)}
        big = {**big_mix, **big_ffn}
        names = list(big)
        from_sibling = _exchange_sibling([big[n] for n in names], name=f"reduce_sibling_l{l}")
        partial = [_add_sibling(big[n], r1, parity, tr=ADAM_TILES[n][0], tc=ADAM_TILES[n][1], name=f"l{l}_{n}_add")
                   for n, r1 in zip(names, from_sibling)]
        from_chips = _exchange_chips(partial, name=f"reduce_chips_l{l}")
        for n, p, r2 in zip(names, partial, from_chips):
            results[n] = _adamw_big(p, r2, chip, w[n], mom[n], var[n], results[n], layer_of[n],
                                    tr=ADAM_TILES[n][0], tc=ADAM_TILES[n][1], name=f"l{l}_{n}_adamw")
    grad_x = dy.reshape(x.shape)

    small_names = [n for n in WEIGHT_NAMES if n not in big_names]
    full_shapes = {"gla_w_gate_up": (2, GLA_GATE_RANK, GLA_DK), "gla_gate_bias": (2, GLA_DK), "gla_norm_g": (2, GLA_HEAD_V),
                   "ffn_conv_w": (DEPTH, 3, 2 * D_FF), "ffn_conv_b": (DEPTH, 2 * D_FF),
                   "ln_g": (DEPTH, 2, D_MODEL), "ln_b": (DEPTH, 2, D_MODEL)}
    shapes = [full_shapes[n] for n in small_names]
    rows = _rows_for(shapes)
    packed = _pack_rows([jnp.stack(small_grads[n]) for n in small_names], rows)
    summed = _sum_gathered(_all_gather(packed, name="gather_small_grads"), name="sum_small_grads")
    full = dict(zip(small_names, _unpack_rows(summed, shapes)))
    own = {n: (full[n] if w[n].shape == full[n].shape
               else lax.dynamic_slice_in_dim(full[n], dev * w[n].shape[-1], w[n].shape[-1], axis=full[n].ndim - 1))
           for n in small_names}
    own_shapes = [w[n].shape for n in small_names]
    rows = _rows_for(own_shapes)
    pk = lambda d: _pack_rows([d[n] for n in small_names], rows)
    outs = _adamw_small(pk(own), pk(w), pk(mom), pk(var), name="adamw_small")
    for n in small_names:
        results[n] = [own[n]]
    for k, packed_out in enumerate(outs):
        for n, a in zip(small_names, _unpack_rows(packed_out, own_shapes)):
            results[n].append(a)

    return (loss, grad_x) + tuple(results[n][k] for k in range(4) for n in WEIGHT_NAMES)
```

```python
import functools
import math

import jax
import jax.numpy as jnp
from jax import lax
from jax.experimental import pallas as pl
from jax.experimental.pallas import tpu as pltpu

F32 = jnp.float32
BF16 = jnp.bfloat16

D_MODEL = 2048
SEQ = 2048
DEPTH = 4
N_DEV = 8
GLA_HEADS = 4
GLA_DK = 1024
GLA_DV = 2048
GLA_HEAD_K = 256
GLA_HEAD_V = 512
GLA_GATE_RANK = 16
GLA_GATE_NORMALIZER = 16.0
GLA_CHUNK = 64
GLA_MAIN = 2 * GLA_DK + 2 * GLA_DV
GLA_IN = GLA_MAIN + GLA_GATE_RANK
DIL_PATTERNS = ((128, 1), (512, 4), (2048, 16))
DIL_HEADS = 8
DIL_HEAD_DIM = 128
DIL_WIDTH = DIL_HEADS * DIL_HEAD_DIM
DIL_BLOCK = 128
DIL_IN = 3 * len(DIL_PATTERNS) * DIL_WIDTH
D_FF = 5504
DEEPNORM_ALPHA = (2 * DEPTH) ** 0.25
LN_EPS = 1e-5
RMS_EPS = 1e-6
ADAM_LR = 0.001
ADAM_B1 = 0.9
ADAM_B2 = 0.999
ADAM_EPS = 1e-08
ADAM_WD = 0.01
ADAM_STEP = 10

LANES = 128
SUBLANES = 8
VMEM_LIMIT_BYTES = 56 * 1024 * 1024

FF_SHARD = 2 * D_FF // N_DEV
FF_SHARD_PAD = 1408
FF_HALF_PAD = 4 * FF_SHARD_PAD
GLOW_PAD = LANES


def _cparams(dims=None):
    return pltpu.CompilerParams(dimension_semantics=dims, vmem_limit_bytes=VMEM_LIMIT_BYTES)


def _mm(a, b, *, grid, a_spec, b_spec, o_spec, out_shape, acc_shape, ta=False, tb=False, res=None, res_scale=1.0, name):
    nk = grid[2]
    dims = (((0 if ta else 1,), (1 if tb else 0,)), ((), ()))
    has_res = res is not None

    def body(*refs):
        a_ref, b_ref = refs[0], refs[1]
        res_ref = refs[2] if has_res else None
        o_ref = refs[3] if has_res else refs[2]
        acc_ref = refs[-1] if nk > 1 else None
        p = lax.dot_general(a_ref[...], b_ref[...], dims, preferred_element_type=F32)

        def finish(acc):
            if has_res:
                acc = acc + res_scale * res_ref[...]
            o_ref[...] = acc.astype(o_ref.dtype)

        if nk == 1:
            finish(p)
        else:
            kk = pl.program_id(2)

            @pl.when(kk == 0)
            def _():
                acc_ref[...] = p

            @pl.when(kk > 0)
            def _():
                acc_ref[...] += p

            @pl.when(kk == nk - 1)
            def _():
                finish(acc_ref[...])

    in_specs = [a_spec, b_spec] + ([o_spec] if has_res else [])
    args = (a, b) + ((res,) if has_res else ())
    return pl.pallas_call(
        body,
        name=name,
        grid=grid,
        in_specs=in_specs,
        out_specs=o_spec,
        out_shape=out_shape,
        scratch_shapes=[pltpu.VMEM(acc_shape, F32)] if nk > 1 else [],
        compiler_params=_cparams(("parallel", "parallel", "arbitrary")),
    )(*args)


def _matmul(a, b, *, ta=False, tb=False, tm, tn, tk, out_dtype=F32, res=None, res_scale=1.0, name):
    m, k = (a.shape[1], a.shape[0]) if ta else a.shape
    n = b.shape[0] if tb else b.shape[1]
    assert (b.shape[1] if tb else b.shape[0]) == k
    assert m % tm == 0 and n % tn == 0 and k % tk == 0, (m, n, k, tm, tn, tk)
    a_spec = pl.BlockSpec((tk, tm), lambda i, j, kk: (kk, i)) if ta else pl.BlockSpec((tm, tk), lambda i, j, kk: (i, kk))
    b_spec = pl.BlockSpec((tn, tk), lambda i, j, kk: (j, kk)) if tb else pl.BlockSpec((tk, tn), lambda i, j, kk: (kk, j))
    return _mm(a, b, grid=(m // tm, n // tn, k // tk), a_spec=a_spec, b_spec=b_spec,
               o_spec=pl.BlockSpec((tm, tn), lambda i, j, kk: (i, j)), out_shape=jax.ShapeDtypeStruct((m, n), out_dtype),
               acc_shape=(tm, tn), ta=ta, tb=tb, res=res, res_scale=res_scale, name=name)


MM_ROWS = 1024


def _mm_colblocks(a, wb, *, name):
    m, k = a.shape
    nb, _, w = wb.shape
    return _mm(a, wb, grid=(m // MM_ROWS, nb, 1),
               a_spec=pl.BlockSpec((MM_ROWS, k), lambda i, j, kk: (i, 0)),
               b_spec=pl.BlockSpec((None, k, w), lambda i, j, kk: (j, 0, 0)),
               o_spec=pl.BlockSpec((MM_ROWS, w), lambda i, j, kk: (i, j)),
               out_shape=jax.ShapeDtypeStruct((m, nb * w), F32), acc_shape=(MM_ROWS, w), name=name)


def _mm_colblocks_t(a, wb, res, res_scale, *, name):
    m = a.shape[0]
    nb, n, w = wb.shape
    tn = 1024
    return _mm(a, wb, grid=(m // MM_ROWS, n // tn, nb),
               a_spec=pl.BlockSpec((MM_ROWS, w), lambda i, j, kk: (i, kk)),
               b_spec=pl.BlockSpec((None, tn, w), lambda i, j, kk: (kk, j, 0)),
               o_spec=pl.BlockSpec((MM_ROWS, tn), lambda i, j, kk: (i, j)),
               out_shape=jax.ShapeDtypeStruct((m, n), F32), acc_shape=(MM_ROWS, tn), tb=True,
               res=res, res_scale=res_scale, name=name)


def _mm_grad_colblocks(x, dy, w, *, name):
    t, k = x.shape
    nb = dy.shape[1] // w
    tm, tk = 1024, 2048
    return _mm(x, dy, grid=(k // tm, nb, t // tk),
               a_spec=pl.BlockSpec((tk, tm), lambda i, j, kk: (kk, i)),
               b_spec=pl.BlockSpec((tk, w), lambda i, j, kk: (kk, j)),
               o_spec=pl.BlockSpec((None, tm, w), lambda i, j, kk: (j, i, 0)),
               out_shape=jax.ShapeDtypeStruct((nb, k, w), F32), acc_shape=(tm, w), ta=True, name=name)


def _ffn_hidden(y, wt, *, name):
    t, k = y.shape
    ni = t // MM_ROWS
    return _mm(y, wt, grid=(ni, N_DEV, 1),
               a_spec=pl.BlockSpec((MM_ROWS, k), lambda i, j, kk: (i, 0)),
               b_spec=pl.BlockSpec((None, FF_SHARD_PAD, k), lambda i, j, kk: (j, 0, 0)),
               o_spec=pl.BlockSpec((MM_ROWS, FF_SHARD_PAD), lambda i, j, kk: ((j // 4) * ni + i, j % 4)),
               out_shape=jax.ShapeDtypeStruct((2 * t, FF_HALF_PAD), F32), acc_shape=(MM_ROWS, FF_SHARD_PAD), tb=True,
               name=name)


def _ffn_hidden_dy(dh, wt, res, res_scale, *, name):
    t = dh.shape[0] // 2
    ni, tn = t // MM_ROWS, 1024
    return _mm(dh, wt, grid=(ni, D_MODEL // tn, N_DEV),
               a_spec=pl.BlockSpec((MM_ROWS, FF_SHARD_PAD), lambda i, j, kk: ((kk // 4) * ni + i, kk % 4)),
               b_spec=pl.BlockSpec((None, FF_SHARD_PAD, tn), lambda i, j, kk: (kk, 0, j)),
               o_spec=pl.BlockSpec((MM_ROWS, tn), lambda i, j, kk: (i, j)),
               out_shape=jax.ShapeDtypeStruct((t, D_MODEL), F32), acc_shape=(MM_ROWS, tn),
               res=res, res_scale=res_scale, name=name)


def _ffn_hidden_dw(dh, y, *, name):
    t, k = y.shape
    tk, tn = 2048, 1024
    nk = t // tk
    return _mm(dh, y, grid=(N_DEV, k // tn, nk),
               a_spec=pl.BlockSpec((tk, FF_SHARD_PAD), lambda i, j, kk: ((i // 4) * nk + kk, i % 4)),
               b_spec=pl.BlockSpec((tk, tn), lambda i, j, kk: (kk, j)),
               o_spec=pl.BlockSpec((None, FF_SHARD_PAD, tn), lambda i, j, kk: (i, 0, j)),
               out_shape=jax.ShapeDtypeStruct((N_DEV, FF_SHARD_PAD, k), F32), acc_shape=(FF_SHARD_PAD, tn), ta=True,
               name=name)


def _ffn_down_dw(act, dz, *, name):
    t, k = dz.shape
    tk, tn = 2048, 1024
    return _mm(act, dz, grid=(4, k // tn, t // tk),
               a_spec=pl.BlockSpec((tk, FF_SHARD_PAD), lambda i, j, kk: (kk, i)),
               b_spec=pl.BlockSpec((tk, tn), lambda i, j, kk: (kk, j)),
               o_spec=pl.BlockSpec((None, FF_SHARD_PAD, tn), lambda i, j, kk: (i, 0, j)),
               out_shape=jax.ShapeDtypeStruct((4, FF_SHARD_PAD, k), F32), acc_shape=(FF_SHARD_PAD, tn), ta=True,
               name=name)


LN_ROWS = 256


def _ln_fwd(x, f, g, b, *, name):
    t, d = x.shape

    def body(x_ref, f_ref, g_ref, b_ref, y_ref, yb_ref, xh_ref, rs_ref):
        z = DEEPNORM_ALPHA * x_ref[...] + f_ref[...]
        mu = jnp.mean(z, axis=-1, keepdims=True)
        zc = z - mu
        var = jnp.mean(zc * zc, axis=-1, keepdims=True)
        rstd = lax.rsqrt(var + LN_EPS)
        xh = zc * rstd
        y = xh * g_ref[...] + b_ref[...]
        y_ref[...] = y
        yb_ref[...] = y.astype(BF16)
        xh_ref[...] = xh
        rs_ref[...] = rstd

    row = pl.BlockSpec((LN_ROWS, d), lambda i: (i, 0))
    vec = pl.BlockSpec((1, d), lambda i: (0, 0))
    return pl.pallas_call(
        body,
        name=name,
        grid=(t // LN_ROWS,),
        in_specs=[row, row, vec, vec],
        out_specs=[row, row, row, pl.BlockSpec((LN_ROWS, 1), lambda i: (i, 0))],
        out_shape=[jax.ShapeDtypeStruct((t, d), F32), jax.ShapeDtypeStruct((t, d), BF16),
                   jax.ShapeDtypeStruct((t, d), F32), jax.ShapeDtypeStruct((t, 1), F32)],
        compiler_params=_cparams(("parallel",)),
    )(x, f, g, b)


def _ln_bwd(dy, xhat, rstd, g, *, name):
    t, d = dy.shape

    def body(dy_ref, xh_ref, rs_ref, g_ref, dz_ref, dzb_ref, dg_ref, db_ref):
        dyv = dy_ref[...]
        xh = xh_ref[...]
        dyg = dyv * g_ref[...]
        m1 = jnp.mean(dyg, axis=-1, keepdims=True)
        m2 = jnp.mean(dyg * xh, axis=-1, keepdims=True)
        dz = rs_ref[...] * (dyg - m1 - xh * m2)
        dz_ref[...] = dz
        dzb_ref[...] = dz.astype(BF16)
        dg_part = jnp.sum(dyv * xh, axis=0, keepdims=True)
        db_part = jnp.sum(dyv, axis=0, keepdims=True)

        @pl.when(pl.program_id(0) == 0)
        def _():
            dg_ref[...] = dg_part
            db_ref[...] = db_part

        @pl.when(pl.program_id(0) > 0)
        def _():
            dg_ref[...] += dg_part
            db_ref[...] += db_part

    row = pl.BlockSpec((LN_ROWS, d), lambda i: (i, 0))
    vec = pl.BlockSpec((1, d), lambda i: (0, 0))
    return pl.pallas_call(
        body,
        name=name,
        grid=(t // LN_ROWS,),
        in_specs=[row, row, pl.BlockSpec((LN_ROWS, 1), lambda i: (i, 0)), vec],
        out_specs=[row, row, vec, vec],
        out_shape=[jax.ShapeDtypeStruct((t, d), F32), jax.ShapeDtypeStruct((t, d), BF16),
                   jax.ShapeDtypeStruct((1, d), F32), jax.ShapeDtypeStruct((1, d), F32)],
        compiler_params=_cparams(("arbitrary",)),
    )(dy, xhat, rstd, g)


def _loss_fwd_bwd(y, target, *, name):
    t, d = y.shape

    def body(y_ref, t_ref, loss_ref, dy_ref):
        err = y_ref[...] - t_ref[...]
        dy_ref[...] = err * (1.0 / d)
        part = 0.5 * jnp.sum(jnp.mean(err * err, axis=-1, keepdims=True), axis=0, keepdims=True)

        @pl.when(pl.program_id(0) == 0)
        def _():
            loss_ref[...] = part

        @pl.when(pl.program_id(0) > 0)
        def _():
            loss_ref[...] += part

    row = pl.BlockSpec((LN_ROWS, d), lambda i: (i, 0))
    return pl.pallas_call(
        body,
        name=name,
        grid=(t // LN_ROWS,),
        in_specs=[row, row],
        out_specs=[pl.BlockSpec((1, 1), lambda i: (0, 0)), row],
        out_shape=[jax.ShapeDtypeStruct((1, 1), F32), jax.ShapeDtypeStruct((t, d), F32)],
        compiler_params=_cparams(("arbitrary",)),
    )(y, target)


FFN_COLS = 256


def _shift_rows(h, s):
    rows = lax.broadcasted_iota(jnp.int32, h.shape, 0)
    return jnp.where(rows >= s, pltpu.roll(h, s, 0), 0.0)


def _shift_rows_up(h, s):
    n = h.shape[0]
    rows = lax.broadcasted_iota(jnp.int32, h.shape, 0)
    return jnp.where(rows < n - s, pltpu.roll(h, n - s, 0), 0.0)


def _causal_conv(h, w, b):
    return w[0:1, :] * _shift_rows(h, 2) + w[1:2, :] * _shift_rows(h, 1) + w[2:3, :] * h + b


def _sigmoid(x):
    return 1.0 / (1.0 + jnp.exp(-x))


def _convgate_fwd(h, cw, cb, *, name):
    t, n = h.shape[0] // 2, h.shape[1]
    nb = t // SEQ

    def body(hg_ref, hu_ref, wg_ref, wu_ref, bg_ref, bu_ref, a_ref):
        gate = _causal_conv(hg_ref[...], wg_ref[...], bg_ref[...])
        up = _causal_conv(hu_ref[...], wu_ref[...], bu_ref[...])
        a_ref[...] = (gate * _sigmoid(gate) * up).astype(BF16)

    def half(rows, k):
        return pl.BlockSpec((None, rows, FFN_COLS), lambda s, j: (k, 0, j))

    return pl.pallas_call(
        body,
        name=name,
        grid=(nb, n // FFN_COLS),
        in_specs=[pl.BlockSpec((SEQ, FFN_COLS), lambda s, j: (s, j)), pl.BlockSpec((SEQ, FFN_COLS), lambda s, j: (nb + s, j)),
                  half(3, 0), half(3, 1), half(1, 0), half(1, 1)],
        out_specs=pl.BlockSpec((SEQ, FFN_COLS), lambda s, j: (s, j)),
        out_shape=jax.ShapeDtypeStruct((t, n), BF16),
        compiler_params=_cparams(("parallel", "parallel")),
    )(h, h, cw, cw, cb, cb)


def _convgate_bwd(h, dact, cw, cb, *, name):
    t, n = h.shape[0] // 2, h.shape[1]
    nb = t // SEQ

    def body(hg_ref, hu_ref, da_ref, wg_ref, wu_ref, bg_ref, bu_ref, dh_ref, dw_ref, db_ref):
        dhg_ref, dhu_ref = dh_ref.at[0], dh_ref.at[1]
        dwg_ref, dwu_ref = dw_ref.at[0], dw_ref.at[1]
        dbg_ref, dbu_ref = db_ref.at[0], db_ref.at[1]
        hgv, huv = hg_ref[...], hu_ref[...]
        wgv, wuv = wg_ref[...], wu_ref[...]
        gate = _causal_conv(hgv, wgv, bg_ref[...])
        up = _causal_conv(huv, wuv, bu_ref[...])
        sg = _sigmoid(gate)
        da = da_ref[...]
        dgate = da * up * (sg * (1.0 + gate * (1.0 - sg)))
        dup = da * (gate * sg)

        def conv_bwd(dc, h, w, dh_ref, dw_ref, db_ref):
            dh = w[2:3, :] * dc + w[1:2, :] * _shift_rows_up(dc, 1) + w[0:1, :] * _shift_rows_up(dc, 2)
            dh_ref[...] = dh.astype(BF16)
            dws = [jnp.sum(dc * _shift_rows(h, 2), axis=0, keepdims=True),
                   jnp.sum(dc * _shift_rows(h, 1), axis=0, keepdims=True),
                   jnp.sum(dc * h, axis=0, keepdims=True)]
            db = jnp.sum(dc, axis=0, keepdims=True)

            @pl.when(pl.program_id(1) == 0)
            def _():
                for r in range(3):
                    dw_ref[r:r + 1, :] = dws[r]
                db_ref[...] = db

            @pl.when(pl.program_id(1) > 0)
            def _():
                for r in range(3):
                    dw_ref[r:r + 1, :] += dws[r]
                db_ref[...] += db

        conv_bwd(dgate, hgv, wgv, dhg_ref, dwg_ref, dbg_ref)
        conv_bwd(dup, huv, wuv, dhu_ref, dwu_ref, dbu_ref)

    def half(rows, k):
        return pl.BlockSpec((None, rows, FFN_COLS), lambda j, s: (k, 0, j))

    def both(rows):
        return pl.BlockSpec((2, rows, FFN_COLS), lambda j, s: (0, 0, j))

    return pl.pallas_call(
        body,
        name=name,
        grid=(n // FFN_COLS, nb),
        in_specs=[pl.BlockSpec((SEQ, FFN_COLS), lambda j, s: (s, j)), pl.BlockSpec((SEQ, FFN_COLS), lambda j, s: (nb + s, j)),
                  pl.BlockSpec((SEQ, FFN_COLS), lambda j, s: (s, j)), half(3, 0), half(3, 1), half(1, 0), half(1, 1)],
        out_specs=[pl.BlockSpec((2, SEQ, FFN_COLS), lambda j, s: (0, s, j)), both(3), both(1)],
        out_shape=[jax.ShapeDtypeStruct((2, t, n), BF16), jax.ShapeDtypeStruct((2, 3, n), F32),
                   jax.ShapeDtypeStruct((2, 1, n), F32)],
        compiler_params=_cparams(("parallel", "arbitrary")),
    )(h, h, dact, cw, cw, cb, cb)


GLA_Q_SCALE = GLA_HEAD_K ** -0.5
GLA_NC = SEQ // GLA_CHUNK
_NT = (((1,), (1,)), ((), ()))
_TN = (((0,), (0,)), ((), ()))


def _cumsum_rows(g):
    n = g.shape[0]
    rows = lax.broadcasted_iota(jnp.int32, g.shape, 0)
    s = 1
    while s < n:
        g = g + jnp.where(rows >= s, pltpu.roll(g, s, 0), 0.0)
        s *= 2
    return g


def _suffix_sum_rows(x):
    n = x.shape[0]
    rows = lax.broadcasted_iota(jnp.int32, x.shape, 0)
    s = 1
    while s < n:
        x = x + jnp.where(rows < n - s, pltpu.roll(x, n - s, 0), 0.0)
        s *= 2
    return x


def _gla_log_gate(gl_ref, wgu_ref, bias_ref):
    pre = jnp.dot(gl_ref[...].astype(BF16), wgu_ref[...], preferred_element_type=F32) + bias_ref[...]
    log_sig = jnp.minimum(pre, 0.0) - jnp.log(1.0 + jnp.exp(-jnp.abs(pre)))
    return pre, log_sig * (1.0 / GLA_GATE_NORMALIZER)


def _pair_rows(j):
    return (j // SUBLANES) * SUBLANES


def _gla_pair_fwd(q_ref, k_ref, b_scr, a_scr, h):
    c = GLA_CHUNK
    kc = pl.ds(h * GLA_HEAD_K, GLA_HEAD_K)
    a_scr[...] = jnp.zeros(a_scr.shape, F32)
    lane = lax.broadcasted_iota(jnp.int32, (1, c), 1)
    for j in range(c):
        r0 = _pair_rows(j)
        rs = pl.ds(r0, c - r0)
        rows = lax.broadcasted_iota(jnp.int32, (c - r0, 1), 0) + r0
        e = jnp.exp(jnp.minimum(b_scr[rs, kc] - b_scr[pl.ds(j, 1), kc], 0.0))
        w = (q_ref[rs, kc] * GLA_Q_SCALE) * k_ref[pl.ds(j, 1), kc] * e
        col = jnp.where(rows >= j, jnp.sum(w, axis=-1, keepdims=True), 0.0)
        a_scr[rs, :] += col * (lane == j).astype(F32)


def _gla_fwd(proj, wgu, bias, ng, *, name):
    t = proj.shape[0]
    nb, nc, c = t // SEQ, GLA_NC, GLA_CHUNK

    def body(q_ref, k_ref, v_ref, r_ref, gl_ref, wgu_ref, bias_ref, ng_ref,
             y_ref, o_ref, a_ref, st_ref, state, b_scr, a_scr):
        @pl.when(pl.program_id(1) == 0)
        def _():
            state[...] = jnp.zeros(state.shape, F32)

        _, g = _gla_log_gate(gl_ref, wgu_ref, bias_ref)
        b_scr[...] = _cumsum_rows(g)
        for h in range(GLA_HEADS):
            kc = pl.ds(h * GLA_HEAD_K, GLA_HEAD_K)
            vc = pl.ds(h * GLA_HEAD_V, GLA_HEAD_V)
            qh = q_ref[:, kc] * GLA_Q_SCALE
            kh = k_ref[:, kc]
            vh = v_ref[:, vc].astype(BF16)
            bh = b_scr[:, kc]
            blast = b_scr[pl.ds(c - 1, 1), kc]
            st = state[h]
            st_ref[h] = st
            o_inter = lax.dot_general((qh * jnp.exp(bh)).astype(BF16), st.astype(BF16), _NT, preferred_element_type=F32)
            _gla_pair_fwd(q_ref, k_ref, b_scr, a_scr, h)
            a = a_scr[...]
            a_ref[h] = a
            o = o_inter + jnp.dot(a.astype(BF16), vh, preferred_element_type=F32)
            kd = (kh * jnp.exp(blast - bh)).astype(BF16)
            state[h] = st * jnp.exp(blast) + lax.dot_general(vh, kd, _TN, preferred_element_type=F32)
            o_ref[:, vc] = o
            rs = lax.rsqrt(jnp.mean(o * o, axis=-1, keepdims=True) + RMS_EPS)
            rh = r_ref[:, vc]
            y_ref[:, vc] = ((o * rs * ng_ref[...]) * (rh * _sigmoid(rh))).astype(BF16)

    def tok(width, col):
        return pl.BlockSpec((c, width), lambda b, i: (b * nc + i, col))

    whole = lambda shape: pl.BlockSpec(shape, lambda b, i: (0,) * len(shape))
    return pl.pallas_call(
        body,
        name=name,
        grid=(nb, nc),
        in_specs=[tok(GLA_DK, 0), tok(GLA_DK, 1), tok(GLA_DV, 1), tok(GLA_DV, 2), tok(GLOW_PAD, GLA_MAIN // GLOW_PAD),
                  whole((GLOW_PAD, GLA_DK)), whole((1, GLA_DK)), whole((1, GLA_HEAD_V))],
        out_specs=[tok(GLA_DV, 0), tok(GLA_DV, 0),
                   pl.BlockSpec((GLA_HEADS, c, c), lambda b, i: (0, b * nc + i, 0)),
                   pl.BlockSpec((None, GLA_HEADS, GLA_HEAD_V, GLA_HEAD_K), lambda b, i: (b * nc + i, 0, 0, 0))],
        out_shape=[jax.ShapeDtypeStruct((t, GLA_DV), BF16), jax.ShapeDtypeStruct((t, GLA_DV), F32),
                   jax.ShapeDtypeStruct((GLA_HEADS, t, c), F32),
                   jax.ShapeDtypeStruct((t // c, GLA_HEADS, GLA_HEAD_V, GLA_HEAD_K), F32)],
        scratch_shapes=[pltpu.VMEM((GLA_HEADS, GLA_HEAD_V, GLA_HEAD_K), F32), pltpu.VMEM((c, GLA_DK), F32),
                        pltpu.VMEM((c, c), F32)],
        compiler_params=_cparams(("parallel", "arbitrary")),
    )(proj, proj, proj, proj, proj, wgu, bias, ng)


def _gla_pair_bwd(q_ref, k_ref, b_scr, da_scr, dq_scr, dk_scr, h):
    c = GLA_CHUNK
    kc = pl.ds(h * GLA_HEAD_K, GLA_HEAD_K)
    lane = lax.broadcasted_iota(jnp.int32, (1, c), 1)
    for j in range(c):
        r0 = _pair_rows(j)
        rs = pl.ds(r0, c - r0)
        rows = lax.broadcasted_iota(jnp.int32, (c - r0, 1), 0) + r0
        e = jnp.exp(jnp.minimum(b_scr[rs, kc] - b_scr[pl.ds(j, 1), kc], 0.0))
        dacol = jnp.sum(jnp.where(lane == j, da_scr[rs, :], 0.0), axis=-1, keepdims=True)
        t1 = jnp.where(rows >= j, dacol, 0.0) * e
        dq_scr[rs, kc] += t1 * k_ref[pl.ds(j, 1), kc]
        dk_scr[pl.ds(j, 1), kc] += jnp.sum(t1 * (q_ref[rs, kc] * GLA_Q_SCALE), axis=0, keepdims=True)


def _gla_bwd(proj, wgu, bias, ng, o, a, states, dy, *, name):
    t = proj.shape[0]
    nb, nc, c = t // SEQ, GLA_NC, GLA_CHUNK

    def body(q_ref, k_ref, v_ref, r_ref, gl_ref, wgu_ref, bias_ref, ng_ref, o_ref, a_ref, stp_ref, stn_ref, dy_ref,
             dq_ref, dk_ref, dv_ref, dr_ref, dgl_ref, dwgu_ref, dbias_ref, dng_ref,
             dstate, b_scr, da_scr, dq_scr, dk_scr, dg_scr):
        first = jnp.logical_and(pl.program_id(0) == 0, pl.program_id(1) == 0)

        @pl.when(first)
        def _():
            dwgu_ref[...] = jnp.zeros(dwgu_ref.shape, F32)
            dbias_ref[...] = jnp.zeros(dbias_ref.shape, F32)
            dng_ref[...] = jnp.zeros(dng_ref.shape, F32)

        @pl.when(pl.program_id(1) == 0)
        def _():
            dstate[...] = jnp.zeros(dstate.shape, F32)

        pre, g = _gla_log_gate(gl_ref, wgu_ref, bias_ref)
        b_scr[...] = _cumsum_rows(g)
        ngv = ng_ref[...]
        tri = lax.broadcasted_iota(jnp.int32, (c, c), 0) >= lax.broadcasted_iota(jnp.int32, (c, c), 1)
        for h in range(GLA_HEADS):
            kc = pl.ds(h * GLA_HEAD_K, GLA_HEAD_K)
            vc = pl.ds(h * GLA_HEAD_V, GLA_HEAD_V)
            oh = o_ref[:, vc]
            rh = r_ref[:, vc]
            dyh = dy_ref[:, vc]
            rs = lax.rsqrt(jnp.mean(oh * oh, axis=-1, keepdims=True) + RMS_EPS)
            u = oh * rs
            sg = _sigmoid(rh)
            sr = rh * sg
            dr_ref[:, vc] = (dyh * (u * ngv) * (sg * (1.0 + rh * (1.0 - sg)))).astype(BF16)
            dng_ref[...] += jnp.sum(dyh * sr * u, axis=0, keepdims=True)
            du = dyh * sr * ngv
            do = (rs * (du - u * jnp.mean(du * u, axis=-1, keepdims=True))).astype(BF16)
            qh = q_ref[:, kc] * GLA_Q_SCALE
            kh = k_ref[:, kc]
            vh = v_ref[:, vc].astype(BF16)
            bh = b_scr[:, kc]
            blast = b_scr[pl.ds(c - 1, 1), kc]
            eb = jnp.exp(bh)
            ek = jnp.exp(blast - bh)
            dst = dstate[h]
            dst_b = dst.astype(BF16)
            dg_carry = jnp.sum(dst * stn_ref[h], axis=0, keepdims=True)
            da = lax.dot_general(do, vh, _NT, preferred_element_type=F32)
            da_scr[...] = jnp.where(tri, da, 0.0)
            dv = lax.dot_general(a_ref[h].astype(BF16), do, _TN, preferred_element_type=F32)
            dv = dv + lax.dot_general((kh * ek).astype(BF16), dst_b, _NT, preferred_element_type=F32)
            dv_ref[:, vc] = dv.astype(BF16)
            dq_scr[:, kc] = jnp.dot(do, stp_ref[h].astype(BF16), preferred_element_type=F32) * eb
            dk_scr[:, kc] = jnp.dot(vh, dst_b, preferred_element_type=F32) * ek
            _gla_pair_bwd(q_ref, k_ref, b_scr, da_scr, dq_scr, dk_scr, h)
            dq = dq_scr[:, kc]
            dk = dk_scr[:, kc]
            dg_scr[:, kc] = _suffix_sum_rows(qh * dq - kh * dk) + dg_carry
            dstate[h] = dst * jnp.exp(blast) + lax.dot_general(do, (qh * eb).astype(BF16), _TN, preferred_element_type=F32)
        dq_ref[...] = (dq_scr[...] * GLA_Q_SCALE).astype(BF16)
        dk_ref[...] = dk_scr[...].astype(BF16)
        dpre = dg_scr[...] * ((1.0 - _sigmoid(pre)) * (1.0 / GLA_GATE_NORMALIZER))
        dpre_b = dpre.astype(BF16)
        dbias_ref[...] += jnp.sum(dpre, axis=0, keepdims=True)
        dwgu_ref[...] += lax.dot_general(gl_ref[...].astype(BF16), dpre_b, _TN, preferred_element_type=F32)
        dgl_ref[...] = lax.dot_general(dpre_b, wgu_ref[...], _NT, preferred_element_type=F32).astype(BF16)

    def chunk(b, i):
        return b * nc + (nc - 1 - i)

    def tok(width, col):
        return pl.BlockSpec((c, width), lambda b, i: (chunk(b, i), col))

    whole = lambda shape: pl.BlockSpec(shape, lambda b, i: (0,) * len(shape))
    st_shape = (None, GLA_HEADS, GLA_HEAD_V, GLA_HEAD_K)
    return pl.pallas_call(
        body,
        name=name,
        grid=(nb, nc),
        in_specs=[tok(GLA_DK, 0), tok(GLA_DK, 1), tok(GLA_DV, 1), tok(GLA_DV, 2), tok(GLOW_PAD, GLA_MAIN // GLOW_PAD),
                  whole((GLOW_PAD, GLA_DK)), whole((1, GLA_DK)), whole((1, GLA_HEAD_V)),
                  tok(GLA_DV, 0),
                  pl.BlockSpec((GLA_HEADS, c, c), lambda b, i: (0, chunk(b, i), 0)),
                  pl.BlockSpec(st_shape, lambda b, i: (chunk(b, i), 0, 0, 0)),
                  pl.BlockSpec(st_shape, lambda b, i: (b * nc + jnp.minimum(nc - i, nc - 1), 0, 0, 0)),
                  tok(GLA_DV, 0)],
        out_specs=[tok(GLA_DK, 0), tok(GLA_DK, 0), tok(GLA_DV, 0), tok(GLA_DV, 0), tok(GLOW_PAD, 0),
                   whole((GLOW_PAD, GLA_DK)), whole((1, GLA_DK)), whole((1, GLA_HEAD_V))],
        out_shape=[jax.ShapeDtypeStruct((t, GLA_DK), BF16), jax.ShapeDtypeStruct((t, GLA_DK), BF16),
                   jax.ShapeDtypeStruct((t, GLA_DV), BF16), jax.ShapeDtypeStruct((t, GLA_DV), BF16),
                   jax.ShapeDtypeStruct((t, GLOW_PAD), BF16),
                   jax.ShapeDtypeStruct((GLOW_PAD, GLA_DK), F32), jax.ShapeDtypeStruct((1, GLA_DK), F32),
                   jax.ShapeDtypeStruct((1, GLA_HEAD_V), F32)],
        scratch_shapes=[pltpu.VMEM((GLA_HEADS, GLA_HEAD_V, GLA_HEAD_K), F32), pltpu.VMEM((c, GLA_DK), F32),
                        pltpu.VMEM((c, c), F32), pltpu.VMEM((c, GLA_DK), F32), pltpu.VMEM((c, GLA_DK), F32),
                        pltpu.VMEM((c, GLA_DK), F32)],
        compiler_params=_cparams(("arbitrary", "arbitrary")),
    )(proj, proj, proj, proj, proj, wgu, bias, ng, o, a, states, states, dy)


DIL_STEPS = DIL_BLOCK
DIL_SCALE = DIL_HEAD_DIM ** -0.5
DIL_COLS = DIL_IN // DIL_WIDTH


def _dil_mask(i):
    rowi = lax.broadcasted_iota(jnp.int32, (DIL_BLOCK, 2 * DIL_BLOCK), 0)
    colj = lax.broadcasted_iota(jnp.int32, (DIL_BLOCK, 2 * DIL_BLOCK), 1)
    dist = rowi + DIL_BLOCK - colj
    band = jnp.logical_and(dist >= 0, dist <= DIL_STEPS)
    return jnp.logical_and(band, jnp.logical_or(i > 0, colj >= DIL_BLOCK))


def _dil_geometry(t, gi):
    _, d = DIL_PATTERNS[gi]
    length = SEQ // d
    return d, length, length // DIL_BLOCK, t // SEQ


def _dil_in_specs(gi, d, nq, clamp):
    def spec(j, prev):
        def index(b, r, i):
            ic = clamp(i)
            return (b * nq + (jnp.maximum(ic - 1, 0) if prev else ic), r * DIL_COLS + gi * 3 + j)
        return pl.BlockSpec((DIL_BLOCK, DIL_WIDTH), index)
    return [spec(0, False), spec(1, False), spec(1, True), spec(2, False), spec(2, True)]


def _dil_fwd(proj, gi, *, name):
    t = proj.shape[0]
    d, length, nq, nb = _dil_geometry(t, gi)
    pv = proj.reshape(nb * length, d * DIL_IN)

    def body(q_ref, kc_ref, kp_ref, vc_ref, vp_ref, o_ref, lse_ref):
        mask = _dil_mask(pl.program_id(2))
        for h in range(DIL_HEADS):
            hc = pl.ds(h * DIL_HEAD_DIM, DIL_HEAD_DIM)
            qh = q_ref[:, hc].astype(BF16)
            kcat = jnp.concatenate([kp_ref[:, hc], kc_ref[:, hc]], axis=0).astype(BF16)
            vcat = jnp.concatenate([vp_ref[:, hc], vc_ref[:, hc]], axis=0).astype(BF16)
            s = lax.dot_general(qh, kcat, _NT, preferred_element_type=F32) * DIL_SCALE
            s = jnp.where(mask, s, -jnp.inf)
            m = jnp.max(s, axis=-1, keepdims=True)
            p = jnp.exp(s - m)
            l = jnp.sum(p, axis=-1, keepdims=True)
            o_ref[:, hc] = jnp.dot((p / l).astype(BF16), vcat, preferred_element_type=F32)
            lse_ref[:, hc] = jnp.broadcast_to(m + jnp.log(l), (DIL_BLOCK, DIL_HEAD_DIM))

    out_spec = pl.BlockSpec((DIL_BLOCK, DIL_WIDTH), lambda b, r, i: (b * nq + i, r))
    o, lse = pl.pallas_call(
        body,
        name=name,
        grid=(nb, d, nq),
        in_specs=_dil_in_specs(gi, d, nq, lambda i: i),
        out_specs=[out_spec, out_spec],
        out_shape=[jax.ShapeDtypeStruct((nb * length, d * DIL_WIDTH), F32)] * 2,
        compiler_params=_cparams(("parallel", "parallel", "parallel")),
    )(pv, pv, pv, pv, pv)
    return o.reshape(t, DIL_WIDTH), lse.reshape(t, DIL_WIDTH)


def _dil_bwd(proj, gi, lse, do, delta, *, name):
    t = proj.shape[0]
    d, length, nq, nb = _dil_geometry(t, gi)
    pv = proj.reshape(nb * length, d * DIL_IN)
    view = lambda a: a.reshape(nb * length, d * DIL_WIDTH)

    def body(q_ref, kc_ref, kp_ref, vc_ref, vp_ref, lse_ref, do_ref, dl_ref, dq_ref, dk_ref, dv_ref, ck, cv):
        i = pl.program_id(2)

        @pl.when(i < nq)
        def _():
            mask = _dil_mask(i)
            for h in range(DIL_HEADS):
                hc = pl.ds(h * DIL_HEAD_DIM, DIL_HEAD_DIM)
                h1 = pl.ds(h * DIL_HEAD_DIM, 1)
                qh = q_ref[:, hc].astype(BF16)
                kcat = jnp.concatenate([kp_ref[:, hc], kc_ref[:, hc]], axis=0).astype(BF16)
                vcat = jnp.concatenate([vp_ref[:, hc], vc_ref[:, hc]], axis=0).astype(BF16)
                doh = do_ref[:, hc]
                s = lax.dot_general(qh, kcat, _NT, preferred_element_type=F32) * DIL_SCALE
                p = jnp.exp(jnp.where(mask, s, -jnp.inf) - lse_ref[:, h1])
                dp = lax.dot_general(doh, vcat, _NT, preferred_element_type=F32)
                ds = (p * (dp + dl_ref[:, h1]) * DIL_SCALE).astype(BF16)
                dq_ref[:, hc] = jnp.dot(ds, kcat, preferred_element_type=F32).astype(BF16)
                dkcat = lax.dot_general(ds, qh, _TN, preferred_element_type=F32)
                dvcat = lax.dot_general(p.astype(BF16), doh, _TN, preferred_element_type=F32)

                @pl.when(i > 0)
                def _():
                    dk_ref[:, hc] = (ck[:, hc] + dkcat[:DIL_BLOCK]).astype(BF16)
                    dv_ref[:, hc] = (cv[:, hc] + dvcat[:DIL_BLOCK]).astype(BF16)

                ck[:, hc] = dkcat[DIL_BLOCK:]
                cv[:, hc] = dvcat[DIL_BLOCK:]

        @pl.when(i == nq)
        def _():
            dk_ref[...] = ck[...].astype(BF16)
            dv_ref[...] = cv[...].astype(BF16)

    clamp = lambda i: jnp.minimum(i, nq - 1)
    cur = pl.BlockSpec((DIL_BLOCK, DIL_WIDTH), lambda b, r, i: (b * nq + clamp(i), r))
    done = pl.BlockSpec((DIL_BLOCK, DIL_WIDTH), lambda b, r, i: (b * nq + jnp.maximum(i - 1, 0), r))
    shape = jax.ShapeDtypeStruct((nb * length, d * DIL_WIDTH), BF16)
    dq, dk, dv = pl.pallas_call(
        body,
        name=name,
        grid=(nb, d, nq + 1),
        in_specs=_dil_in_specs(gi, d, nq, clamp) + [cur, cur, cur],
        out_specs=[cur, done, done],
        out_shape=[shape, shape, shape],
        scratch_shapes=[pltpu.VMEM((DIL_BLOCK, DIL_WIDTH), F32), pltpu.VMEM((DIL_BLOCK, DIL_WIDTH), F32)],
        compiler_params=_cparams(("parallel", "parallel", "arbitrary")),
    )(pv, pv, pv, pv, pv, view(lse), view(do), view(delta))
    return dq.reshape(t, DIL_WIDTH), dk.reshape(t, DIL_WIDTH), dv.reshape(t, DIL_WIDTH)


MIX_ROWS = 256


def _head_rowsum(x):
    parts = []
    for h in range(DIL_HEADS):
        s = jnp.sum(x[:, h * DIL_HEAD_DIM:(h + 1) * DIL_HEAD_DIM], axis=-1, keepdims=True)
        parts.append(jnp.broadcast_to(s, (x.shape[0], DIL_HEAD_DIM)))
    return jnp.concatenate(parts, axis=-1)


def _mix_weights(lse_refs):
    ls = [r[...] for r in lse_refs]
    m = jnp.maximum(jnp.maximum(ls[0], ls[1]), ls[2])
    es = [jnp.exp(l - m) for l in ls]
    inv = 1.0 / (es[0] + es[1] + es[2])
    return [e * inv for e in es]


def _dil_mix_fwd(os_, lses, *, name):
    t = os_[0].shape[0]

    def body(o0, o1, o2, l0, l1, l2, out_ref):
        w = _mix_weights((l0, l1, l2))
        out_ref[...] = (w[0] * o0[...] + w[1] * o1[...] + w[2] * o2[...]).astype(BF16)

    row = pl.BlockSpec((MIX_ROWS, DIL_WIDTH), lambda i: (i, 0))
    return pl.pallas_call(
        body, name=name, grid=(t // MIX_ROWS,), in_specs=[row] * 6, out_specs=row,
        out_shape=jax.ShapeDtypeStruct((t, DIL_WIDTH), BF16), compiler_params=_cparams(("parallel",)),
    )(*os_, *lses)


def _dil_mix_bwd(os_, lses, dout, *, name):
    t = os_[0].shape[0]

    def body(o0, o1, o2, l0, l1, l2, d_ref, do0, do1, do2, dl0, dl1, dl2):
        w = _mix_weights((l0, l1, l2))
        dv = d_ref[...]
        mix = w[0] * o0[...] + w[1] * o1[...] + w[2] * o2[...]
        bar = _head_rowsum(dv * mix)
        for wg, do_ref, dl_ref in zip(w, (do0, do1, do2), (dl0, dl1, dl2)):
            do_ref[...] = (wg * dv).astype(BF16)
            dl_ref[...] = -wg * bar

    row = pl.BlockSpec((MIX_ROWS, DIL_WIDTH), lambda i: (i, 0))
    outs = pl.pallas_call(
        body, name=name, grid=(t // MIX_ROWS,), in_specs=[row] * 7, out_specs=[row] * 6,
        out_shape=[jax.ShapeDtypeStruct((t, DIL_WIDTH), BF16)] * 3 + [jax.ShapeDtypeStruct((t, DIL_WIDTH), F32)] * 3,
        compiler_params=_cparams(("parallel",)),
    )(*os_, *lses, dout)
    return outs[:3], outs[3:]


_MESH = pl.DeviceIdType.MESH
_ANY = pl.BlockSpec(memory_space=pl.ANY)


def _position():
    return lax.axis_index("x"), lax.axis_index("y"), lax.axis_index("c")


AG_COPIES = 7


def _all_gather(shards, *, name):
    n = len(shards)

    def body(*refs):
        x_refs, out_refs = refs[:n], refs[n:2 * n]
        send_sems, recv_sems, local_sems = refs[2 * n:]
        x, y, cc = _position()
        me, sibling = (x, y, cc), (x, y, 1 - cc)
        chips = [(1 - x, y), (x, 1 - y), (1 - x, 1 - y)]

        def copy(w, k, block, to, own=False):
            px, py, pc = block
            slot = out_refs[w].at[4 * px + 2 * py + pc]
            return pltpu.make_async_remote_copy(
                src_ref=x_refs[w] if own else slot, dst_ref=slot,
                send_sem=send_sems.at[AG_COPIES * w + k], recv_sem=recv_sems.at[AG_COPIES * w + k],
                device_id=to, device_id_type=_MESH)

        mine = [pltpu.make_async_copy(x_refs[w], out_refs[w].at[4 * x + 2 * y + cc], local_sems.at[w]) for w in range(n)]
        first = []
        for w in range(n):
            mine[w].start()
            first.append(copy(w, 0, me, sibling, own=True))
            first += [copy(w, 1 + j, me, (*chip, cc), own=True) for j, chip in enumerate(chips)]
        for cp in first:
            cp.start()
        passed = []
        for j, chip in enumerate(chips):
            for w in range(n):
                copy(w, 1 + j, (*chip, cc), me).wait_recv()
                passed.append(copy(w, 4 + j, (*chip, cc), sibling))
                passed[-1].start()
        for w in range(n):
            copy(w, 0, sibling, me).wait_recv()
            for j, chip in enumerate(chips):
                copy(w, 4 + j, (*chip, 1 - cc), me).wait_recv()
        for cp in first + passed:
            cp.wait_send()
        for cp in mine:
            cp.wait()

    return pl.pallas_call(
        body,
        name=name,
        out_shape=[jax.ShapeDtypeStruct((N_DEV,) + s.shape, s.dtype) for s in shards],
        in_specs=[_ANY] * n,
        out_specs=[_ANY] * n,
        scratch_shapes=[pltpu.SemaphoreType.DMA((AG_COPIES * n,)), pltpu.SemaphoreType.DMA((AG_COPIES * n,)),
                        pltpu.SemaphoreType.DMA((n,))],
    )(*shards)


def _parity_half(ref, parity, half_rows):
    if half_rows is None:
        return ref.at[:, parity]
    return ref.at[:, pl.ds(parity * half_rows, half_rows), :]


def _exchange_sibling(gs, half_rows, *, name):
    n = len(gs)

    def body(*refs):
        g_refs, out_refs = refs[:n], refs[n:2 * n]
        send_sems, recv_sems = refs[2 * n], refs[2 * n + 1]
        x, y, cc = _position()
        copies = [pltpu.make_async_remote_copy(
            src_ref=_parity_half(g_refs[k], 1 - cc, half_rows[k]), dst_ref=out_refs[k],
            send_sem=send_sems.at[k], recv_sem=recv_sems.at[k],
            device_id=(x, y, 1 - cc), device_id_type=_MESH) for k in range(n)]
        for cp in copies:
            cp.start()
        for cp in copies:
            cp.wait()

    def out_shape(g, hr):
        return jax.ShapeDtypeStruct((4,) + (g.shape[2:] if hr is None else (hr, g.shape[2])), g.dtype)

    return pl.pallas_call(
        body,
        name=name,
        out_shape=[out_shape(g, hr) for g, hr in zip(gs, half_rows)],
        in_specs=[_ANY] * n,
        out_specs=[_ANY] * n,
        scratch_shapes=[pltpu.SemaphoreType.DMA((n,)), pltpu.SemaphoreType.DMA((n,))],
    )(*gs)


def _exchange_chips(ps, *, name):
    n = len(ps)

    def body(*refs):
        p_refs, out_refs = refs[:n], refs[n:2 * n]
        send_sems, recv_sems = refs[2 * n], refs[2 * n + 1]
        x, y, cc = _position()
        copies = []
        for w in range(n):
            for k in (1, 2, 3):
                px = 1 - x if k >> 1 else x
                py = 1 - y if k & 1 else y
                copies.append(pltpu.make_async_remote_copy(
                    src_ref=p_refs[w].at[2 * px + py], dst_ref=out_refs[w].at[k - 1],
                    send_sem=send_sems.at[3 * w + k - 1], recv_sem=recv_sems.at[3 * w + k - 1],
                    device_id=(px, py, cc), device_id_type=_MESH))
        for cp in copies:
            cp.start()
        for cp in copies:
            cp.wait()

    return pl.pallas_call(
        body,
        name=name,
        out_shape=[jax.ShapeDtypeStruct((3,) + p.shape[1:], p.dtype) for p in ps],
        in_specs=[_ANY] * n,
        out_specs=[_ANY] * n,
        scratch_shapes=[pltpu.SemaphoreType.DMA((3 * n,)), pltpu.SemaphoreType.DMA((3 * n,))],
    )(*ps)


def _add_sibling(g, r1, place, half_rows, *, tr, tc, name):
    _, r, c = r1.shape
    if half_rows is None:
        g_spec = pl.BlockSpec((None, None, tr, tc), lambda i, j, k, pc: (k, pc[0], i, j))
    else:
        per_half = half_rows // tr
        g_spec = pl.BlockSpec((None, tr, tc), lambda i, j, k, pc: (k, pc[0] * per_half + i, j))

    def body(pc_ref, g_ref, r_ref, pb_ref, own_ref):
        s = g_ref[...] + r_ref[...]
        pb_ref[...] = s.astype(BF16)

        @pl.when(pl.program_id(2) == pc_ref[1])
        def _():
            own_ref[...] = s

    grid_spec = pltpu.PrefetchScalarGridSpec(
        num_scalar_prefetch=1,
        grid=(r // tr, c // tc, 4),
        in_specs=[g_spec, pl.BlockSpec((None, tr, tc), lambda i, j, k, pc: (k, i, j))],
        out_specs=[pl.BlockSpec((None, tr, tc), lambda i, j, k, pc: (k, i, j)),
                   pl.BlockSpec((tr, tc), lambda i, j, k, pc: (i, j))],
    )
    return pl.pallas_call(
        body, name=name, grid_spec=grid_spec,
        out_shape=[jax.ShapeDtypeStruct((4, r, c), BF16), jax.ShapeDtypeStruct((r, c), F32)],
        compiler_params=_cparams(("parallel", "parallel", "arbitrary")),
    )(place, g, r1)


def _adamw_math(g, w, m, v):
    m = ADAM_B1 * m + (1.0 - ADAM_B1) * g
    v = ADAM_B2 * v + (1.0 - ADAM_B2) * (g * g)
    m_hat = m / (1.0 - ADAM_B1 ** ADAM_STEP)
    v_hat = v / (1.0 - ADAM_B2 ** ADAM_STEP)
    delta = -ADAM_LR * (m_hat / (jnp.sqrt(v_hat) + ADAM_EPS) + ADAM_WD * w)
    return delta, m, v


def _adamw_big(own, r2, w, m, v, prev, layer, *, tr, tc, name):
    _, r, c = w.shape

    def body(p_ref, r2_ref, w_ref, m_ref, v_ref, a0, a1, a2, a3, g_ref, d_ref, mo_ref, vo_ref):
        g = ((p_ref[...] + r2_ref[0].astype(F32)) + r2_ref[1].astype(F32)) + r2_ref[2].astype(F32)
        delta, mn, vn = _adamw_math(g, w_ref[...], m_ref[...], v_ref[...])
        g_ref[...] = g
        d_ref[...] = delta
        mo_ref[...] = mn
        vo_ref[...] = vn

    lay = pl.BlockSpec((None, tr, tc), lambda i, j: (layer, i, j))
    return pl.pallas_call(
        body, name=name, grid=(r // tr, c // tc),
        in_specs=[pl.BlockSpec((tr, tc), lambda i, j: (i, j)), pl.BlockSpec((3, tr, tc), lambda i, j: (0, i, j)),
                  lay, lay, lay, _ANY, _ANY, _ANY, _ANY],
        out_specs=[lay, lay, lay, lay],
        out_shape=[jax.ShapeDtypeStruct(w.shape, F32)] * 4,
        input_output_aliases={5: 0, 6: 1, 7: 2, 8: 3},
        compiler_params=_cparams(("parallel", "parallel")),
    )(own, r2, w, m, v, *prev)


SMALL_COLS = 1024


def _sum_gathered(parts, *, name):
    _, r, c = parts.shape

    def body(p_ref, o_ref):
        acc = p_ref[0]
        for k in range(1, N_DEV):
            acc = acc + p_ref[k]
        o_ref[...] = acc

    return pl.pallas_call(
        body, name=name, grid=(1,), in_specs=[pl.BlockSpec((N_DEV, r, c), lambda i: (0, 0, 0))],
        out_specs=pl.BlockSpec((r, c), lambda i: (0, 0)), out_shape=jax.ShapeDtypeStruct((r, c), F32),
        compiler_params=_cparams(("arbitrary",)),
    )(parts)


def _adamw_small(g, w, m, v, *, name):
    r, c = g.shape

    def body(g_ref, w_ref, m_ref, v_ref, d_ref, mo_ref, vo_ref):
        delta, mn, vn = _adamw_math(g_ref[...], w_ref[...], m_ref[...], v_ref[...])
        d_ref[...] = delta
        mo_ref[...] = mn
        vo_ref[...] = vn

    spec = pl.BlockSpec((r, c), lambda i: (0, 0))
    return pl.pallas_call(
        body, name=name, grid=(1,), in_specs=[spec] * 4, out_specs=[spec] * 3,
        out_shape=[jax.ShapeDtypeStruct((r, c), F32)] * 3, compiler_params=_cparams(("arbitrary",)),
    )(g, w, m, v)


WEIGHT_NAMES = ("gla_w_in", "gla_w_gate_up", "gla_gate_bias", "gla_norm_g", "gla_w_out", "dil_w_in", "dil_w_out",
                "ffn_w_up", "ffn_conv_w", "ffn_conv_b", "ffn_w_down", "ln_g", "ln_b")
ADAM_TILES = {"gla_w_in": (256, 770), "gla_w_out": (128, 2048), "dil_w_in": (256, 1152), "dil_w_out": (512, 256),
              "ffn_w_up": (344, 1024), "ffn_w_down": (344, 1024)}
ADD_TILES = {**ADAM_TILES, "ffn_w_up": (352, 1024)}
FF_DOWN_SHARD = D_FF // N_DEV
VEC_COLS = 128


def _pad_axis(a, axis, to):
    pads = [(0, 0)] * a.ndim
    pads[axis] = (0, to - a.shape[axis])
    return jnp.pad(a, pads)


def _ff_cols(blocks):
    r = blocks.shape[1]
    return _pad_axis(blocks, 2, FF_SHARD_PAD).reshape(2, 4, r, FF_SHARD_PAD).transpose(0, 2, 1, 3).reshape(2, r, FF_HALF_PAD)


def _ff_cols_back(a):
    r = a.shape[1]
    return a.reshape(2, r, 4, FF_SHARD_PAD)[..., :FF_SHARD].transpose(1, 0, 2, 3).reshape(r, 2 * D_FF)


def _gather_layer(l, w):
    j = l // 2
    gla = l % 2 == 0
    vec_parts = [w["ffn_conv_w"][l], w["ln_g"][l], w["ln_b"][l]] + ([w["gla_w_gate_up"][j]] if gla else [])
    vec_shapes = [a.shape for a in vec_parts]
    vec_rows = -(-sum(math.prod(s) for s in vec_shapes) // (VEC_COLS * SUBLANES)) * SUBLANES
    vecs = _pack_rows(vec_parts, vec_rows, VEC_COLS)
    w_up_t = _pad_axis(jnp.swapaxes(w["ffn_w_up"][l], 0, 1).astype(BF16), 0, FF_SHARD_PAD)
    mats = [(w["gla_w_in"] if gla else w["dil_w_in"])[j].astype(BF16), (w["gla_w_out"] if gla else w["dil_w_out"])[j].astype(BF16),
            w_up_t, w["ffn_w_down"][l].astype(BF16)]
    g_in, g_out, g_up, g_down, g_vec = _all_gather(mats + [vecs], name=f"gather_l{l}")
    out = {"w_up_t": g_up}
    wd = g_down.reshape(4, FF_SHARD, D_MODEL)
    out["w_down"] = _pad_axis(wd, 1, FF_SHARD_PAD).reshape(FF_HALF_PAD, D_MODEL)
    vec = [jnp.stack(p) for p in zip(*[_unpack_rows(g_vec[d], vec_shapes) for d in range(N_DEV)])]
    out["conv_w"] = _ff_cols(vec[0])
    out["conv_b"] = _pad_axis(w["ffn_conv_b"][l].reshape(N_DEV, FF_SHARD), 1, FF_SHARD_PAD).reshape(2, 1, FF_HALF_PAD)
    out["ln_g"] = vec[1].transpose(1, 0, 2).reshape(2, 1, D_MODEL)
    out["ln_b"] = vec[2].transpose(1, 0, 2).reshape(2, 1, D_MODEL)
    if gla:
        win = g_in.transpose(1, 0, 2).reshape(D_MODEL, GLA_IN)
        out["w_in"] = _pad_axis(win, 1, GLA_MAIN + GLOW_PAD)
        wgu = vec[3].transpose(1, 0, 2).reshape(GLA_GATE_RANK, GLA_DK).astype(BF16)
        out["w_gate_up"] = _pad_axis(wgu, 0, GLOW_PAD)
        out["w_out"] = g_out.reshape(GLA_DV, D_MODEL)
        out["gate_bias"] = w["gla_gate_bias"][j].reshape(1, GLA_DK)
        out["norm_g"] = w["gla_norm_g"][j].reshape(1, GLA_HEAD_V)
    else:
        out["w_in"] = g_in
        out["w_out"] = g_out.transpose(1, 0, 2).reshape(DIL_WIDTH, D_MODEL)
    return out


def _by_chip_parity(blocks):
    return blocks.reshape((4, 2) + blocks.shape[1:])


def _col_blocks(dw, width):
    r = dw.shape[0]
    return dw.reshape(r, N_DEV, width).transpose(1, 0, 2)


def _ffn_fwd(l, yb, lw):
    h = _ffn_hidden(yb, lw["w_up_t"], name=f"l{l}_ffn_hidden")
    act = _convgate_fwd(h, lw["conv_w"], lw["conv_b"], name=f"l{l}_convgate")
    ffn = _matmul(act, lw["w_down"], tm=1024, tn=1024, tk=2816, name=f"l{l}_ffn_down")
    return ffn, (h, act)


def _ffn_bwd(l, yb, dz, dzb, lw, saved):
    h, act = saved
    t = yb.shape[0]
    dact = _matmul(dzb, lw["w_down"], tb=True, tm=512, tn=2816, tk=D_MODEL, name=f"l{l}_ffn_dact")
    d_down = _ffn_down_dw(act, dzb, name=f"l{l}_ffn_dwdown")
    dh, dcw, dcb = _convgate_bwd(h, dact, lw["conv_w"], lw["conv_b"], name=f"l{l}_convgate_bwd")
    dh = dh.reshape(2 * t, FF_HALF_PAD)
    d_up_t = _ffn_hidden_dw(dh, yb, name=f"l{l}_ffn_dwup")
    dy = _ffn_hidden_dy(dh, lw["w_up_t"], dz, DEEPNORM_ALPHA, name=f"l{l}_ffn_dy")
    big = {"ffn_w_up": (_by_chip_parity(d_up_t), None), "ffn_w_down": (d_down, FF_DOWN_SHARD)}
    small = {"ffn_conv_w": _ff_cols_back(dcw), "ffn_conv_b": _ff_cols_back(dcb)[0]}
    return dy, big, small


def _gla_layer_fwd(l, hb, lw):
    proj = _matmul(hb, lw["w_in"], tm=1024, tn=896, tk=D_MODEL, name=f"l{l}_gla_proj")
    y, o, a, st = _gla_fwd(proj, lw["w_gate_up"], lw["gate_bias"], lw["norm_g"], name=f"l{l}_gla")
    mix = _matmul(y, lw["w_out"], tm=1024, tn=1024, tk=GLA_DV, name=f"l{l}_gla_out")
    return mix, (proj, y, o, a, st)


def _gla_layer_bwd(l, hb, dz, dzb, lw, saved):
    proj, y, o, a, st = saved
    dy = _matmul(dzb, lw["w_out"], tb=True, tm=1024, tn=1024, tk=D_MODEL, name=f"l{l}_gla_dy")
    d_out = _matmul(y, dzb, ta=True, tm=1024, tn=1024, tk=2048, name=f"l{l}_gla_dwout")
    dq, dk, dv, dr, dgl, dwgu, dbias, dng = _gla_bwd(proj, lw["w_gate_up"], lw["gate_bias"], lw["norm_g"], o, a, st, dy, name=f"l{l}_gla_bwd")
    dproj = jnp.concatenate([dq, dk, dv, dr, dgl], axis=-1)
    d_in = _matmul(hb, dproj, ta=True, tm=1024, tn=896, tk=2048, name=f"l{l}_gla_dwin")
    dx = _matmul(dproj, lw["w_in"], tb=True, tm=1024, tn=1024, tk=896, res=dz, res_scale=DEEPNORM_ALPHA, name=f"l{l}_gla_dx")
    big = {"gla_w_in": (_by_chip_parity(_col_blocks(d_in[:, :GLA_IN], GLA_IN // N_DEV)), None),
           "gla_w_out": (_by_chip_parity(d_out.reshape(N_DEV, GLA_DV // N_DEV, D_MODEL)), None)}
    small = {"gla_w_gate_up": dwgu[:GLA_GATE_RANK], "gla_gate_bias": dbias[0], "gla_norm_g": dng[0]}
    return dx, big, small


def _dil_layer_fwd(l, hb, lw):
    proj = _mm_colblocks(hb, lw["w_in"], name=f"l{l}_dil_proj")
    os_, lses = [], []
    for gi in range(len(DIL_PATTERNS)):
        o, lse = _dil_fwd(proj, gi, name=f"l{l}_dil_attn{gi}")
        os_.append(o)
        lses.append(lse)
    omix = _dil_mix_fwd(os_, lses, name=f"l{l}_dil_mix")
    mix = _matmul(omix, lw["w_out"], tm=1024, tn=1024, tk=DIL_WIDTH, name=f"l{l}_dil_out")
    return mix, (proj, os_, lses, omix)


def _dil_layer_bwd(l, hb, dz, dzb, lw, saved):
    proj, os_, lses, omix = saved
    dout = _matmul(dzb, lw["w_out"], tb=True, tm=1024, tn=1024, tk=D_MODEL, name=f"l{l}_dil_dy")
    d_out = _matmul(omix, dzb, ta=True, tm=1024, tn=1024, tk=2048, name=f"l{l}_dil_dwout")
    dos, dls = _dil_mix_bwd(os_, lses, dout, name=f"l{l}_dil_mix_bwd")
    parts = []
    for gi in range(len(DIL_PATTERNS)):
        parts += list(_dil_bwd(proj, gi, lses[gi], dos[gi], dls[gi], name=f"l{l}_dil_attn_bwd{gi}"))
    dproj = jnp.concatenate(parts, axis=-1)
    d_in = _mm_grad_colblocks(hb, dproj, DIL_IN // N_DEV, name=f"l{l}_dil_dwin")
    dx = _mm_colblocks_t(dproj, lw["w_in"], dz, DEEPNORM_ALPHA, name=f"l{l}_dil_dx")
    big = {"dil_w_in": (_by_chip_parity(d_in), None),
           "dil_w_out": (_by_chip_parity(_col_blocks(d_out, D_MODEL // N_DEV)), None)}
    return dx, big, {}


def _pack_rows(arrays, rows, cols=SMALL_COLS):
    flat = [a.reshape(-1) for a in arrays]
    used = sum(f.shape[0] for f in flat)
    return jnp.concatenate(flat + [jnp.zeros((rows * cols - used,), F32)]).reshape(rows, cols)


def _unpack_rows(packed, shapes):
    flat, out, off = packed.reshape(-1), [], 0
    for s in shapes:
        n = math.prod(s)
        out.append(flat[off:off + n].reshape(s))
        off += n
    return out


def _rows_for(shapes):
    n = sum(math.prod(s) for s in shapes)
    return -(-n // (SMALL_COLS * SUBLANES)) * SUBLANES


def kernel(x, gla_w_in, gla_w_gate_up, gla_gate_bias, gla_norm_g, gla_w_out, dil_w_in, dil_w_out, ffn_w_up, ffn_conv_w, ffn_conv_b, ffn_w_down, ln_g, ln_b, loss_target, m_gla_w_in, m_gla_w_gate_up, m_gla_gate_bias, m_gla_norm_g, m_gla_w_out, m_dil_w_in, m_dil_w_out, m_ffn_w_up, m_ffn_conv_w, m_ffn_conv_b, m_ffn_w_down, m_ln_g, m_ln_b, v_gla_w_in, v_gla_w_gate_up, v_gla_gate_bias, v_gla_norm_g, v_gla_w_out, v_dil_w_in, v_dil_w_out, v_ffn_w_up, v_ffn_conv_w, v_ffn_conv_b, v_ffn_w_down, v_ln_g, v_ln_b):
    w = dict(zip(WEIGHT_NAMES, (gla_w_in, gla_w_gate_up, gla_gate_bias, gla_norm_g, gla_w_out, dil_w_in, dil_w_out,
                                ffn_w_up, ffn_conv_w, ffn_conv_b, ffn_w_down, ln_g, ln_b)))
    mom = dict(zip(WEIGHT_NAMES, (m_gla_w_in, m_gla_w_gate_up, m_gla_gate_bias, m_gla_norm_g, m_gla_w_out, m_dil_w_in,
                                  m_dil_w_out, m_ffn_w_up, m_ffn_conv_w, m_ffn_conv_b, m_ffn_w_down, m_ln_g, m_ln_b)))
    var = dict(zip(WEIGHT_NAMES, (v_gla_w_in, v_gla_w_gate_up, v_gla_gate_bias, v_gla_norm_g, v_gla_w_out, v_dil_w_in,
                                  v_dil_w_out, v_ffn_w_up, v_ffn_conv_w, v_ffn_conv_b, v_ffn_w_down, v_ln_g, v_ln_b)))
    xi, yi, ci = _position()
    dev = 4 * xi + 2 * yi + ci
    place = jnp.stack([ci, 2 * xi + yi]).astype(jnp.int32)
    t = x.shape[0] * x.shape[1]
    h = x.reshape(t, D_MODEL)
    hb = h.astype(BF16)
    target = loss_target.reshape(t, D_MODEL)

    lws = [_gather_layer(l, w) for l in range(DEPTH)]

    saved = []
    for l in range(DEPTH):
        lw = lws[l]
        mix, mixer_saved = (_gla_layer_fwd if l % 2 == 0 else _dil_layer_fwd)(l, hb, lw)
        y1, y1b, xh1, rs1 = _ln_fwd(h, mix, lw["ln_g"][0], lw["ln_b"][0], name=f"l{l}_ln1")
        ffn, ffn_saved = _ffn_fwd(l, y1b, lw)
        y2, y2b, xh2, rs2 = _ln_fwd(y1, ffn, lw["ln_g"][1], lw["ln_b"][1], name=f"l{l}_ln2")
        saved.append((hb, mixer_saved, y1b, xh1, rs1, ffn_saved, xh2, rs2))
        h, hb = y2, y2b
    loss_local, dy = _loss_fwd_bwd(h, target, name="loss")
    loss = lax.psum(loss_local[0, 0], ("x", "y", "c"))

    big_names = tuple(ADAM_TILES)
    as_updated = lambda n, a: jnp.swapaxes(a, 1, 2) if n == "ffn_w_up" else a
    wt, mt, vt = ({n: as_updated(n, d[n]) for n in big_names} for d in (w, mom, var))
    results = {n: [lax.empty(wt[n].shape, F32) for _ in range(4)] for n in big_names}
    small_grads = {n: [None] * w[n].shape[0] for n in WEIGHT_NAMES if n not in big_names}
    for l in reversed(range(DEPTH)):
        lw = lws[l]
        hb_in, mixer_saved, y1b, xh1, rs1, ffn_saved, xh2, rs2 = saved[l]
        j = l // 2
        dz2, dz2b, dg2, db2 = _ln_bwd(dy, xh2, rs2, lw["ln_g"][1], name=f"l{l}_ln2_bwd")
        dy1, big_ffn, small_ffn = _ffn_bwd(l, y1b, dz2, dz2b, lw, ffn_saved)
        dz1, dz1b, dg1, db1 = _ln_bwd(dy1, xh1, rs1, lw["ln_g"][0], name=f"l{l}_ln1_bwd")
        dy, big_mix, small_mix = (_gla_layer_bwd if l % 2 == 0 else _dil_layer_bwd)(l, hb_in, dz1, dz1b, lw, mixer_saved)
        small_grads["ln_g"][l] = jnp.concatenate([dg1, dg2], axis=0)
        small_grads["ln_b"][l] = jnp.concatenate([db1, db2], axis=0)
        for n, g in small_ffn.items():
            small_grads[n][l] = g
        for n, g in small_mix.items():
            small_grads[n][j] = g
        layer_of = {n: (l if n.startswith("ffn") else j) for n in (*big_ffn, *big_mix)}
        big = {**big_mix, **big_ffn}
        names = list(big)
        halves = [big[n][1] for n in names]
        from_sibling = _exchange_sibling([big[n][0] for n in names], halves, name=f"reduce_sibling_l{l}")
        sums = [_add_sibling(big[n][0], r1, place, big[n][1], tr=ADD_TILES[n][0], tc=ADD_TILES[n][1], name=f"l{l}_{n}_add")
                for n, r1 in zip(names, from_sibling)]
        from_chips = _exchange_chips([s[0] for s in sums], name=f"reduce_chips_l{l}")
        for n, (_, own), r2 in zip(names, sums, from_chips):
            results[n] = _adamw_big(own, r2, wt[n], mt[n], vt[n], results[n], layer_of[n],
                                    tr=ADAM_TILES[n][0], tc=ADAM_TILES[n][1], name=f"l{l}_{n}_adamw")
    results = {n: [as_updated(n, a) for a in results[n]] for n in big_names}
    grad_x = dy.reshape(x.shape)

    small_names = [n for n in WEIGHT_NAMES if n not in big_names]
    full_shapes = {"gla_w_gate_up": (2, GLA_GATE_RANK, GLA_DK), "gla_gate_bias": (2, GLA_DK), "gla_norm_g": (2, GLA_HEAD_V),
                   "ffn_conv_w": (DEPTH, 3, 2 * D_FF), "ffn_conv_b": (DEPTH, 2 * D_FF),
                   "ln_g": (DEPTH, 2, D_MODEL), "ln_b": (DEPTH, 2, D_MODEL)}
    shapes = [full_shapes[n] for n in small_names]
    rows = _rows_for(shapes)
    packed = _pack_rows([jnp.stack(small_grads[n]) for n in small_names], rows)
    summed = _sum_gathered(_all_gather([packed], name="gather_small_grads")[0], name="sum_small_grads")
    full = dict(zip(small_names, _unpack_rows(summed, shapes)))
    own = {n: (full[n] if w[n].shape == full[n].shape
               else lax.dynamic_slice_in_dim(full[n], dev * w[n].shape[-1], w[n].shape[-1], axis=full[n].ndim - 1))
           for n in small_names}
    own_shapes = [w[n].shape for n in small_names]
    rows = _rows_for(own_shapes)
    pk = lambda d: _pack_rows([d[n] for n in small_names], rows)
    outs = _adamw_small(pk(own), pk(w), pk(mom), pk(var), name="adamw_small")
    for n in small_names:
        results[n] = [own[n]]
    for k, packed_out in enumerate(outs):
        for n, a in zip(small_names, _unpack_rows(packed_out, own_shapes)):
            results[n].append(a)

    return (loss, grad_x) + tuple(results[n][k] for k in range(4) for n in WEIGHT_NAMES)
```

```python
import functools
import math

import jax
import jax.numpy as jnp
from jax import lax
from jax.experimental import pallas as pl
from jax.experimental.pallas import tpu as pltpu

F32 = jnp.float32
BF16 = jnp.bfloat16

D_MODEL = 2048
SEQ = 2048
DEPTH = 4
N_DEV = 8
GLA_HEADS = 4
GLA_DK = 1024
GLA_DV = 2048
GLA_HEAD_K = 256
GLA_HEAD_V = 512
GLA_GATE_RANK = 16
GLA_GATE_NORMALIZER = 16.0
GLA_CHUNK = 64
GLA_MAIN = 2 * GLA_DK + 2 * GLA_DV
GLA_IN = GLA_MAIN + GLA_GATE_RANK
DIL_PATTERNS = ((128, 1), (512, 4), (2048, 16))
DIL_HEADS = 8
DIL_HEAD_DIM = 128
DIL_WIDTH = DIL_HEADS * DIL_HEAD_DIM
DIL_BLOCK = 128
DIL_IN = 3 * len(DIL_PATTERNS) * DIL_WIDTH
D_FF = 5504
DEEPNORM_ALPHA = (2 * DEPTH) ** 0.25
LN_EPS = 1e-5
RMS_EPS = 1e-6
ADAM_LR = 0.001
ADAM_B1 = 0.9
ADAM_B2 = 0.999
ADAM_EPS = 1e-08
ADAM_WD = 0.01
ADAM_STEP = 10

LANES = 128
SUBLANES = 8
VMEM_LIMIT_BYTES = 56 * 1024 * 1024

FF_SHARD = 2 * D_FF // N_DEV
FF_SHARD_PAD = 1408
FF_HALF_PAD = 4 * FF_SHARD_PAD
GLOW_PAD = LANES


def _cparams(dims=None):
    return pltpu.CompilerParams(dimension_semantics=dims, vmem_limit_bytes=VMEM_LIMIT_BYTES)


class _Comm:
    def __init__(self, inputs, out_shapes, sem_shapes, start, middle, finish):
        self.inputs, self.out_shapes, self.sem_shapes = list(inputs), list(out_shapes), list(sem_shapes)
        self.start, self.middle, self.finish = start, middle, finish


def _join_comm(comms):
    def cut(refs, counts):
        out, off = [], 0
        for c in counts:
            out.append(refs[off:off + c])
            off += c
        return out

    n_in = [len(c.inputs) for c in comms]
    n_out = [len(c.out_shapes) for c in comms]
    n_sem = [len(c.sem_shapes) for c in comms]

    def hook(which):
        def run(ins, outs, sems):
            for c, i, o, s in zip(comms, cut(ins, n_in), cut(outs, n_out), cut(sems, n_sem)):
                getattr(c, which)(i, o, s)
        return run

    return _Comm([a for c in comms for a in c.inputs], [s for c in comms for s in c.out_shapes],
                 [s for c in comms for s in c.sem_shapes], hook("start"), hook("middle"), hook("finish"))


def _mm(a, b, *, grid, a_spec, b_spec, o_spec, out_shape, acc_shape, ta=False, tb=False, res=None, res_scale=1.0,
        comm=None, name):
    nk = grid[2]
    dims = (((0 if ta else 1,), (1 if tb else 0,)), ((), ()))
    has_res = res is not None
    n_in = 2 + has_res
    n_cin = len(comm.inputs) if comm else 0
    n_cout = len(comm.out_shapes) if comm else 0

    def body(*refs):
        a_ref, b_ref = refs[0], refs[1]
        res_ref = refs[2] if has_res else None
        o_ref = refs[n_in + n_cin]
        scratch = refs[n_in + n_cin + 1 + n_cout:]
        acc_ref = scratch[0] if nk > 1 else None
        if comm:
            task = (refs[n_in:n_in + n_cin], refs[n_in + n_cin + 1:n_in + n_cin + 1 + n_cout],
                    scratch[1:] if nk > 1 else scratch)
            step = (pl.program_id(0) * grid[1] + pl.program_id(1)) * nk + pl.program_id(2)
            steps = grid[0] * grid[1] * nk

            @pl.when(step == 0)
            def _():
                comm.start(*task)

            if steps >= 4:
                @pl.when(step == steps // 2)
                def _():
                    comm.middle(*task)

        p = lax.dot_general(a_ref[...], b_ref[...], dims, preferred_element_type=F32)

        def finish(acc):
            if has_res:
                acc = acc + res_scale * res_ref[...]
            o_ref[...] = acc.astype(o_ref.dtype)

        if nk == 1:
            finish(p)
        else:
            kk = pl.program_id(2)

            @pl.when(kk == 0)
            def _():
                acc_ref[...] = p

            @pl.when(kk > 0)
            def _():
                acc_ref[...] += p

            @pl.when(kk == nk - 1)
            def _():
                finish(acc_ref[...])

        if comm:
            @pl.when(step == steps - 1)
            def _():
                if steps < 4:
                    comm.middle(*task)
                comm.finish(*task)

    in_specs = [a_spec, b_spec] + ([o_spec] if has_res else [])
    args = (a, b) + ((res,) if has_res else ())
    acc = [pltpu.VMEM(acc_shape, F32)] if nk > 1 else []
    if not comm:
        return pl.pallas_call(
            body, name=name, grid=grid, in_specs=in_specs, out_specs=o_spec, out_shape=out_shape, scratch_shapes=acc,
            compiler_params=_cparams(("parallel", "parallel", "arbitrary")),
        )(*args)
    outs = pl.pallas_call(
        body, name=name, grid=grid, in_specs=in_specs + [_ANY] * n_cin, out_specs=[o_spec] + [_ANY] * n_cout,
        out_shape=[out_shape] + comm.out_shapes, scratch_shapes=acc + comm.sem_shapes,
        compiler_params=_cparams(("arbitrary", "arbitrary", "arbitrary")),
    )(*args, *comm.inputs)
    return outs[0], list(outs[1:])


def _matmul(a, b, *, ta=False, tb=False, tm, tn, tk, out_dtype=F32, res=None, res_scale=1.0, comm=None, name):
    m, k = (a.shape[1], a.shape[0]) if ta else a.shape
    n = b.shape[0] if tb else b.shape[1]
    assert (b.shape[1] if tb else b.shape[0]) == k
    assert m % tm == 0 and n % tn == 0 and k % tk == 0, (m, n, k, tm, tn, tk)
    a_spec = pl.BlockSpec((tk, tm), lambda i, j, kk: (kk, i)) if ta else pl.BlockSpec((tm, tk), lambda i, j, kk: (i, kk))
    b_spec = pl.BlockSpec((tn, tk), lambda i, j, kk: (j, kk)) if tb else pl.BlockSpec((tk, tn), lambda i, j, kk: (kk, j))
    return _mm(a, b, grid=(m // tm, n // tn, k // tk), a_spec=a_spec, b_spec=b_spec,
               o_spec=pl.BlockSpec((tm, tn), lambda i, j, kk: (i, j)), out_shape=jax.ShapeDtypeStruct((m, n), out_dtype),
               acc_shape=(tm, tn), ta=ta, tb=tb, res=res, res_scale=res_scale, comm=comm, name=name)


MM_ROWS = 1024


def _mm_colblocks(a, wb, *, comm=None, name):
    m, k = a.shape
    nb, _, w = wb.shape
    return _mm(a, wb, grid=(m // MM_ROWS, nb, 1),
               a_spec=pl.BlockSpec((MM_ROWS, k), lambda i, j, kk: (i, 0)),
               b_spec=pl.BlockSpec((None, k, w), lambda i, j, kk: (j, 0, 0)),
               o_spec=pl.BlockSpec((MM_ROWS, w), lambda i, j, kk: (i, j)),
               out_shape=jax.ShapeDtypeStruct((m, nb * w), F32), acc_shape=(MM_ROWS, w), comm=comm, name=name)


def _mm_colblocks_t(a, wb, res, res_scale, *, comm=None, name):
    m = a.shape[0]
    nb, n, w = wb.shape
    tn = 1024
    return _mm(a, wb, grid=(m // MM_ROWS, n // tn, nb),
               a_spec=pl.BlockSpec((MM_ROWS, w), lambda i, j, kk: (i, kk)),
               b_spec=pl.BlockSpec((None, tn, w), lambda i, j, kk: (kk, j, 0)),
               o_spec=pl.BlockSpec((MM_ROWS, tn), lambda i, j, kk: (i, j)),
               out_shape=jax.ShapeDtypeStruct((m, n), F32), acc_shape=(MM_ROWS, tn), tb=True,
               res=res, res_scale=res_scale, comm=comm, name=name)


def _mm_grad_colblocks(x, dy, w, *, comm=None, name):
    t, k = x.shape
    nb = dy.shape[1] // w
    tm, tk = 1024, 2048
    return _mm(x, dy, grid=(k // tm, nb, t // tk),
               a_spec=pl.BlockSpec((tk, tm), lambda i, j, kk: (kk, i)),
               b_spec=pl.BlockSpec((tk, w), lambda i, j, kk: (kk, j)),
               o_spec=pl.BlockSpec((None, tm, w), lambda i, j, kk: (j, i, 0)),
               out_shape=jax.ShapeDtypeStruct((nb, k, w), F32), acc_shape=(tm, w), ta=True, comm=comm, name=name)


def _ffn_hidden(y, wt, *, comm=None, name):
    t, k = y.shape
    ni = t // MM_ROWS
    return _mm(y, wt, grid=(ni, N_DEV, 1),
               a_spec=pl.BlockSpec((MM_ROWS, k), lambda i, j, kk: (i, 0)),
               b_spec=pl.BlockSpec((None, FF_SHARD_PAD, k), lambda i, j, kk: (j, 0, 0)),
               o_spec=pl.BlockSpec((MM_ROWS, FF_SHARD_PAD), lambda i, j, kk: ((j // 4) * ni + i, j % 4)),
               out_shape=jax.ShapeDtypeStruct((2 * t, FF_HALF_PAD), F32), acc_shape=(MM_ROWS, FF_SHARD_PAD), tb=True,
               comm=comm, name=name)


def _ffn_hidden_dy(dh, wt, res, res_scale, *, comm=None, name):
    t = dh.shape[0] // 2
    ni, tn = t // MM_ROWS, 1024
    return _mm(dh, wt, grid=(ni, D_MODEL // tn, N_DEV),
               a_spec=pl.BlockSpec((MM_ROWS, FF_SHARD_PAD), lambda i, j, kk: ((kk // 4) * ni + i, kk % 4)),
               b_spec=pl.BlockSpec((None, FF_SHARD_PAD, tn), lambda i, j, kk: (kk, 0, j)),
               o_spec=pl.BlockSpec((MM_ROWS, tn), lambda i, j, kk: (i, j)),
               out_shape=jax.ShapeDtypeStruct((t, D_MODEL), F32), acc_shape=(MM_ROWS, tn),
               res=res, res_scale=res_scale, comm=comm, name=name)


def _ffn_hidden_dw(dh, y, *, comm=None, name):
    t, k = y.shape
    tk, tn = 2048, 1024
    nk = t // tk
    return _mm(dh, y, grid=(N_DEV, k // tn, nk),
               a_spec=pl.BlockSpec((tk, FF_SHARD_PAD), lambda i, j, kk: ((i // 4) * nk + kk, i % 4)),
               b_spec=pl.BlockSpec((tk, tn), lambda i, j, kk: (kk, j)),
               o_spec=pl.BlockSpec((None, FF_SHARD_PAD, tn), lambda i, j, kk: (i, 0, j)),
               out_shape=jax.ShapeDtypeStruct((N_DEV, FF_SHARD_PAD, k), F32), acc_shape=(FF_SHARD_PAD, tn), ta=True,
               comm=comm, name=name)


def _ffn_down_dw(act, dz, *, comm=None, name):
    t, k = dz.shape
    tk, tn = 2048, 1024
    return _mm(act, dz, grid=(4, k // tn, t // tk),
               a_spec=pl.BlockSpec((tk, FF_SHARD_PAD), lambda i, j, kk: (kk, i)),
               b_spec=pl.BlockSpec((tk, tn), lambda i, j, kk: (kk, j)),
               o_spec=pl.BlockSpec((None, FF_SHARD_PAD, tn), lambda i, j, kk: (i, 0, j)),
               out_shape=jax.ShapeDtypeStruct((4, FF_SHARD_PAD, k), F32), acc_shape=(FF_SHARD_PAD, tn), ta=True,
               comm=comm, name=name)


LN_ROWS = 256


def _ln_fwd(x, f, g, b, *, name):
    t, d = x.shape

    def body(x_ref, f_ref, g_ref, b_ref, y_ref, yb_ref, xh_ref, rs_ref):
        z = DEEPNORM_ALPHA * x_ref[...] + f_ref[...]
        mu = jnp.mean(z, axis=-1, keepdims=True)
        zc = z - mu
        var = jnp.mean(zc * zc, axis=-1, keepdims=True)
        rstd = lax.rsqrt(var + LN_EPS)
        xh = zc * rstd
        y = xh * g_ref[...] + b_ref[...]
        y_ref[...] = y
        yb_ref[...] = y.astype(BF16)
        xh_ref[...] = xh
        rs_ref[...] = rstd

    row = pl.BlockSpec((LN_ROWS, d), lambda i: (i, 0))
    vec = pl.BlockSpec((1, d), lambda i: (0, 0))
    return pl.pallas_call(
        body,
        name=name,
        grid=(t // LN_ROWS,),
        in_specs=[row, row, vec, vec],
        out_specs=[row, row, row, pl.BlockSpec((LN_ROWS, 1), lambda i: (i, 0))],
        out_shape=[jax.ShapeDtypeStruct((t, d), F32), jax.ShapeDtypeStruct((t, d), BF16),
                   jax.ShapeDtypeStruct((t, d), F32), jax.ShapeDtypeStruct((t, 1), F32)],
        compiler_params=_cparams(("parallel",)),
    )(x, f, g, b)


def _ln_bwd(dy, xhat, rstd, g, *, name):
    t, d = dy.shape

    def body(dy_ref, xh_ref, rs_ref, g_ref, dz_ref, dzb_ref, dg_ref, db_ref):
        dyv = dy_ref[...]
        xh = xh_ref[...]
        dyg = dyv * g_ref[...]
        m1 = jnp.mean(dyg, axis=-1, keepdims=True)
        m2 = jnp.mean(dyg * xh, axis=-1, keepdims=True)
        dz = rs_ref[...] * (dyg - m1 - xh * m2)
        dz_ref[...] = dz
        dzb_ref[...] = dz.astype(BF16)
        dg_part = jnp.sum(dyv * xh, axis=0, keepdims=True)
        db_part = jnp.sum(dyv, axis=0, keepdims=True)

        @pl.when(pl.program_id(0) == 0)
        def _():
            dg_ref[...] = dg_part
            db_ref[...] = db_part

        @pl.when(pl.program_id(0) > 0)
        def _():
            dg_ref[...] += dg_part
            db_ref[...] += db_part

    row = pl.BlockSpec((LN_ROWS, d), lambda i: (i, 0))
    vec = pl.BlockSpec((1, d), lambda i: (0, 0))
    return pl.pallas_call(
        body,
        name=name,
        grid=(t // LN_ROWS,),
        in_specs=[row, row, pl.BlockSpec((LN_ROWS, 1), lambda i: (i, 0)), vec],
        out_specs=[row, row, vec, vec],
        out_shape=[jax.ShapeDtypeStruct((t, d), F32), jax.ShapeDtypeStruct((t, d), BF16),
                   jax.ShapeDtypeStruct((1, d), F32), jax.ShapeDtypeStruct((1, d), F32)],
        compiler_params=_cparams(("arbitrary",)),
    )(dy, xhat, rstd, g)


def _loss_fwd_bwd(y, target, *, name):
    t, d = y.shape

    def body(y_ref, t_ref, loss_ref, dy_ref):
        err = y_ref[...] - t_ref[...]
        dy_ref[...] = err * (1.0 / d)
        part = 0.5 * jnp.sum(jnp.mean(err * err, axis=-1, keepdims=True), axis=0, keepdims=True)

        @pl.when(pl.program_id(0) == 0)
        def _():
            loss_ref[...] = part

        @pl.when(pl.program_id(0) > 0)
        def _():
            loss_ref[...] += part

    row = pl.BlockSpec((LN_ROWS, d), lambda i: (i, 0))
    return pl.pallas_call(
        body,
        name=name,
        grid=(t // LN_ROWS,),
        in_specs=[row, row],
        out_specs=[pl.BlockSpec((1, 1), lambda i: (0, 0)), row],
        out_shape=[jax.ShapeDtypeStruct((1, 1), F32), jax.ShapeDtypeStruct((t, d), F32)],
        compiler_params=_cparams(("arbitrary",)),
    )(y, target)


FFN_COLS = 256


def _shift_rows(h, s):
    rows = lax.broadcasted_iota(jnp.int32, h.shape, 0)
    return jnp.where(rows >= s, pltpu.roll(h, s, 0), 0.0)


def _shift_rows_up(h, s):
    n = h.shape[0]
    rows = lax.broadcasted_iota(jnp.int32, h.shape, 0)
    return jnp.where(rows < n - s, pltpu.roll(h, n - s, 0), 0.0)


def _causal_conv(h, w, b):
    return w[0:1, :] * _shift_rows(h, 2) + w[1:2, :] * _shift_rows(h, 1) + w[2:3, :] * h + b


def _sigmoid(x):
    return 1.0 / (1.0 + jnp.exp(-x))


def _convgate_fwd(h, cw, cb, *, name):
    t, n = h.shape[0] // 2, h.shape[1]
    nb = t // SEQ

    def body(hg_ref, hu_ref, wg_ref, wu_ref, bg_ref, bu_ref, a_ref):
        gate = _causal_conv(hg_ref[...], wg_ref[...], bg_ref[...])
        up = _causal_conv(hu_ref[...], wu_ref[...], bu_ref[...])
        a_ref[...] = (gate * _sigmoid(gate) * up).astype(BF16)

    def half(rows, k):
        return pl.BlockSpec((None, rows, FFN_COLS), lambda s, j: (k, 0, j))

    return pl.pallas_call(
        body,
        name=name,
        grid=(nb, n // FFN_COLS),
        in_specs=[pl.BlockSpec((SEQ, FFN_COLS), lambda s, j: (s, j)), pl.BlockSpec((SEQ, FFN_COLS), lambda s, j: (nb + s, j)),
                  half(3, 0), half(3, 1), half(1, 0), half(1, 1)],
        out_specs=pl.BlockSpec((SEQ, FFN_COLS), lambda s, j: (s, j)),
        out_shape=jax.ShapeDtypeStruct((t, n), BF16),
        compiler_params=_cparams(("parallel", "parallel")),
    )(h, h, cw, cw, cb, cb)


def _convgate_bwd(h, dact, cw, cb, *, name):
    t, n = h.shape[0] // 2, h.shape[1]
    nb = t // SEQ

    def body(hg_ref, hu_ref, da_ref, wg_ref, wu_ref, bg_ref, bu_ref, dh_ref, dw_ref, db_ref):
        dhg_ref, dhu_ref = dh_ref.at[0], dh_ref.at[1]
        dwg_ref, dwu_ref = dw_ref.at[0], dw_ref.at[1]
        dbg_ref, dbu_ref = db_ref.at[0], db_ref.at[1]
        hgv, huv = hg_ref[...], hu_ref[...]
        wgv, wuv = wg_ref[...], wu_ref[...]
        gate = _causal_conv(hgv, wgv, bg_ref[...])
        up = _causal_conv(huv, wuv, bu_ref[...])
        sg = _sigmoid(gate)
        da = da_ref[...]
        dgate = da * up * (sg * (1.0 + gate * (1.0 - sg)))
        dup = da * (gate * sg)

        def conv_bwd(dc, h, w, dh_ref, dw_ref, db_ref):
            dh = w[2:3, :] * dc + w[1:2, :] * _shift_rows_up(dc, 1) + w[0:1, :] * _shift_rows_up(dc, 2)
            dh_ref[...] = dh.astype(BF16)
            dws = [jnp.sum(dc * _shift_rows(h, 2), axis=0, keepdims=True),
                   jnp.sum(dc * _shift_rows(h, 1), axis=0, keepdims=True),
                   jnp.sum(dc * h, axis=0, keepdims=True)]
            db = jnp.sum(dc, axis=0, keepdims=True)

            @pl.when(pl.program_id(1) == 0)
            def _():
                for r in range(3):
                    dw_ref[r:r + 1, :] = dws[r]
                db_ref[...] = db

            @pl.when(pl.program_id(1) > 0)
            def _():
                for r in range(3):
                    dw_ref[r:r + 1, :] += dws[r]
                db_ref[...] += db

        conv_bwd(dgate, hgv, wgv, dhg_ref, dwg_ref, dbg_ref)
        conv_bwd(dup, huv, wuv, dhu_ref, dwu_ref, dbu_ref)

    def half(rows, k):
        return pl.BlockSpec((None, rows, FFN_COLS), lambda j, s: (k, 0, j))

    def both(rows):
        return pl.BlockSpec((2, rows, FFN_COLS), lambda j, s: (0, 0, j))

    return pl.pallas_call(
        body,
        name=name,
        grid=(n // FFN_COLS, nb),
        in_specs=[pl.BlockSpec((SEQ, FFN_COLS), lambda j, s: (s, j)), pl.BlockSpec((SEQ, FFN_COLS), lambda j, s: (nb + s, j)),
                  pl.BlockSpec((SEQ, FFN_COLS), lambda j, s: (s, j)), half(3, 0), half(3, 1), half(1, 0), half(1, 1)],
        out_specs=[pl.BlockSpec((2, SEQ, FFN_COLS), lambda j, s: (0, s, j)), both(3), both(1)],
        out_shape=[jax.ShapeDtypeStruct((2, t, n), BF16), jax.ShapeDtypeStruct((2, 3, n), F32),
                   jax.ShapeDtypeStruct((2, 1, n), F32)],
        compiler_params=_cparams(("parallel", "arbitrary")),
    )(h, h, dact, cw, cw, cb, cb)


GLA_Q_SCALE = GLA_HEAD_K ** -0.5
GLA_NC = SEQ // GLA_CHUNK
_NT = (((1,), (1,)), ((), ()))
_TN = (((0,), (0,)), ((), ()))


def _cumsum_rows(g):
    n = g.shape[0]
    rows = lax.broadcasted_iota(jnp.int32, g.shape, 0)
    s = 1
    while s < n:
        g = g + jnp.where(rows >= s, pltpu.roll(g, s, 0), 0.0)
        s *= 2
    return g


def _suffix_sum_rows(x):
    n = x.shape[0]
    rows = lax.broadcasted_iota(jnp.int32, x.shape, 0)
    s = 1
    while s < n:
        x = x + jnp.where(rows < n - s, pltpu.roll(x, n - s, 0), 0.0)
        s *= 2
    return x


def _gla_log_gate(gl_ref, wgu_ref, bias_ref):
    pre = jnp.dot(gl_ref[...].astype(BF16), wgu_ref[...], preferred_element_type=F32) + bias_ref[...]
    log_sig = jnp.minimum(pre, 0.0) - jnp.log(1.0 + jnp.exp(-jnp.abs(pre)))
    return pre, log_sig * (1.0 / GLA_GATE_NORMALIZER)


def _pair_rows(j):
    return (j // SUBLANES) * SUBLANES


def _gla_pair_fwd(q_ref, k_ref, b_scr, a_scr, h):
    c = GLA_CHUNK
    kc = pl.ds(h * GLA_HEAD_K, GLA_HEAD_K)
    a_scr[...] = jnp.zeros(a_scr.shape, F32)
    lane = lax.broadcasted_iota(jnp.int32, (1, c), 1)
    for j in range(c):
        r0 = _pair_rows(j)
        rs = pl.ds(r0, c - r0)
        rows = lax.broadcasted_iota(jnp.int32, (c - r0, 1), 0) + r0
        e = jnp.exp(jnp.minimum(b_scr[rs, kc] - b_scr[pl.ds(j, 1), kc], 0.0))
        w = (q_ref[rs, kc] * GLA_Q_SCALE) * k_ref[pl.ds(j, 1), kc] * e
        col = jnp.where(rows >= j, jnp.sum(w, axis=-1, keepdims=True), 0.0)
        a_scr[rs, :] += col * (lane == j).astype(F32)


def _gla_fwd(proj, wgu, bias, ng, *, name):
    t = proj.shape[0]
    nb, nc, c = t // SEQ, GLA_NC, GLA_CHUNK

    def body(q_ref, k_ref, v_ref, r_ref, gl_ref, wgu_ref, bias_ref, ng_ref,
             y_ref, o_ref, a_ref, st_ref, state, b_scr, a_scr):
        @pl.when(pl.program_id(1) == 0)
        def _():
            state[...] = jnp.zeros(state.shape, F32)

        _, g = _gla_log_gate(gl_ref, wgu_ref, bias_ref)
        b_scr[...] = _cumsum_rows(g)
        for h in range(GLA_HEADS):
            kc = pl.ds(h * GLA_HEAD_K, GLA_HEAD_K)
            vc = pl.ds(h * GLA_HEAD_V, GLA_HEAD_V)
            qh = q_ref[:, kc] * GLA_Q_SCALE
            kh = k_ref[:, kc]
            vh = v_ref[:, vc].astype(BF16)
            bh = b_scr[:, kc]
            blast = b_scr[pl.ds(c - 1, 1), kc]
            st = state[h]
            st_ref[h] = st
            o_inter = lax.dot_general((qh * jnp.exp(bh)).astype(BF16), st.astype(BF16), _NT, preferred_element_type=F32)
            _gla_pair_fwd(q_ref, k_ref, b_scr, a_scr, h)
            a = a_scr[...]
            a_ref[h] = a
            o = o_inter + jnp.dot(a.astype(BF16), vh, preferred_element_type=F32)
            kd = (kh * jnp.exp(blast - bh)).astype(BF16)
            state[h] = st * jnp.exp(blast) + lax.dot_general(vh, kd, _TN, preferred_element_type=F32)
            o_ref[:, vc] = o
            rs = lax.rsqrt(jnp.mean(o * o, axis=-1, keepdims=True) + RMS_EPS)
            rh = r_ref[:, vc]
            y_ref[:, vc] = ((o * rs * ng_ref[...]) * (rh * _sigmoid(rh))).astype(BF16)

    def tok(width, col):
        return pl.BlockSpec((c, width), lambda b, i: (b * nc + i, col))

    whole = lambda shape: pl.BlockSpec(shape, lambda b, i: (0,) * len(shape))
    return pl.pallas_call(
        body,
        name=name,
        grid=(nb, nc),
        in_specs=[tok(GLA_DK, 0), tok(GLA_DK, 1), tok(GLA_DV, 1), tok(GLA_DV, 2), tok(GLOW_PAD, GLA_MAIN // GLOW_PAD),
                  whole((GLOW_PAD, GLA_DK)), whole((1, GLA_DK)), whole((1, GLA_HEAD_V))],
        out_specs=[tok(GLA_DV, 0), tok(GLA_DV, 0),
                   pl.BlockSpec((GLA_HEADS, c, c), lambda b, i: (0, b * nc + i, 0)),
                   pl.BlockSpec((None, GLA_HEADS, GLA_HEAD_V, GLA_HEAD_K), lambda b, i: (b * nc + i, 0, 0, 0))],
        out_shape=[jax.ShapeDtypeStruct((t, GLA_DV), BF16), jax.ShapeDtypeStruct((t, GLA_DV), F32),
                   jax.ShapeDtypeStruct((GLA_HEADS, t, c), F32),
                   jax.ShapeDtypeStruct((t // c, GLA_HEADS, GLA_HEAD_V, GLA_HEAD_K), F32)],
        scratch_shapes=[pltpu.VMEM((GLA_HEADS, GLA_HEAD_V, GLA_HEAD_K), F32), pltpu.VMEM((c, GLA_DK), F32),
                        pltpu.VMEM((c, c), F32)],
        compiler_params=_cparams(("parallel", "arbitrary")),
    )(proj, proj, proj, proj, proj, wgu, bias, ng)


def _gla_pair_bwd(q_ref, k_ref, b_scr, da_scr, dq_scr, dk_scr, h):
    c = GLA_CHUNK
    kc = pl.ds(h * GLA_HEAD_K, GLA_HEAD_K)
    lane = lax.broadcasted_iota(jnp.int32, (1, c), 1)
    for j in range(c):
        r0 = _pair_rows(j)
        rs = pl.ds(r0, c - r0)
        rows = lax.broadcasted_iota(jnp.int32, (c - r0, 1), 0) + r0
        e = jnp.exp(jnp.minimum(b_scr[rs, kc] - b_scr[pl.ds(j, 1), kc], 0.0))
        dacol = jnp.sum(jnp.where(lane == j, da_scr[rs, :], 0.0), axis=-1, keepdims=True)
        t1 = jnp.where(rows >= j, dacol, 0.0) * e
        dq_scr[rs, kc] += t1 * k_ref[pl.ds(j, 1), kc]
        dk_scr[pl.ds(j, 1), kc] += jnp.sum(t1 * (q_ref[rs, kc] * GLA_Q_SCALE), axis=0, keepdims=True)


def _gla_bwd(proj, wgu, bias, ng, o, a, states, dy, *, name):
    t = proj.shape[0]
    nb, nc, c = t // SEQ, GLA_NC, GLA_CHUNK

    def body(q_ref, k_ref, v_ref, r_ref, gl_ref, wgu_ref, bias_ref, ng_ref, o_ref, a_ref, stp_ref, stn_ref, dy_ref,
             dq_ref, dk_ref, dv_ref, dr_ref, dgl_ref, dwgu_ref, dbias_ref, dng_ref,
             dstate, b_scr, da_scr, dq_scr, dk_scr, dg_scr):
        first = jnp.logical_and(pl.program_id(0) == 0, pl.program_id(1) == 0)

        @pl.when(first)
        def _():
            dwgu_ref[...] = jnp.zeros(dwgu_ref.shape, F32)
            dbias_ref[...] = jnp.zeros(dbias_ref.shape, F32)
            dng_ref[...] = jnp.zeros(dng_ref.shape, F32)

        @pl.when(pl.program_id(1) == 0)
        def _():
            dstate[...] = jnp.zeros(dstate.shape, F32)

        pre, g = _gla_log_gate(gl_ref, wgu_ref, bias_ref)
        b_scr[...] = _cumsum_rows(g)
        ngv = ng_ref[...]
        tri = lax.broadcasted_iota(jnp.int32, (c, c), 0) >= lax.broadcasted_iota(jnp.int32, (c, c), 1)
        for h in range(GLA_HEADS):
            kc = pl.ds(h * GLA_HEAD_K, GLA_HEAD_K)
            vc = pl.ds(h * GLA_HEAD_V, GLA_HEAD_V)
            oh = o_ref[:, vc]
            rh = r_ref[:, vc]
            dyh = dy_ref[:, vc]
            rs = lax.rsqrt(jnp.mean(oh * oh, axis=-1, keepdims=True) + RMS_EPS)
            u = oh * rs
            sg = _sigmoid(rh)
            sr = rh * sg
            dr_ref[:, vc] = (dyh * (u * ngv) * (sg * (1.0 + rh * (1.0 - sg)))).astype(BF16)
            dng_ref[...] += jnp.sum(dyh * sr * u, axis=0, keepdims=True)
            du = dyh * sr * ngv
            do = (rs * (du - u * jnp.mean(du * u, axis=-1, keepdims=True))).astype(BF16)
            qh = q_ref[:, kc] * GLA_Q_SCALE
            kh = k_ref[:, kc]
            vh = v_ref[:, vc].astype(BF16)
            bh = b_scr[:, kc]
            blast = b_scr[pl.ds(c - 1, 1), kc]
            eb = jnp.exp(bh)
            ek = jnp.exp(blast - bh)
            dst = dstate[h]
            dst_b = dst.astype(BF16)
            dg_carry = jnp.sum(dst * stn_ref[h], axis=0, keepdims=True)
            da = lax.dot_general(do, vh, _NT, preferred_element_type=F32)
            da_scr[...] = jnp.where(tri, da, 0.0)
            dv = lax.dot_general(a_ref[h].astype(BF16), do, _TN, preferred_element_type=F32)
            dv = dv + lax.dot_general((kh * ek).astype(BF16), dst_b, _NT, preferred_element_type=F32)
            dv_ref[:, vc] = dv.astype(BF16)
            dq_scr[:, kc] = jnp.dot(do, stp_ref[h].astype(BF16), preferred_element_type=F32) * eb
            dk_scr[:, kc] = jnp.dot(vh, dst_b, preferred_element_type=F32) * ek
            _gla_pair_bwd(q_ref, k_ref, b_scr, da_scr, dq_scr, dk_scr, h)
            dq = dq_scr[:, kc]
            dk = dk_scr[:, kc]
            dg_scr[:, kc] = _suffix_sum_rows(qh * dq - kh * dk) + dg_carry
            dstate[h] = dst * jnp.exp(blast) + lax.dot_general(do, (qh * eb).astype(BF16), _TN, preferred_element_type=F32)
        dq_ref[...] = (dq_scr[...] * GLA_Q_SCALE).astype(BF16)
        dk_ref[...] = dk_scr[...].astype(BF16)
        dpre = dg_scr[...] * ((1.0 - _sigmoid(pre)) * (1.0 / GLA_GATE_NORMALIZER))
        dpre_b = dpre.astype(BF16)
        dbias_ref[...] += jnp.sum(dpre, axis=0, keepdims=True)
        dwgu_ref[...] += lax.dot_general(gl_ref[...].astype(BF16), dpre_b, _TN, preferred_element_type=F32)
        dgl_ref[...] = lax.dot_general(dpre_b, wgu_ref[...], _NT, preferred_element_type=F32).astype(BF16)

    def chunk(b, i):
        return b * nc + (nc - 1 - i)

    def tok(width, col):
        return pl.BlockSpec((c, width), lambda b, i: (chunk(b, i), col))

    whole = lambda shape: pl.BlockSpec(shape, lambda b, i: (0,) * len(shape))
    st_shape = (None, GLA_HEADS, GLA_HEAD_V, GLA_HEAD_K)
    return pl.pallas_call(
        body,
        name=name,
        grid=(nb, nc),
        in_specs=[tok(GLA_DK, 0), tok(GLA_DK, 1), tok(GLA_DV, 1), tok(GLA_DV, 2), tok(GLOW_PAD, GLA_MAIN // GLOW_PAD),
                  whole((GLOW_PAD, GLA_DK)), whole((1, GLA_DK)), whole((1, GLA_HEAD_V)),
                  tok(GLA_DV, 0),
                  pl.BlockSpec((GLA_HEADS, c, c), lambda b, i: (0, chunk(b, i), 0)),
                  pl.BlockSpec(st_shape, lambda b, i: (chunk(b, i), 0, 0, 0)),
                  pl.BlockSpec(st_shape, lambda b, i: (b * nc + jnp.minimum(nc - i, nc - 1), 0, 0, 0)),
                  tok(GLA_DV, 0)],
        out_specs=[tok(GLA_DK, 0), tok(GLA_DK, 0), tok(GLA_DV, 0), tok(GLA_DV, 0), tok(GLOW_PAD, 0),
                   whole((GLOW_PAD, GLA_DK)), whole((1, GLA_DK)), whole((1, GLA_HEAD_V))],
        out_shape=[jax.ShapeDtypeStruct((t, GLA_DK), BF16), jax.ShapeDtypeStruct((t, GLA_DK), BF16),
                   jax.ShapeDtypeStruct((t, GLA_DV), BF16), jax.ShapeDtypeStruct((t, GLA_DV), BF16),
                   jax.ShapeDtypeStruct((t, GLOW_PAD), BF16),
                   jax.ShapeDtypeStruct((GLOW_PAD, GLA_DK), F32), jax.ShapeDtypeStruct((1, GLA_DK), F32),
                   jax.ShapeDtypeStruct((1, GLA_HEAD_V), F32)],
        scratch_shapes=[pltpu.VMEM((GLA_HEADS, GLA_HEAD_V, GLA_HEAD_K), F32), pltpu.VMEM((c, GLA_DK), F32),
                        pltpu.VMEM((c, c), F32), pltpu.VMEM((c, GLA_DK), F32), pltpu.VMEM((c, GLA_DK), F32),
                        pltpu.VMEM((c, GLA_DK), F32)],
        compiler_params=_cparams(("arbitrary", "arbitrary")),
    )(proj, proj, proj, proj, proj, wgu, bias, ng, o, a, states, states, dy)


DIL_STEPS = DIL_BLOCK
DIL_SCALE = DIL_HEAD_DIM ** -0.5
DIL_COLS = DIL_IN // DIL_WIDTH


def _dil_mask(i):
    rowi = lax.broadcasted_iota(jnp.int32, (DIL_BLOCK, 2 * DIL_BLOCK), 0)
    colj = lax.broadcasted_iota(jnp.int32, (DIL_BLOCK, 2 * DIL_BLOCK), 1)
    dist = rowi + DIL_BLOCK - colj
    band = jnp.logical_and(dist >= 0, dist <= DIL_STEPS)
    return jnp.logical_and(band, jnp.logical_or(i > 0, colj >= DIL_BLOCK))


def _dil_geometry(t, gi):
    _, d = DIL_PATTERNS[gi]
    length = SEQ // d
    return d, length, length // DIL_BLOCK, t // SEQ


def _dil_in_specs(gi, d, nq, clamp):
    def spec(j, prev):
        def index(b, r, i):
            ic = clamp(i)
            return (b * nq + (jnp.maximum(ic - 1, 0) if prev else ic), r * DIL_COLS + gi * 3 + j)
        return pl.BlockSpec((DIL_BLOCK, DIL_WIDTH), index)
    return [spec(0, False), spec(1, False), spec(1, True), spec(2, False), spec(2, True)]


def _dil_fwd(proj, gi, *, name):
    t = proj.shape[0]
    d, length, nq, nb = _dil_geometry(t, gi)
    pv = proj.reshape(nb * length, d * DIL_IN)

    def body(q_ref, kc_ref, kp_ref, vc_ref, vp_ref, o_ref, lse_ref):
        mask = _dil_mask(pl.program_id(2))
        for h in range(DIL_HEADS):
            hc = pl.ds(h * DIL_HEAD_DIM, DIL_HEAD_DIM)
            qh = q_ref[:, hc].astype(BF16)
            kcat = jnp.concatenate([kp_ref[:, hc], kc_ref[:, hc]], axis=0).astype(BF16)
            vcat = jnp.concatenate([vp_ref[:, hc], vc_ref[:, hc]], axis=0).astype(BF16)
            s = lax.dot_general(qh, kcat, _NT, preferred_element_type=F32) * DIL_SCALE
            s = jnp.where(mask, s, -jnp.inf)
            m = jnp.max(s, axis=-1, keepdims=True)
            p = jnp.exp(s - m)
            l = jnp.sum(p, axis=-1, keepdims=True)
            o_ref[:, hc] = jnp.dot((p / l).astype(BF16), vcat, preferred_element_type=F32)
            lse_ref[:, hc] = jnp.broadcast_to(m + jnp.log(l), (DIL_BLOCK, DIL_HEAD_DIM))

    out_spec = pl.BlockSpec((DIL_BLOCK, DIL_WIDTH), lambda b, r, i: (b * nq + i, r))
    o, lse = pl.pallas_call(
        body,
        name=name,
        grid=(nb, d, nq),
        in_specs=_dil_in_specs(gi, d, nq, lambda i: i),
        out_specs=[out_spec, out_spec],
        out_shape=[jax.ShapeDtypeStruct((nb * length, d * DIL_WIDTH), F32)] * 2,
        compiler_params=_cparams(("parallel", "parallel", "parallel")),
    )(pv, pv, pv, pv, pv)
    return o.reshape(t, DIL_WIDTH), lse.reshape(t, DIL_WIDTH)


def _dil_bwd(proj, gi, lse, do, delta, *, name):
    t = proj.shape[0]
    d, length, nq, nb = _dil_geometry(t, gi)
    pv = proj.reshape(nb * length, d * DIL_IN)
    view = lambda a: a.reshape(nb * length, d * DIL_WIDTH)

    def body(q_ref, kc_ref, kp_ref, vc_ref, vp_ref, lse_ref, do_ref, dl_ref, dq_ref, dk_ref, dv_ref, ck, cv):
        i = pl.program_id(2)

        @pl.when(i < nq)
        def _():
            mask = _dil_mask(i)
            for h in range(DIL_HEADS):
                hc = pl.ds(h * DIL_HEAD_DIM, DIL_HEAD_DIM)
                h1 = pl.ds(h * DIL_HEAD_DIM, 1)
                qh = q_ref[:, hc].astype(BF16)
                kcat = jnp.concatenate([kp_ref[:, hc], kc_ref[:, hc]], axis=0).astype(BF16)
                vcat = jnp.concatenate([vp_ref[:, hc], vc_ref[:, hc]], axis=0).astype(BF16)
                doh = do_ref[:, hc]
                s = lax.dot_general(qh, kcat, _NT, preferred_element_type=F32) * DIL_SCALE
                p = jnp.exp(jnp.where(mask, s, -jnp.inf) - lse_ref[:, h1])
                dp = lax.dot_general(doh, vcat, _NT, preferred_element_type=F32)
                ds = (p * (dp + dl_ref[:, h1]) * DIL_SCALE).astype(BF16)
                dq_ref[:, hc] = jnp.dot(ds, kcat, preferred_element_type=F32).astype(BF16)
                dkcat = lax.dot_general(ds, qh, _TN, preferred_element_type=F32)
                dvcat = lax.dot_general(p.astype(BF16), doh, _TN, preferred_element_type=F32)

                @pl.when(i > 0)
                def _():
                    dk_ref[:, hc] = (ck[:, hc] + dkcat[:DIL_BLOCK]).astype(BF16)
                    dv_ref[:, hc] = (cv[:, hc] + dvcat[:DIL_BLOCK]).astype(BF16)

                ck[:, hc] = dkcat[DIL_BLOCK:]
                cv[:, hc] = dvcat[DIL_BLOCK:]

        @pl.when(i == nq)
        def _():
            dk_ref[...] = ck[...].astype(BF16)
            dv_ref[...] = cv[...].astype(BF16)

    clamp = lambda i: jnp.minimum(i, nq - 1)
    cur = pl.BlockSpec((DIL_BLOCK, DIL_WIDTH), lambda b, r, i: (b * nq + clamp(i), r))
    done = pl.BlockSpec((DIL_BLOCK, DIL_WIDTH), lambda b, r, i: (b * nq + jnp.maximum(i - 1, 0), r))
    shape = jax.ShapeDtypeStruct((nb * length, d * DIL_WIDTH), BF16)
    dq, dk, dv = pl.pallas_call(
        body,
        name=name,
        grid=(nb, d, nq + 1),
        in_specs=_dil_in_specs(gi, d, nq, clamp) + [cur, cur, cur],
        out_specs=[cur, done, done],
        out_shape=[shape, shape, shape],
        scratch_shapes=[pltpu.VMEM((DIL_BLOCK, DIL_WIDTH), F32), pltpu.VMEM((DIL_BLOCK, DIL_WIDTH), F32)],
        compiler_params=_cparams(("parallel", "parallel", "arbitrary")),
    )(pv, pv, pv, pv, pv, view(lse), view(do), view(delta))
    return dq.reshape(t, DIL_WIDTH), dk.reshape(t, DIL_WIDTH), dv.reshape(t, DIL_WIDTH)


MIX_ROWS = 256


def _head_rowsum(x):
    parts = []
    for h in range(DIL_HEADS):
        s = jnp.sum(x[:, h * DIL_HEAD_DIM:(h + 1) * DIL_HEAD_DIM], axis=-1, keepdims=True)
        parts.append(jnp.broadcast_to(s, (x.shape[0], DIL_HEAD_DIM)))
    return jnp.concatenate(parts, axis=-1)


def _mix_weights(lse_refs):
    ls = [r[...] for r in lse_refs]
    m = jnp.maximum(jnp.maximum(ls[0], ls[1]), ls[2])
    es = [jnp.exp(l - m) for l in ls]
    inv = 1.0 / (es[0] + es[1] + es[2])
    return [e * inv for e in es]


def _dil_mix_fwd(os_, lses, *, name):
    t = os_[0].shape[0]

    def body(o0, o1, o2, l0, l1, l2, out_ref):
        w = _mix_weights((l0, l1, l2))
        out_ref[...] = (w[0] * o0[...] + w[1] * o1[...] + w[2] * o2[...]).astype(BF16)

    row = pl.BlockSpec((MIX_ROWS, DIL_WIDTH), lambda i: (i, 0))
    return pl.pallas_call(
        body, name=name, grid=(t // MIX_ROWS,), in_specs=[row] * 6, out_specs=row,
        out_shape=jax.ShapeDtypeStruct((t, DIL_WIDTH), BF16), compiler_params=_cparams(("parallel",)),
    )(*os_, *lses)


def _dil_mix_bwd(os_, lses, dout, *, name):
    t = os_[0].shape[0]

    def body(o0, o1, o2, l0, l1, l2, d_ref, do0, do1, do2, dl0, dl1, dl2):
        w = _mix_weights((l0, l1, l2))
        dv = d_ref[...]
        mix = w[0] * o0[...] + w[1] * o1[...] + w[2] * o2[...]
        bar = _head_rowsum(dv * mix)
        for wg, do_ref, dl_ref in zip(w, (do0, do1, do2), (dl0, dl1, dl2)):
            do_ref[...] = (wg * dv).astype(BF16)
            dl_ref[...] = -wg * bar

    row = pl.BlockSpec((MIX_ROWS, DIL_WIDTH), lambda i: (i, 0))
    outs = pl.pallas_call(
        body, name=name, grid=(t // MIX_ROWS,), in_specs=[row] * 7, out_specs=[row] * 6,
        out_shape=[jax.ShapeDtypeStruct((t, DIL_WIDTH), BF16)] * 3 + [jax.ShapeDtypeStruct((t, DIL_WIDTH), F32)] * 3,
        compiler_params=_cparams(("parallel",)),
    )(*os_, *lses, dout)
    return outs[:3], outs[3:]


_MESH = pl.DeviceIdType.MESH
_ANY = pl.BlockSpec(memory_space=pl.ANY)


def _position():
    return lax.axis_index("x"), lax.axis_index("y"), lax.axis_index("c")


AG_COPIES = 7


def _run_comm(task, *, name):
    n_in, n_out = len(task.inputs), len(task.out_shapes)

    def body(*refs):
        parts = (refs[:n_in], refs[n_in:n_in + n_out], refs[n_in + n_out:])
        task.start(*parts)
        task.middle(*parts)
        task.finish(*parts)

    return pl.pallas_call(
        body, name=name, out_shape=task.out_shapes, in_specs=[_ANY] * n_in, out_specs=[_ANY] * n_out,
        scratch_shapes=task.sem_shapes,
    )(*task.inputs)


def _gather_task(shards):
    n = len(shards)

    def copies(x_refs, out_refs, sems):
        send_sems, recv_sems, local_sems = sems
        x, y, cc = _position()
        me, sibling = (x, y, cc), (x, y, 1 - cc)
        chips = [(1 - x, y), (x, 1 - y), (1 - x, 1 - y)]

        def copy(w, k, block, to, own=False):
            px, py, pc = block
            slot = out_refs[w].at[4 * px + 2 * py + pc]
            return pltpu.make_async_remote_copy(
                src_ref=x_refs[w] if own else slot, dst_ref=slot,
                send_sem=send_sems.at[AG_COPIES * w + k], recv_sem=recv_sems.at[AG_COPIES * w + k],
                device_id=to, device_id_type=_MESH)

        mine = [pltpu.make_async_copy(x_refs[w], out_refs[w].at[4 * x + 2 * y + cc], local_sems.at[w]) for w in range(n)]
        first = [[copy(w, 0, me, sibling, own=True)] + [copy(w, 1 + j, me, (*chip, cc), own=True) for j, chip in enumerate(chips)]
                 for w in range(n)]
        landed = [[copy(w, 1 + j, (*chip, cc), me) for j, chip in enumerate(chips)] for w in range(n)]
        passed = [[copy(w, 4 + j, (*chip, cc), sibling) for j, chip in enumerate(chips)] for w in range(n)]
        from_sibling = [[copy(w, 0, sibling, me)] + [copy(w, 4 + j, (*chip, 1 - cc), me) for j, chip in enumerate(chips)]
                        for w in range(n)]
        return mine, first, landed, passed, from_sibling

    def start(ins, outs, sems):
        mine, first, _, _, _ = copies(ins, outs, sems)
        for w in range(n):
            mine[w].start()
            for cp in first[w]:
                cp.start()

    def middle(ins, outs, sems):
        _, _, landed, passed, _ = copies(ins, outs, sems)
        for j in range(3):
            for w in range(n):
                landed[w][j].wait_recv()
                passed[w][j].start()

    def finish(ins, outs, sems):
        mine, first, _, passed, from_sibling = copies(ins, outs, sems)
        for w in range(n):
            for cp in from_sibling[w]:
                cp.wait_recv()
        for w in range(n):
            for cp in first[w] + passed[w]:
                cp.wait_send()
            mine[w].wait()

    return _Comm(shards, [jax.ShapeDtypeStruct((N_DEV,) + s.shape, s.dtype) for s in shards],
                 [pltpu.SemaphoreType.DMA((AG_COPIES * n,)), pltpu.SemaphoreType.DMA((AG_COPIES * n,)),
                  pltpu.SemaphoreType.DMA((n,))], start, middle, finish)


def _parity_half(ref, parity, half_rows):
    if half_rows is None:
        return ref.at[:, parity]
    return ref.at[:, pl.ds(parity * half_rows, half_rows), :]


def _exchange_task(make_copies, inputs, out_shapes, n_copies):
    def start(ins, outs, sems):
        for cp in make_copies(ins, outs, sems):
            cp.start()

    def finish(ins, outs, sems):
        for cp in make_copies(ins, outs, sems):
            cp.wait()

    return _Comm(inputs, out_shapes, [pltpu.SemaphoreType.DMA((n_copies,)), pltpu.SemaphoreType.DMA((n_copies,))],
                 start, lambda ins, outs, sems: None, finish)


def _sibling_task(gs, half_rows):
    n = len(gs)

    def make_copies(g_refs, out_refs, sems):
        x, y, cc = _position()
        return [pltpu.make_async_remote_copy(
            src_ref=_parity_half(g_refs[k], 1 - cc, half_rows[k]), dst_ref=out_refs[k],
            send_sem=sems[0].at[k], recv_sem=sems[1].at[k],
            device_id=(x, y, 1 - cc), device_id_type=_MESH) for k in range(n)]

    def out_shape(g, hr):
        return jax.ShapeDtypeStruct((4,) + (g.shape[2:] if hr is None else (hr, g.shape[2])), g.dtype)

    return _exchange_task(make_copies, gs, [out_shape(g, hr) for g, hr in zip(gs, half_rows)], n)


def _chips_task(ps):
    n = len(ps)

    def make_copies(p_refs, out_refs, sems):
        x, y, cc = _position()
        copies = []
        for w in range(n):
            for k in (1, 2, 3):
                px = 1 - x if k >> 1 else x
                py = 1 - y if k & 1 else y
                copies.append(pltpu.make_async_remote_copy(
                    src_ref=p_refs[w].at[2 * px + py], dst_ref=out_refs[w].at[k - 1],
                    send_sem=sems[0].at[3 * w + k - 1], recv_sem=sems[1].at[3 * w + k - 1],
                    device_id=(px, py, cc), device_id_type=_MESH))
        return copies

    return _exchange_task(make_copies, ps, [jax.ShapeDtypeStruct((3,) + p.shape[1:], p.dtype) for p in ps], 3 * n)


def _add_sibling(g, r1, place, half_rows, *, tr, tc, name):
    _, r, c = r1.shape
    if half_rows is None:
        g_spec = pl.BlockSpec((None, None, tr, tc), lambda i, j, k, pc: (k, pc[0], i, j))
    else:
        per_half = half_rows // tr
        g_spec = pl.BlockSpec((None, tr, tc), lambda i, j, k, pc: (k, pc[0] * per_half + i, j))

    def body(pc_ref, g_ref, r_ref, pb_ref, own_ref):
        s = g_ref[...] + r_ref[...]
        pb_ref[...] = s.astype(BF16)

        @pl.when(pl.program_id(2) == pc_ref[1])
        def _():
            own_ref[...] = s

    grid_spec = pltpu.PrefetchScalarGridSpec(
        num_scalar_prefetch=1,
        grid=(r // tr, c // tc, 4),
        in_specs=[g_spec, pl.BlockSpec((None, tr, tc), lambda i, j, k, pc: (k, i, j))],
        out_specs=[pl.BlockSpec((None, tr, tc), lambda i, j, k, pc: (k, i, j)),
                   pl.BlockSpec((tr, tc), lambda i, j, k, pc: (i, j))],
    )
    return pl.pallas_call(
        body, name=name, grid_spec=grid_spec,
        out_shape=[jax.ShapeDtypeStruct((4, r, c), BF16), jax.ShapeDtypeStruct((r, c), F32)],
        compiler_params=_cparams(("parallel", "parallel", "arbitrary")),
    )(place, g, r1)


def _adamw_math(g, w, m, v):
    m = ADAM_B1 * m + (1.0 - ADAM_B1) * g
    v = ADAM_B2 * v + (1.0 - ADAM_B2) * (g * g)
    m_hat = m / (1.0 - ADAM_B1 ** ADAM_STEP)
    v_hat = v / (1.0 - ADAM_B2 ** ADAM_STEP)
    delta = -ADAM_LR * (m_hat / (jnp.sqrt(v_hat) + ADAM_EPS) + ADAM_WD * w)
    return delta, m, v


def _adamw_big(own, r2, w, m, v, prev, layer, *, tr, tc, name):
    _, r, c = w.shape

    def body(p_ref, r2_ref, w_ref, m_ref, v_ref, a0, a1, a2, a3, g_ref, d_ref, mo_ref, vo_ref):
        g = ((p_ref[...] + r2_ref[0].astype(F32)) + r2_ref[1].astype(F32)) + r2_ref[2].astype(F32)
        delta, mn, vn = _adamw_math(g, w_ref[...], m_ref[...], v_ref[...])
        g_ref[...] = g
        d_ref[...] = delta
        mo_ref[...] = mn
        vo_ref[...] = vn

    lay = pl.BlockSpec((None, tr, tc), lambda i, j: (layer, i, j))
    return pl.pallas_call(
        body, name=name, grid=(r // tr, c // tc),
        in_specs=[pl.BlockSpec((tr, tc), lambda i, j: (i, j)), pl.BlockSpec((3, tr, tc), lambda i, j: (0, i, j)),
                  lay, lay, lay, _ANY, _ANY, _ANY, _ANY],
        out_specs=[lay, lay, lay, lay],
        out_shape=[jax.ShapeDtypeStruct(w.shape, F32)] * 4,
        input_output_aliases={5: 0, 6: 1, 7: 2, 8: 3},
        compiler_params=_cparams(("parallel", "parallel")),
    )(own, r2, w, m, v, *prev)


SMALL_COLS = 1024


def _sum_gathered(parts, *, name):
    _, r, c = parts.shape

    def body(p_ref, o_ref):
        acc = p_ref[0]
        for k in range(1, N_DEV):
            acc = acc + p_ref[k]
        o_ref[...] = acc

    return pl.pallas_call(
        body, name=name, grid=(1,), in_specs=[pl.BlockSpec((N_DEV, r, c), lambda i: (0, 0, 0))],
        out_specs=pl.BlockSpec((r, c), lambda i: (0, 0)), out_shape=jax.ShapeDtypeStruct((r, c), F32),
        compiler_params=_cparams(("arbitrary",)),
    )(parts)


def _adamw_small(g, w, m, v, *, name):
    r, c = g.shape

    def body(g_ref, w_ref, m_ref, v_ref, d_ref, mo_ref, vo_ref):
        delta, mn, vn = _adamw_math(g_ref[...], w_ref[...], m_ref[...], v_ref[...])
        d_ref[...] = delta
        mo_ref[...] = mn
        vo_ref[...] = vn

    spec = pl.BlockSpec((r, c), lambda i: (0, 0))
    return pl.pallas_call(
        body, name=name, grid=(1,), in_specs=[spec] * 4, out_specs=[spec] * 3,
        out_shape=[jax.ShapeDtypeStruct((r, c), F32)] * 3, compiler_params=_cparams(("arbitrary",)),
    )(g, w, m, v)


WEIGHT_NAMES = ("gla_w_in", "gla_w_gate_up", "gla_gate_bias", "gla_norm_g", "gla_w_out", "dil_w_in", "dil_w_out",
                "ffn_w_up", "ffn_conv_w", "ffn_conv_b", "ffn_w_down", "ln_g", "ln_b")
ADAM_TILES = {"gla_w_in": (256, 770), "gla_w_out": (128, 2048), "dil_w_in": (256, 1152), "dil_w_out": (512, 256),
              "ffn_w_up": (344, 1024), "ffn_w_down": (344, 1024)}
ADD_TILES = {**ADAM_TILES, "ffn_w_up": (352, 1024)}
FF_DOWN_SHARD = D_FF // N_DEV
VEC_COLS = 128


def _pad_axis(a, axis, to):
    pads = [(0, 0)] * a.ndim
    pads[axis] = (0, to - a.shape[axis])
    return jnp.pad(a, pads)


def _ff_cols(blocks):
    r = blocks.shape[1]
    return _pad_axis(blocks, 2, FF_SHARD_PAD).reshape(2, 4, r, FF_SHARD_PAD).transpose(0, 2, 1, 3).reshape(2, r, FF_HALF_PAD)


def _ff_cols_back(a):
    r = a.shape[1]
    return a.reshape(2, r, 4, FF_SHARD_PAD)[..., :FF_SHARD].transpose(1, 0, 2, 3).reshape(r, 2 * D_FF)


def _vec_parts(l, w):
    return [w["ffn_conv_w"][l], w["ln_g"][l], w["ln_b"][l]] + ([w["gla_w_gate_up"][l // 2]] if l % 2 == 0 else [])


def _layer_shards(l, w):
    j = l // 2
    gla = l % 2 == 0
    parts = _vec_parts(l, w)
    vec_rows = -(-sum(math.prod(a.shape) for a in parts) // (VEC_COLS * SUBLANES)) * SUBLANES
    mixer = [(w["gla_w_in"] if gla else w["dil_w_in"])[j].astype(BF16),
             (w["gla_w_out"] if gla else w["dil_w_out"])[j].astype(BF16), _pack_rows(parts, vec_rows, VEC_COLS)]
    w_up_t = _pad_axis(jnp.swapaxes(w["ffn_w_up"][l], 0, 1).astype(BF16), 0, FF_SHARD_PAD)
    return mixer, w_up_t, w["ffn_w_down"][l].astype(BF16)


def _layer_weights(l, w, gathered_mixer, g_up, g_down):
    j = l // 2
    gla = l % 2 == 0
    g_in, g_out, g_vec = gathered_mixer
    vec_shapes = [a.shape for a in _vec_parts(l, w)]
    out = {"w_up_t": g_up}
    wd = g_down.reshape(4, FF_SHARD, D_MODEL)
    out["w_down"] = _pad_axis(wd, 1, FF_SHARD_PAD).reshape(FF_HALF_PAD, D_MODEL)
    vec = [jnp.stack(p) for p in zip(*[_unpack_rows(g_vec[d], vec_shapes) for d in range(N_DEV)])]
    out["conv_w"] = _ff_cols(vec[0])
    out["conv_b"] = _pad_axis(w["ffn_conv_b"][l].reshape(N_DEV, FF_SHARD), 1, FF_SHARD_PAD).reshape(2, 1, FF_HALF_PAD)
    out["ln_g"] = vec[1].transpose(1, 0, 2).reshape(2, 1, D_MODEL)
    out["ln_b"] = vec[2].transpose(1, 0, 2).reshape(2, 1, D_MODEL)
    if gla:
        win = g_in.transpose(1, 0, 2).reshape(D_MODEL, GLA_IN)
        out["w_in"] = _pad_axis(win, 1, GLA_MAIN + GLOW_PAD)
        wgu = vec[3].transpose(1, 0, 2).reshape(GLA_GATE_RANK, GLA_DK).astype(BF16)
        out["w_gate_up"] = _pad_axis(wgu, 0, GLOW_PAD)
        out["w_out"] = g_out.reshape(GLA_DV, D_MODEL)
        out["gate_bias"] = w["gla_gate_bias"][j].reshape(1, GLA_DK)
        out["norm_g"] = w["gla_norm_g"][j].reshape(1, GLA_HEAD_V)
    else:
        out["w_in"] = g_in
        out["w_out"] = g_out.transpose(1, 0, 2).reshape(DIL_WIDTH, D_MODEL)
    return out


def _by_chip_parity(blocks):
    return blocks.reshape((4, 2) + blocks.shape[1:])


def _col_blocks(dw, width):
    r = dw.shape[0]
    return dw.reshape(r, N_DEV, width).transpose(1, 0, 2)


def _carry(call, task):
    if task is None:
        return call(None), []
    return call(task)


def _sibling_sum(l, n, g, half_rows, from_sibling, place):
    return _add_sibling(g, from_sibling, place, half_rows, tr=ADD_TILES[n][0], tc=ADD_TILES[n][1], name=f"l{l}_{n}_add")


def _ffn_fwd(l, yb, lw, task_up, task_down):
    h, got_up = _carry(lambda c: _ffn_hidden(yb, lw["w_up_t"], comm=c, name=f"l{l}_ffn_hidden"), task_up)
    act = _convgate_fwd(h, lw["conv_w"], lw["conv_b"], name=f"l{l}_convgate")
    ffn, got_down = _carry(lambda c: _matmul(act, lw["w_down"], tm=1024, tn=1024, tk=2816, comm=c, name=f"l{l}_ffn_down"), task_down)
    return ffn, (h, act), got_up, got_down


def _ffn_bwd(l, yb, dz, dzb, lw, saved, place, pending):
    h, act = saved
    t = yb.shape[0]
    dact = _matmul(dzb, lw["w_down"], tb=True, tm=512, tn=2816, tk=D_MODEL, name=f"l{l}_ffn_dact")
    d_down = _ffn_down_dw(act, dzb, name=f"l{l}_ffn_dwdown")
    dh, dcw, dcb = _convgate_bwd(h, dact, lw["conv_w"], lw["conv_b"], name=f"l{l}_convgate_bwd")
    dh = dh.reshape(2 * t, FF_HALF_PAD)
    tasks = [_sibling_task([d_down], [FF_DOWN_SHARD])] + ([_chips_task(pending)] if pending else [])
    d_up_t, got = _ffn_hidden_dw(dh, yb, comm=_join_comm(tasks), name=f"l{l}_ffn_dwup")
    d_up = _by_chip_parity(d_up_t)
    dy, got_up = _ffn_hidden_dy(dh, lw["w_up_t"], dz, DEEPNORM_ALPHA, comm=_sibling_task([d_up], [None]), name=f"l{l}_ffn_dy")
    sums = {"ffn_w_down": _sibling_sum(l, "ffn_w_down", d_down, FF_DOWN_SHARD, got[0], place),
            "ffn_w_up": _sibling_sum(l, "ffn_w_up", d_up, None, got_up[0], place)}
    small = {"ffn_conv_w": _ff_cols_back(dcw), "ffn_conv_b": _ff_cols_back(dcb)[0]}
    return dy, sums, small, got[1:]


def _gla_layer_fwd(l, hb, lw, task):
    proj, got = _carry(lambda c: _matmul(hb, lw["w_in"], tm=1024, tn=896, tk=D_MODEL, comm=c, name=f"l{l}_gla_proj"), task)
    y, o, a, st = _gla_fwd(proj, lw["w_gate_up"], lw["gate_bias"], lw["norm_g"], name=f"l{l}_gla")
    mix = _matmul(y, lw["w_out"], tm=1024, tn=1024, tk=GLA_DV, name=f"l{l}_gla_out")
    return mix, (proj, y, o, a, st), got


def _gla_layer_bwd(l, hb, dz, dzb, lw, saved, ffn_sums):
    proj, y, o, a, st = saved
    dy = _matmul(dzb, lw["w_out"], tb=True, tm=1024, tn=1024, tk=D_MODEL, name=f"l{l}_gla_dy")
    d_out = _matmul(y, dzb, ta=True, tm=1024, tn=1024, tk=2048, name=f"l{l}_gla_dwout")
    dq, dk, dv, dr, dgl, dwgu, dbias, dng = _gla_bwd(proj, lw["w_gate_up"], lw["gate_bias"], lw["norm_g"], o, a, st, dy, name=f"l{l}_gla_bwd")
    dproj = jnp.concatenate([dq, dk, dv, dr, dgl], axis=-1)
    d_in, got_down = _matmul(hb, dproj, ta=True, tm=1024, tn=896, tk=2048, comm=_chips_task([ffn_sums["ffn_w_down"][0]]),
                             name=f"l{l}_gla_dwin")
    dx, got_up = _matmul(dproj, lw["w_in"], tb=True, tm=1024, tn=1024, tk=896, res=dz, res_scale=DEEPNORM_ALPHA,
                         comm=_chips_task([ffn_sums["ffn_w_up"][0]]), name=f"l{l}_gla_dx")
    big = {"gla_w_in": _by_chip_parity(_col_blocks(d_in[:, :GLA_IN], GLA_IN // N_DEV)),
           "gla_w_out": _by_chip_parity(d_out.reshape(N_DEV, GLA_DV // N_DEV, D_MODEL))}
    small = {"gla_w_gate_up": dwgu[:GLA_GATE_RANK], "gla_gate_bias": dbias[0], "gla_norm_g": dng[0]}
    return dx, big, small, {"ffn_w_down": got_down[0], "ffn_w_up": got_up[0]}


def _dil_layer_fwd(l, hb, lw, task):
    proj, got = _carry(lambda c: _mm_colblocks(hb, lw["w_in"], comm=c, name=f"l{l}_dil_proj"), task)
    os_, lses = [], []
    for gi in range(len(DIL_PATTERNS)):
        o, lse = _dil_fwd(proj, gi, name=f"l{l}_dil_attn{gi}")
        os_.append(o)
        lses.append(lse)
    omix = _dil_mix_fwd(os_, lses, name=f"l{l}_dil_mix")
    mix = _matmul(omix, lw["w_out"], tm=1024, tn=1024, tk=DIL_WIDTH, name=f"l{l}_dil_out")
    return mix, (proj, os_, lses, omix), got


def _dil_layer_bwd(l, hb, dz, dzb, lw, saved, ffn_sums):
    proj, os_, lses, omix = saved
    dout = _matmul(dzb, lw["w_out"], tb=True, tm=1024, tn=1024, tk=D_MODEL, name=f"l{l}_dil_dy")
    d_out = _matmul(omix, dzb, ta=True, tm=1024, tn=1024, tk=2048, name=f"l{l}_dil_dwout")
    dos, dls = _dil_mix_bwd(os_, lses, dout, name=f"l{l}_dil_mix_bwd")
    parts = []
    for gi in range(len(DIL_PATTERNS)):
        parts += list(_dil_bwd(proj, gi, lses[gi], dos[gi], dls[gi], name=f"l{l}_dil_attn_bwd{gi}"))
    dproj = jnp.concatenate(parts, axis=-1)
    d_in, got_down = _mm_grad_colblocks(hb, dproj, DIL_IN // N_DEV, comm=_chips_task([ffn_sums["ffn_w_down"][0]]),
                                        name=f"l{l}_dil_dwin")
    dx, got_up = _mm_colblocks_t(dproj, lw["w_in"], dz, DEEPNORM_ALPHA, comm=_chips_task([ffn_sums["ffn_w_up"][0]]),
                                 name=f"l{l}_dil_dx")
    big = {"dil_w_in": _by_chip_parity(d_in), "dil_w_out": _by_chip_parity(_col_blocks(d_out, D_MODEL // N_DEV))}
    return dx, big, {}, {"ffn_w_down": got_down[0], "ffn_w_up": got_up[0]}


def _pack_rows(arrays, rows, cols=SMALL_COLS):
    flat = [a.reshape(-1) for a in arrays]
    used = sum(f.shape[0] for f in flat)
    return jnp.concatenate(flat + [jnp.zeros((rows * cols - used,), F32)]).reshape(rows, cols)


def _unpack_rows(packed, shapes):
    flat, out, off = packed.reshape(-1), [], 0
    for s in shapes:
        n = math.prod(s)
        out.append(flat[off:off + n].reshape(s))
        off += n
    return out


def _rows_for(shapes):
    n = sum(math.prod(s) for s in shapes)
    return -(-n // (SMALL_COLS * SUBLANES)) * SUBLANES


def kernel(x, gla_w_in, gla_w_gate_up, gla_gate_bias, gla_norm_g, gla_w_out, dil_w_in, dil_w_out, ffn_w_up, ffn_conv_w, ffn_conv_b, ffn_w_down, ln_g, ln_b, loss_target, m_gla_w_in, m_gla_w_gate_up, m_gla_gate_bias, m_gla_norm_g, m_gla_w_out, m_dil_w_in, m_dil_w_out, m_ffn_w_up, m_ffn_conv_w, m_ffn_conv_b, m_ffn_w_down, m_ln_g, m_ln_b, v_gla_w_in, v_gla_w_gate_up, v_gla_gate_bias, v_gla_norm_g, v_gla_w_out, v_dil_w_in, v_dil_w_out, v_ffn_w_up, v_ffn_conv_w, v_ffn_conv_b, v_ffn_w_down, v_ln_g, v_ln_b):
    w = dict(zip(WEIGHT_NAMES, (gla_w_in, gla_w_gate_up, gla_gate_bias, gla_norm_g, gla_w_out, dil_w_in, dil_w_out,
                                ffn_w_up, ffn_conv_w, ffn_conv_b, ffn_w_down, ln_g, ln_b)))
    mom = dict(zip(WEIGHT_NAMES, (m_gla_w_in, m_gla_w_gate_up, m_gla_gate_bias, m_gla_norm_g, m_gla_w_out, m_dil_w_in,
                                  m_dil_w_out, m_ffn_w_up, m_ffn_conv_w, m_ffn_conv_b, m_ffn_w_down, m_ln_g, m_ln_b)))
    var = dict(zip(WEIGHT_NAMES, (v_gla_w_in, v_gla_w_gate_up, v_gla_gate_bias, v_gla_norm_g, v_gla_w_out, v_dil_w_in,
                                  v_dil_w_out, v_ffn_w_up, v_ffn_conv_w, v_ffn_conv_b, v_ffn_w_down, v_ln_g, v_ln_b)))
    xi, yi, ci = _position()
    dev = 4 * xi + 2 * yi + ci
    place = jnp.stack([ci, 2 * xi + yi]).astype(jnp.int32)
    t = x.shape[0] * x.shape[1]
    h = x.reshape(t, D_MODEL)
    hb = h.astype(BF16)
    target = loss_target.reshape(t, D_MODEL)

    mixer0, up0, down0 = _layer_shards(0, w)
    got = _run_comm(_gather_task(mixer0 + [up0, down0]), name="gather_l0")
    lws = [_layer_weights(0, w, got[:3], got[3], got[4])]
    saved = []
    for l in range(DEPTH):
        lw = lws[l]
        nxt = _layer_shards(l + 1, w) if l + 1 < DEPTH else None
        task = lambda arrays: _gather_task(arrays) if nxt else None
        mix, mixer_saved, got_mixer = (_gla_layer_fwd if l % 2 == 0 else _dil_layer_fwd)(l, hb, lw, task(nxt and nxt[0]))
        y1, y1b, xh1, rs1 = _ln_fwd(h, mix, lw["ln_g"][0], lw["ln_b"][0], name=f"l{l}_ln1")
        ffn, ffn_saved, got_up, got_down = _ffn_fwd(l, y1b, lw, task(nxt and [nxt[1]]), task(nxt and [nxt[2]]))
        y2, y2b, xh2, rs2 = _ln_fwd(y1, ffn, lw["ln_g"][1], lw["ln_b"][1], name=f"l{l}_ln2")
        saved.append((hb, mixer_saved, y1b, xh1, rs1, ffn_saved, xh2, rs2))
        h, hb = y2, y2b
        if nxt:
            lws.append(_layer_weights(l + 1, w, got_mixer, got_up[0], got_down[0]))
    loss_local, dy = _loss_fwd_bwd(h, target, name="loss")
    loss = lax.psum(loss_local[0, 0], ("x", "y", "c"))

    big_names = tuple(ADAM_TILES)
    as_updated = lambda n, a: jnp.swapaxes(a, 1, 2) if n == "ffn_w_up" else a
    wt, mt, vt = ({n: as_updated(n, d[n]) for n in big_names} for d in (w, mom, var))
    results = {n: [lax.empty(wt[n].shape, F32) for _ in range(4)] for n in big_names}
    small_grads = {n: [None] * w[n].shape[0] for n in WEIGHT_NAMES if n not in big_names}

    def adamw(l, n, own, from_chips):
        results[n] = _adamw_big(own, from_chips, wt[n], mt[n], vt[n], results[n], l if n.startswith("ffn") else l // 2,
                                tr=ADAM_TILES[n][0], tc=ADAM_TILES[n][1], name=f"l{l}_{n}_adamw")

    pending = None
    for l in reversed(range(DEPTH)):
        lw = lws[l]
        hb_in, mixer_saved, y1b, xh1, rs1, ffn_saved, xh2, rs2 = saved[l]
        dz2, dz2b, dg2, db2 = _ln_bwd(dy, xh2, rs2, lw["ln_g"][1], name=f"l{l}_ln2_bwd")
        dy1, ffn_sums, small_ffn, got_pending = _ffn_bwd(l, y1b, dz2, dz2b, lw, ffn_saved, place,
                                                          [s[0] for s in pending[1].values()] if pending else None)
        if pending:
            for (n, (_, own)), r2 in zip(pending[1].items(), got_pending):
                adamw(pending[0], n, own, r2)
        dz1, dz1b, dg1, db1 = _ln_bwd(dy1, xh1, rs1, lw["ln_g"][0], name=f"l{l}_ln1_bwd")
        dy, big_mix, small_mix, got_ffn = (_gla_layer_bwd if l % 2 == 0 else _dil_layer_bwd)(
            l, hb_in, dz1, dz1b, lw, mixer_saved, ffn_sums)
        for n, (_, own) in ffn_sums.items():
            adamw(l, n, own, got_ffn[n])
        from_sibling = _run_comm(_sibling_task(list(big_mix.values()), [None] * len(big_mix)), name=f"reduce_sibling_l{l}")
        pending = (l, {n: _sibling_sum(l, n, g, None, r1, place) for (n, g), r1 in zip(big_mix.items(), from_sibling)})
        small_grads["ln_g"][l] = jnp.concatenate([dg1, dg2], axis=0)
        small_grads["ln_b"][l] = jnp.concatenate([db1, db2], axis=0)
        for n, g in small_ffn.items():
            small_grads[n][l] = g
        for n, g in small_mix.items():
            small_grads[n][l // 2] = g
    from_chips = _run_comm(_chips_task([s[0] for s in pending[1].values()]), name="reduce_chips_l0")
    for (n, (_, own)), r2 in zip(pending[1].items(), from_chips):
        adamw(pending[0], n, own, r2)
    results = {n: [as_updated(n, a) for a in results[n]] for n in big_names}
    grad_x = dy.reshape(x.shape)

    small_names = [n for n in WEIGHT_NAMES if n not in big_names]
    full_shapes = {"gla_w_gate_up": (2, GLA_GATE_RANK, GLA_DK), "gla_gate_bias": (2, GLA_DK), "gla_norm_g": (2, GLA_HEAD_V),
                   "ffn_conv_w": (DEPTH, 3, 2 * D_FF), "ffn_conv_b": (DEPTH, 2 * D_FF),
                   "ln_g": (DEPTH, 2, D_MODEL), "ln_b": (DEPTH, 2, D_MODEL)}
    shapes = [full_shapes[n] for n in small_names]
    rows = _rows_for(shapes)
    packed = _pack_rows([jnp.stack(small_grads[n]) for n in small_names], rows)
    summed = _sum_gathered(_run_comm(_gather_task([packed]), name="gather_small_grads")[0], name="sum_small_grads")
    full = dict(zip(small_names, _unpack_rows(summed, shapes)))
    own = {n: (full[n] if w[n].shape == full[n].shape
               else lax.dynamic_slice_in_dim(full[n], dev * w[n].shape[-1], w[n].shape[-1], axis=full[n].ndim - 1))
           for n in small_names}
    own_shapes = [w[n].shape for n in small_names]
    rows = _rows_for(own_shapes)
    pk = lambda d: _pack_rows([d[n] for n in small_names], rows)
    outs = _adamw_small(pk(own), pk(w), pk(mom), pk(var), name="adamw_small")
    for n in small_names:
        results[n] = [own[n]]
    for k, packed_out in enumerate(outs):
        for n, a in zip(small_names, _unpack_rows(packed_out, own_shapes)):
            results[n].append(a)

    return (loss, grad_x) + tuple(results[n][k] for k in range(4) for n in WEIGHT_NAMES)
```

```python
import functools
import math

import jax
import jax.numpy as jnp
from jax import lax
from jax.experimental import pallas as pl
from jax.experimental.pallas import tpu as pltpu

F32 = jnp.float32
BF16 = jnp.bfloat16

D_MODEL = 2048
SEQ = 2048
DEPTH = 4
N_DEV = 8
GLA_HEADS = 4
GLA_DK = 1024
GLA_DV = 2048
GLA_HEAD_K = 256
GLA_HEAD_V = 512
GLA_GATE_RANK = 16
GLA_GATE_NORMALIZER = 16.0
GLA_CHUNK = 64
GLA_MAIN = 2 * GLA_DK + 2 * GLA_DV
GLA_IN = GLA_MAIN + GLA_GATE_RANK
DIL_PATTERNS = ((128, 1), (512, 4), (2048, 16))
DIL_HEADS = 8
DIL_HEAD_DIM = 128
DIL_WIDTH = DIL_HEADS * DIL_HEAD_DIM
DIL_BLOCK = 128
DIL_IN = 3 * len(DIL_PATTERNS) * DIL_WIDTH
D_FF = 5504
DEEPNORM_ALPHA = (2 * DEPTH) ** 0.25
LN_EPS = 1e-5
RMS_EPS = 1e-6
ADAM_LR = 0.001
ADAM_B1 = 0.9
ADAM_B2 = 0.999
ADAM_EPS = 1e-08
ADAM_WD = 0.01
ADAM_STEP = 10

LANES = 128
SUBLANES = 8
VMEM_LIMIT_BYTES = 56 * 1024 * 1024

FF_SHARD = 2 * D_FF // N_DEV
FF_SHARD_PAD = 1408
FF_HALF_PAD = 4 * FF_SHARD_PAD
GLOW_PAD = LANES


def _cparams(dims=None):
    return pltpu.CompilerParams(dimension_semantics=dims, vmem_limit_bytes=VMEM_LIMIT_BYTES)


class _Comm:
    def __init__(self, inputs, out_shapes, sem_shapes, start, middle, finish):
        self.inputs, self.out_shapes, self.sem_shapes = list(inputs), list(out_shapes), list(sem_shapes)
        self.start, self.middle, self.finish = start, middle, finish


def _join_comm(comms):
    def cut(refs, counts):
        out, off = [], 0
        for c in counts:
            out.append(refs[off:off + c])
            off += c
        return out

    n_in = [len(c.inputs) for c in comms]
    n_out = [len(c.out_shapes) for c in comms]
    n_sem = [len(c.sem_shapes) for c in comms]

    def hook(which):
        def run(ins, outs, sems):
            for c, i, o, s in zip(comms, cut(ins, n_in), cut(outs, n_out), cut(sems, n_sem)):
                getattr(c, which)(i, o, s)
        return run

    return _Comm([a for c in comms for a in c.inputs], [s for c in comms for s in c.out_shapes],
                 [s for c in comms for s in c.sem_shapes], hook("start"), hook("middle"), hook("finish"))


def _mm(a, b, *, grid, a_spec, b_spec, o_spec, out_shape, acc_shape, ta=False, tb=False, res=None, res_scale=1.0,
        comm=None, name):
    nk = grid[2]
    dims = (((0 if ta else 1,), (1 if tb else 0,)), ((), ()))
    has_res = res is not None
    n_in = 2 + has_res
    n_cin = len(comm.inputs) if comm else 0
    n_cout = len(comm.out_shapes) if comm else 0

    def body(*refs):
        a_ref, b_ref = refs[0], refs[1]
        res_ref = refs[2] if has_res else None
        o_ref = refs[n_in + n_cin]
        scratch = refs[n_in + n_cin + 1 + n_cout:]
        acc_ref = scratch[0] if nk > 1 else None
        if comm:
            task = (refs[n_in:n_in + n_cin], refs[n_in + n_cin + 1:n_in + n_cin + 1 + n_cout],
                    scratch[1:] if nk > 1 else scratch)
            step = (pl.program_id(0) * grid[1] + pl.program_id(1)) * nk + pl.program_id(2)
            steps = grid[0] * grid[1] * nk

            @pl.when(step == 0)
            def _():
                comm.start(*task)

        p = lax.dot_general(a_ref[...], b_ref[...], dims, preferred_element_type=F32)

        def finish(acc):
            if has_res:
                acc = acc + res_scale * res_ref[...]
            o_ref[...] = acc.astype(o_ref.dtype)

        if nk == 1:
            finish(p)
        else:
            kk = pl.program_id(2)

            @pl.when(kk == 0)
            def _():
                acc_ref[...] = p

            @pl.when(kk > 0)
            def _():
                acc_ref[...] += p

            @pl.when(kk == nk - 1)
            def _():
                finish(acc_ref[...])

        if comm:
            @pl.when(step == steps - 1)
            def _():
                comm.middle(*task)
                comm.finish(*task)

    in_specs = [a_spec, b_spec] + ([o_spec] if has_res else [])
    args = (a, b) + ((res,) if has_res else ())
    acc = [pltpu.VMEM(acc_shape, F32)] if nk > 1 else []
    if not comm:
        return pl.pallas_call(
            body, name=name, grid=grid, in_specs=in_specs, out_specs=o_spec, out_shape=out_shape, scratch_shapes=acc,
            compiler_params=_cparams(("parallel", "parallel", "arbitrary")),
        )(*args)
    outs = pl.pallas_call(
        body, name=name, grid=grid, in_specs=in_specs + [_ANY] * n_cin, out_specs=[o_spec] + [_ANY] * n_cout,
        out_shape=[out_shape] + comm.out_shapes, scratch_shapes=acc + comm.sem_shapes,
        compiler_params=_cparams(("arbitrary", "arbitrary", "arbitrary")),
    )(*args, *comm.inputs)
    return outs[0], list(outs[1:])


def _matmul(a, b, *, ta=False, tb=False, tm, tn, tk, out_dtype=F32, res=None, res_scale=1.0, comm=None, name):
    m, k = (a.shape[1], a.shape[0]) if ta else a.shape
    n = b.shape[0] if tb else b.shape[1]
    assert (b.shape[1] if tb else b.shape[0]) == k
    assert m % tm == 0 and n % tn == 0 and k % tk == 0, (m, n, k, tm, tn, tk)
    a_spec = pl.BlockSpec((tk, tm), lambda i, j, kk: (kk, i)) if ta else pl.BlockSpec((tm, tk), lambda i, j, kk: (i, kk))
    b_spec = pl.BlockSpec((tn, tk), lambda i, j, kk: (j, kk)) if tb else pl.BlockSpec((tk, tn), lambda i, j, kk: (kk, j))
    return _mm(a, b, grid=(m // tm, n // tn, k // tk), a_spec=a_spec, b_spec=b_spec,
               o_spec=pl.BlockSpec((tm, tn), lambda i, j, kk: (i, j)), out_shape=jax.ShapeDtypeStruct((m, n), out_dtype),
               acc_shape=(tm, tn), ta=ta, tb=tb, res=res, res_scale=res_scale, comm=comm, name=name)


MM_ROWS = 1024


def _mm_colblocks(a, wb, *, comm=None, name):
    m, k = a.shape
    nb, _, w = wb.shape
    return _mm(a, wb, grid=(m // MM_ROWS, nb, 1),
               a_spec=pl.BlockSpec((MM_ROWS, k), lambda i, j, kk: (i, 0)),
               b_spec=pl.BlockSpec((None, k, w), lambda i, j, kk: (j, 0, 0)),
               o_spec=pl.BlockSpec((MM_ROWS, w), lambda i, j, kk: (i, j)),
               out_shape=jax.ShapeDtypeStruct((m, nb * w), F32), acc_shape=(MM_ROWS, w), comm=comm, name=name)


def _mm_colblocks_t(a, wb, res, res_scale, *, comm=None, name):
    m = a.shape[0]
    nb, n, w = wb.shape
    tn = 1024
    return _mm(a, wb, grid=(m // MM_ROWS, n // tn, nb),
               a_spec=pl.BlockSpec((MM_ROWS, w), lambda i, j, kk: (i, kk)),
               b_spec=pl.BlockSpec((None, tn, w), lambda i, j, kk: (kk, j, 0)),
               o_spec=pl.BlockSpec((MM_ROWS, tn), lambda i, j, kk: (i, j)),
               out_shape=jax.ShapeDtypeStruct((m, n), F32), acc_shape=(MM_ROWS, tn), tb=True,
               res=res, res_scale=res_scale, comm=comm, name=name)


def _mm_grad_colblocks(x, dy, w, *, comm=None, name):
    t, k = x.shape
    nb = dy.shape[1] // w
    tm, tk = 1024, 2048
    return _mm(x, dy, grid=(k // tm, nb, t // tk),
               a_spec=pl.BlockSpec((tk, tm), lambda i, j, kk: (kk, i)),
               b_spec=pl.BlockSpec((tk, w), lambda i, j, kk: (kk, j)),
               o_spec=pl.BlockSpec((None, tm, w), lambda i, j, kk: (j, i, 0)),
               out_shape=jax.ShapeDtypeStruct((nb, k, w), F32), acc_shape=(tm, w), ta=True, comm=comm, name=name)


def _ffn_hidden(y, wt, *, comm=None, name):
    t, k = y.shape
    ni = t // MM_ROWS
    return _mm(y, wt, grid=(ni, N_DEV, 1),
               a_spec=pl.BlockSpec((MM_ROWS, k), lambda i, j, kk: (i, 0)),
               b_spec=pl.BlockSpec((None, FF_SHARD_PAD, k), lambda i, j, kk: (j, 0, 0)),
               o_spec=pl.BlockSpec((MM_ROWS, FF_SHARD_PAD), lambda i, j, kk: ((j // 4) * ni + i, j % 4)),
               out_shape=jax.ShapeDtypeStruct((2 * t, FF_HALF_PAD), F32), acc_shape=(MM_ROWS, FF_SHARD_PAD), tb=True,
               comm=comm, name=name)


def _ffn_hidden_dy(dh, wt, res, res_scale, *, comm=None, name):
    t = dh.shape[0] // 2
    ni, tn = t // MM_ROWS, 1024
    return _mm(dh, wt, grid=(ni, D_MODEL // tn, N_DEV),
               a_spec=pl.BlockSpec((MM_ROWS, FF_SHARD_PAD), lambda i, j, kk: ((kk // 4) * ni + i, kk % 4)),
               b_spec=pl.BlockSpec((None, FF_SHARD_PAD, tn), lambda i, j, kk: (kk, 0, j)),
               o_spec=pl.BlockSpec((MM_ROWS, tn), lambda i, j, kk: (i, j)),
               out_shape=jax.ShapeDtypeStruct((t, D_MODEL), F32), acc_shape=(MM_ROWS, tn),
               res=res, res_scale=res_scale, comm=comm, name=name)


def _ffn_hidden_dw(dh, y, *, comm=None, name):
    t, k = y.shape
    tk, tn = 2048, 1024
    nk = t // tk
    return _mm(dh, y, grid=(N_DEV, k // tn, nk),
               a_spec=pl.BlockSpec((tk, FF_SHARD_PAD), lambda i, j, kk: ((i // 4) * nk + kk, i % 4)),
               b_spec=pl.BlockSpec((tk, tn), lambda i, j, kk: (kk, j)),
               o_spec=pl.BlockSpec((None, FF_SHARD_PAD, tn), lambda i, j, kk: (i, 0, j)),
               out_shape=jax.ShapeDtypeStruct((N_DEV, FF_SHARD_PAD, k), F32), acc_shape=(FF_SHARD_PAD, tn), ta=True,
               comm=comm, name=name)


def _ffn_down_dw(act, dz, *, comm=None, name):
    t, k = dz.shape
    tk, tn = 2048, 1024
    return _mm(act, dz, grid=(4, k // tn, t // tk),
               a_spec=pl.BlockSpec((tk, FF_SHARD_PAD), lambda i, j, kk: (kk, i)),
               b_spec=pl.BlockSpec((tk, tn), lambda i, j, kk: (kk, j)),
               o_spec=pl.BlockSpec((None, FF_SHARD_PAD, tn), lambda i, j, kk: (i, 0, j)),
               out_shape=jax.ShapeDtypeStruct((4, FF_SHARD_PAD, k), F32), acc_shape=(FF_SHARD_PAD, tn), ta=True,
               comm=comm, name=name)


LN_ROWS = 256


def _ln_fwd(x, f, g, b, *, name):
    t, d = x.shape

    def body(x_ref, f_ref, g_ref, b_ref, y_ref, yb_ref, xh_ref, rs_ref):
        z = DEEPNORM_ALPHA * x_ref[...] + f_ref[...]
        mu = jnp.mean(z, axis=-1, keepdims=True)
        zc = z - mu
        var = jnp.mean(zc * zc, axis=-1, keepdims=True)
        rstd = lax.rsqrt(var + LN_EPS)
        xh = zc * rstd
        y = xh * g_ref[...] + b_ref[...]
        y_ref[...] = y
        yb_ref[...] = y.astype(BF16)
        xh_ref[...] = xh
        rs_ref[...] = rstd

    row = pl.BlockSpec((LN_ROWS, d), lambda i: (i, 0))
    vec = pl.BlockSpec((1, d), lambda i: (0, 0))
    return pl.pallas_call(
        body,
        name=name,
        grid=(t // LN_ROWS,),
        in_specs=[row, row, vec, vec],
        out_specs=[row, row, row, pl.BlockSpec((LN_ROWS, 1), lambda i: (i, 0))],
        out_shape=[jax.ShapeDtypeStruct((t, d), F32), jax.ShapeDtypeStruct((t, d), BF16),
                   jax.ShapeDtypeStruct((t, d), F32), jax.ShapeDtypeStruct((t, 1), F32)],
        compiler_params=_cparams(("parallel",)),
    )(x, f, g, b)


def _ln_bwd(dy, xhat, rstd, g, *, name):
    t, d = dy.shape

    def body(dy_ref, xh_ref, rs_ref, g_ref, dz_ref, dzb_ref, dg_ref, db_ref):
        dyv = dy_ref[...]
        xh = xh_ref[...]
        dyg = dyv * g_ref[...]
        m1 = jnp.mean(dyg, axis=-1, keepdims=True)
        m2 = jnp.mean(dyg * xh, axis=-1, keepdims=True)
        dz = rs_ref[...] * (dyg - m1 - xh * m2)
        dz_ref[...] = dz
        dzb_ref[...] = dz.astype(BF16)
        dg_part = jnp.sum(dyv * xh, axis=0, keepdims=True)
        db_part = jnp.sum(dyv, axis=0, keepdims=True)

        @pl.when(pl.program_id(0) == 0)
        def _():
            dg_ref[...] = dg_part
            db_ref[...] = db_part

        @pl.when(pl.program_id(0) > 0)
        def _():
            dg_ref[...] += dg_part
            db_ref[...] += db_part

    row = pl.BlockSpec((LN_ROWS, d), lambda i: (i, 0))
    vec = pl.BlockSpec((1, d), lambda i: (0, 0))
    return pl.pallas_call(
        body,
        name=name,
        grid=(t // LN_ROWS,),
        in_specs=[row, row, pl.BlockSpec((LN_ROWS, 1), lambda i: (i, 0)), vec],
        out_specs=[row, row, vec, vec],
        out_shape=[jax.ShapeDtypeStruct((t, d), F32), jax.ShapeDtypeStruct((t, d), BF16),
                   jax.ShapeDtypeStruct((1, d), F32), jax.ShapeDtypeStruct((1, d), F32)],
        compiler_params=_cparams(("arbitrary",)),
    )(dy, xhat, rstd, g)


def _loss_fwd_bwd(y, target, *, name):
    t, d = y.shape

    def body(y_ref, t_ref, loss_ref, dy_ref):
        err = y_ref[...] - t_ref[...]
        dy_ref[...] = err * (1.0 / d)
        part = 0.5 * jnp.sum(jnp.mean(err * err, axis=-1, keepdims=True), axis=0, keepdims=True)

        @pl.when(pl.program_id(0) == 0)
        def _():
            loss_ref[...] = part

        @pl.when(pl.program_id(0) > 0)
        def _():
            loss_ref[...] += part

    row = pl.BlockSpec((LN_ROWS, d), lambda i: (i, 0))
    return pl.pallas_call(
        body,
        name=name,
        grid=(t // LN_ROWS,),
        in_specs=[row, row],
        out_specs=[pl.BlockSpec((1, 1), lambda i: (0, 0)), row],
        out_shape=[jax.ShapeDtypeStruct((1, 1), F32), jax.ShapeDtypeStruct((t, d), F32)],
        compiler_params=_cparams(("arbitrary",)),
    )(y, target)


FFN_COLS = 256


def _shift_rows(h, s):
    rows = lax.broadcasted_iota(jnp.int32, h.shape, 0)
    return jnp.where(rows >= s, pltpu.roll(h, s, 0), 0.0)


def _shift_rows_up(h, s):
    n = h.shape[0]
    rows = lax.broadcasted_iota(jnp.int32, h.shape, 0)
    return jnp.where(rows < n - s, pltpu.roll(h, n - s, 0), 0.0)


def _causal_conv(h, w, b):
    return w[0:1, :] * _shift_rows(h, 2) + w[1:2, :] * _shift_rows(h, 1) + w[2:3, :] * h + b


def _sigmoid(x):
    return 1.0 / (1.0 + jnp.exp(-x))


def _convgate_fwd(h, cw, cb, *, name):
    t, n = h.shape[0] // 2, h.shape[1]
    nb = t // SEQ

    def body(hg_ref, hu_ref, wg_ref, wu_ref, bg_ref, bu_ref, a_ref):
        gate = _causal_conv(hg_ref[...], wg_ref[...], bg_ref[...])
        up = _causal_conv(hu_ref[...], wu_ref[...], bu_ref[...])
        a_ref[...] = (gate * _sigmoid(gate) * up).astype(BF16)

    def half(rows, k):
        return pl.BlockSpec((None, rows, FFN_COLS), lambda s, j: (k, 0, j))

    return pl.pallas_call(
        body,
        name=name,
        grid=(nb, n // FFN_COLS),
        in_specs=[pl.BlockSpec((SEQ, FFN_COLS), lambda s, j: (s, j)), pl.BlockSpec((SEQ, FFN_COLS), lambda s, j: (nb + s, j)),
                  half(3, 0), half(3, 1), half(1, 0), half(1, 1)],
        out_specs=pl.BlockSpec((SEQ, FFN_COLS), lambda s, j: (s, j)),
        out_shape=jax.ShapeDtypeStruct((t, n), BF16),
        compiler_params=_cparams(("parallel", "parallel")),
    )(h, h, cw, cw, cb, cb)


def _convgate_bwd(h, dact, cw, cb, *, name):
    t, n = h.shape[0] // 2, h.shape[1]
    nb = t // SEQ

    def body(hg_ref, hu_ref, da_ref, wg_ref, wu_ref, bg_ref, bu_ref, dh_ref, dw_ref, db_ref):
        dhg_ref, dhu_ref = dh_ref.at[0], dh_ref.at[1]
        dwg_ref, dwu_ref = dw_ref.at[0], dw_ref.at[1]
        dbg_ref, dbu_ref = db_ref.at[0], db_ref.at[1]
        hgv, huv = hg_ref[...], hu_ref[...]
        wgv, wuv = wg_ref[...], wu_ref[...]
        gate = _causal_conv(hgv, wgv, bg_ref[...])
        up = _causal_conv(huv, wuv, bu_ref[...])
        sg = _sigmoid(gate)
        da = da_ref[...]
        dgate = da * up * (sg * (1.0 + gate * (1.0 - sg)))
        dup = da * (gate * sg)

        def conv_bwd(dc, h, w, dh_ref, dw_ref, db_ref):
            dh = w[2:3, :] * dc + w[1:2, :] * _shift_rows_up(dc, 1) + w[0:1, :] * _shift_rows_up(dc, 2)
            dh_ref[...] = dh.astype(BF16)
            dws = [jnp.sum(dc * _shift_rows(h, 2), axis=0, keepdims=True),
                   jnp.sum(dc * _shift_rows(h, 1), axis=0, keepdims=True),
                   jnp.sum(dc * h, axis=0, keepdims=True)]
            db = jnp.sum(dc, axis=0, keepdims=True)

            @pl.when(pl.program_id(1) == 0)
            def _():
                for r in range(3):
                    dw_ref[r:r + 1, :] = dws[r]
                db_ref[...] = db

            @pl.when(pl.program_id(1) > 0)
            def _():
                for r in range(3):
                    dw_ref[r:r + 1, :] += dws[r]
                db_ref[...] += db

        conv_bwd(dgate, hgv, wgv, dhg_ref, dwg_ref, dbg_ref)
        conv_bwd(dup, huv, wuv, dhu_ref, dwu_ref, dbu_ref)

    def half(rows, k):
        return pl.BlockSpec((None, rows, FFN_COLS), lambda j, s: (k, 0, j))

    def both(rows):
        return pl.BlockSpec((2, rows, FFN_COLS), lambda j, s: (0, 0, j))

    return pl.pallas_call(
        body,
        name=name,
        grid=(n // FFN_COLS, nb),
        in_specs=[pl.BlockSpec((SEQ, FFN_COLS), lambda j, s: (s, j)), pl.BlockSpec((SEQ, FFN_COLS), lambda j, s: (nb + s, j)),
                  pl.BlockSpec((SEQ, FFN_COLS), lambda j, s: (s, j)), half(3, 0), half(3, 1), half(1, 0), half(1, 1)],
        out_specs=[pl.BlockSpec((2, SEQ, FFN_COLS), lambda j, s: (0, s, j)), both(3), both(1)],
        out_shape=[jax.ShapeDtypeStruct((2, t, n), BF16), jax.ShapeDtypeStruct((2, 3, n), F32),
                   jax.ShapeDtypeStruct((2, 1, n), F32)],
        compiler_params=_cparams(("parallel", "arbitrary")),
    )(h, h, dact, cw, cw, cb, cb)


GLA_Q_SCALE = GLA_HEAD_K ** -0.5
GLA_NC = SEQ // GLA_CHUNK
_NT = (((1,), (1,)), ((), ()))
_TN = (((0,), (0,)), ((), ()))


def _cumsum_rows(g):
    n = g.shape[0]
    rows = lax.broadcasted_iota(jnp.int32, g.shape, 0)
    s = 1
    while s < n:
        g = g + jnp.where(rows >= s, pltpu.roll(g, s, 0), 0.0)
        s *= 2
    return g


def _suffix_sum_rows(x):
    n = x.shape[0]
    rows = lax.broadcasted_iota(jnp.int32, x.shape, 0)
    s = 1
    while s < n:
        x = x + jnp.where(rows < n - s, pltpu.roll(x, n - s, 0), 0.0)
        s *= 2
    return x


def _gla_log_gate(gl_ref, wgu_ref, bias_ref):
    pre = jnp.dot(gl_ref[...].astype(BF16), wgu_ref[...], preferred_element_type=F32) + bias_ref[...]
    log_sig = jnp.minimum(pre, 0.0) - jnp.log(1.0 + jnp.exp(-jnp.abs(pre)))
    return pre, log_sig * (1.0 / GLA_GATE_NORMALIZER)


def _pair_rows(j):
    return (j // SUBLANES) * SUBLANES


def _gla_pair_fwd(q_ref, k_ref, b_scr, a_scr, h):
    c = GLA_CHUNK
    kc = pl.ds(h * GLA_HEAD_K, GLA_HEAD_K)
    a_scr[...] = jnp.zeros(a_scr.shape, F32)
    lane = lax.broadcasted_iota(jnp.int32, (1, c), 1)
    for j in range(c):
        r0 = _pair_rows(j)
        rs = pl.ds(r0, c - r0)
        rows = lax.broadcasted_iota(jnp.int32, (c - r0, 1), 0) + r0
        e = jnp.exp(jnp.minimum(b_scr[rs, kc] - b_scr[pl.ds(j, 1), kc], 0.0))
        w = (q_ref[rs, kc] * GLA_Q_SCALE) * k_ref[pl.ds(j, 1), kc] * e
        col = jnp.where(rows >= j, jnp.sum(w, axis=-1, keepdims=True), 0.0)
        a_scr[rs, :] += col * (lane == j).astype(F32)


def _gla_fwd(proj, wgu, bias, ng, *, name):
    t = proj.shape[0]
    nb, nc, c = t // SEQ, GLA_NC, GLA_CHUNK

    def body(q_ref, k_ref, v_ref, r_ref, gl_ref, wgu_ref, bias_ref, ng_ref,
             y_ref, o_ref, a_ref, st_ref, state, b_scr, a_scr):
        @pl.when(pl.program_id(1) == 0)
        def _():
            state[...] = jnp.zeros(state.shape, F32)

        _, g = _gla_log_gate(gl_ref, wgu_ref, bias_ref)
        b_scr[...] = _cumsum_rows(g)
        for h in range(GLA_HEADS):
            kc = pl.ds(h * GLA_HEAD_K, GLA_HEAD_K)
            vc = pl.ds(h * GLA_HEAD_V, GLA_HEAD_V)
            qh = q_ref[:, kc] * GLA_Q_SCALE
            kh = k_ref[:, kc]
            vh = v_ref[:, vc].astype(BF16)
            bh = b_scr[:, kc]
            blast = b_scr[pl.ds(c - 1, 1), kc]
            st = state[h]
            st_ref[h] = st
            o_inter = lax.dot_general((qh * jnp.exp(bh)).astype(BF16), st.astype(BF16), _NT, preferred_element_type=F32)
            _gla_pair_fwd(q_ref, k_ref, b_scr, a_scr, h)
            a = a_scr[...]
            a_ref[h] = a
            o = o_inter + jnp.dot(a.astype(BF16), vh, preferred_element_type=F32)
            kd = (kh * jnp.exp(blast - bh)).astype(BF16)
            state[h] = st * jnp.exp(blast) + lax.dot_general(vh, kd, _TN, preferred_element_type=F32)
            o_ref[:, vc] = o
            rs = lax.rsqrt(jnp.mean(o * o, axis=-1, keepdims=True) + RMS_EPS)
            rh = r_ref[:, vc]
            y_ref[:, vc] = ((o * rs * ng_ref[...]) * (rh * _sigmoid(rh))).astype(BF16)

    def tok(width, col):
        return pl.BlockSpec((c, width), lambda b, i: (b * nc + i, col))

    whole = lambda shape: pl.BlockSpec(shape, lambda b, i: (0,) * len(shape))
    return pl.pallas_call(
        body,
        name=name,
        grid=(nb, nc),
        in_specs=[tok(GLA_DK, 0), tok(GLA_DK, 1), tok(GLA_DV, 1), tok(GLA_DV, 2), tok(GLOW_PAD, GLA_MAIN // GLOW_PAD),
                  whole((GLOW_PAD, GLA_DK)), whole((1, GLA_DK)), whole((1, GLA_HEAD_V))],
        out_specs=[tok(GLA_DV, 0), tok(GLA_DV, 0),
                   pl.BlockSpec((GLA_HEADS, c, c), lambda b, i: (0, b * nc + i, 0)),
                   pl.BlockSpec((None, GLA_HEADS, GLA_HEAD_V, GLA_HEAD_K), lambda b, i: (b * nc + i, 0, 0, 0))],
        out_shape=[jax.ShapeDtypeStruct((t, GLA_DV), BF16), jax.ShapeDtypeStruct((t, GLA_DV), F32),
                   jax.ShapeDtypeStruct((GLA_HEADS, t, c), F32),
                   jax.ShapeDtypeStruct((t // c, GLA_HEADS, GLA_HEAD_V, GLA_HEAD_K), F32)],
        scratch_shapes=[pltpu.VMEM((GLA_HEADS, GLA_HEAD_V, GLA_HEAD_K), F32), pltpu.VMEM((c, GLA_DK), F32),
                        pltpu.VMEM((c, c), F32)],
        compiler_params=_cparams(("parallel", "arbitrary")),
    )(proj, proj, proj, proj, proj, wgu, bias, ng)


def _gla_pair_bwd(q_ref, k_ref, b_scr, da_scr, dq_scr, dk_scr, h):
    c = GLA_CHUNK
    kc = pl.ds(h * GLA_HEAD_K, GLA_HEAD_K)
    lane = lax.broadcasted_iota(jnp.int32, (1, c), 1)
    for j in range(c):
        r0 = _pair_rows(j)
        rs = pl.ds(r0, c - r0)
        rows = lax.broadcasted_iota(jnp.int32, (c - r0, 1), 0) + r0
        e = jnp.exp(jnp.minimum(b_scr[rs, kc] - b_scr[pl.ds(j, 1), kc], 0.0))
        dacol = jnp.sum(jnp.where(lane == j, da_scr[rs, :], 0.0), axis=-1, keepdims=True)
        t1 = jnp.where(rows >= j, dacol, 0.0) * e
        dq_scr[rs, kc] += t1 * k_ref[pl.ds(j, 1), kc]
        dk_scr[pl.ds(j, 1), kc] += jnp.sum(t1 * (q_ref[rs, kc] * GLA_Q_SCALE), axis=0, keepdims=True)


def _gla_bwd(proj, wgu, bias, ng, o, a, states, dy, *, name):
    t = proj.shape[0]
    nb, nc, c = t // SEQ, GLA_NC, GLA_CHUNK

    def body(q_ref, k_ref, v_ref, r_ref, gl_ref, wgu_ref, bias_ref, ng_ref, o_ref, a_ref, stp_ref, stn_ref, dy_ref,
             dq_ref, dk_ref, dv_ref, dr_ref, dgl_ref, dwgu_ref, dbias_ref, dng_ref,
             dstate, b_scr, da_scr, dq_scr, dk_scr, dg_scr):
        first = jnp.logical_and(pl.program_id(0) == 0, pl.program_id(1) == 0)

        @pl.when(first)
        def _():
            dwgu_ref[...] = jnp.zeros(dwgu_ref.shape, F32)
            dbias_ref[...] = jnp.zeros(dbias_ref.shape, F32)
            dng_ref[...] = jnp.zeros(dng_ref.shape, F32)

        @pl.when(pl.program_id(1) == 0)
        def _():
            dstate[...] = jnp.zeros(dstate.shape, F32)

        pre, g = _gla_log_gate(gl_ref, wgu_ref, bias_ref)
        b_scr[...] = _cumsum_rows(g)
        ngv = ng_ref[...]
        tri = lax.broadcasted_iota(jnp.int32, (c, c), 0) >= lax.broadcasted_iota(jnp.int32, (c, c), 1)
        for h in range(GLA_HEADS):
            kc = pl.ds(h * GLA_HEAD_K, GLA_HEAD_K)
            vc = pl.ds(h * GLA_HEAD_V, GLA_HEAD_V)
            oh = o_ref[:, vc]
            rh = r_ref[:, vc]
            dyh = dy_ref[:, vc]
            rs = lax.rsqrt(jnp.mean(oh * oh, axis=-1, keepdims=True) + RMS_EPS)
            u = oh * rs
            sg = _sigmoid(rh)
            sr = rh * sg
            dr_ref[:, vc] = (dyh * (u * ngv) * (sg * (1.0 + rh * (1.0 - sg)))).astype(BF16)
            dng_ref[...] += jnp.sum(dyh * sr * u, axis=0, keepdims=True)
            du = dyh * sr * ngv
            do = (rs * (du - u * jnp.mean(du * u, axis=-1, keepdims=True))).astype(BF16)
            qh = q_ref[:, kc] * GLA_Q_SCALE
            kh = k_ref[:, kc]
            vh = v_ref[:, vc].astype(BF16)
            bh = b_scr[:, kc]
            blast = b_scr[pl.ds(c - 1, 1), kc]
            eb = jnp.exp(bh)
            ek = jnp.exp(blast - bh)
            dst = dstate[h]
            dst_b = dst.astype(BF16)
            dg_carry = jnp.sum(dst * stn_ref[h], axis=0, keepdims=True)
            da = lax.dot_general(do, vh, _NT, preferred_element_type=F32)
            da_scr[...] = jnp.where(tri, da, 0.0)
            dv = lax.dot_general(a_ref[h].astype(BF16), do, _TN, preferred_element_type=F32)
            dv = dv + lax.dot_general((kh * ek).astype(BF16), dst_b, _NT, preferred_element_type=F32)
            dv_ref[:, vc] = dv.astype(BF16)
            dq_scr[:, kc] = jnp.dot(do, stp_ref[h].astype(BF16), preferred_element_type=F32) * eb
            dk_scr[:, kc] = jnp.dot(vh, dst_b, preferred_element_type=F32) * ek
            _gla_pair_bwd(q_ref, k_ref, b_scr, da_scr, dq_scr, dk_scr, h)
            dq = dq_scr[:, kc]
            dk = dk_scr[:, kc]
            dg_scr[:, kc] = _suffix_sum_rows(qh * dq - kh * dk) + dg_carry
            dstate[h] = dst * jnp.exp(blast) + lax.dot_general(do, (qh * eb).astype(BF16), _TN, preferred_element_type=F32)
        dq_ref[...] = (dq_scr[...] * GLA_Q_SCALE).astype(BF16)
        dk_ref[...] = dk_scr[...].astype(BF16)
        dpre = dg_scr[...] * ((1.0 - _sigmoid(pre)) * (1.0 / GLA_GATE_NORMALIZER))
        dpre_b = dpre.astype(BF16)
        dbias_ref[...] += jnp.sum(dpre, axis=0, keepdims=True)
        dwgu_ref[...] += lax.dot_general(gl_ref[...].astype(BF16), dpre_b, _TN, preferred_element_type=F32)
        dgl_ref[...] = lax.dot_general(dpre_b, wgu_ref[...], _NT, preferred_element_type=F32).astype(BF16)

    def chunk(b, i):
        return b * nc + (nc - 1 - i)

    def tok(width, col):
        return pl.BlockSpec((c, width), lambda b, i: (chunk(b, i), col))

    whole = lambda shape: pl.BlockSpec(shape, lambda b, i: (0,) * len(shape))
    st_shape = (None, GLA_HEADS, GLA_HEAD_V, GLA_HEAD_K)
    return pl.pallas_call(
        body,
        name=name,
        grid=(nb, nc),
        in_specs=[tok(GLA_DK, 0), tok(GLA_DK, 1), tok(GLA_DV, 1), tok(GLA_DV, 2), tok(GLOW_PAD, GLA_MAIN // GLOW_PAD),
                  whole((GLOW_PAD, GLA_DK)), whole((1, GLA_DK)), whole((1, GLA_HEAD_V)),
                  tok(GLA_DV, 0),
                  pl.BlockSpec((GLA_HEADS, c, c), lambda b, i: (0, chunk(b, i), 0)),
                  pl.BlockSpec(st_shape, lambda b, i: (chunk(b, i), 0, 0, 0)),
                  pl.BlockSpec(st_shape, lambda b, i: (b * nc + jnp.minimum(nc - i, nc - 1), 0, 0, 0)),
                  tok(GLA_DV, 0)],
        out_specs=[tok(GLA_DK, 0), tok(GLA_DK, 0), tok(GLA_DV, 0), tok(GLA_DV, 0), tok(GLOW_PAD, 0),
                   whole((GLOW_PAD, GLA_DK)), whole((1, GLA_DK)), whole((1, GLA_HEAD_V))],
        out_shape=[jax.ShapeDtypeStruct((t, GLA_DK), BF16), jax.ShapeDtypeStruct((t, GLA_DK), BF16),
                   jax.ShapeDtypeStruct((t, GLA_DV), BF16), jax.ShapeDtypeStruct((t, GLA_DV), BF16),
                   jax.ShapeDtypeStruct((t, GLOW_PAD), BF16),
                   jax.ShapeDtypeStruct((GLOW_PAD, GLA_DK), F32), jax.ShapeDtypeStruct((1, GLA_DK), F32),
                   jax.ShapeDtypeStruct((1, GLA_HEAD_V), F32)],
        scratch_shapes=[pltpu.VMEM((GLA_HEADS, GLA_HEAD_V, GLA_HEAD_K), F32), pltpu.VMEM((c, GLA_DK), F32),
                        pltpu.VMEM((c, c), F32), pltpu.VMEM((c, GLA_DK), F32), pltpu.VMEM((c, GLA_DK), F32),
                        pltpu.VMEM((c, GLA_DK), F32)],
        compiler_params=_cparams(("arbitrary", "arbitrary")),
    )(proj, proj, proj, proj, proj, wgu, bias, ng, o, a, states, states, dy)


DIL_STEPS = DIL_BLOCK
DIL_SCALE = DIL_HEAD_DIM ** -0.5
DIL_HEADS_PER_STEP = {1: 8, 4: 1, 16: 1}


def _dil_mask(i):
    rowi = lax.broadcasted_iota(jnp.int32, (DIL_BLOCK, 2 * DIL_BLOCK), 0)
    colj = lax.broadcasted_iota(jnp.int32, (DIL_BLOCK, 2 * DIL_BLOCK), 1)
    dist = rowi + DIL_BLOCK - colj
    band = jnp.logical_and(dist >= 0, dist <= DIL_STEPS)
    return jnp.logical_and(band, jnp.logical_or(i > 0, colj >= DIL_BLOCK))


def _dil_geometry(t, gi):
    _, d = DIL_PATTERNS[gi]
    return d, SEQ // d // DIL_BLOCK, t // SEQ, DIL_BLOCK * d, DIL_HEADS_PER_STEP[d]


def _dil_specs(gi, d, nq, rows, hps, order):
    width = hps * DIL_HEAD_DIM
    per_part = DIL_WIDTH // width

    def named(f):
        return lambda *idx: f(**dict(zip(order, idx)))

    def block(i, prev):
        ic = jnp.minimum(i, nq - 1)
        return jnp.maximum(ic - 1, 0) if prev else ic

    def part(j, prev):
        return pl.BlockSpec((rows, width), named(lambda b, i, h: (b * nq + block(i, prev), (gi * 3 + j) * per_part + h)))

    cur = pl.BlockSpec((rows, width), named(lambda b, i, h: (b * nq + block(i, False), h)))
    done = pl.BlockSpec((rows, width), named(lambda b, i, h: (b * nq + jnp.maximum(i - 1, 0), h)))
    return [part(0, False), part(1, False), part(1, True), part(2, False), part(2, True)], cur, done


def _dil_rows(r, d):
    return pl.ds(r, DIL_BLOCK, stride=d) if d > 1 else pl.ds(0, DIL_BLOCK)


def _dil_fwd(proj, gi, *, name):
    t = proj.shape[0]
    d, nq, nb, rows, hps = _dil_geometry(t, gi)

    def body(q_ref, kc_ref, kp_ref, vc_ref, vp_ref, o_ref, lse_ref):
        mask = _dil_mask(pl.program_id(1))
        for h in range(hps):
            hc = pl.ds(h * DIL_HEAD_DIM, DIL_HEAD_DIM)
            for r in range(d):
                rr = _dil_rows(r, d)
                qh = q_ref[rr, hc].astype(BF16)
                kcat = jnp.concatenate([kp_ref[rr, hc], kc_ref[rr, hc]], axis=0).astype(BF16)
                vcat = jnp.concatenate([vp_ref[rr, hc], vc_ref[rr, hc]], axis=0).astype(BF16)
                s = lax.dot_general(qh, kcat, _NT, preferred_element_type=F32) * DIL_SCALE
                s = jnp.where(mask, s, -jnp.inf)
                m = jnp.max(s, axis=-1, keepdims=True)
                p = jnp.exp(s - m)
                l = jnp.sum(p, axis=-1, keepdims=True)
                o_ref[rr, hc] = jnp.dot((p / l).astype(BF16), vcat, preferred_element_type=F32)
                lse_ref[rr, hc] = jnp.broadcast_to(m + jnp.log(l), (DIL_BLOCK, DIL_HEAD_DIM))

    parts, cur, _ = _dil_specs(gi, d, nq, rows, hps, "bih")
    return pl.pallas_call(
        body,
        name=name,
        grid=(nb, nq, DIL_HEADS // hps),
        in_specs=parts,
        out_specs=[cur, cur],
        out_shape=[jax.ShapeDtypeStruct((t, DIL_WIDTH), F32)] * 2,
        compiler_params=_cparams(("parallel", "parallel", "parallel")),
    )(proj, proj, proj, proj, proj)


def _dil_bwd(proj, gi, lse, do, delta, *, name):
    t = proj.shape[0]
    d, nq, nb, rows, hps = _dil_geometry(t, gi)

    def body(q_ref, kc_ref, kp_ref, vc_ref, vp_ref, lse_ref, do_ref, dl_ref, dq_ref, dk_ref, dv_ref,
             ck, cv, fk, fv, dq_s):
        i = pl.program_id(2)

        @pl.when(i == 0)
        def _():
            ck[...] = jnp.zeros(ck.shape, F32)
            cv[...] = jnp.zeros(cv.shape, F32)

        @pl.when(i < nq)
        def _():
            mask = _dil_mask(i)
            for h in range(hps):
                hc = pl.ds(h * DIL_HEAD_DIM, DIL_HEAD_DIM)
                h1 = pl.ds(h * DIL_HEAD_DIM, 1)
                for r in range(d):
                    rr = _dil_rows(r, d)
                    qh = q_ref[rr, hc].astype(BF16)
                    kcat = jnp.concatenate([kp_ref[rr, hc], kc_ref[rr, hc]], axis=0).astype(BF16)
                    vcat = jnp.concatenate([vp_ref[rr, hc], vc_ref[rr, hc]], axis=0).astype(BF16)
                    doh = do_ref[rr, hc].astype(BF16)
                    s = lax.dot_general(qh, kcat, _NT, preferred_element_type=F32) * DIL_SCALE
                    p = jnp.exp(jnp.where(mask, s, -jnp.inf) - lse_ref[rr, h1])
                    dp = lax.dot_general(doh, vcat, _NT, preferred_element_type=F32)
                    ds = (p * (dp + dl_ref[rr, h1]) * DIL_SCALE).astype(BF16)
                    dq_s[rr, hc] = jnp.dot(ds, kcat, preferred_element_type=F32)
                    dkcat = lax.dot_general(ds, qh, _TN, preferred_element_type=F32)
                    dvcat = lax.dot_general(p.astype(BF16), doh, _TN, preferred_element_type=F32)
                    fk[rr, hc] = ck[rr, hc] + dkcat[:DIL_BLOCK]
                    fv[rr, hc] = cv[rr, hc] + dvcat[:DIL_BLOCK]
                    ck[rr, hc] = dkcat[DIL_BLOCK:]
                    cv[rr, hc] = dvcat[DIL_BLOCK:]
            dq_ref[...] = dq_s[...].astype(BF16)

            @pl.when(i > 0)
            def _():
                dk_ref[...] = fk[...].astype(BF16)
                dv_ref[...] = fv[...].astype(BF16)

        @pl.when(i == nq)
        def _():
            dk_ref[...] = ck[...].astype(BF16)
            dv_ref[...] = cv[...].astype(BF16)

    parts, cur, done = _dil_specs(gi, d, nq, rows, hps, "bhi")
    shape = jax.ShapeDtypeStruct((t, DIL_WIDTH), BF16)
    tile = pltpu.VMEM((rows, hps * DIL_HEAD_DIM), F32)
    return pl.pallas_call(
        body,
        name=name,
        grid=(nb, DIL_HEADS // hps, nq + 1),
        in_specs=parts + [cur, cur, cur],
        out_specs=[cur, done, done],
        out_shape=[shape, shape, shape],
        scratch_shapes=[tile] * 5,
        compiler_params=_cparams(("parallel", "parallel", "arbitrary")),
    )(proj, proj, proj, proj, proj, lse, do, delta)


MIX_ROWS = 256


def _head_rowsum(x):
    parts = []
    for h in range(DIL_HEADS):
        s = jnp.sum(x[:, h * DIL_HEAD_DIM:(h + 1) * DIL_HEAD_DIM], axis=-1, keepdims=True)
        parts.append(jnp.broadcast_to(s, (x.shape[0], DIL_HEAD_DIM)))
    return jnp.concatenate(parts, axis=-1)


def _mix_weights(lse_refs):
    ls = [r[...] for r in lse_refs]
    m = jnp.maximum(jnp.maximum(ls[0], ls[1]), ls[2])
    es = [jnp.exp(l - m) for l in ls]
    inv = 1.0 / (es[0] + es[1] + es[2])
    return [e * inv for e in es]


def _dil_mix_fwd(os_, lses, *, name):
    t = os_[0].shape[0]

    def body(o0, o1, o2, l0, l1, l2, out_ref):
        w = _mix_weights((l0, l1, l2))
        out_ref[...] = (w[0] * o0[...] + w[1] * o1[...] + w[2] * o2[...]).astype(BF16)

    row = pl.BlockSpec((MIX_ROWS, DIL_WIDTH), lambda i: (i, 0))
    return pl.pallas_call(
        body, name=name, grid=(t // MIX_ROWS,), in_specs=[row] * 6, out_specs=row,
        out_shape=jax.ShapeDtypeStruct((t, DIL_WIDTH), BF16), compiler_params=_cparams(("parallel",)),
    )(*os_, *lses)


def _dil_mix_bwd(os_, lses, dout, *, name):
    t = os_[0].shape[0]

    def body(o0, o1, o2, l0, l1, l2, d_ref, do0, do1, do2, dl0, dl1, dl2):
        w = _mix_weights((l0, l1, l2))
        dv = d_ref[...]
        mix = w[0] * o0[...] + w[1] * o1[...] + w[2] * o2[...]
        bar = _head_rowsum(dv * mix)
        for wg, do_ref, dl_ref in zip(w, (do0, do1, do2), (dl0, dl1, dl2)):
            do_ref[...] = wg * dv
            dl_ref[...] = -wg * bar

    row = pl.BlockSpec((MIX_ROWS, DIL_WIDTH), lambda i: (i, 0))
    outs = pl.pallas_call(
        body, name=name, grid=(t // MIX_ROWS,), in_specs=[row] * 7, out_specs=[row] * 6,
        out_shape=[jax.ShapeDtypeStruct((t, DIL_WIDTH), F32)] * 6,
        compiler_params=_cparams(("parallel",)),
    )(*os_, *lses, dout)
    return outs[:3], outs[3:]


_MESH = pl.DeviceIdType.MESH
_ANY = pl.BlockSpec(memory_space=pl.ANY)


def _position():
    return lax.axis_index("x"), lax.axis_index("y"), lax.axis_index("c")


AG_COPIES = 7


def _run_comm(task, *, name):
    n_in, n_out = len(task.inputs), len(task.out_shapes)

    def body(*refs):
        parts = (refs[:n_in], refs[n_in:n_in + n_out], refs[n_in + n_out:])
        task.start(*parts)
        task.middle(*parts)
        task.finish(*parts)

    return pl.pallas_call(
        body, name=name, out_shape=task.out_shapes, in_specs=[_ANY] * n_in, out_specs=[_ANY] * n_out,
        scratch_shapes=task.sem_shapes,
    )(*task.inputs)


def _gather_task(shards):
    n = len(shards)

    def copies(x_refs, out_refs, sems):
        send_sems, recv_sems, local_sems = sems
        x, y, cc = _position()
        me, sibling = (x, y, cc), (x, y, 1 - cc)
        chips = [(1 - x, y), (x, 1 - y), (1 - x, 1 - y)]

        def copy(w, k, block, to, own=False):
            px, py, pc = block
            slot = out_refs[w].at[4 * px + 2 * py + pc]
            return pltpu.make_async_remote_copy(
                src_ref=x_refs[w] if own else slot, dst_ref=slot,
                send_sem=send_sems.at[AG_COPIES * w + k], recv_sem=recv_sems.at[AG_COPIES * w + k],
                device_id=to, device_id_type=_MESH)

        mine = [pltpu.make_async_copy(x_refs[w], out_refs[w].at[4 * x + 2 * y + cc], local_sems.at[w]) for w in range(n)]
        first = [[copy(w, 0, me, sibling, own=True)] + [copy(w, 1 + j, me, (*chip, cc), own=True) for j, chip in enumerate(chips)]
                 for w in range(n)]
        landed = [[copy(w, 1 + j, (*chip, cc), me) for j, chip in enumerate(chips)] for w in range(n)]
        passed = [[copy(w, 4 + j, (*chip, cc), sibling) for j, chip in enumerate(chips)] for w in range(n)]
        from_sibling = [[copy(w, 0, sibling, me)] + [copy(w, 4 + j, (*chip, 1 - cc), me) for j, chip in enumerate(chips)]
                        for w in range(n)]
        return mine, first, landed, passed, from_sibling

    def start(ins, outs, sems):
        mine, first, _, _, _ = copies(ins, outs, sems)
        for w in range(n):
            mine[w].start()
            for cp in first[w]:
                cp.start()

    def middle(ins, outs, sems):
        _, _, landed, passed, _ = copies(ins, outs, sems)
        for j in range(3):
            for w in range(n):
                landed[w][j].wait_recv()
                passed[w][j].start()

    def finish(ins, outs, sems):
        mine, first, _, passed, from_sibling = copies(ins, outs, sems)
        for w in range(n):
            for cp in from_sibling[w]:
                cp.wait_recv()
        for w in range(n):
            for cp in first[w] + passed[w]:
                cp.wait_send()
            mine[w].wait()

    return _Comm(shards, [jax.ShapeDtypeStruct((N_DEV,) + s.shape, s.dtype) for s in shards],
                 [pltpu.SemaphoreType.DMA((AG_COPIES * n,)), pltpu.SemaphoreType.DMA((AG_COPIES * n,)),
                  pltpu.SemaphoreType.DMA((n,))], start, middle, finish)


def _parity_half(ref, parity, half_rows):
    if half_rows is None:
        return ref.at[:, parity]
    return ref.at[:, pl.ds(parity * half_rows, half_rows), :]


def _exchange_task(make_copies, inputs, out_shapes, n_copies):
    def start(ins, outs, sems):
        for cp in make_copies(ins, outs, sems):
            cp.start()

    def finish(ins, outs, sems):
        for cp in make_copies(ins, outs, sems):
            cp.wait()

    return _Comm(inputs, out_shapes, [pltpu.SemaphoreType.DMA((n_copies,)), pltpu.SemaphoreType.DMA((n_copies,))],
                 start, lambda ins, outs, sems: None, finish)


def _sibling_task(gs, half_rows):
    n = len(gs)

    def make_copies(g_refs, out_refs, sems):
        x, y, cc = _position()
        return [pltpu.make_async_remote_copy(
            src_ref=_parity_half(g_refs[k], 1 - cc, half_rows[k]), dst_ref=out_refs[k],
            send_sem=sems[0].at[k], recv_sem=sems[1].at[k],
            device_id=(x, y, 1 - cc), device_id_type=_MESH) for k in range(n)]

    def out_shape(g, hr):
        return jax.ShapeDtypeStruct((4,) + (g.shape[2:] if hr is None else (hr, g.shape[2])), g.dtype)

    return _exchange_task(make_copies, gs, [out_shape(g, hr) for g, hr in zip(gs, half_rows)], n)


def _chips_task(ps):
    n = len(ps)

    def make_copies(p_refs, out_refs, sems):
        x, y, cc = _position()
        copies = []
        for w in range(n):
            for k in (1, 2, 3):
                px = 1 - x if k >> 1 else x
                py = 1 - y if k & 1 else y
                copies.append(pltpu.make_async_remote_copy(
                    src_ref=p_refs[w].at[2 * px + py], dst_ref=out_refs[w].at[k - 1],
                    send_sem=sems[0].at[3 * w + k - 1], recv_sem=sems[1].at[3 * w + k - 1],
                    device_id=(px, py, cc), device_id_type=_MESH))
        return copies

    return _exchange_task(make_copies, ps, [jax.ShapeDtypeStruct((3,) + p.shape[1:], p.dtype) for p in ps], 3 * n)


def _add_sibling(g, r1, place, half_rows, *, tr, tc, name):
    _, r, c = r1.shape
    if half_rows is None:
        g_spec = pl.BlockSpec((None, None, tr, tc), lambda i, j, k, pc: (k, pc[0], i, j))
    else:
        per_half = half_rows // tr
        g_spec = pl.BlockSpec((None, tr, tc), lambda i, j, k, pc: (k, pc[0] * per_half + i, j))

    def body(pc_ref, g_ref, r_ref, pb_ref, own_ref):
        s = g_ref[...] + r_ref[...]
        pb_ref[...] = s.astype(BF16)

        @pl.when(pl.program_id(2) == pc_ref[1])
        def _():
            own_ref[...] = s

    grid_spec = pltpu.PrefetchScalarGridSpec(
        num_scalar_prefetch=1,
        grid=(r // tr, c // tc, 4),
        in_specs=[g_spec, pl.BlockSpec((None, tr, tc), lambda i, j, k, pc: (k, i, j))],
        out_specs=[pl.BlockSpec((None, tr, tc), lambda i, j, k, pc: (k, i, j)),
                   pl.BlockSpec((tr, tc), lambda i, j, k, pc: (i, j))],
    )
    return pl.pallas_call(
        body, name=name, grid_spec=grid_spec,
        out_shape=[jax.ShapeDtypeStruct((4, r, c), BF16), jax.ShapeDtypeStruct((r, c), F32)],
        compiler_params=_cparams(("parallel", "parallel", "arbitrary")),
    )(place, g, r1)


def _adamw_math(g, w, m, v):
    m = ADAM_B1 * m + (1.0 - ADAM_B1) * g
    v = ADAM_B2 * v + (1.0 - ADAM_B2) * (g * g)
    m_hat = m / (1.0 - ADAM_B1 ** ADAM_STEP)
    v_hat = v / (1.0 - ADAM_B2 ** ADAM_STEP)
    delta = -ADAM_LR * (m_hat / (jnp.sqrt(v_hat) + ADAM_EPS) + ADAM_WD * w)
    return delta, m, v


def _adamw_big(own, r2, w, m, v, prev, layer, *, tr, tc, name):
    _, r, c = w.shape

    def body(p_ref, r2_ref, w_ref, m_ref, v_ref, a0, a1, a2, a3, g_ref, d_ref, mo_ref, vo_ref):
        g = ((p_ref[...] + r2_ref[0].astype(F32)) + r2_ref[1].astype(F32)) + r2_ref[2].astype(F32)
        delta, mn, vn = _adamw_math(g, w_ref[...], m_ref[...], v_ref[...])
        g_ref[...] = g
        d_ref[...] = delta
        mo_ref[...] = mn
        vo_ref[...] = vn

    lay = pl.BlockSpec((None, tr, tc), lambda i, j: (layer, i, j))
    return pl.pallas_call(
        body, name=name, grid=(r // tr, c // tc),
        in_specs=[pl.BlockSpec((tr, tc), lambda i, j: (i, j)), pl.BlockSpec((3, tr, tc), lambda i, j: (0, i, j)),
                  lay, lay, lay, _ANY, _ANY, _ANY, _ANY],
        out_specs=[lay, lay, lay, lay],
        out_shape=[jax.ShapeDtypeStruct(w.shape, F32)] * 4,
        input_output_aliases={5: 0, 6: 1, 7: 2, 8: 3},
        compiler_params=_cparams(("parallel", "parallel")),
    )(own, r2, w, m, v, *prev)


SMALL_COLS = 1024


def _sum_gathered(parts, *, name):
    _, r, c = parts.shape

    def body(p_ref, o_ref):
        acc = p_ref[0]
        for k in range(1, N_DEV):
            acc = acc + p_ref[k]
        o_ref[...] = acc

    return pl.pallas_call(
        body, name=name, grid=(1,), in_specs=[pl.BlockSpec((N_DEV, r, c), lambda i: (0, 0, 0))],
        out_specs=pl.BlockSpec((r, c), lambda i: (0, 0)), out_shape=jax.ShapeDtypeStruct((r, c), F32),
        compiler_params=_cparams(("arbitrary",)),
    )(parts)


def _adamw_small(g, w, m, v, *, name):
    r, c = g.shape

    def body(g_ref, w_ref, m_ref, v_ref, d_ref, mo_ref, vo_ref):
        delta, mn, vn = _adamw_math(g_ref[...], w_ref[...], m_ref[...], v_ref[...])
        d_ref[...] = delta
        mo_ref[...] = mn
        vo_ref[...] = vn

    spec = pl.BlockSpec((r, c), lambda i: (0, 0))
    return pl.pallas_call(
        body, name=name, grid=(1,), in_specs=[spec] * 4, out_specs=[spec] * 3,
        out_shape=[jax.ShapeDtypeStruct((r, c), F32)] * 3, compiler_params=_cparams(("arbitrary",)),
    )(g, w, m, v)


WEIGHT_NAMES = ("gla_w_in", "gla_w_gate_up", "gla_gate_bias", "gla_norm_g", "gla_w_out", "dil_w_in", "dil_w_out",
                "ffn_w_up", "ffn_conv_w", "ffn_conv_b", "ffn_w_down", "ln_g", "ln_b")
ADAM_TILES = {"gla_w_in": (256, 770), "gla_w_out": (128, 2048), "dil_w_in": (256, 1152), "dil_w_out": (512, 256),
              "ffn_w_up": (344, 1024), "ffn_w_down": (344, 1024)}
ADD_TILES = {**ADAM_TILES, "ffn_w_up": (352, 1024)}
FF_DOWN_SHARD = D_FF // N_DEV
VEC_COLS = 128


def _pad_axis(a, axis, to):
    pads = [(0, 0)] * a.ndim
    pads[axis] = (0, to - a.shape[axis])
    return jnp.pad(a, pads)


def _ff_cols(blocks):
    r = blocks.shape[1]
    return _pad_axis(blocks, 2, FF_SHARD_PAD).reshape(2, 4, r, FF_SHARD_PAD).transpose(0, 2, 1, 3).reshape(2, r, FF_HALF_PAD)


def _ff_cols_back(a):
    r = a.shape[1]
    return a.reshape(2, r, 4, FF_SHARD_PAD)[..., :FF_SHARD].transpose(1, 0, 2, 3).reshape(r, 2 * D_FF)


def _vec_parts(l, w):
    return [w["ffn_conv_w"][l], w["ln_g"][l], w["ln_b"][l]] + ([w["gla_w_gate_up"][l // 2]] if l % 2 == 0 else [])


def _layer_shards(l, w):
    j = l // 2
    gla = l % 2 == 0
    parts = _vec_parts(l, w)
    vec_rows = -(-sum(math.prod(a.shape) for a in parts) // (VEC_COLS * SUBLANES)) * SUBLANES
    mixer = [(w["gla_w_in"] if gla else w["dil_w_in"])[j].astype(BF16),
             (w["gla_w_out"] if gla else w["dil_w_out"])[j].astype(BF16), _pack_rows(parts, vec_rows, VEC_COLS)]
    w_up_t = _pad_axis(jnp.swapaxes(w["ffn_w_up"][l], 0, 1).astype(BF16), 0, FF_SHARD_PAD)
    return mixer, w_up_t, w["ffn_w_down"][l].astype(BF16)


def _layer_weights(l, w, gathered_mixer, g_up, g_down):
    j = l // 2
    gla = l % 2 == 0
    g_in, g_out, g_vec = gathered_mixer
    vec_shapes = [a.shape for a in _vec_parts(l, w)]
    out = {"w_up_t": g_up}
    wd = g_down.reshape(4, FF_SHARD, D_MODEL)
    out["w_down"] = _pad_axis(wd, 1, FF_SHARD_PAD).reshape(FF_HALF_PAD, D_MODEL)
    vec = [jnp.stack(p) for p in zip(*[_unpack_rows(g_vec[d], vec_shapes) for d in range(N_DEV)])]
    out["conv_w"] = _ff_cols(vec[0])
    out["conv_b"] = _pad_axis(w["ffn_conv_b"][l].reshape(N_DEV, FF_SHARD), 1, FF_SHARD_PAD).reshape(2, 1, FF_HALF_PAD)
    out["ln_g"] = vec[1].transpose(1, 0, 2).reshape(2, 1, D_MODEL)
    out["ln_b"] = vec[2].transpose(1, 0, 2).reshape(2, 1, D_MODEL)
    if gla:
        win = g_in.transpose(1, 0, 2).reshape(D_MODEL, GLA_IN)
        out["w_in"] = _pad_axis(win, 1, GLA_MAIN + GLOW_PAD)
        wgu = vec[3].transpose(1, 0, 2).reshape(GLA_GATE_RANK, GLA_DK).astype(BF16)
        out["w_gate_up"] = _pad_axis(wgu, 0, GLOW_PAD)
        out["w_out"] = g_out.reshape(GLA_DV, D_MODEL)
        out["gate_bias"] = w["gla_gate_bias"][j].reshape(1, GLA_DK)
        out["norm_g"] = w["gla_norm_g"][j].reshape(1, GLA_HEAD_V)
    else:
        out["w_in"] = g_in
        out["w_out"] = g_out.transpose(1, 0, 2).reshape(DIL_WIDTH, D_MODEL)
    return out


def _by_chip_parity(blocks):
    return blocks.reshape((4, 2) + blocks.shape[1:])


def _col_blocks(dw, width):
    r = dw.shape[0]
    return dw.reshape(r, N_DEV, width).transpose(1, 0, 2)


def _carry(call, task):
    if task is None:
        return call(None), []
    return call(task)


def _sibling_sum(l, n, g, half_rows, from_sibling, place):
    return _add_sibling(g, from_sibling, place, half_rows, tr=ADD_TILES[n][0], tc=ADD_TILES[n][1], name=f"l{l}_{n}_add")


def _ffn_fwd(l, yb, lw, task_up, task_down):
    h, got_up = _carry(lambda c: _ffn_hidden(yb, lw["w_up_t"], comm=c, name=f"l{l}_ffn_hidden"), task_up)
    act = _convgate_fwd(h, lw["conv_w"], lw["conv_b"], name=f"l{l}_convgate")
    ffn, got_down = _carry(lambda c: _matmul(act, lw["w_down"], tm=1024, tn=1024, tk=2816, comm=c, name=f"l{l}_ffn_down"), task_down)
    return ffn, (h, act), got_up, got_down


def _ffn_bwd(l, yb, dz, dzb, lw, saved, place, pending):
    h, act = saved
    t = yb.shape[0]
    dact = _matmul(dzb, lw["w_down"], tb=True, tm=512, tn=2816, tk=D_MODEL, name=f"l{l}_ffn_dact")
    d_down = _ffn_down_dw(act, dzb, name=f"l{l}_ffn_dwdown")
    dh, dcw, dcb = _convgate_bwd(h, dact, lw["conv_w"], lw["conv_b"], name=f"l{l}_convgate_bwd")
    dh = dh.reshape(2 * t, FF_HALF_PAD)
    tasks = [_sibling_task([d_down], [FF_DOWN_SHARD])] + ([_chips_task(pending)] if pending else [])
    d_up_t, got = _ffn_hidden_dw(dh, yb, comm=_join_comm(tasks), name=f"l{l}_ffn_dwup")
    d_up = _by_chip_parity(d_up_t)
    dy, got_up = _ffn_hidden_dy(dh, lw["w_up_t"], dz, DEEPNORM_ALPHA, comm=_sibling_task([d_up], [None]), name=f"l{l}_ffn_dy")
    sums = {"ffn_w_down": _sibling_sum(l, "ffn_w_down", d_down, FF_DOWN_SHARD, got[0], place),
            "ffn_w_up": _sibling_sum(l, "ffn_w_up", d_up, None, got_up[0], place)}
    small = {"ffn_conv_w": _ff_cols_back(dcw), "ffn_conv_b": _ff_cols_back(dcb)[0]}
    return dy, sums, small, got[1:]


def _gla_layer_fwd(l, hb, lw, task):
    proj, got = _carry(lambda c: _matmul(hb, lw["w_in"], tm=1024, tn=896, tk=D_MODEL, comm=c, name=f"l{l}_gla_proj"), task)
    y, o, a, st = _gla_fwd(proj, lw["w_gate_up"], lw["gate_bias"], lw["norm_g"], name=f"l{l}_gla")
    mix = _matmul(y, lw["w_out"], tm=1024, tn=1024, tk=GLA_DV, name=f"l{l}_gla_out")
    return mix, (proj, y, o, a, st), got


def _gla_layer_bwd(l, hb, dz, dzb, lw, saved, ffn_sums):
    proj, y, o, a, st = saved
    dy = _matmul(dzb, lw["w_out"], tb=True, tm=1024, tn=1024, tk=D_MODEL, name=f"l{l}_gla_dy")
    d_out = _matmul(y, dzb, ta=True, tm=1024, tn=1024, tk=2048, name=f"l{l}_gla_dwout")
    dq, dk, dv, dr, dgl, dwgu, dbias, dng = _gla_bwd(proj, lw["w_gate_up"], lw["gate_bias"], lw["norm_g"], o, a, st, dy, name=f"l{l}_gla_bwd")
    dproj = jnp.concatenate([dq, dk, dv, dr, dgl], axis=-1)
    d_in, got_down = _matmul(hb, dproj, ta=True, tm=1024, tn=896, tk=2048, comm=_chips_task([ffn_sums["ffn_w_down"][0]]),
                             name=f"l{l}_gla_dwin")
    dx, got_up = _matmul(dproj, lw["w_in"], tb=True, tm=1024, tn=1024, tk=896, res=dz, res_scale=DEEPNORM_ALPHA,
                         comm=_chips_task([ffn_sums["ffn_w_up"][0]]), name=f"l{l}_gla_dx")
    big = {"gla_w_in": _by_chip_parity(_col_blocks(d_in[:, :GLA_IN], GLA_IN // N_DEV)),
           "gla_w_out": _by_chip_parity(d_out.reshape(N_DEV, GLA_DV // N_DEV, D_MODEL))}
    small = {"gla_w_gate_up": dwgu[:GLA_GATE_RANK], "gla_gate_bias": dbias[0], "gla_norm_g": dng[0]}
    return dx, big, small, {"ffn_w_down": got_down[0], "ffn_w_up": got_up[0]}


def _dil_layer_fwd(l, hb, lw, task):
    proj, got = _carry(lambda c: _mm_colblocks(hb, lw["w_in"], comm=c, name=f"l{l}_dil_proj"), task)
    os_, lses = [], []
    for gi in range(len(DIL_PATTERNS)):
        o, lse = _dil_fwd(proj, gi, name=f"l{l}_dil_attn{gi}")
        os_.append(o)
        lses.append(lse)
    omix = _dil_mix_fwd(os_, lses, name=f"l{l}_dil_mix")
    mix = _matmul(omix, lw["w_out"], tm=1024, tn=1024, tk=DIL_WIDTH, name=f"l{l}_dil_out")
    return mix, (proj, os_, lses, omix), got


def _dil_layer_bwd(l, hb, dz, dzb, lw, saved, ffn_sums):
    proj, os_, lses, omix = saved
    dout = _matmul(dzb, lw["w_out"], tb=True, tm=1024, tn=1024, tk=D_MODEL, name=f"l{l}_dil_dy")
    d_out = _matmul(omix, dzb, ta=True, tm=1024, tn=1024, tk=2048, name=f"l{l}_dil_dwout")
    dos, dls = _dil_mix_bwd(os_, lses, dout, name=f"l{l}_dil_mix_bwd")
    parts = []
    for gi in range(len(DIL_PATTERNS)):
        parts += list(_dil_bwd(proj, gi, lses[gi], dos[gi], dls[gi], name=f"l{l}_dil_attn_bwd{gi}"))
    dproj = jnp.concatenate(parts, axis=-1)
    d_in, got_down = _mm_grad_colblocks(hb, dproj, DIL_IN // N_DEV, comm=_chips_task([ffn_sums["ffn_w_down"][0]]),
                                        name=f"l{l}_dil_dwin")
    dx, got_up = _mm_colblocks_t(dproj, lw["w_in"], dz, DEEPNORM_ALPHA, comm=_chips_task([ffn_sums["ffn_w_up"][0]]),
                                 name=f"l{l}_dil_dx")
    big = {"dil_w_in": _by_chip_parity(d_in), "dil_w_out": _by_chip_parity(_col_blocks(d_out, D_MODEL // N_DEV))}
    return dx, big, {}, {"ffn_w_down": got_down[0], "ffn_w_up": got_up[0]}


def _pack_rows(arrays, rows, cols=SMALL_COLS):
    flat = [a.reshape(-1) for a in arrays]
    used = sum(f.shape[0] for f in flat)
    return jnp.concatenate(flat + [jnp.zeros((rows * cols - used,), F32)]).reshape(rows, cols)


def _unpack_rows(packed, shapes):
    flat, out, off = packed.reshape(-1), [], 0
    for s in shapes:
        n = math.prod(s)
        out.append(flat[off:off + n].reshape(s))
        off += n
    return out


def _rows_for(shapes):
    n = sum(math.prod(s) for s in shapes)
    return -(-n // (SMALL_COLS * SUBLANES)) * SUBLANES


def kernel(x, gla_w_in, gla_w_gate_up, gla_gate_bias, gla_norm_g, gla_w_out, dil_w_in, dil_w_out, ffn_w_up, ffn_conv_w, ffn_conv_b, ffn_w_down, ln_g, ln_b, loss_target, m_gla_w_in, m_gla_w_gate_up, m_gla_gate_bias, m_gla_norm_g, m_gla_w_out, m_dil_w_in, m_dil_w_out, m_ffn_w_up, m_ffn_conv_w, m_ffn_conv_b, m_ffn_w_down, m_ln_g, m_ln_b, v_gla_w_in, v_gla_w_gate_up, v_gla_gate_bias, v_gla_norm_g, v_gla_w_out, v_dil_w_in, v_dil_w_out, v_ffn_w_up, v_ffn_conv_w, v_ffn_conv_b, v_ffn_w_down, v_ln_g, v_ln_b):
    w = dict(zip(WEIGHT_NAMES, (gla_w_in, gla_w_gate_up, gla_gate_bias, gla_norm_g, gla_w_out, dil_w_in, dil_w_out,
                                ffn_w_up, ffn_conv_w, ffn_conv_b, ffn_w_down, ln_g, ln_b)))
    mom = dict(zip(WEIGHT_NAMES, (m_gla_w_in, m_gla_w_gate_up, m_gla_gate_bias, m_gla_norm_g, m_gla_w_out, m_dil_w_in,
                                  m_dil_w_out, m_ffn_w_up, m_ffn_conv_w, m_ffn_conv_b, m_ffn_w_down, m_ln_g, m_ln_b)))
    var = dict(zip(WEIGHT_NAMES, (v_gla_w_in, v_gla_w_gate_up, v_gla_gate_bias, v_gla_norm_g, v_gla_w_out, v_dil_w_in,
                                  v_dil_w_out, v_ffn_w_up, v_ffn_conv_w, v_ffn_conv_b, v_ffn_w_down, v_ln_g, v_ln_b)))
    xi, yi, ci = _position()
    dev = 4 * xi + 2 * yi + ci
    place = jnp.stack([ci, 2 * xi + yi]).astype(jnp.int32)
    t = x.shape[0] * x.shape[1]
    h = x.reshape(t, D_MODEL)
    hb = h.astype(BF16)
    target = loss_target.reshape(t, D_MODEL)

    mixer0, up0, down0 = _layer_shards(0, w)
    got = _run_comm(_gather_task(mixer0 + [up0, down0]), name="gather_l0")
    lws = [_layer_weights(0, w, got[:3], got[3], got[4])]
    saved = []
    for l in range(DEPTH):
        lw = lws[l]
        nxt = _layer_shards(l + 1, w) if l + 1 < DEPTH else None
        task = lambda arrays: _gather_task(arrays) if nxt else None
        mix, mixer_saved, got_mixer = (_gla_layer_fwd if l % 2 == 0 else _dil_layer_fwd)(l, hb, lw, task(nxt and nxt[0]))
        y1, y1b, xh1, rs1 = _ln_fwd(h, mix, lw["ln_g"][0], lw["ln_b"][0], name=f"l{l}_ln1")
        ffn, ffn_saved, got_up, got_down = _ffn_fwd(l, y1b, lw, task(nxt and [nxt[1]]), task(nxt and [nxt[2]]))
        y2, y2b, xh2, rs2 = _ln_fwd(y1, ffn, lw["ln_g"][1], lw["ln_b"][1], name=f"l{l}_ln2")
        saved.append((hb, mixer_saved, y1b, xh1, rs1, ffn_saved, xh2, rs2))
        h, hb = y2, y2b
        if nxt:
            lws.append(_layer_weights(l + 1, w, got_mixer, got_up[0], got_down[0]))
    loss_local, dy = _loss_fwd_bwd(h, target, name="loss")
    loss = lax.psum(loss_local[0, 0], ("x", "y", "c"))

    big_names = tuple(ADAM_TILES)
    as_updated = lambda n, a: jnp.swapaxes(a, 1, 2) if n == "ffn_w_up" else a
    wt, mt, vt = ({n: as_updated(n, d[n]) for n in big_names} for d in (w, mom, var))
    results = {n: [lax.empty(wt[n].shape, F32) for _ in range(4)] for n in big_names}
    small_grads = {n: [None] * w[n].shape[0] for n in WEIGHT_NAMES if n not in big_names}

    def adamw(l, n, own, from_chips):
        results[n] = _adamw_big(own, from_chips, wt[n], mt[n], vt[n], results[n], l if n.startswith("ffn") else l // 2,
                                tr=ADAM_TILES[n][0], tc=ADAM_TILES[n][1], name=f"l{l}_{n}_adamw")

    pending = None
    for l in reversed(range(DEPTH)):
        lw = lws[l]
        hb_in, mixer_saved, y1b, xh1, rs1, ffn_saved, xh2, rs2 = saved[l]
        dz2, dz2b, dg2, db2 = _ln_bwd(dy, xh2, rs2, lw["ln_g"][1], name=f"l{l}_ln2_bwd")
        dy1, ffn_sums, small_ffn, got_pending = _ffn_bwd(l, y1b, dz2, dz2b, lw, ffn_saved, place,
                                                          [s[0] for s in pending[1].values()] if pending else None)
        if pending:
            for (n, (_, own)), r2 in zip(pending[1].items(), got_pending):
                adamw(pending[0], n, own, r2)
        dz1, dz1b, dg1, db1 = _ln_bwd(dy1, xh1, rs1, lw["ln_g"][0], name=f"l{l}_ln1_bwd")
        dy, big_mix, small_mix, got_ffn = (_gla_layer_bwd if l % 2 == 0 else _dil_layer_bwd)(
            l, hb_in, dz1, dz1b, lw, mixer_saved, ffn_sums)
        for n, (_, own) in ffn_sums.items():
            adamw(l, n, own, got_ffn[n])
        from_sibling = _run_comm(_sibling_task(list(big_mix.values()), [None] * len(big_mix)), name=f"reduce_sibling_l{l}")
        pending = (l, {n: _sibling_sum(l, n, g, None, r1, place) for (n, g), r1 in zip(big_mix.items(), from_sibling)})
        small_grads["ln_g"][l] = jnp.concatenate([dg1, dg2], axis=0)
        small_grads["ln_b"][l] = jnp.concatenate([db1, db2], axis=0)
        for n, g in small_ffn.items():
            small_grads[n][l] = g
        for n, g in small_mix.items():
            small_grads[n][l // 2] = g
    from_chips = _run_comm(_chips_task([s[0] for s in pending[1].values()]), name="reduce_chips_l0")
    for (n, (_, own)), r2 in zip(pending[1].items(), from_chips):
        adamw(pending[0], n, own, r2)
    results = {n: [as_updated(n, a) for a in results[n]] for n in big_names}
    grad_x = dy.reshape(x.shape)

    small_names = [n for n in WEIGHT_NAMES if n not in big_names]
    full_shapes = {"gla_w_gate_up": (2, GLA_GATE_RANK, GLA_DK), "gla_gate_bias": (2, GLA_DK), "gla_norm_g": (2, GLA_HEAD_V),
                   "ffn_conv_w": (DEPTH, 3, 2 * D_FF), "ffn_conv_b": (DEPTH, 2 * D_FF),
                   "ln_g": (DEPTH, 2, D_MODEL), "ln_b": (DEPTH, 2, D_MODEL)}
    shapes = [full_shapes[n] for n in small_names]
    rows = _rows_for(shapes)
    packed = _pack_rows([jnp.stack(small_grads[n]) for n in small_names], rows)
    summed = _sum_gathered(_run_comm(_gather_task([packed]), name="gather_small_grads")[0], name="sum_small_grads")
    full = dict(zip(small_names, _unpack_rows(summed, shapes)))
    own = {n: (full[n] if w[n].shape == full[n].shape
               else lax.dynamic_slice_in_dim(full[n], dev * w[n].shape[-1], w[n].shape[-1], axis=full[n].ndim - 1))
           for n in small_names}
    own_shapes = [w[n].shape for n in small_names]
    rows = _rows_for(own_shapes)
    pk = lambda d: _pack_rows([d[n] for n in small_names], rows)
    outs = _adamw_small(pk(own), pk(w), pk(mom), pk(var), name="adamw_small")
    for n in small_names:
        results[n] = [own[n]]
    for k, packed_out in enumerate(outs):
        for n, a in zip(small_names, _unpack_rows(packed_out, own_shapes)):
            results[n].append(a)

    return (loss, grad_x) + tuple(results[n][k] for k in range(4) for n in WEIGHT_NAMES)
```

```python
import functools
import math

import jax
import jax.numpy as jnp
from jax import lax
from jax.experimental import pallas as pl
from jax.experimental.pallas import tpu as pltpu

F32 = jnp.float32
BF16 = jnp.bfloat16

D_MODEL = 2048
SEQ = 2048
DEPTH = 4
N_DEV = 8
GLA_HEADS = 4
GLA_DK = 1024
GLA_DV = 2048
GLA_HEAD_K = 256
GLA_HEAD_V = 512
GLA_GATE_RANK = 16
GLA_GATE_NORMALIZER = 16.0
GLA_CHUNK = 64
GLA_MAIN = 2 * GLA_DK + 2 * GLA_DV
GLA_IN = GLA_MAIN + GLA_GATE_RANK
DIL_PATTERNS = ((128, 1), (512, 4), (2048, 16))
DIL_HEADS = 8
DIL_HEAD_DIM = 128
DIL_WIDTH = DIL_HEADS * DIL_HEAD_DIM
DIL_BLOCK = 128
DIL_IN = 3 * len(DIL_PATTERNS) * DIL_WIDTH
D_FF = 5504
DEEPNORM_ALPHA = (2 * DEPTH) ** 0.25
LN_EPS = 1e-5
RMS_EPS = 1e-6
ADAM_LR = 0.001
ADAM_B1 = 0.9
ADAM_B2 = 0.999
ADAM_EPS = 1e-08
ADAM_WD = 0.01
ADAM_STEP = 10

LANES = 128
SUBLANES = 8
VMEM_LIMIT_BYTES = 56 * 1024 * 1024

FF_SHARD = 2 * D_FF // N_DEV
FF_SHARD_PAD = 1408
FF_HALF_PAD = 4 * FF_SHARD_PAD
GLOW_PAD = LANES


def _cparams(dims=None):
    return pltpu.CompilerParams(dimension_semantics=dims, vmem_limit_bytes=VMEM_LIMIT_BYTES)


class _Comm:
    def __init__(self, inputs, out_shapes, sem_shapes, start, middle, finish):
        self.inputs, self.out_shapes, self.sem_shapes = list(inputs), list(out_shapes), list(sem_shapes)
        self.start, self.middle, self.finish = start, middle, finish


def _join_comm(comms):
    def cut(refs, counts):
        out, off = [], 0
        for c in counts:
            out.append(refs[off:off + c])
            off += c
        return out

    n_in = [len(c.inputs) for c in comms]
    n_out = [len(c.out_shapes) for c in comms]
    n_sem = [len(c.sem_shapes) for c in comms]

    def hook(which):
        def run(ins, outs, sems):
            for c, i, o, s in zip(comms, cut(ins, n_in), cut(outs, n_out), cut(sems, n_sem)):
                getattr(c, which)(i, o, s)
        return run

    return _Comm([a for c in comms for a in c.inputs], [s for c in comms for s in c.out_shapes],
                 [s for c in comms for s in c.sem_shapes], hook("start"), hook("middle"), hook("finish"))


def _mm(a, b, *, grid, a_spec, b_spec, o_spec, out_shape, acc_shape, ta=False, tb=False, res=None, res_scale=1.0,
        comm=None, name):
    nk = grid[2]
    dims = (((0 if ta else 1,), (1 if tb else 0,)), ((), ()))
    has_res = res is not None
    n_in = 2 + has_res
    n_cin = len(comm.inputs) if comm else 0
    n_cout = len(comm.out_shapes) if comm else 0

    def body(*refs):
        a_ref, b_ref = refs[0], refs[1]
        res_ref = refs[2] if has_res else None
        o_ref = refs[n_in + n_cin]
        scratch = refs[n_in + n_cin + 1 + n_cout:]
        acc_ref = scratch[0] if nk > 1 else None
        if comm:
            task = (refs[n_in:n_in + n_cin], refs[n_in + n_cin + 1:n_in + n_cin + 1 + n_cout],
                    scratch[1:] if nk > 1 else scratch)
            step = (pl.program_id(0) * grid[1] + pl.program_id(1)) * nk + pl.program_id(2)
            steps = grid[0] * grid[1] * nk

            @pl.when(step == 0)
            def _():
                comm.start(*task)

        p = lax.dot_general(a_ref[...], b_ref[...], dims, preferred_element_type=F32)

        def finish(acc):
            if has_res:
                acc = acc + res_scale * res_ref[...]
            o_ref[...] = acc.astype(o_ref.dtype)

        if nk == 1:
            finish(p)
        else:
            kk = pl.program_id(2)

            @pl.when(kk == 0)
            def _():
                acc_ref[...] = p

            @pl.when(kk > 0)
            def _():
                acc_ref[...] += p

            @pl.when(kk == nk - 1)
            def _():
                finish(acc_ref[...])

        if comm:
            @pl.when(step == steps - 1)
            def _():
                comm.middle(*task)
                comm.finish(*task)

    in_specs = [a_spec, b_spec] + ([o_spec] if has_res else [])
    args = (a, b) + ((res,) if has_res else ())
    acc = [pltpu.VMEM(acc_shape, F32)] if nk > 1 else []
    if not comm:
        return pl.pallas_call(
            body, name=name, grid=grid, in_specs=in_specs, out_specs=o_spec, out_shape=out_shape, scratch_shapes=acc,
            compiler_params=_cparams(("parallel", "parallel", "arbitrary")),
        )(*args)
    outs = pl.pallas_call(
        body, name=name, grid=grid, in_specs=in_specs + [_ANY] * n_cin, out_specs=[o_spec] + [_ANY] * n_cout,
        out_shape=[out_shape] + comm.out_shapes, scratch_shapes=acc + comm.sem_shapes,
        compiler_params=_cparams(("arbitrary", "arbitrary", "arbitrary")),
    )(*args, *comm.inputs)
    return outs[0], list(outs[1:])


def _matmul(a, b, *, ta=False, tb=False, tm, tn, tk, out_dtype=F32, res=None, res_scale=1.0, comm=None, name):
    m, k = (a.shape[1], a.shape[0]) if ta else a.shape
    n = b.shape[0] if tb else b.shape[1]
    assert (b.shape[1] if tb else b.shape[0]) == k
    assert m % tm == 0 and n % tn == 0 and k % tk == 0, (m, n, k, tm, tn, tk)
    a_spec = pl.BlockSpec((tk, tm), lambda i, j, kk: (kk, i)) if ta else pl.BlockSpec((tm, tk), lambda i, j, kk: (i, kk))
    b_spec = pl.BlockSpec((tn, tk), lambda i, j, kk: (j, kk)) if tb else pl.BlockSpec((tk, tn), lambda i, j, kk: (kk, j))
    return _mm(a, b, grid=(m // tm, n // tn, k // tk), a_spec=a_spec, b_spec=b_spec,
               o_spec=pl.BlockSpec((tm, tn), lambda i, j, kk: (i, j)), out_shape=jax.ShapeDtypeStruct((m, n), out_dtype),
               acc_shape=(tm, tn), ta=ta, tb=tb, res=res, res_scale=res_scale, comm=comm, name=name)


MM_ROWS = 1024


def _mm_colblocks(a, wb, *, comm=None, name):
    m, k = a.shape
    nb, _, w = wb.shape
    return _mm(a, wb, grid=(m // MM_ROWS, nb, 1),
               a_spec=pl.BlockSpec((MM_ROWS, k), lambda i, j, kk: (i, 0)),
               b_spec=pl.BlockSpec((None, k, w), lambda i, j, kk: (j, 0, 0)),
               o_spec=pl.BlockSpec((MM_ROWS, w), lambda i, j, kk: (i, j)),
               out_shape=jax.ShapeDtypeStruct((m, nb * w), F32), acc_shape=(MM_ROWS, w), comm=comm, name=name)


def _mm_colblocks_t(a, wb, res, res_scale, *, comm=None, name):
    m = a.shape[0]
    nb, n, w = wb.shape
    tn = 1024
    return _mm(a, wb, grid=(m // MM_ROWS, n // tn, nb),
               a_spec=pl.BlockSpec((MM_ROWS, w), lambda i, j, kk: (i, kk)),
               b_spec=pl.BlockSpec((None, tn, w), lambda i, j, kk: (kk, j, 0)),
               o_spec=pl.BlockSpec((MM_ROWS, tn), lambda i, j, kk: (i, j)),
               out_shape=jax.ShapeDtypeStruct((m, n), F32), acc_shape=(MM_ROWS, tn), tb=True,
               res=res, res_scale=res_scale, comm=comm, name=name)


def _mm_grad_colblocks(x, dy, w, *, comm=None, name):
    t, k = x.shape
    nb = dy.shape[1] // w
    tm, tk = 1024, 2048
    return _mm(x, dy, grid=(k // tm, nb, t // tk),
               a_spec=pl.BlockSpec((tk, tm), lambda i, j, kk: (kk, i)),
               b_spec=pl.BlockSpec((tk, w), lambda i, j, kk: (kk, j)),
               o_spec=pl.BlockSpec((None, tm, w), lambda i, j, kk: (j, i, 0)),
               out_shape=jax.ShapeDtypeStruct((nb, k, w), F32), acc_shape=(tm, w), ta=True, comm=comm, name=name)


def _ffn_hidden(y, wt, *, comm=None, name):
    t, k = y.shape
    ni = t // MM_ROWS
    return _mm(y, wt, grid=(ni, N_DEV, 1),
               a_spec=pl.BlockSpec((MM_ROWS, k), lambda i, j, kk: (i, 0)),
               b_spec=pl.BlockSpec((None, FF_SHARD_PAD, k), lambda i, j, kk: (j, 0, 0)),
               o_spec=pl.BlockSpec((MM_ROWS, FF_SHARD_PAD), lambda i, j, kk: ((j // 4) * ni + i, j % 4)),
               out_shape=jax.ShapeDtypeStruct((2 * t, FF_HALF_PAD), F32), acc_shape=(MM_ROWS, FF_SHARD_PAD), tb=True,
               comm=comm, name=name)


def _ffn_hidden_dy(dh, wt, res, res_scale, *, comm=None, name):
    t = dh.shape[0] // 2
    ni, tn = t // MM_ROWS, 1024
    return _mm(dh, wt, grid=(ni, D_MODEL // tn, N_DEV),
               a_spec=pl.BlockSpec((MM_ROWS, FF_SHARD_PAD), lambda i, j, kk: ((kk // 4) * ni + i, kk % 4)),
               b_spec=pl.BlockSpec((None, FF_SHARD_PAD, tn), lambda i, j, kk: (kk, 0, j)),
               o_spec=pl.BlockSpec((MM_ROWS, tn), lambda i, j, kk: (i, j)),
               out_shape=jax.ShapeDtypeStruct((t, D_MODEL), F32), acc_shape=(MM_ROWS, tn),
               res=res, res_scale=res_scale, comm=comm, name=name)


def _ffn_hidden_dw(dh, y, *, comm=None, name):
    t, k = y.shape
    tk, tn = 2048, 1024
    nk = t // tk
    return _mm(dh, y, grid=(N_DEV, k // tn, nk),
               a_spec=pl.BlockSpec((tk, FF_SHARD_PAD), lambda i, j, kk: ((i // 4) * nk + kk, i % 4)),
               b_spec=pl.BlockSpec((tk, tn), lambda i, j, kk: (kk, j)),
               o_spec=pl.BlockSpec((None, FF_SHARD_PAD, tn), lambda i, j, kk: (i, 0, j)),
               out_shape=jax.ShapeDtypeStruct((N_DEV, FF_SHARD_PAD, k), F32), acc_shape=(FF_SHARD_PAD, tn), ta=True,
               comm=comm, name=name)


def _ffn_down_dw(act, dz, *, comm=None, name):
    t, k = dz.shape
    tk, tn = 2048, 1024
    return _mm(act, dz, grid=(4, k // tn, t // tk),
               a_spec=pl.BlockSpec((tk, FF_SHARD_PAD), lambda i, j, kk: (kk, i)),
               b_spec=pl.BlockSpec((tk, tn), lambda i, j, kk: (kk, j)),
               o_spec=pl.BlockSpec((None, FF_SHARD_PAD, tn), lambda i, j, kk: (i, 0, j)),
               out_shape=jax.ShapeDtypeStruct((4, FF_SHARD_PAD, k), F32), acc_shape=(FF_SHARD_PAD, tn), ta=True,
               comm=comm, name=name)


LN_ROWS = 256


def _ln_fwd(x, f, g, b, *, name):
    t, d = x.shape

    def body(x_ref, f_ref, g_ref, b_ref, y_ref, yb_ref, xh_ref, rs_ref):
        z = DEEPNORM_ALPHA * x_ref[...] + f_ref[...]
        mu = jnp.mean(z, axis=-1, keepdims=True)
        zc = z - mu
        var = jnp.mean(zc * zc, axis=-1, keepdims=True)
        rstd = lax.rsqrt(var + LN_EPS)
        xh = zc * rstd
        y = xh * g_ref[...] + b_ref[...]
        y_ref[...] = y
        yb_ref[...] = y.astype(BF16)
        xh_ref[...] = xh
        rs_ref[...] = rstd

    row = pl.BlockSpec((LN_ROWS, d), lambda i: (i, 0))
    vec = pl.BlockSpec((1, d), lambda i: (0, 0))
    return pl.pallas_call(
        body,
        name=name,
        grid=(t // LN_ROWS,),
        in_specs=[row, row, vec, vec],
        out_specs=[row, row, row, pl.BlockSpec((LN_ROWS, 1), lambda i: (i, 0))],
        out_shape=[jax.ShapeDtypeStruct((t, d), F32), jax.ShapeDtypeStruct((t, d), BF16),
                   jax.ShapeDtypeStruct((t, d), F32), jax.ShapeDtypeStruct((t, 1), F32)],
        compiler_params=_cparams(("parallel",)),
    )(x, f, g, b)


def _ln_bwd(dy, xhat, rstd, g, *, name):
    t, d = dy.shape

    def body(dy_ref, xh_ref, rs_ref, g_ref, dz_ref, dzb_ref, dg_ref, db_ref):
        dyv = dy_ref[...]
        xh = xh_ref[...]
        dyg = dyv * g_ref[...]
        m1 = jnp.mean(dyg, axis=-1, keepdims=True)
        m2 = jnp.mean(dyg * xh, axis=-1, keepdims=True)
        dz = rs_ref[...] * (dyg - m1 - xh * m2)
        dz_ref[...] = dz
        dzb_ref[...] = dz.astype(BF16)
        dg_part = jnp.sum(dyv * xh, axis=0, keepdims=True)
        db_part = jnp.sum(dyv, axis=0, keepdims=True)

        @pl.when(pl.program_id(0) == 0)
        def _():
            dg_ref[...] = dg_part
            db_ref[...] = db_part

        @pl.when(pl.program_id(0) > 0)
        def _():
            dg_ref[...] += dg_part
            db_ref[...] += db_part

    row = pl.BlockSpec((LN_ROWS, d), lambda i: (i, 0))
    vec = pl.BlockSpec((1, d), lambda i: (0, 0))
    return pl.pallas_call(
        body,
        name=name,
        grid=(t // LN_ROWS,),
        in_specs=[row, row, pl.BlockSpec((LN_ROWS, 1), lambda i: (i, 0)), vec],
        out_specs=[row, row, vec, vec],
        out_shape=[jax.ShapeDtypeStruct((t, d), F32), jax.ShapeDtypeStruct((t, d), BF16),
                   jax.ShapeDtypeStruct((1, d), F32), jax.ShapeDtypeStruct((1, d), F32)],
        compiler_params=_cparams(("arbitrary",)),
    )(dy, xhat, rstd, g)


def _loss_fwd_bwd(y, target, *, name):
    t, d = y.shape

    def body(y_ref, t_ref, loss_ref, dy_ref):
        err = y_ref[...] - t_ref[...]
        dy_ref[...] = err * (1.0 / d)
        part = 0.5 * jnp.sum(jnp.mean(err * err, axis=-1, keepdims=True), axis=0, keepdims=True)

        @pl.when(pl.program_id(0) == 0)
        def _():
            loss_ref[...] = part

        @pl.when(pl.program_id(0) > 0)
        def _():
            loss_ref[...] += part

    row = pl.BlockSpec((LN_ROWS, d), lambda i: (i, 0))
    return pl.pallas_call(
        body,
        name=name,
        grid=(t // LN_ROWS,),
        in_specs=[row, row],
        out_specs=[pl.BlockSpec((1, 1), lambda i: (0, 0)), row],
        out_shape=[jax.ShapeDtypeStruct((1, 1), F32), jax.ShapeDtypeStruct((t, d), F32)],
        compiler_params=_cparams(("arbitrary",)),
    )(y, target)


FFN_COLS = 256


def _shift_rows(h, s):
    rows = lax.broadcasted_iota(jnp.int32, h.shape, 0)
    return jnp.where(rows >= s, pltpu.roll(h, s, 0), 0.0)


def _shift_rows_up(h, s):
    n = h.shape[0]
    rows = lax.broadcasted_iota(jnp.int32, h.shape, 0)
    return jnp.where(rows < n - s, pltpu.roll(h, n - s, 0), 0.0)


def _causal_conv(h, w, b):
    return w[0:1, :] * _shift_rows(h, 2) + w[1:2, :] * _shift_rows(h, 1) + w[2:3, :] * h + b


def _sigmoid(x):
    return 1.0 / (1.0 + jnp.exp(-x))


def _convgate_fwd(h, cw, cb, *, name):
    t, n = h.shape[0] // 2, h.shape[1]
    nb = t // SEQ

    def body(hg_ref, hu_ref, wg_ref, wu_ref, bg_ref, bu_ref, a_ref):
        gate = _causal_conv(hg_ref[...], wg_ref[...], bg_ref[...])
        up = _causal_conv(hu_ref[...], wu_ref[...], bu_ref[...])
        a_ref[...] = (gate * _sigmoid(gate) * up).astype(BF16)

    def half(rows, k):
        return pl.BlockSpec((None, rows, FFN_COLS), lambda s, j: (k, 0, j))

    return pl.pallas_call(
        body,
        name=name,
        grid=(nb, n // FFN_COLS),
        in_specs=[pl.BlockSpec((SEQ, FFN_COLS), lambda s, j: (s, j)), pl.BlockSpec((SEQ, FFN_COLS), lambda s, j: (nb + s, j)),
                  half(3, 0), half(3, 1), half(1, 0), half(1, 1)],
        out_specs=pl.BlockSpec((SEQ, FFN_COLS), lambda s, j: (s, j)),
        out_shape=jax.ShapeDtypeStruct((t, n), BF16),
        compiler_params=_cparams(("parallel", "parallel")),
    )(h, h, cw, cw, cb, cb)


def _convgate_bwd(h, dact, cw, cb, *, name):
    t, n = h.shape[0] // 2, h.shape[1]
    nb = t // SEQ

    def body(hg_ref, hu_ref, da_ref, wg_ref, wu_ref, bg_ref, bu_ref, dh_ref, dw_ref, db_ref):
        dhg_ref, dhu_ref = dh_ref.at[0], dh_ref.at[1]
        dwg_ref, dwu_ref = dw_ref.at[0], dw_ref.at[1]
        dbg_ref, dbu_ref = db_ref.at[0], db_ref.at[1]
        hgv, huv = hg_ref[...], hu_ref[...]
        wgv, wuv = wg_ref[...], wu_ref[...]
        gate = _causal_conv(hgv, wgv, bg_ref[...])
        up = _causal_conv(huv, wuv, bu_ref[...])
        sg = _sigmoid(gate)
        da = da_ref[...]
        dgate = da * up * (sg * (1.0 + gate * (1.0 - sg)))
        dup = da * (gate * sg)

        def conv_bwd(dc, h, w, dh_ref, dw_ref, db_ref):
            dh = w[2:3, :] * dc + w[1:2, :] * _shift_rows_up(dc, 1) + w[0:1, :] * _shift_rows_up(dc, 2)
            dh_ref[...] = dh.astype(BF16)
            dws = [jnp.sum(dc * _shift_rows(h, 2), axis=0, keepdims=True),
                   jnp.sum(dc * _shift_rows(h, 1), axis=0, keepdims=True),
                   jnp.sum(dc * h, axis=0, keepdims=True)]
            db = jnp.sum(dc, axis=0, keepdims=True)

            @pl.when(pl.program_id(1) == 0)
            def _():
                for r in range(3):
                    dw_ref[r:r + 1, :] = dws[r]
                db_ref[...] = db

            @pl.when(pl.program_id(1) > 0)
            def _():
                for r in range(3):
                    dw_ref[r:r + 1, :] += dws[r]
                db_ref[...] += db

        conv_bwd(dgate, hgv, wgv, dhg_ref, dwg_ref, dbg_ref)
        conv_bwd(dup, huv, wuv, dhu_ref, dwu_ref, dbu_ref)

    def half(rows, k):
        return pl.BlockSpec((None, rows, FFN_COLS), lambda j, s: (k, 0, j))

    def both(rows):
        return pl.BlockSpec((2, rows, FFN_COLS), lambda j, s: (0, 0, j))

    return pl.pallas_call(
        body,
        name=name,
        grid=(n // FFN_COLS, nb),
        in_specs=[pl.BlockSpec((SEQ, FFN_COLS), lambda j, s: (s, j)), pl.BlockSpec((SEQ, FFN_COLS), lambda j, s: (nb + s, j)),
                  pl.BlockSpec((SEQ, FFN_COLS), lambda j, s: (s, j)), half(3, 0), half(3, 1), half(1, 0), half(1, 1)],
        out_specs=[pl.BlockSpec((2, SEQ, FFN_COLS), lambda j, s: (0, s, j)), both(3), both(1)],
        out_shape=[jax.ShapeDtypeStruct((2, t, n), BF16), jax.ShapeDtypeStruct((2, 3, n), F32),
                   jax.ShapeDtypeStruct((2, 1, n), F32)],
        compiler_params=_cparams(("parallel", "arbitrary")),
    )(h, h, dact, cw, cw, cb, cb)


GLA_Q_SCALE = GLA_HEAD_K ** -0.5
GLA_NC = SEQ // GLA_CHUNK
_NT = (((1,), (1,)), ((), ()))
_TN = (((0,), (0,)), ((), ()))


def _cumsum_rows(g):
    n = g.shape[0]
    rows = lax.broadcasted_iota(jnp.int32, g.shape, 0)
    s = 1
    while s < n:
        g = g + jnp.where(rows >= s, pltpu.roll(g, s, 0), 0.0)
        s *= 2
    return g


def _suffix_sum_rows(x):
    n = x.shape[0]
    rows = lax.broadcasted_iota(jnp.int32, x.shape, 0)
    s = 1
    while s < n:
        x = x + jnp.where(rows < n - s, pltpu.roll(x, n - s, 0), 0.0)
        s *= 2
    return x


def _gla_log_gate(gl_ref, wgu_ref, bias_ref):
    pre = jnp.dot(gl_ref[...].astype(BF16), wgu_ref[...], preferred_element_type=F32) + bias_ref[...]
    log_sig = jnp.minimum(pre, 0.0) - jnp.log(1.0 + jnp.exp(-jnp.abs(pre)))
    return pre, log_sig * (1.0 / GLA_GATE_NORMALIZER)


def _pair_rows(j):
    return (j // SUBLANES) * SUBLANES


def _gla_pair_fwd(q_scr, k_ref, b_scr, a_scr, h):
    c = GLA_CHUNK
    kc = pl.ds(h * GLA_HEAD_K, GLA_HEAD_K)
    a_scr[...] = jnp.zeros(a_scr.shape, F32)
    lane = lax.broadcasted_iota(jnp.int32, (1, c), 1)
    for j in range(c):
        r0 = _pair_rows(j)
        rs = pl.ds(r0, c - r0)
        rows = lax.broadcasted_iota(jnp.int32, (c - r0, 1), 0) + r0
        e = jnp.exp2(jnp.minimum(b_scr[rs, kc] - b_scr[pl.ds(j, 1), kc], 0.0))
        w = q_scr[rs, kc] * k_ref[pl.ds(j, 1), kc] * e
        col = jnp.where(rows >= j, jnp.sum(w, axis=-1, keepdims=True), 0.0)
        a_scr[rs, :] += col * (lane == j).astype(F32)


def _call_with_comm(body, comm, *, n_in, n_out, grid, in_specs, out_specs, out_shape, scratch_shapes, name, args):
    n_cin, n_cout, n_scr = len(comm.inputs), len(comm.out_shapes), len(scratch_shapes)

    def carrier(*refs):
        ins, cins = refs[:n_in], refs[n_in:n_in + n_cin]
        outs, couts = refs[n_in + n_cin:n_in + n_cin + n_out], refs[n_in + n_cin + n_out:n_in + n_cin + n_out + n_cout]
        scr = refs[n_in + n_cin + n_out + n_cout:]
        task = (cins, couts, scr[n_scr:])
        step, steps = 0, 1
        for axis, size in enumerate(grid):
            step = step * size + pl.program_id(axis)
            steps *= size

        @pl.when(step == 0)
        def _():
            comm.start(*task)

        body(*ins, *outs, *scr[:n_scr])

        @pl.when(step == steps - 1)
        def _():
            comm.middle(*task)
            comm.finish(*task)

    res = pl.pallas_call(
        carrier, name=name, grid=grid, in_specs=list(in_specs) + [_ANY] * n_cin, out_specs=list(out_specs) + [_ANY] * n_cout,
        out_shape=list(out_shape) + comm.out_shapes, scratch_shapes=list(scratch_shapes) + comm.sem_shapes,
        compiler_params=_cparams(("arbitrary",) * len(grid)),
    )(*args, *comm.inputs)
    return list(res[:n_out]), list(res[n_out:])


LOG2_E = 1.4426950408889634


def _gla_fwd(proj, wgu, bias, ng, *, comm=None, name):
    t = proj.shape[0]
    nb, nc, c = t // SEQ, GLA_NC, GLA_CHUNK

    def body(q_ref, k_ref, v_ref, r_ref, gl_ref, wgu_ref, bias_ref, ng_ref,
             y_ref, o_ref, a_ref, st_ref, state, b_scr, a_scr, q_scr):
        @pl.when(pl.program_id(1) == 0)
        def _():
            state[...] = jnp.zeros(state.shape, F32)

        _, g = _gla_log_gate(gl_ref, wgu_ref, bias_ref)
        b_scr[...] = _cumsum_rows(g) * LOG2_E
        q_scr[...] = q_ref[...] * GLA_Q_SCALE
        for h in range(GLA_HEADS):
            kc = pl.ds(h * GLA_HEAD_K, GLA_HEAD_K)
            vc = pl.ds(h * GLA_HEAD_V, GLA_HEAD_V)
            qh = q_scr[:, kc]
            kh = k_ref[:, kc]
            vh = v_ref[:, vc].astype(BF16)
            bh = b_scr[:, kc]
            blast = b_scr[pl.ds(c - 1, 1), kc]
            st = state[h]
            st_ref[h] = st
            o_inter = lax.dot_general((qh * jnp.exp2(bh)).astype(BF16), st.astype(BF16), _NT, preferred_element_type=F32)
            _gla_pair_fwd(q_scr, k_ref, b_scr, a_scr, h)
            a = a_scr[...]
            a_ref[h] = a
            o = o_inter + jnp.dot(a.astype(BF16), vh, preferred_element_type=F32)
            kd = (kh * jnp.exp2(blast - bh)).astype(BF16)
            state[h] = st * jnp.exp2(blast) + lax.dot_general(vh, kd, _TN, preferred_element_type=F32)
            o_ref[:, vc] = o
            rs = lax.rsqrt(jnp.mean(o * o, axis=-1, keepdims=True) + RMS_EPS)
            rh = r_ref[:, vc]
            y_ref[:, vc] = ((o * rs * ng_ref[...]) * (rh * _sigmoid(rh))).astype(BF16)

    def tok(width, col):
        return pl.BlockSpec((c, width), lambda b, i: (b * nc + i, col))

    whole = lambda shape: pl.BlockSpec(shape, lambda b, i: (0,) * len(shape))
    call = dict(
        grid=(nb, nc),
        in_specs=[tok(GLA_DK, 0), tok(GLA_DK, 1), tok(GLA_DV, 1), tok(GLA_DV, 2), tok(GLOW_PAD, GLA_MAIN // GLOW_PAD),
                  whole((GLOW_PAD, GLA_DK)), whole((1, GLA_DK)), whole((1, GLA_HEAD_V))],
        out_specs=[tok(GLA_DV, 0), tok(GLA_DV, 0),
                   pl.BlockSpec((GLA_HEADS, c, c), lambda b, i: (0, b * nc + i, 0)),
                   pl.BlockSpec((None, GLA_HEADS, GLA_HEAD_V, GLA_HEAD_K), lambda b, i: (b * nc + i, 0, 0, 0))],
        out_shape=[jax.ShapeDtypeStruct((t, GLA_DV), BF16), jax.ShapeDtypeStruct((t, GLA_DV), F32),
                   jax.ShapeDtypeStruct((GLA_HEADS, t, c), F32),
                   jax.ShapeDtypeStruct((t // c, GLA_HEADS, GLA_HEAD_V, GLA_HEAD_K), F32)],
        scratch_shapes=[pltpu.VMEM((GLA_HEADS, GLA_HEAD_V, GLA_HEAD_K), F32), pltpu.VMEM((c, GLA_DK), F32),
                        pltpu.VMEM((c, c), F32), pltpu.VMEM((c, GLA_DK), F32)],
        name=name)
    args = (proj, proj, proj, proj, proj, wgu, bias, ng)
    if comm is None:
        return pl.pallas_call(body, compiler_params=_cparams(("parallel", "arbitrary")), **call)(*args)
    return _call_with_comm(body, comm, n_in=8, n_out=4, args=args, **call)


def _gla_pair_bwd(q_scr, k_ref, b_scr, da_scr, dq_scr, dk_scr, h):
    c = GLA_CHUNK
    kc = pl.ds(h * GLA_HEAD_K, GLA_HEAD_K)
    lane = lax.broadcasted_iota(jnp.int32, (1, c), 1)
    for j in range(c):
        r0 = _pair_rows(j)
        rs = pl.ds(r0, c - r0)
        rows = lax.broadcasted_iota(jnp.int32, (c - r0, 1), 0) + r0
        e = jnp.exp2(jnp.minimum(b_scr[rs, kc] - b_scr[pl.ds(j, 1), kc], 0.0))
        dacol = jnp.sum(jnp.where(lane == j, da_scr[rs, :], 0.0), axis=-1, keepdims=True)
        t1 = jnp.where(rows >= j, dacol, 0.0) * e
        dq_scr[rs, kc] += t1 * k_ref[pl.ds(j, 1), kc]
        dk_scr[pl.ds(j, 1), kc] += jnp.sum(t1 * q_scr[rs, kc], axis=0, keepdims=True)


def _gla_bwd(proj, wgu, bias, ng, o, a, states, dy, *, name):
    t = proj.shape[0]
    nb, nc, c = t // SEQ, GLA_NC, GLA_CHUNK

    def body(q_ref, k_ref, v_ref, r_ref, gl_ref, wgu_ref, bias_ref, ng_ref, o_ref, a_ref, stp_ref, stn_ref, dy_ref,
             dq_ref, dk_ref, dv_ref, dr_ref, dgl_ref, dwgu_ref, dbias_ref, dng_ref,
             dstate, b_scr, da_scr, dq_scr, dk_scr, dg_scr, q_scr):
        first = jnp.logical_and(pl.program_id(0) == 0, pl.program_id(1) == 0)

        @pl.when(first)
        def _():
            dwgu_ref[...] = jnp.zeros(dwgu_ref.shape, F32)
            dbias_ref[...] = jnp.zeros(dbias_ref.shape, F32)
            dng_ref[...] = jnp.zeros(dng_ref.shape, F32)

        @pl.when(pl.program_id(1) == 0)
        def _():
            dstate[...] = jnp.zeros(dstate.shape, F32)

        pre, g = _gla_log_gate(gl_ref, wgu_ref, bias_ref)
        b_scr[...] = _cumsum_rows(g) * LOG2_E
        q_scr[...] = q_ref[...] * GLA_Q_SCALE
        ngv = ng_ref[...]
        tri = lax.broadcasted_iota(jnp.int32, (c, c), 0) >= lax.broadcasted_iota(jnp.int32, (c, c), 1)
        for h in range(GLA_HEADS):
            kc = pl.ds(h * GLA_HEAD_K, GLA_HEAD_K)
            vc = pl.ds(h * GLA_HEAD_V, GLA_HEAD_V)
            oh = o_ref[:, vc]
            rh = r_ref[:, vc]
            dyh = dy_ref[:, vc]
            rs = lax.rsqrt(jnp.mean(oh * oh, axis=-1, keepdims=True) + RMS_EPS)
            u = oh * rs
            sg = _sigmoid(rh)
            sr = rh * sg
            dr_ref[:, vc] = (dyh * (u * ngv) * (sg * (1.0 + rh * (1.0 - sg)))).astype(BF16)
            dng_ref[...] += jnp.sum(dyh * sr * u, axis=0, keepdims=True)
            du = dyh * sr * ngv
            do = (rs * (du - u * jnp.mean(du * u, axis=-1, keepdims=True))).astype(BF16)
            qh = q_scr[:, kc]
            kh = k_ref[:, kc]
            vh = v_ref[:, vc].astype(BF16)
            bh = b_scr[:, kc]
            blast = b_scr[pl.ds(c - 1, 1), kc]
            eb = jnp.exp2(bh)
            ek = jnp.exp2(blast - bh)
            dst = dstate[h]
            dst_b = dst.astype(BF16)
            dg_carry = jnp.sum(dst * stn_ref[h], axis=0, keepdims=True)
            da = lax.dot_general(do, vh, _NT, preferred_element_type=F32)
            da_scr[...] = jnp.where(tri, da, 0.0)
            dv = lax.dot_general(a_ref[h].astype(BF16), do, _TN, preferred_element_type=F32)
            dv = dv + lax.dot_general((kh * ek).astype(BF16), dst_b, _NT, preferred_element_type=F32)
            dv_ref[:, vc] = dv.astype(BF16)
            dq_scr[:, kc] = jnp.dot(do, stp_ref[h].astype(BF16), preferred_element_type=F32) * eb
            dk_scr[:, kc] = jnp.dot(vh, dst_b, preferred_element_type=F32) * ek
            _gla_pair_bwd(q_scr, k_ref, b_scr, da_scr, dq_scr, dk_scr, h)
            dq = dq_scr[:, kc]
            dk = dk_scr[:, kc]
            dg_scr[:, kc] = _suffix_sum_rows(qh * dq - kh * dk) + dg_carry
            dstate[h] = dst * jnp.exp2(blast) + lax.dot_general(do, (qh * eb).astype(BF16), _TN, preferred_element_type=F32)
        dq_ref[...] = (dq_scr[...] * GLA_Q_SCALE).astype(BF16)
        dk_ref[...] = dk_scr[...].astype(BF16)
        dpre = dg_scr[...] * ((1.0 - _sigmoid(pre)) * (1.0 / GLA_GATE_NORMALIZER))
        dpre_b = dpre.astype(BF16)
        dbias_ref[...] += jnp.sum(dpre, axis=0, keepdims=True)
        dwgu_ref[...] += lax.dot_general(gl_ref[...].astype(BF16), dpre_b, _TN, preferred_element_type=F32)
        dgl_ref[...] = lax.dot_general(dpre_b, wgu_ref[...], _NT, preferred_element_type=F32).astype(BF16)

    def chunk(b, i):
        return b * nc + (nc - 1 - i)

    def tok(width, col):
        return pl.BlockSpec((c, width), lambda b, i: (chunk(b, i), col))

    whole = lambda shape: pl.BlockSpec(shape, lambda b, i: (0,) * len(shape))
    st_shape = (None, GLA_HEADS, GLA_HEAD_V, GLA_HEAD_K)
    return pl.pallas_call(
        body,
        name=name,
        grid=(nb, nc),
        in_specs=[tok(GLA_DK, 0), tok(GLA_DK, 1), tok(GLA_DV, 1), tok(GLA_DV, 2), tok(GLOW_PAD, GLA_MAIN // GLOW_PAD),
                  whole((GLOW_PAD, GLA_DK)), whole((1, GLA_DK)), whole((1, GLA_HEAD_V)),
                  tok(GLA_DV, 0),
                  pl.BlockSpec((GLA_HEADS, c, c), lambda b, i: (0, chunk(b, i), 0)),
                  pl.BlockSpec(st_shape, lambda b, i: (chunk(b, i), 0, 0, 0)),
                  pl.BlockSpec(st_shape, lambda b, i: (b * nc + jnp.minimum(nc - i, nc - 1), 0, 0, 0)),
                  tok(GLA_DV, 0)],
        out_specs=[tok(GLA_DK, 0), tok(GLA_DK, 0), tok(GLA_DV, 0), tok(GLA_DV, 0), tok(GLOW_PAD, 0),
                   whole((GLOW_PAD, GLA_DK)), whole((1, GLA_DK)), whole((1, GLA_HEAD_V))],
        out_shape=[jax.ShapeDtypeStruct((t, GLA_DK), BF16), jax.ShapeDtypeStruct((t, GLA_DK), BF16),
                   jax.ShapeDtypeStruct((t, GLA_DV), BF16), jax.ShapeDtypeStruct((t, GLA_DV), BF16),
                   jax.ShapeDtypeStruct((t, GLOW_PAD), BF16),
                   jax.ShapeDtypeStruct((GLOW_PAD, GLA_DK), F32), jax.ShapeDtypeStruct((1, GLA_DK), F32),
                   jax.ShapeDtypeStruct((1, GLA_HEAD_V), F32)],
        scratch_shapes=[pltpu.VMEM((GLA_HEADS, GLA_HEAD_V, GLA_HEAD_K), F32), pltpu.VMEM((c, GLA_DK), F32),
                        pltpu.VMEM((c, c), F32), pltpu.VMEM((c, GLA_DK), F32), pltpu.VMEM((c, GLA_DK), F32),
                        pltpu.VMEM((c, GLA_DK), F32), pltpu.VMEM((c, GLA_DK), F32)],
        compiler_params=_cparams(("arbitrary", "arbitrary")),
    )(proj, proj, proj, proj, proj, wgu, bias, ng, o, a, states, states, dy)


DIL_STEPS = DIL_BLOCK
DIL_SCALE = DIL_HEAD_DIM ** -0.5
DIL_HEADS_PER_STEP = {1: 8, 4: 1, 16: 1}


def _dil_mask(i):
    rowi = lax.broadcasted_iota(jnp.int32, (DIL_BLOCK, 2 * DIL_BLOCK), 0)
    colj = lax.broadcasted_iota(jnp.int32, (DIL_BLOCK, 2 * DIL_BLOCK), 1)
    dist = rowi + DIL_BLOCK - colj
    band = jnp.logical_and(dist >= 0, dist <= DIL_STEPS)
    return jnp.logical_and(band, jnp.logical_or(i > 0, colj >= DIL_BLOCK))


def _dil_geometry(t, gi):
    _, d = DIL_PATTERNS[gi]
    return d, SEQ // d // DIL_BLOCK, t // SEQ, DIL_BLOCK * d, DIL_HEADS_PER_STEP[d]


def _dil_specs(gi, d, nq, rows, hps, order):
    width = hps * DIL_HEAD_DIM
    per_part = DIL_WIDTH // width

    def named(f):
        return lambda *idx: f(**dict(zip(order, idx)))

    def block(i, prev):
        ic = jnp.minimum(i, nq - 1)
        return jnp.maximum(ic - 1, 0) if prev else ic

    def part(j, prev):
        return pl.BlockSpec((rows, width), named(lambda b, i, h: (b * nq + block(i, prev), (gi * 3 + j) * per_part + h)))

    cur = pl.BlockSpec((rows, width), named(lambda b, i, h: (b * nq + block(i, False), h)))
    done = pl.BlockSpec((rows, width), named(lambda b, i, h: (b * nq + jnp.maximum(i - 1, 0), h)))
    return [part(0, False), part(1, False), part(1, True), part(2, False), part(2, True)], cur, done


def _dil_rows(r, d):
    return pl.ds(r, DIL_BLOCK, stride=d) if d > 1 else pl.ds(0, DIL_BLOCK)


def _dil_fwd(proj, gi, *, name):
    t = proj.shape[0]
    d, nq, nb, rows, hps = _dil_geometry(t, gi)

    def body(q_ref, kc_ref, kp_ref, vc_ref, vp_ref, o_ref, lse_ref):
        mask = _dil_mask(pl.program_id(1))
        for h in range(hps):
            hc = pl.ds(h * DIL_HEAD_DIM, DIL_HEAD_DIM)
            for r in range(d):
                rr = _dil_rows(r, d)
                qh = q_ref[rr, hc].astype(BF16)
                kcat = jnp.concatenate([kp_ref[rr, hc], kc_ref[rr, hc]], axis=0).astype(BF16)
                vcat = jnp.concatenate([vp_ref[rr, hc], vc_ref[rr, hc]], axis=0).astype(BF16)
                s = lax.dot_general(qh, kcat, _NT, preferred_element_type=F32) * DIL_SCALE
                s = jnp.where(mask, s, -jnp.inf)
                m = jnp.max(s, axis=-1, keepdims=True)
                p = jnp.exp(s - m)
                l = jnp.sum(p, axis=-1, keepdims=True)
                o_ref[rr, hc] = jnp.dot((p / l).astype(BF16), vcat, preferred_element_type=F32)
                lse_ref[rr, hc] = jnp.broadcast_to(m + jnp.log(l), (DIL_BLOCK, DIL_HEAD_DIM))

    parts, cur, _ = _dil_specs(gi, d, nq, rows, hps, "bih")
    return pl.pallas_call(
        body,
        name=name,
        grid=(nb, nq, DIL_HEADS // hps),
        in_specs=parts,
        out_specs=[cur, cur],
        out_shape=[jax.ShapeDtypeStruct((t, DIL_WIDTH), F32)] * 2,
        compiler_params=_cparams(("parallel", "parallel", "parallel")),
    )(proj, proj, proj, proj, proj)


def _dil_bwd(proj, gi, lse, do, delta, *, name):
    t = proj.shape[0]
    d, nq, nb, rows, hps = _dil_geometry(t, gi)

    def body(q_ref, kc_ref, kp_ref, vc_ref, vp_ref, lse_ref, do_ref, dl_ref, dq_ref, dk_ref, dv_ref,
             ck, cv, fk, fv, dq_s):
        i = pl.program_id(2)

        @pl.when(i == 0)
        def _():
            ck[...] = jnp.zeros(ck.shape, F32)
            cv[...] = jnp.zeros(cv.shape, F32)

        @pl.when(i < nq)
        def _():
            mask = _dil_mask(i)
            for h in range(hps):
                hc = pl.ds(h * DIL_HEAD_DIM, DIL_HEAD_DIM)
                h1 = pl.ds(h * DIL_HEAD_DIM, 1)
                for r in range(d):
                    rr = _dil_rows(r, d)
                    qh = q_ref[rr, hc].astype(BF16)
                    kcat = jnp.concatenate([kp_ref[rr, hc], kc_ref[rr, hc]], axis=0).astype(BF16)
                    vcat = jnp.concatenate([vp_ref[rr, hc], vc_ref[rr, hc]], axis=0).astype(BF16)
                    doh = do_ref[rr, hc].astype(BF16)
                    s = lax.dot_general(qh, kcat, _NT, preferred_element_type=F32) * DIL_SCALE
                    p = jnp.exp(jnp.where(mask, s, -jnp.inf) - lse_ref[rr, h1])
                    dp = lax.dot_general(doh, vcat, _NT, preferred_element_type=F32)
                    ds = (p * (dp + dl_ref[rr, h1]) * DIL_SCALE).astype(BF16)
                    dq_s[rr, hc] = jnp.dot(ds, kcat, preferred_element_type=F32)
                    dkcat = lax.dot_general(ds, qh, _TN, preferred_element_type=F32)
                    dvcat = lax.dot_general(p.astype(BF16), doh, _TN, preferred_element_type=F32)
                    fk[rr, hc] = ck[rr, hc] + dkcat[:DIL_BLOCK]
                    fv[rr, hc] = cv[rr, hc] + dvcat[:DIL_BLOCK]
                    ck[rr, hc] = dkcat[DIL_BLOCK:]
                    cv[rr, hc] = dvcat[DIL_BLOCK:]
            dq_ref[...] = dq_s[...].astype(BF16)

            @pl.when(i > 0)
            def _():
                dk_ref[...] = fk[...].astype(BF16)
                dv_ref[...] = fv[...].astype(BF16)

        @pl.when(i == nq)
        def _():
            dk_ref[...] = ck[...].astype(BF16)
            dv_ref[...] = cv[...].astype(BF16)

    parts, cur, done = _dil_specs(gi, d, nq, rows, hps, "bhi")
    shape = jax.ShapeDtypeStruct((t, DIL_WIDTH), BF16)
    tile = pltpu.VMEM((rows, hps * DIL_HEAD_DIM), F32)
    return pl.pallas_call(
        body,
        name=name,
        grid=(nb, DIL_HEADS // hps, nq + 1),
        in_specs=parts + [cur, cur, cur],
        out_specs=[cur, done, done],
        out_shape=[shape, shape, shape],
        scratch_shapes=[tile] * 5,
        compiler_params=_cparams(("parallel", "parallel", "arbitrary")),
    )(proj, proj, proj, proj, proj, lse, do, delta)


MIX_ROWS = 256


def _head_rowsum(x):
    parts = []
    for h in range(DIL_HEADS):
        s = jnp.sum(x[:, h * DIL_HEAD_DIM:(h + 1) * DIL_HEAD_DIM], axis=-1, keepdims=True)
        parts.append(jnp.broadcast_to(s, (x.shape[0], DIL_HEAD_DIM)))
    return jnp.concatenate(parts, axis=-1)


def _mix_weights(lse_refs):
    ls = [r[...] for r in lse_refs]
    m = jnp.maximum(jnp.maximum(ls[0], ls[1]), ls[2])
    es = [jnp.exp(l - m) for l in ls]
    inv = 1.0 / (es[0] + es[1] + es[2])
    return [e * inv for e in es]


def _dil_mix_fwd(os_, lses, *, name):
    t = os_[0].shape[0]

    def body(o0, o1, o2, l0, l1, l2, out_ref):
        w = _mix_weights((l0, l1, l2))
        out_ref[...] = (w[0] * o0[...] + w[1] * o1[...] + w[2] * o2[...]).astype(BF16)

    row = pl.BlockSpec((MIX_ROWS, DIL_WIDTH), lambda i: (i, 0))
    return pl.pallas_call(
        body, name=name, grid=(t // MIX_ROWS,), in_specs=[row] * 6, out_specs=row,
        out_shape=jax.ShapeDtypeStruct((t, DIL_WIDTH), BF16), compiler_params=_cparams(("parallel",)),
    )(*os_, *lses)


def _dil_mix_bwd(os_, lses, dout, *, name):
    t = os_[0].shape[0]

    def body(o0, o1, o2, l0, l1, l2, d_ref, do0, do1, do2, dl0, dl1, dl2):
        w = _mix_weights((l0, l1, l2))
        dv = d_ref[...]
        mix = w[0] * o0[...] + w[1] * o1[...] + w[2] * o2[...]
        bar = _head_rowsum(dv * mix)
        for wg, do_ref, dl_ref in zip(w, (do0, do1, do2), (dl0, dl1, dl2)):
            do_ref[...] = wg * dv
            dl_ref[...] = -wg * bar

    row = pl.BlockSpec((MIX_ROWS, DIL_WIDTH), lambda i: (i, 0))
    outs = pl.pallas_call(
        body, name=name, grid=(t // MIX_ROWS,), in_specs=[row] * 7, out_specs=[row] * 6,
        out_shape=[jax.ShapeDtypeStruct((t, DIL_WIDTH), F32)] * 6,
        compiler_params=_cparams(("parallel",)),
    )(*os_, *lses, dout)
    return outs[:3], outs[3:]


_MESH = pl.DeviceIdType.MESH
_ANY = pl.BlockSpec(memory_space=pl.ANY)


def _position():
    return lax.axis_index("x"), lax.axis_index("y"), lax.axis_index("c")


AG_COPIES = 7


def _run_comm(task, *, name):
    n_in, n_out = len(task.inputs), len(task.out_shapes)

    def body(*refs):
        parts = (refs[:n_in], refs[n_in:n_in + n_out], refs[n_in + n_out:])
        task.start(*parts)
        task.middle(*parts)
        task.finish(*parts)

    return pl.pallas_call(
        body, name=name, out_shape=task.out_shapes, in_specs=[_ANY] * n_in, out_specs=[_ANY] * n_out,
        scratch_shapes=task.sem_shapes,
    )(*task.inputs)


def _gather_task(shards):
    n = len(shards)

    def copies(x_refs, out_refs, sems):
        send_sems, recv_sems, local_sems = sems
        x, y, cc = _position()
        me, sibling = (x, y, cc), (x, y, 1 - cc)
        chips = [(1 - x, y), (x, 1 - y), (1 - x, 1 - y)]

        def copy(w, k, block, to, own=False):
            px, py, pc = block
            slot = out_refs[w].at[4 * px + 2 * py + pc]
            return pltpu.make_async_remote_copy(
                src_ref=x_refs[w] if own else slot, dst_ref=slot,
                send_sem=send_sems.at[AG_COPIES * w + k], recv_sem=recv_sems.at[AG_COPIES * w + k],
                device_id=to, device_id_type=_MESH)

        mine = [pltpu.make_async_copy(x_refs[w], out_refs[w].at[4 * x + 2 * y + cc], local_sems.at[w]) for w in range(n)]
        first = [[copy(w, 0, me, sibling, own=True)] + [copy(w, 1 + j, me, (*chip, cc), own=True) for j, chip in enumerate(chips)]
                 for w in range(n)]
        landed = [[copy(w, 1 + j, (*chip, cc), me) for j, chip in enumerate(chips)] for w in range(n)]
        passed = [[copy(w, 4 + j, (*chip, cc), sibling) for j, chip in enumerate(chips)] for w in range(n)]
        from_sibling = [[copy(w, 0, sibling, me)] + [copy(w, 4 + j, (*chip, 1 - cc), me) for j, chip in enumerate(chips)]
                        for w in range(n)]
        return mine, first, landed, passed, from_sibling

    def start(ins, outs, sems):
        mine, first, _, _, _ = copies(ins, outs, sems)
        for w in range(n):
            mine[w].start()
            for cp in first[w]:
                cp.start()

    def middle(ins, outs, sems):
        _, _, landed, passed, _ = copies(ins, outs, sems)
        for j in range(3):
            for w in range(n):
                landed[w][j].wait_recv()
                passed[w][j].start()

    def finish(ins, outs, sems):
        mine, first, _, passed, from_sibling = copies(ins, outs, sems)
        for w in range(n):
            for cp in from_sibling[w]:
                cp.wait_recv()
        for w in range(n):
            for cp in first[w] + passed[w]:
                cp.wait_send()
            mine[w].wait()

    return _Comm(shards, [jax.ShapeDtypeStruct((N_DEV,) + s.shape, s.dtype) for s in shards],
                 [pltpu.SemaphoreType.DMA((AG_COPIES * n,)), pltpu.SemaphoreType.DMA((AG_COPIES * n,)),
                  pltpu.SemaphoreType.DMA((n,))], start, middle, finish)


def _parity_half(ref, parity, half_rows):
    if half_rows is None:
        return ref.at[:, parity]
    return ref.at[:, pl.ds(parity * half_rows, half_rows), :]


def _exchange_task(make_copies, inputs, out_shapes, n_copies):
    def start(ins, outs, sems):
        for cp in make_copies(ins, outs, sems):
            cp.start()

    def finish(ins, outs, sems):
        for cp in make_copies(ins, outs, sems):
            cp.wait()

    return _Comm(inputs, out_shapes, [pltpu.SemaphoreType.DMA((n_copies,)), pltpu.SemaphoreType.DMA((n_copies,))],
                 start, lambda ins, outs, sems: None, finish)


def _sibling_task(gs, half_rows):
    n = len(gs)

    def make_copies(g_refs, out_refs, sems):
        x, y, cc = _position()
        return [pltpu.make_async_remote_copy(
            src_ref=_parity_half(g_refs[k], 1 - cc, half_rows[k]), dst_ref=out_refs[k],
            send_sem=sems[0].at[k], recv_sem=sems[1].at[k],
            device_id=(x, y, 1 - cc), device_id_type=_MESH) for k in range(n)]

    def out_shape(g, hr):
        return jax.ShapeDtypeStruct((4,) + (g.shape[2:] if hr is None else (hr, g.shape[2])), g.dtype)

    return _exchange_task(make_copies, gs, [out_shape(g, hr) for g, hr in zip(gs, half_rows)], n)


def _chips_task(ps):
    n = len(ps)

    def make_copies(p_refs, out_refs, sems):
        x, y, cc = _position()
        copies = []
        for w in range(n):
            for k in (1, 2, 3):
                px = 1 - x if k >> 1 else x
                py = 1 - y if k & 1 else y
                copies.append(pltpu.make_async_remote_copy(
                    src_ref=p_refs[w].at[2 * px + py], dst_ref=out_refs[w].at[k - 1],
                    send_sem=sems[0].at[3 * w + k - 1], recv_sem=sems[1].at[3 * w + k - 1],
                    device_id=(px, py, cc), device_id_type=_MESH))
        return copies

    return _exchange_task(make_copies, ps, [jax.ShapeDtypeStruct((3,) + p.shape[1:], p.dtype) for p in ps], 3 * n)


def _add_sibling(g, r1, place, half_rows, *, tr, tc, name):
    _, r, c = r1.shape
    if half_rows is None:
        g_spec = pl.BlockSpec((None, None, tr, tc), lambda i, j, k, pc: (k, pc[0], i, j))
    else:
        per_half = half_rows // tr
        g_spec = pl.BlockSpec((None, tr, tc), lambda i, j, k, pc: (k, pc[0] * per_half + i, j))

    def body(pc_ref, g_ref, r_ref, pb_ref, own_ref):
        s = g_ref[...] + r_ref[...]
        pb_ref[...] = s.astype(BF16)

        @pl.when(pl.program_id(2) == pc_ref[1])
        def _():
            own_ref[...] = s

    grid_spec = pltpu.PrefetchScalarGridSpec(
        num_scalar_prefetch=1,
        grid=(r // tr, c // tc, 4),
        in_specs=[g_spec, pl.BlockSpec((None, tr, tc), lambda i, j, k, pc: (k, i, j))],
        out_specs=[pl.BlockSpec((None, tr, tc), lambda i, j, k, pc: (k, i, j)),
                   pl.BlockSpec((tr, tc), lambda i, j, k, pc: (i, j))],
    )
    return pl.pallas_call(
        body, name=name, grid_spec=grid_spec,
        out_shape=[jax.ShapeDtypeStruct((4, r, c), BF16), jax.ShapeDtypeStruct((r, c), F32)],
        compiler_params=_cparams(("parallel", "parallel", "arbitrary")),
    )(place, g, r1)


def _adamw_math(g, w, m, v):
    m = ADAM_B1 * m + (1.0 - ADAM_B1) * g
    v = ADAM_B2 * v + (1.0 - ADAM_B2) * (g * g)
    m_hat = m / (1.0 - ADAM_B1 ** ADAM_STEP)
    v_hat = v / (1.0 - ADAM_B2 ** ADAM_STEP)
    delta = -ADAM_LR * (m_hat / (jnp.sqrt(v_hat) + ADAM_EPS) + ADAM_WD * w)
    return delta, m, v


def _adamw_big(own, r2, w, m, v, prev, layer, *, tr, tc, name):
    _, r, c = w.shape

    def body(p_ref, r2_ref, w_ref, m_ref, v_ref, a0, a1, a2, a3, g_ref, d_ref, mo_ref, vo_ref):
        g = ((p_ref[...] + r2_ref[0].astype(F32)) + r2_ref[1].astype(F32)) + r2_ref[2].astype(F32)
        delta, mn, vn = _adamw_math(g, w_ref[...], m_ref[...], v_ref[...])
        g_ref[...] = g
        d_ref[...] = delta
        mo_ref[...] = mn
        vo_ref[...] = vn

    lay = pl.BlockSpec((None, tr, tc), lambda i, j: (layer, i, j))
    return pl.pallas_call(
        body, name=name, grid=(r // tr, c // tc),
        in_specs=[pl.BlockSpec((tr, tc), lambda i, j: (i, j)), pl.BlockSpec((3, tr, tc), lambda i, j: (0, i, j)),
                  lay, lay, lay, _ANY, _ANY, _ANY, _ANY],
        out_specs=[lay, lay, lay, lay],
        out_shape=[jax.ShapeDtypeStruct(w.shape, F32)] * 4,
        input_output_aliases={5: 0, 6: 1, 7: 2, 8: 3},
        compiler_params=_cparams(("parallel", "parallel")),
    )(own, r2, w, m, v, *prev)


SMALL_COLS = 1024


def _sum_gathered(parts, *, name):
    _, r, c = parts.shape

    def body(p_ref, o_ref):
        acc = p_ref[0]
        for k in range(1, N_DEV):
            acc = acc + p_ref[k]
        o_ref[...] = acc

    return pl.pallas_call(
        body, name=name, grid=(1,), in_specs=[pl.BlockSpec((N_DEV, r, c), lambda i: (0, 0, 0))],
        out_specs=pl.BlockSpec((r, c), lambda i: (0, 0)), out_shape=jax.ShapeDtypeStruct((r, c), F32),
        compiler_params=_cparams(("arbitrary",)),
    )(parts)


def _adamw_small(g, w, m, v, *, name):
    r, c = g.shape

    def body(g_ref, w_ref, m_ref, v_ref, d_ref, mo_ref, vo_ref):
        delta, mn, vn = _adamw_math(g_ref[...], w_ref[...], m_ref[...], v_ref[...])
        d_ref[...] = delta
        mo_ref[...] = mn
        vo_ref[...] = vn

    spec = pl.BlockSpec((r, c), lambda i: (0, 0))
    return pl.pallas_call(
        body, name=name, grid=(1,), in_specs=[spec] * 4, out_specs=[spec] * 3,
        out_shape=[jax.ShapeDtypeStruct((r, c), F32)] * 3, compiler_params=_cparams(("arbitrary",)),
    )(g, w, m, v)


WEIGHT_NAMES = ("gla_w_in", "gla_w_gate_up", "gla_gate_bias", "gla_norm_g", "gla_w_out", "dil_w_in", "dil_w_out",
                "ffn_w_up", "ffn_conv_w", "ffn_conv_b", "ffn_w_down", "ln_g", "ln_b")
ADAM_TILES = {"gla_w_in": (256, 770), "gla_w_out": (128, 2048), "dil_w_in": (256, 1152), "dil_w_out": (512, 256),
              "ffn_w_up": (344, 1024), "ffn_w_down": (344, 1024)}
ADD_TILES = {**ADAM_TILES, "ffn_w_up": (352, 1024)}
FF_DOWN_SHARD = D_FF // N_DEV
VEC_COLS = 128


def _pad_axis(a, axis, to):
    pads = [(0, 0)] * a.ndim
    pads[axis] = (0, to - a.shape[axis])
    return jnp.pad(a, pads)


def _ff_cols(blocks):
    r = blocks.shape[1]
    return _pad_axis(blocks, 2, FF_SHARD_PAD).reshape(2, 4, r, FF_SHARD_PAD).transpose(0, 2, 1, 3).reshape(2, r, FF_HALF_PAD)


def _ff_cols_back(a):
    r = a.shape[1]
    return a.reshape(2, r, 4, FF_SHARD_PAD)[..., :FF_SHARD].transpose(1, 0, 2, 3).reshape(r, 2 * D_FF)


def _vec_parts(l, w):
    return [w["ffn_conv_w"][l], w["ln_g"][l], w["ln_b"][l]] + ([w["gla_w_gate_up"][l // 2]] if l % 2 == 0 else [])


def _layer_shards(l, w):
    j = l // 2
    gla = l % 2 == 0
    parts = _vec_parts(l, w)
    vec_rows = -(-sum(math.prod(a.shape) for a in parts) // (VEC_COLS * SUBLANES)) * SUBLANES
    mixer = [(w["gla_w_in"] if gla else w["dil_w_in"])[j].astype(BF16),
             (w["gla_w_out"] if gla else w["dil_w_out"])[j].astype(BF16), _pack_rows(parts, vec_rows, VEC_COLS)]
    w_up_t = _pad_axis(jnp.swapaxes(w["ffn_w_up"][l], 0, 1).astype(BF16), 0, FF_SHARD_PAD)
    return mixer, w_up_t, w["ffn_w_down"][l].astype(BF16)


def _w_down_layout(g_down):
    return _pad_axis(g_down.reshape(4, FF_SHARD, D_MODEL), 1, FF_SHARD_PAD).reshape(FF_HALF_PAD, D_MODEL)


def _layer_weights(l, w, g_in, g_out, g_vec):
    j = l // 2
    gla = l % 2 == 0
    vec_shapes = [a.shape for a in _vec_parts(l, w)]
    out = {}
    vec = [jnp.stack(p) for p in zip(*[_unpack_rows(g_vec[d], vec_shapes) for d in range(N_DEV)])]
    out["conv_w"] = _ff_cols(vec[0])
    out["conv_b"] = _pad_axis(w["ffn_conv_b"][l].reshape(N_DEV, FF_SHARD), 1, FF_SHARD_PAD).reshape(2, 1, FF_HALF_PAD)
    out["ln_g"] = vec[1].transpose(1, 0, 2).reshape(2, 1, D_MODEL)
    out["ln_b"] = vec[2].transpose(1, 0, 2).reshape(2, 1, D_MODEL)
    if gla:
        win = g_in.transpose(1, 0, 2).reshape(D_MODEL, GLA_IN)
        out["w_in"] = _pad_axis(win, 1, GLA_MAIN + GLOW_PAD)
        wgu = vec[3].transpose(1, 0, 2).reshape(GLA_GATE_RANK, GLA_DK).astype(BF16)
        out["w_gate_up"] = _pad_axis(wgu, 0, GLOW_PAD)
        out["w_out"] = g_out.reshape(GLA_DV, D_MODEL)
        out["gate_bias"] = w["gla_gate_bias"][j].reshape(1, GLA_DK)
        out["norm_g"] = w["gla_norm_g"][j].reshape(1, GLA_HEAD_V)
    else:
        out["w_in"] = g_in
        out["w_out"] = g_out.transpose(1, 0, 2).reshape(DIL_WIDTH, D_MODEL)
    return out


def _by_chip_parity(blocks):
    return blocks.reshape((4, 2) + blocks.shape[1:])


def _col_blocks(dw, width):
    r = dw.shape[0]
    return dw.reshape(r, N_DEV, width).transpose(1, 0, 2)


def _carry(call, task):
    if task is None:
        return call(None), []
    return call(task)


def _sibling_sum(l, n, g, half_rows, from_sibling, place):
    return _add_sibling(g, from_sibling, place, half_rows, tr=ADD_TILES[n][0], tc=ADD_TILES[n][1], name=f"l{l}_{n}_add")


def _ffn_fwd(l, yb, lw, task_up, task_down):
    h, got_up = _carry(lambda c: _ffn_hidden(yb, lw["w_up_t"], comm=c, name=f"l{l}_ffn_hidden"), task_up)
    act = _convgate_fwd(h, lw["conv_w"], lw["conv_b"], name=f"l{l}_convgate")
    ffn, got_down = _carry(lambda c: _matmul(act, lw["w_down"], tm=1024, tn=1024, tk=2816, comm=c, name=f"l{l}_ffn_down"), task_down)
    return ffn, (h, act), got_up, got_down


def _ffn_bwd(l, yb, dz, dzb, lw, saved, place, pending):
    h, act = saved
    t = yb.shape[0]
    task = _sibling_task(list(pending[1].values()), [None] * len(pending[1])) if pending else None
    dact, from_sibling = _carry(lambda c: _matmul(dzb, lw["w_down"], tb=True, tm=512, tn=2816, tk=D_MODEL, comm=c,
                                                  name=f"l{l}_ffn_dact"), task)
    pend_sums = {n: _sibling_sum(pending[0], n, g, None, r1, place)
                 for (n, g), r1 in zip(pending[1].items(), from_sibling)} if pending else {}
    d_down = _ffn_down_dw(act, dzb, name=f"l{l}_ffn_dwdown")
    dh, dcw, dcb = _convgate_bwd(h, dact, lw["conv_w"], lw["conv_b"], name=f"l{l}_convgate_bwd")
    dh = dh.reshape(2 * t, FF_HALF_PAD)
    tasks = [_sibling_task([d_down], [FF_DOWN_SHARD])] + ([_chips_task([s[0] for s in pend_sums.values()])] if pending else [])
    d_up_t, got = _ffn_hidden_dw(dh, yb, comm=_join_comm(tasks), name=f"l{l}_ffn_dwup")
    d_up = _by_chip_parity(d_up_t)
    dy, got_up = _ffn_hidden_dy(dh, lw["w_up_t"], dz, DEEPNORM_ALPHA, comm=_sibling_task([d_up], [None]), name=f"l{l}_ffn_dy")
    sums = {"ffn_w_down": _sibling_sum(l, "ffn_w_down", d_down, FF_DOWN_SHARD, got[0], place),
            "ffn_w_up": _sibling_sum(l, "ffn_w_up", d_up, None, got_up[0], place)}
    small = {"ffn_conv_w": _ff_cols_back(dcw), "ffn_conv_b": _ff_cols_back(dcb)[0]}
    return dy, sums, small, pend_sums, got[1:]


def _gla_layer_fwd(l, hb, lw, tasks):
    proj, got_proj = _carry(lambda c: _matmul(hb, lw["w_in"], tm=1024, tn=896, tk=D_MODEL, comm=c, name=f"l{l}_gla_proj"),
                            tasks["proj"])
    (y, o, a, st), got_gla = _carry(lambda c: _gla_fwd(proj, lw["w_gate_up"], lw["gate_bias"], lw["norm_g"], comm=c,
                                                       name=f"l{l}_gla"), tasks["gla"])
    mix, got_out = _carry(lambda c: _matmul(y, lw["w_out"], tm=1024, tn=1024, tk=GLA_DV, comm=c, name=f"l{l}_gla_out"),
                          tasks["out"])
    return mix, (proj, y, o, a, st), {"proj": got_proj, "gla": got_gla, "out": got_out}


def _gla_layer_bwd(l, hb, dz, dzb, lw, saved, ffn_sums):
    proj, y, o, a, st = saved
    dy = _matmul(dzb, lw["w_out"], tb=True, tm=1024, tn=1024, tk=D_MODEL, name=f"l{l}_gla_dy")
    d_out = _matmul(y, dzb, ta=True, tm=1024, tn=1024, tk=2048, name=f"l{l}_gla_dwout")
    dq, dk, dv, dr, dgl, dwgu, dbias, dng = _gla_bwd(proj, lw["w_gate_up"], lw["gate_bias"], lw["norm_g"], o, a, st, dy, name=f"l{l}_gla_bwd")
    dproj = jnp.concatenate([dq, dk, dv, dr, dgl], axis=-1)
    d_in, got_down = _matmul(hb, dproj, ta=True, tm=1024, tn=896, tk=2048, comm=_chips_task([ffn_sums["ffn_w_down"][0]]),
                             name=f"l{l}_gla_dwin")
    dx, got_up = _matmul(dproj, lw["w_in"], tb=True, tm=1024, tn=1024, tk=896, res=dz, res_scale=DEEPNORM_ALPHA,
                         comm=_chips_task([ffn_sums["ffn_w_up"][0]]), name=f"l{l}_gla_dx")
    big = {"gla_w_in": _by_chip_parity(_col_blocks(d_in[:, :GLA_IN], GLA_IN // N_DEV)),
           "gla_w_out": _by_chip_parity(d_out.reshape(N_DEV, GLA_DV // N_DEV, D_MODEL))}
    small = {"gla_w_gate_up": dwgu[:GLA_GATE_RANK], "gla_gate_bias": dbias[0], "gla_norm_g": dng[0]}
    return dx, big, small, {"ffn_w_down": got_down[0], "ffn_w_up": got_up[0]}


def _dil_layer_fwd(l, hb, lw, tasks):
    proj, got_proj = _carry(lambda c: _mm_colblocks(hb, lw["w_in"], comm=c, name=f"l{l}_dil_proj"), tasks["proj"])
    os_, lses = [], []
    for gi in range(len(DIL_PATTERNS)):
        o, lse = _dil_fwd(proj, gi, name=f"l{l}_dil_attn{gi}")
        os_.append(o)
        lses.append(lse)
    omix = _dil_mix_fwd(os_, lses, name=f"l{l}_dil_mix")
    mix, got_out = _carry(lambda c: _matmul(omix, lw["w_out"], tm=1024, tn=1024, tk=DIL_WIDTH, comm=c, name=f"l{l}_dil_out"),
                          tasks["out"])
    return mix, (proj, os_, lses, omix), {"proj": got_proj, "out": got_out}


def _dil_layer_bwd(l, hb, dz, dzb, lw, saved, ffn_sums):
    proj, os_, lses, omix = saved
    dout = _matmul(dzb, lw["w_out"], tb=True, tm=1024, tn=1024, tk=D_MODEL, name=f"l{l}_dil_dy")
    d_out = _matmul(omix, dzb, ta=True, tm=1024, tn=1024, tk=2048, name=f"l{l}_dil_dwout")
    dos, dls = _dil_mix_bwd(os_, lses, dout, name=f"l{l}_dil_mix_bwd")
    parts = []
    for gi in range(len(DIL_PATTERNS)):
        parts += list(_dil_bwd(proj, gi, lses[gi], dos[gi], dls[gi], name=f"l{l}_dil_attn_bwd{gi}"))
    dproj = jnp.concatenate(parts, axis=-1)
    d_in, got_down = _mm_grad_colblocks(hb, dproj, DIL_IN // N_DEV, comm=_chips_task([ffn_sums["ffn_w_down"][0]]),
                                        name=f"l{l}_dil_dwin")
    dx, got_up = _mm_colblocks_t(dproj, lw["w_in"], dz, DEEPNORM_ALPHA, comm=_chips_task([ffn_sums["ffn_w_up"][0]]),
                                 name=f"l{l}_dil_dx")
    big = {"dil_w_in": _by_chip_parity(d_in), "dil_w_out": _by_chip_parity(_col_blocks(d_out, D_MODEL // N_DEV))}
    return dx, big, {}, {"ffn_w_down": got_down[0], "ffn_w_up": got_up[0]}


def _pack_rows(arrays, rows, cols=SMALL_COLS):
    flat = [a.reshape(-1) for a in arrays]
    used = sum(f.shape[0] for f in flat)
    return jnp.concatenate(flat + [jnp.zeros((rows * cols - used,), F32)]).reshape(rows, cols)


def _unpack_rows(packed, shapes):
    flat, out, off = packed.reshape(-1), [], 0
    for s in shapes:
        n = math.prod(s)
        out.append(flat[off:off + n].reshape(s))
        off += n
    return out


def _rows_for(shapes):
    n = sum(math.prod(s) for s in shapes)
    return -(-n // (SMALL_COLS * SUBLANES)) * SUBLANES


def kernel(x, gla_w_in, gla_w_gate_up, gla_gate_bias, gla_norm_g, gla_w_out, dil_w_in, dil_w_out, ffn_w_up, ffn_conv_w, ffn_conv_b, ffn_w_down, ln_g, ln_b, loss_target, m_gla_w_in, m_gla_w_gate_up, m_gla_gate_bias, m_gla_norm_g, m_gla_w_out, m_dil_w_in, m_dil_w_out, m_ffn_w_up, m_ffn_conv_w, m_ffn_conv_b, m_ffn_w_down, m_ln_g, m_ln_b, v_gla_w_in, v_gla_w_gate_up, v_gla_gate_bias, v_gla_norm_g, v_gla_w_out, v_dil_w_in, v_dil_w_out, v_ffn_w_up, v_ffn_conv_w, v_ffn_conv_b, v_ffn_w_down, v_ln_g, v_ln_b):
    w = dict(zip(WEIGHT_NAMES, (gla_w_in, gla_w_gate_up, gla_gate_bias, gla_norm_g, gla_w_out, dil_w_in, dil_w_out,
                                ffn_w_up, ffn_conv_w, ffn_conv_b, ffn_w_down, ln_g, ln_b)))
    mom = dict(zip(WEIGHT_NAMES, (m_gla_w_in, m_gla_w_gate_up, m_gla_gate_bias, m_gla_norm_g, m_gla_w_out, m_dil_w_in,
                                  m_dil_w_out, m_ffn_w_up, m_ffn_conv_w, m_ffn_conv_b, m_ffn_w_down, m_ln_g, m_ln_b)))
    var = dict(zip(WEIGHT_NAMES, (v_gla_w_in, v_gla_w_gate_up, v_gla_gate_bias, v_gla_norm_g, v_gla_w_out, v_dil_w_in,
                                  v_dil_w_out, v_ffn_w_up, v_ffn_conv_w, v_ffn_conv_b, v_ffn_w_down, v_ln_g, v_ln_b)))
    xi, yi, ci = _position()
    dev = 4 * xi + 2 * yi + ci
    place = jnp.stack([ci, 2 * xi + yi]).astype(jnp.int32)
    t = x.shape[0] * x.shape[1]
    h = x.reshape(t, D_MODEL)
    hb = h.astype(BF16)
    target = loss_target.reshape(t, D_MODEL)

    shards = [_layer_shards(l, w) for l in range(DEPTH)]
    arrived = {0: dict(zip(("in", "out", "vec"), _run_comm(_gather_task(shards[0][0]), name="gather_l0")))}
    lws, saved = [], []
    for l in range(DEPTH):
        gla = l % 2 == 0
        nxt = l + 1 if l + 1 < DEPTH else None
        nxt_dil = nxt is not None and nxt % 2 == 1
        here = arrived[l]
        lw = _layer_weights(l, w, here["in"], here["out"], here["vec"])
        tasks = {"proj": _gather_task([shards[l][2]]),
                 "gla": _gather_task([shards[l][1]]) if gla else None,
                 "out": _gather_task(shards[nxt][0][1:]) if nxt else None}
        mix, mixer_saved, got = (_gla_layer_fwd if gla else _dil_layer_fwd)(l, hb, lw, tasks)
        lw["w_down"] = _w_down_layout(got["proj"][0])
        lw["w_up_t"] = got["gla"][0] if gla else here["up"]
        y1, y1b, xh1, rs1 = _ln_fwd(h, mix, lw["ln_g"][0], lw["ln_b"][0], name=f"l{l}_ln1")
        task_hidden = _gather_task([shards[nxt][1] if nxt_dil else shards[nxt][0][0]]) if nxt else None
        task_down = _gather_task([shards[nxt][0][0]]) if nxt_dil else None
        ffn, ffn_saved, got_hidden, got_down = _ffn_fwd(l, y1b, lw, task_hidden, task_down)
        y2, y2b, xh2, rs2 = _ln_fwd(y1, ffn, lw["ln_g"][1], lw["ln_b"][1], name=f"l{l}_ln2")
        saved.append((hb, mixer_saved, y1b, xh1, rs1, ffn_saved, xh2, rs2))
        lws.append(lw)
        h, hb = y2, y2b
        if nxt:
            arrived[nxt] = {"out": got["out"][0], "vec": got["out"][1]}
            if nxt_dil:
                arrived[nxt].update({"up": got_hidden[0], "in": got_down[0]})
            else:
                arrived[nxt]["in"] = got_hidden[0]
    loss_local, dy = _loss_fwd_bwd(h, target, name="loss")
    loss = lax.psum(loss_local[0, 0], ("x", "y", "c"))

    big_names = tuple(ADAM_TILES)
    as_updated = lambda n, a: jnp.swapaxes(a, 1, 2) if n == "ffn_w_up" else a
    wt, mt, vt = ({n: as_updated(n, d[n]) for n in big_names} for d in (w, mom, var))
    results = {n: [lax.empty(wt[n].shape, F32) for _ in range(4)] for n in big_names}
    small_grads = {n: [None] * w[n].shape[0] for n in WEIGHT_NAMES if n not in big_names}

    def adamw(l, n, own, from_chips):
        results[n] = _adamw_big(own, from_chips, wt[n], mt[n], vt[n], results[n], l if n.startswith("ffn") else l // 2,
                                tr=ADAM_TILES[n][0], tc=ADAM_TILES[n][1], name=f"l{l}_{n}_adamw")

    pending = None
    for l in reversed(range(DEPTH)):
        lw = lws[l]
        hb_in, mixer_saved, y1b, xh1, rs1, ffn_saved, xh2, rs2 = saved[l]
        dz2, dz2b, dg2, db2 = _ln_bwd(dy, xh2, rs2, lw["ln_g"][1], name=f"l{l}_ln2_bwd")
        dy1, ffn_sums, small_ffn, pend_sums, got_pending = _ffn_bwd(l, y1b, dz2, dz2b, lw, ffn_saved, place, pending)
        for (n, (_, own)), r2 in zip(pend_sums.items(), got_pending):
            adamw(pending[0], n, own, r2)
        dz1, dz1b, dg1, db1 = _ln_bwd(dy1, xh1, rs1, lw["ln_g"][0], name=f"l{l}_ln1_bwd")
        dy, big_mix, small_mix, got_ffn = (_gla_layer_bwd if l % 2 == 0 else _dil_layer_bwd)(
            l, hb_in, dz1, dz1b, lw, mixer_saved, ffn_sums)
        for n, (_, own) in ffn_sums.items():
            adamw(l, n, own, got_ffn[n])
        pending = (l, big_mix)
        small_grads["ln_g"][l] = jnp.concatenate([dg1, dg2], axis=0)
        small_grads["ln_b"][l] = jnp.concatenate([db1, db2], axis=0)
        for n, g in small_ffn.items():
            small_grads[n][l] = g
        for n, g in small_mix.items():
            small_grads[n][l // 2] = g
    from_sibling = _run_comm(_sibling_task(list(pending[1].values()), [None] * len(pending[1])), name="reduce_sibling_l0")
    last_sums = {n: _sibling_sum(0, n, g, None, r1, place) for (n, g), r1 in zip(pending[1].items(), from_sibling)}
    from_chips = _run_comm(_chips_task([s[0] for s in last_sums.values()]), name="reduce_chips_l0")
    for (n, (_, own)), r2 in zip(last_sums.items(), from_chips):
        adamw(0, n, own, r2)
    results = {n: [as_updated(n, a) for a in results[n]] for n in big_names}
    grad_x = dy.reshape(x.shape)

    small_names = [n for n in WEIGHT_NAMES if n not in big_names]
    full_shapes = {"gla_w_gate_up": (2, GLA_GATE_RANK, GLA_DK), "gla_gate_bias": (2, GLA_DK), "gla_norm_g": (2, GLA_HEAD_V),
                   "ffn_conv_w": (DEPTH, 3, 2 * D_FF), "ffn_conv_b": (DEPTH, 2 * D_FF),
                   "ln_g": (DEPTH, 2, D_MODEL), "ln_b": (DEPTH, 2, D_MODEL)}
    shapes = [full_shapes[n] for n in small_names]
    rows = _rows_for(shapes)
    packed = _pack_rows([jnp.stack(small_grads[n]) for n in small_names], rows)
    summed = _sum_gathered(_run_comm(_gather_task([packed]), name="gather_small_grads")[0], name="sum_small_grads")
    full = dict(zip(small_names, _unpack_rows(summed, shapes)))
    own = {n: (full[n] if w[n].shape == full[n].shape
               else lax.dynamic_slice_in_dim(full[n], dev * w[n].shape[-1], w[n].shape[-1], axis=full[n].ndim - 1))
           for n in small_names}
    own_shapes = [w[n].shape for n in small_names]
    rows = _rows_for(own_shapes)
    pk = lambda d: _pack_rows([d[n] for n in small_names], rows)
    outs = _adamw_small(pk(own), pk(w), pk(mom), pk(var), name="adamw_small")
    for n in small_names:
        results[n] = [own[n]]
    for k, packed_out in enumerate(outs):
        for n, a in zip(small_names, _unpack_rows(packed_out, own_shapes)):
            results[n].append(a)

    return (loss, grad_x) + tuple(results[n][k] for k in range(4) for n in WEIGHT_NAMES)
```

```python
import functools
import math

import jax
import jax.numpy as jnp
from jax import lax
from jax.experimental import pallas as pl
from jax.experimental.pallas import tpu as pltpu

F32 = jnp.float32
BF16 = jnp.bfloat16

D_MODEL = 2048
SEQ = 2048
DEPTH = 4
N_DEV = 8
GLA_HEADS = 4
GLA_DK = 1024
GLA_DV = 2048
GLA_HEAD_K = 256
GLA_HEAD_V = 512
GLA_GATE_RANK = 16
GLA_GATE_NORMALIZER = 16.0
GLA_CHUNK = 64
GLA_MAIN = 2 * GLA_DK + 2 * GLA_DV
GLA_IN = GLA_MAIN + GLA_GATE_RANK
DIL_PATTERNS = ((128, 1), (512, 4), (2048, 16))
DIL_HEADS = 8
DIL_HEAD_DIM = 128
DIL_WIDTH = DIL_HEADS * DIL_HEAD_DIM
DIL_BLOCK = 128
DIL_IN = 3 * len(DIL_PATTERNS) * DIL_WIDTH
D_FF = 5504
DEEPNORM_ALPHA = (2 * DEPTH) ** 0.25
LN_EPS = 1e-5
RMS_EPS = 1e-6
ADAM_LR = 0.001
ADAM_B1 = 0.9
ADAM_B2 = 0.999
ADAM_EPS = 1e-08
ADAM_WD = 0.01
ADAM_STEP = 10

LANES = 128
SUBLANES = 8
VMEM_LIMIT_BYTES = 56 * 1024 * 1024

FF_SHARD = 2 * D_FF // N_DEV
FF_SHARD_PAD = 1408
FF_HALF_PAD = 4 * FF_SHARD_PAD
GLOW_PAD = LANES


def _cparams(dims=None):
    return pltpu.CompilerParams(dimension_semantics=dims, vmem_limit_bytes=VMEM_LIMIT_BYTES)


class _Comm:
    def __init__(self, inputs, out_shapes, sem_shapes, start, middle, finish):
        self.inputs, self.out_shapes, self.sem_shapes = list(inputs), list(out_shapes), list(sem_shapes)
        self.start, self.middle, self.finish = start, middle, finish


def _join_comm(comms):
    def cut(refs, counts):
        out, off = [], 0
        for c in counts:
            out.append(refs[off:off + c])
            off += c
        return out

    n_in = [len(c.inputs) for c in comms]
    n_out = [len(c.out_shapes) for c in comms]
    n_sem = [len(c.sem_shapes) for c in comms]

    def hook(which):
        def run(ins, outs, sems):
            for c, i, o, s in zip(comms, cut(ins, n_in), cut(outs, n_out), cut(sems, n_sem)):
                getattr(c, which)(i, o, s)
        return run

    return _Comm([a for c in comms for a in c.inputs], [s for c in comms for s in c.out_shapes],
                 [s for c in comms for s in c.sem_shapes], hook("start"), hook("middle"), hook("finish"))


def _mm(a, b, *, grid, a_spec, b_spec, o_spec, out_shape, acc_shape, ta=False, tb=False, res=None, res_scale=1.0,
        comm=None, name):
    nk = grid[2]
    dims = (((0 if ta else 1,), (1 if tb else 0,)), ((), ()))
    has_res = res is not None
    n_in = 2 + has_res
    n_cin = len(comm.inputs) if comm else 0
    n_cout = len(comm.out_shapes) if comm else 0

    def body(*refs):
        a_ref, b_ref = refs[0], refs[1]
        res_ref = refs[2] if has_res else None
        o_ref = refs[n_in + n_cin]
        scratch = refs[n_in + n_cin + 1 + n_cout:]
        acc_ref = scratch[0] if nk > 1 else None
        if comm:
            task = (refs[n_in:n_in + n_cin], refs[n_in + n_cin + 1:n_in + n_cin + 1 + n_cout],
                    scratch[1:] if nk > 1 else scratch)
            step = (pl.program_id(0) * grid[1] + pl.program_id(1)) * nk + pl.program_id(2)
            steps = grid[0] * grid[1] * nk

            @pl.when(step == 0)
            def _():
                comm.start(*task)

        p = lax.dot_general(a_ref[...], b_ref[...], dims, preferred_element_type=F32)

        def finish(acc):
            if has_res:
                acc = acc + res_scale * res_ref[...]
            o_ref[...] = acc.astype(o_ref.dtype)

        if nk == 1:
            finish(p)
        else:
            kk = pl.program_id(2)

            @pl.when(kk == 0)
            def _():
                acc_ref[...] = p

            @pl.when(kk > 0)
            def _():
                acc_ref[...] += p

            @pl.when(kk == nk - 1)
            def _():
                finish(acc_ref[...])

        if comm:
            @pl.when(step == steps - 1)
            def _():
                comm.middle(*task)
                comm.finish(*task)

    in_specs = [a_spec, b_spec] + ([o_spec] if has_res else [])
    args = (a, b) + ((res,) if has_res else ())
    acc = [pltpu.VMEM(acc_shape, F32)] if nk > 1 else []
    if not comm:
        return pl.pallas_call(
            body, name=name, grid=grid, in_specs=in_specs, out_specs=o_spec, out_shape=out_shape, scratch_shapes=acc,
            compiler_params=_cparams(("parallel", "parallel", "arbitrary")),
        )(*args)
    outs = pl.pallas_call(
        body, name=name, grid=grid, in_specs=in_specs + [_ANY] * n_cin, out_specs=[o_spec] + [_ANY] * n_cout,
        out_shape=[out_shape] + comm.out_shapes, scratch_shapes=acc + comm.sem_shapes,
        compiler_params=_cparams(("arbitrary", "arbitrary", "arbitrary")),
    )(*args, *comm.inputs)
    return outs[0], list(outs[1:])


def _matmul(a, b, *, ta=False, tb=False, tm, tn, tk, out_dtype=F32, res=None, res_scale=1.0, comm=None, name):
    m, k = (a.shape[1], a.shape[0]) if ta else a.shape
    n = b.shape[0] if tb else b.shape[1]
    assert (b.shape[1] if tb else b.shape[0]) == k
    assert m % tm == 0 and n % tn == 0 and k % tk == 0, (m, n, k, tm, tn, tk)
    a_spec = pl.BlockSpec((tk, tm), lambda i, j, kk: (kk, i)) if ta else pl.BlockSpec((tm, tk), lambda i, j, kk: (i, kk))
    b_spec = pl.BlockSpec((tn, tk), lambda i, j, kk: (j, kk)) if tb else pl.BlockSpec((tk, tn), lambda i, j, kk: (kk, j))
    return _mm(a, b, grid=(m // tm, n // tn, k // tk), a_spec=a_spec, b_spec=b_spec,
               o_spec=pl.BlockSpec((tm, tn), lambda i, j, kk: (i, j)), out_shape=jax.ShapeDtypeStruct((m, n), out_dtype),
               acc_shape=(tm, tn), ta=ta, tb=tb, res=res, res_scale=res_scale, comm=comm, name=name)


MM_ROWS = 1024


def _mm_colblocks(a, wb, *, comm=None, name):
    m, k = a.shape
    nb, _, w = wb.shape
    return _mm(a, wb, grid=(m // MM_ROWS, nb, 1),
               a_spec=pl.BlockSpec((MM_ROWS, k), lambda i, j, kk: (i, 0)),
               b_spec=pl.BlockSpec((None, k, w), lambda i, j, kk: (j, 0, 0)),
               o_spec=pl.BlockSpec((MM_ROWS, w), lambda i, j, kk: (i, j)),
               out_shape=jax.ShapeDtypeStruct((m, nb * w), F32), acc_shape=(MM_ROWS, w), comm=comm, name=name)


def _mm_colblocks_t(a, wb, res, res_scale, *, comm=None, name):
    m = a.shape[0]
    nb, n, w = wb.shape
    tn = 1024
    return _mm(a, wb, grid=(m // MM_ROWS, n // tn, nb),
               a_spec=pl.BlockSpec((MM_ROWS, w), lambda i, j, kk: (i, kk)),
               b_spec=pl.BlockSpec((None, tn, w), lambda i, j, kk: (kk, j, 0)),
               o_spec=pl.BlockSpec((MM_ROWS, tn), lambda i, j, kk: (i, j)),
               out_shape=jax.ShapeDtypeStruct((m, n), F32), acc_shape=(MM_ROWS, tn), tb=True,
               res=res, res_scale=res_scale, comm=comm, name=name)


def _mm_grad_colblocks(x, dy, w, *, comm=None, name):
    t, k = x.shape
    nb = dy.shape[1] // w
    tm, tk = 1024, 2048
    return _mm(x, dy, grid=(k // tm, nb, t // tk),
               a_spec=pl.BlockSpec((tk, tm), lambda i, j, kk: (kk, i)),
               b_spec=pl.BlockSpec((tk, w), lambda i, j, kk: (kk, j)),
               o_spec=pl.BlockSpec((None, tm, w), lambda i, j, kk: (j, i, 0)),
               out_shape=jax.ShapeDtypeStruct((nb, k, w), F32), acc_shape=(tm, w), ta=True, comm=comm, name=name)


def _ffn_hidden(y, wt, *, comm=None, name):
    t, k = y.shape
    ni = t // MM_ROWS
    return _mm(y, wt, grid=(ni, N_DEV, 1),
               a_spec=pl.BlockSpec((MM_ROWS, k), lambda i, j, kk: (i, 0)),
               b_spec=pl.BlockSpec((None, FF_SHARD_PAD, k), lambda i, j, kk: (j, 0, 0)),
               o_spec=pl.BlockSpec((MM_ROWS, FF_SHARD_PAD), lambda i, j, kk: ((j // 4) * ni + i, j % 4)),
               out_shape=jax.ShapeDtypeStruct((2 * t, FF_HALF_PAD), F32), acc_shape=(MM_ROWS, FF_SHARD_PAD), tb=True,
               comm=comm, name=name)


def _ffn_hidden_dy(dh, wt, res, res_scale, *, comm=None, name):
    t = dh.shape[0] // 2
    ni, tn = t // MM_ROWS, 1024
    return _mm(dh, wt, grid=(ni, D_MODEL // tn, N_DEV),
               a_spec=pl.BlockSpec((MM_ROWS, FF_SHARD_PAD), lambda i, j, kk: ((kk // 4) * ni + i, kk % 4)),
               b_spec=pl.BlockSpec((None, FF_SHARD_PAD, tn), lambda i, j, kk: (kk, 0, j)),
               o_spec=pl.BlockSpec((MM_ROWS, tn), lambda i, j, kk: (i, j)),
               out_shape=jax.ShapeDtypeStruct((t, D_MODEL), F32), acc_shape=(MM_ROWS, tn),
               res=res, res_scale=res_scale, comm=comm, name=name)


def _ffn_hidden_dw(dh, y, *, comm=None, name):
    t, k = y.shape
    tk, tn = 2048, 1024
    nk = t // tk
    return _mm(dh, y, grid=(N_DEV, k // tn, nk),
               a_spec=pl.BlockSpec((tk, FF_SHARD_PAD), lambda i, j, kk: ((i // 4) * nk + kk, i % 4)),
               b_spec=pl.BlockSpec((tk, tn), lambda i, j, kk: (kk, j)),
               o_spec=pl.BlockSpec((None, FF_SHARD_PAD, tn), lambda i, j, kk: (i, 0, j)),
               out_shape=jax.ShapeDtypeStruct((N_DEV, FF_SHARD_PAD, k), F32), acc_shape=(FF_SHARD_PAD, tn), ta=True,
               comm=comm, name=name)


def _ffn_down_dw(act, dz, *, comm=None, name):
    t, k = dz.shape
    tk, tn = 2048, 1024
    return _mm(act, dz, grid=(4, k // tn, t // tk),
               a_spec=pl.BlockSpec((tk, FF_SHARD_PAD), lambda i, j, kk: (kk, i)),
               b_spec=pl.BlockSpec((tk, tn), lambda i, j, kk: (kk, j)),
               o_spec=pl.BlockSpec((None, FF_SHARD_PAD, tn), lambda i, j, kk: (i, 0, j)),
               out_shape=jax.ShapeDtypeStruct((4, FF_SHARD_PAD, k), F32), acc_shape=(FF_SHARD_PAD, tn), ta=True,
               comm=comm, name=name)


LN_ROWS = 256


def _ln_fwd(x, f, g, b, *, name):
    t, d = x.shape

    def body(x_ref, f_ref, g_ref, b_ref, y_ref, yb_ref, xh_ref, rs_ref):
        z = DEEPNORM_ALPHA * x_ref[...] + f_ref[...]
        mu = jnp.mean(z, axis=-1, keepdims=True)
        zc = z - mu
        var = jnp.mean(zc * zc, axis=-1, keepdims=True)
        rstd = lax.rsqrt(var + LN_EPS)
        xh = zc * rstd
        y = xh * g_ref[...] + b_ref[...]
        y_ref[...] = y
        yb_ref[...] = y.astype(BF16)
        xh_ref[...] = xh
        rs_ref[...] = rstd

    row = pl.BlockSpec((LN_ROWS, d), lambda i: (i, 0))
    vec = pl.BlockSpec((1, d), lambda i: (0, 0))
    return pl.pallas_call(
        body,
        name=name,
        grid=(t // LN_ROWS,),
        in_specs=[row, row, vec, vec],
        out_specs=[row, row, row, pl.BlockSpec((LN_ROWS, 1), lambda i: (i, 0))],
        out_shape=[jax.ShapeDtypeStruct((t, d), F32), jax.ShapeDtypeStruct((t, d), BF16),
                   jax.ShapeDtypeStruct((t, d), F32), jax.ShapeDtypeStruct((t, 1), F32)],
        compiler_params=_cparams(("parallel",)),
    )(x, f, g, b)


def _ln_bwd(dy, xhat, rstd, g, *, name):
    t, d = dy.shape

    def body(dy_ref, xh_ref, rs_ref, g_ref, dz_ref, dzb_ref, dg_ref, db_ref):
        dyv = dy_ref[...]
        xh = xh_ref[...]
        dyg = dyv * g_ref[...]
        m1 = jnp.mean(dyg, axis=-1, keepdims=True)
        m2 = jnp.mean(dyg * xh, axis=-1, keepdims=True)
        dz = rs_ref[...] * (dyg - m1 - xh * m2)
        dz_ref[...] = dz
        dzb_ref[...] = dz.astype(BF16)
        dg_part = jnp.sum(dyv * xh, axis=0, keepdims=True)
        db_part = jnp.sum(dyv, axis=0, keepdims=True)

        @pl.when(pl.program_id(0) == 0)
        def _():
            dg_ref[...] = dg_part
            db_ref[...] = db_part

        @pl.when(pl.program_id(0) > 0)
        def _():
            dg_ref[...] += dg_part
            db_ref[...] += db_part

    row = pl.BlockSpec((LN_ROWS, d), lambda i: (i, 0))
    vec = pl.BlockSpec((1, d), lambda i: (0, 0))
    return pl.pallas_call(
        body,
        name=name,
        grid=(t // LN_ROWS,),
        in_specs=[row, row, pl.BlockSpec((LN_ROWS, 1), lambda i: (i, 0)), vec],
        out_specs=[row, row, vec, vec],
        out_shape=[jax.ShapeDtypeStruct((t, d), F32), jax.ShapeDtypeStruct((t, d), BF16),
                   jax.ShapeDtypeStruct((1, d), F32), jax.ShapeDtypeStruct((1, d), F32)],
        compiler_params=_cparams(("arbitrary",)),
    )(dy, xhat, rstd, g)


def _loss_fwd_bwd(y, target, *, name):
    t, d = y.shape

    def body(y_ref, t_ref, loss_ref, dy_ref):
        err = y_ref[...] - t_ref[...]
        dy_ref[...] = err * (1.0 / d)
        part = 0.5 * jnp.sum(jnp.mean(err * err, axis=-1, keepdims=True), axis=0, keepdims=True)

        @pl.when(pl.program_id(0) == 0)
        def _():
            loss_ref[...] = part

        @pl.when(pl.program_id(0) > 0)
        def _():
            loss_ref[...] += part

    row = pl.BlockSpec((LN_ROWS, d), lambda i: (i, 0))
    return pl.pallas_call(
        body,
        name=name,
        grid=(t // LN_ROWS,),
        in_specs=[row, row],
        out_specs=[pl.BlockSpec((1, 1), lambda i: (0, 0)), row],
        out_shape=[jax.ShapeDtypeStruct((1, 1), F32), jax.ShapeDtypeStruct((t, d), F32)],
        compiler_params=_cparams(("arbitrary",)),
    )(y, target)


FFN_COLS = 256


def _shift_rows(h, s):
    rows = lax.broadcasted_iota(jnp.int32, h.shape, 0)
    return jnp.where(rows >= s, pltpu.roll(h, s, 0), 0.0)


def _shift_rows_up(h, s):
    n = h.shape[0]
    rows = lax.broadcasted_iota(jnp.int32, h.shape, 0)
    return jnp.where(rows < n - s, pltpu.roll(h, n - s, 0), 0.0)


def _causal_conv(h, w, b):
    return w[0:1, :] * _shift_rows(h, 2) + w[1:2, :] * _shift_rows(h, 1) + w[2:3, :] * h + b


def _sigmoid(x):
    return 1.0 / (1.0 + jnp.exp(-x))


def _convgate_fwd(h, cw, cb, *, name):
    t, n = h.shape[0] // 2, h.shape[1]
    nb = t // SEQ

    def body(hg_ref, hu_ref, wg_ref, wu_ref, bg_ref, bu_ref, a_ref):
        gate = _causal_conv(hg_ref[...], wg_ref[...], bg_ref[...])
        up = _causal_conv(hu_ref[...], wu_ref[...], bu_ref[...])
        a_ref[...] = (gate * _sigmoid(gate) * up).astype(BF16)

    def half(rows, k):
        return pl.BlockSpec((None, rows, FFN_COLS), lambda s, j: (k, 0, j))

    return pl.pallas_call(
        body,
        name=name,
        grid=(nb, n // FFN_COLS),
        in_specs=[pl.BlockSpec((SEQ, FFN_COLS), lambda s, j: (s, j)), pl.BlockSpec((SEQ, FFN_COLS), lambda s, j: (nb + s, j)),
                  half(3, 0), half(3, 1), half(1, 0), half(1, 1)],
        out_specs=pl.BlockSpec((SEQ, FFN_COLS), lambda s, j: (s, j)),
        out_shape=jax.ShapeDtypeStruct((t, n), BF16),
        compiler_params=_cparams(("parallel", "parallel")),
    )(h, h, cw, cw, cb, cb)


def _convgate_bwd(h, dact, cw, cb, *, name):
    t, n = h.shape[0] // 2, h.shape[1]
    nb = t // SEQ

    def body(hg_ref, hu_ref, da_ref, wg_ref, wu_ref, bg_ref, bu_ref, dh_ref, dw_ref, db_ref):
        dhg_ref, dhu_ref = dh_ref.at[0], dh_ref.at[1]
        dwg_ref, dwu_ref = dw_ref.at[0], dw_ref.at[1]
        dbg_ref, dbu_ref = db_ref.at[0], db_ref.at[1]
        hgv, huv = hg_ref[...], hu_ref[...]
        wgv, wuv = wg_ref[...], wu_ref[...]
        gate = _causal_conv(hgv, wgv, bg_ref[...])
        up = _causal_conv(huv, wuv, bu_ref[...])
        sg = _sigmoid(gate)
        da = da_ref[...]
        dgate = da * up * (sg * (1.0 + gate * (1.0 - sg)))
        dup = da * (gate * sg)

        def conv_bwd(dc, h, w, dh_ref, dw_ref, db_ref):
            dh = w[2:3, :] * dc + w[1:2, :] * _shift_rows_up(dc, 1) + w[0:1, :] * _shift_rows_up(dc, 2)
            dh_ref[...] = dh.astype(BF16)
            dws = [jnp.sum(dc * _shift_rows(h, 2), axis=0, keepdims=True),
                   jnp.sum(dc * _shift_rows(h, 1), axis=0, keepdims=True),
                   jnp.sum(dc * h, axis=0, keepdims=True)]
            db = jnp.sum(dc, axis=0, keepdims=True)

            @pl.when(pl.program_id(1) == 0)
            def _():
                for r in range(3):
                    dw_ref[r:r + 1, :] = dws[r]
                db_ref[...] = db

            @pl.when(pl.program_id(1) > 0)
            def _():
                for r in range(3):
                    dw_ref[r:r + 1, :] += dws[r]
                db_ref[...] += db

        conv_bwd(dgate, hgv, wgv, dhg_ref, dwg_ref, dbg_ref)
        conv_bwd(dup, huv, wuv, dhu_ref, dwu_ref, dbu_ref)

    def half(rows, k):
        return pl.BlockSpec((None, rows, FFN_COLS), lambda j, s: (k, 0, j))

    def both(rows):
        return pl.BlockSpec((2, rows, FFN_COLS), lambda j, s: (0, 0, j))

    return pl.pallas_call(
        body,
        name=name,
        grid=(n // FFN_COLS, nb),
        in_specs=[pl.BlockSpec((SEQ, FFN_COLS), lambda j, s: (s, j)), pl.BlockSpec((SEQ, FFN_COLS), lambda j, s: (nb + s, j)),
                  pl.BlockSpec((SEQ, FFN_COLS), lambda j, s: (s, j)), half(3, 0), half(3, 1), half(1, 0), half(1, 1)],
        out_specs=[pl.BlockSpec((2, SEQ, FFN_COLS), lambda j, s: (0, s, j)), both(3), both(1)],
        out_shape=[jax.ShapeDtypeStruct((2, t, n), BF16), jax.ShapeDtypeStruct((2, 3, n), F32),
                   jax.ShapeDtypeStruct((2, 1, n), F32)],
        compiler_params=_cparams(("parallel", "arbitrary")),
    )(h, h, dact, cw, cw, cb, cb)


GLA_Q_SCALE = GLA_HEAD_K ** -0.5
GLA_NC = SEQ // GLA_CHUNK
_NT = (((1,), (1,)), ((), ()))
_TN = (((0,), (0,)), ((), ()))


def _cumsum_rows(g):
    n = g.shape[0]
    rows = lax.broadcasted_iota(jnp.int32, g.shape, 0)
    s = 1
    while s < n:
        g = g + jnp.where(rows >= s, pltpu.roll(g, s, 0), 0.0)
        s *= 2
    return g


def _suffix_sum_rows(x):
    n = x.shape[0]
    rows = lax.broadcasted_iota(jnp.int32, x.shape, 0)
    s = 1
    while s < n:
        x = x + jnp.where(rows < n - s, pltpu.roll(x, n - s, 0), 0.0)
        s *= 2
    return x


def _gla_log_gate(gl_ref, wgu_ref, bias_ref):
    pre = jnp.dot(gl_ref[...].astype(BF16), wgu_ref[...], preferred_element_type=F32) + bias_ref[...]
    log_sig = jnp.minimum(pre, 0.0) - jnp.log(1.0 + jnp.exp(-jnp.abs(pre)))
    return pre, log_sig * (1.0 / GLA_GATE_NORMALIZER)


def _pair_rows(j):
    return (j // SUBLANES) * SUBLANES


def _gla_pair_fwd(q_scr, k_ref, b_scr, a_scr, h):
    c = GLA_CHUNK
    kc = pl.ds(h * GLA_HEAD_K, GLA_HEAD_K)
    a_scr[...] = jnp.zeros(a_scr.shape, F32)
    lane = lax.broadcasted_iota(jnp.int32, (1, c), 1)
    for j in range(c):
        r0 = _pair_rows(j)
        rs = pl.ds(r0, c - r0)
        rows = lax.broadcasted_iota(jnp.int32, (c - r0, 1), 0) + r0
        e = jnp.exp2(jnp.minimum(b_scr[rs, kc] - b_scr[pl.ds(j, 1), kc], 0.0))
        w = q_scr[rs, kc] * k_ref[pl.ds(j, 1), kc] * e
        col = jnp.where(rows >= j, jnp.sum(w, axis=-1, keepdims=True), 0.0)
        a_scr[rs, :] += col * (lane == j).astype(F32)


def _call_with_comm(body, comm, *, n_in, n_out, grid, in_specs, out_specs, out_shape, scratch_shapes, name, args):
    n_cin, n_cout, n_scr = len(comm.inputs), len(comm.out_shapes), len(scratch_shapes)

    def carrier(*refs):
        ins, cins = refs[:n_in], refs[n_in:n_in + n_cin]
        outs, couts = refs[n_in + n_cin:n_in + n_cin + n_out], refs[n_in + n_cin + n_out:n_in + n_cin + n_out + n_cout]
        scr = refs[n_in + n_cin + n_out + n_cout:]
        task = (cins, couts, scr[n_scr:])
        step, steps = 0, 1
        for axis, size in enumerate(grid):
            step = step * size + pl.program_id(axis)
            steps *= size

        @pl.when(step == 0)
        def _():
            comm.start(*task)

        body(*ins, *outs, *scr[:n_scr])

        @pl.when(step == steps - 1)
        def _():
            comm.middle(*task)
            comm.finish(*task)

    res = pl.pallas_call(
        carrier, name=name, grid=grid, in_specs=list(in_specs) + [_ANY] * n_cin, out_specs=list(out_specs) + [_ANY] * n_cout,
        out_shape=list(out_shape) + comm.out_shapes, scratch_shapes=list(scratch_shapes) + comm.sem_shapes,
        compiler_params=_cparams(("arbitrary",) * len(grid)),
    )(*args, *comm.inputs)
    return list(res[:n_out]), list(res[n_out:])


LOG2_E = 1.4426950408889634


def _gla_fwd(proj, wgu, bias, ng, *, comm=None, name):
    t = proj.shape[0]
    nb, nc, c = t // SEQ, GLA_NC, GLA_CHUNK

    def body(q_ref, k_ref, v_ref, r_ref, gl_ref, wgu_ref, bias_ref, ng_ref,
             y_ref, o_ref, a_ref, st_ref, state, b_scr, a_scr, q_scr):
        @pl.when(pl.program_id(1) == 0)
        def _():
            state[...] = jnp.zeros(state.shape, F32)

        _, g = _gla_log_gate(gl_ref, wgu_ref, bias_ref)
        b_scr[...] = _cumsum_rows(g) * LOG2_E
        q_scr[...] = q_ref[...] * GLA_Q_SCALE
        for h in range(GLA_HEADS):
            kc = pl.ds(h * GLA_HEAD_K, GLA_HEAD_K)
            vc = pl.ds(h * GLA_HEAD_V, GLA_HEAD_V)
            qh = q_scr[:, kc]
            kh = k_ref[:, kc]
            vh = v_ref[:, vc].astype(BF16)
            bh = b_scr[:, kc]
            blast = b_scr[pl.ds(c - 1, 1), kc]
            st = state[h]
            st_ref[h] = st
            o_inter = lax.dot_general((qh * jnp.exp2(bh)).astype(BF16), st.astype(BF16), _NT, preferred_element_type=F32)
            _gla_pair_fwd(q_scr, k_ref, b_scr, a_scr, h)
            a = a_scr[...]
            a_ref[h] = a
            o = o_inter + jnp.dot(a.astype(BF16), vh, preferred_element_type=F32)
            kd = (kh * jnp.exp2(blast - bh)).astype(BF16)
            state[h] = st * jnp.exp2(blast) + lax.dot_general(vh, kd, _TN, preferred_element_type=F32)
            o_ref[:, vc] = o
            rs = lax.rsqrt(jnp.mean(o * o, axis=-1, keepdims=True) + RMS_EPS)
            rh = r_ref[:, vc]
            y_ref[:, vc] = ((o * rs * ng_ref[...]) * (rh * _sigmoid(rh))).astype(BF16)

    def tok(width, col):
        return pl.BlockSpec((c, width), lambda b, i: (b * nc + i, col))

    whole = lambda shape: pl.BlockSpec(shape, lambda b, i: (0,) * len(shape))
    call = dict(
        grid=(nb, nc),
        in_specs=[tok(GLA_DK, 0), tok(GLA_DK, 1), tok(GLA_DV, 1), tok(GLA_DV, 2), tok(GLOW_PAD, GLA_MAIN // GLOW_PAD),
                  whole((GLOW_PAD, GLA_DK)), whole((1, GLA_DK)), whole((1, GLA_HEAD_V))],
        out_specs=[tok(GLA_DV, 0), tok(GLA_DV, 0),
                   pl.BlockSpec((GLA_HEADS, c, c), lambda b, i: (0, b * nc + i, 0)),
                   pl.BlockSpec((None, GLA_HEADS, GLA_HEAD_V, GLA_HEAD_K), lambda b, i: (b * nc + i, 0, 0, 0))],
        out_shape=[jax.ShapeDtypeStruct((t, GLA_DV), BF16), jax.ShapeDtypeStruct((t, GLA_DV), F32),
                   jax.ShapeDtypeStruct((GLA_HEADS, t, c), F32),
                   jax.ShapeDtypeStruct((t // c, GLA_HEADS, GLA_HEAD_V, GLA_HEAD_K), F32)],
        scratch_shapes=[pltpu.VMEM((GLA_HEADS, GLA_HEAD_V, GLA_HEAD_K), F32), pltpu.VMEM((c, GLA_DK), F32),
                        pltpu.VMEM((c, c), F32), pltpu.VMEM((c, GLA_DK), F32)],
        name=name)
    args = (proj, proj, proj, proj, proj, wgu, bias, ng)
    if comm is None:
        return pl.pallas_call(body, compiler_params=_cparams(("parallel", "arbitrary")), **call)(*args)
    return _call_with_comm(body, comm, n_in=8, n_out=4, args=args, **call)


def _gla_pair_bwd(q_scr, k_ref, b_scr, da_scr, dq_scr, dk_scr, h):
    c = GLA_CHUNK
    kc = pl.ds(h * GLA_HEAD_K, GLA_HEAD_K)
    lane = lax.broadcasted_iota(jnp.int32, (1, c), 1)
    for j in range(c):
        r0 = _pair_rows(j)
        rs = pl.ds(r0, c - r0)
        rows = lax.broadcasted_iota(jnp.int32, (c - r0, 1), 0) + r0
        e = jnp.exp2(jnp.minimum(b_scr[rs, kc] - b_scr[pl.ds(j, 1), kc], 0.0))
        dacol = jnp.sum(jnp.where(lane == j, da_scr[rs, :], 0.0), axis=-1, keepdims=True)
        t1 = jnp.where(rows >= j, dacol, 0.0) * e
        dq_scr[rs, kc] += t1 * k_ref[pl.ds(j, 1), kc]
        dk_scr[pl.ds(j, 1), kc] += jnp.sum(t1 * q_scr[rs, kc], axis=0, keepdims=True)


def _gla_bwd(proj, wgu, bias, ng, o, a, states, dy, *, comm=None, name):
    t = proj.shape[0]
    nb, nc, c = t // SEQ, GLA_NC, GLA_CHUNK

    def body(q_ref, k_ref, v_ref, r_ref, gl_ref, wgu_ref, bias_ref, ng_ref, o_ref, a_ref, stp_ref, stn_ref, dy_ref,
             dq_ref, dk_ref, dv_ref, dr_ref, dgl_ref, dwgu_ref, dbias_ref, dng_ref,
             dstate, b_scr, da_scr, dq_scr, dk_scr, dg_scr, q_scr):
        first = jnp.logical_and(pl.program_id(0) == 0, pl.program_id(1) == 0)

        @pl.when(first)
        def _():
            dwgu_ref[...] = jnp.zeros(dwgu_ref.shape, F32)
            dbias_ref[...] = jnp.zeros(dbias_ref.shape, F32)
            dng_ref[...] = jnp.zeros(dng_ref.shape, F32)

        @pl.when(pl.program_id(1) == 0)
        def _():
            dstate[...] = jnp.zeros(dstate.shape, F32)

        pre, g = _gla_log_gate(gl_ref, wgu_ref, bias_ref)
        b_scr[...] = _cumsum_rows(g) * LOG2_E
        q_scr[...] = q_ref[...] * GLA_Q_SCALE
        ngv = ng_ref[...]
        tri = lax.broadcasted_iota(jnp.int32, (c, c), 0) >= lax.broadcasted_iota(jnp.int32, (c, c), 1)
        for h in range(GLA_HEADS):
            kc = pl.ds(h * GLA_HEAD_K, GLA_HEAD_K)
            vc = pl.ds(h * GLA_HEAD_V, GLA_HEAD_V)
            oh = o_ref[:, vc]
            rh = r_ref[:, vc]
            dyh = dy_ref[:, vc]
            rs = lax.rsqrt(jnp.mean(oh * oh, axis=-1, keepdims=True) + RMS_EPS)
            u = oh * rs
            sg = _sigmoid(rh)
            sr = rh * sg
            dr_ref[:, vc] = (dyh * (u * ngv) * (sg * (1.0 + rh * (1.0 - sg)))).astype(BF16)
            dng_ref[...] += jnp.sum(dyh * sr * u, axis=0, keepdims=True)
            du = dyh * sr * ngv
            do = (rs * (du - u * jnp.mean(du * u, axis=-1, keepdims=True))).astype(BF16)
            qh = q_scr[:, kc]
            kh = k_ref[:, kc]
            vh = v_ref[:, vc].astype(BF16)
            bh = b_scr[:, kc]
            blast = b_scr[pl.ds(c - 1, 1), kc]
            eb = jnp.exp2(bh)
            ek = jnp.exp2(blast - bh)
            dst = dstate[h]
            dst_b = dst.astype(BF16)
            dg_carry = jnp.sum(dst * stn_ref[h], axis=0, keepdims=True)
            da = lax.dot_general(do, vh, _NT, preferred_element_type=F32)
            da_scr[...] = jnp.where(tri, da, 0.0)
            dv = lax.dot_general(a_ref[h].astype(BF16), do, _TN, preferred_element_type=F32)
            dv = dv + lax.dot_general((kh * ek).astype(BF16), dst_b, _NT, preferred_element_type=F32)
            dv_ref[:, vc] = dv.astype(BF16)
            dq_scr[:, kc] = jnp.dot(do, stp_ref[h].astype(BF16), preferred_element_type=F32) * eb
            dk_scr[:, kc] = jnp.dot(vh, dst_b, preferred_element_type=F32) * ek
            _gla_pair_bwd(q_scr, k_ref, b_scr, da_scr, dq_scr, dk_scr, h)
            dq = dq_scr[:, kc]
            dk = dk_scr[:, kc]
            dg_scr[:, kc] = _suffix_sum_rows(qh * dq - kh * dk) + dg_carry
            dstate[h] = dst * jnp.exp2(blast) + lax.dot_general(do, (qh * eb).astype(BF16), _TN, preferred_element_type=F32)
        dq_ref[...] = (dq_scr[...] * GLA_Q_SCALE).astype(BF16)
        dk_ref[...] = dk_scr[...].astype(BF16)
        dpre = dg_scr[...] * ((1.0 - _sigmoid(pre)) * (1.0 / GLA_GATE_NORMALIZER))
        dpre_b = dpre.astype(BF16)
        dbias_ref[...] += jnp.sum(dpre, axis=0, keepdims=True)
        dwgu_ref[...] += lax.dot_general(gl_ref[...].astype(BF16), dpre_b, _TN, preferred_element_type=F32)
        dgl_ref[...] = lax.dot_general(dpre_b, wgu_ref[...], _NT, preferred_element_type=F32).astype(BF16)

    def chunk(b, i):
        return b * nc + (nc - 1 - i)

    def tok(width, col):
        return pl.BlockSpec((c, width), lambda b, i: (chunk(b, i), col))

    whole = lambda shape: pl.BlockSpec(shape, lambda b, i: (0,) * len(shape))
    st_shape = (None, GLA_HEADS, GLA_HEAD_V, GLA_HEAD_K)
    call = dict(
        name=name,
        grid=(nb, nc),
        in_specs=[tok(GLA_DK, 0), tok(GLA_DK, 1), tok(GLA_DV, 1), tok(GLA_DV, 2), tok(GLOW_PAD, GLA_MAIN // GLOW_PAD),
                  whole((GLOW_PAD, GLA_DK)), whole((1, GLA_DK)), whole((1, GLA_HEAD_V)),
                  tok(GLA_DV, 0),
                  pl.BlockSpec((GLA_HEADS, c, c), lambda b, i: (0, chunk(b, i), 0)),
                  pl.BlockSpec(st_shape, lambda b, i: (chunk(b, i), 0, 0, 0)),
                  pl.BlockSpec(st_shape, lambda b, i: (b * nc + jnp.minimum(nc - i, nc - 1), 0, 0, 0)),
                  tok(GLA_DV, 0)],
        out_specs=[tok(GLA_DK, 0), tok(GLA_DK, 0), tok(GLA_DV, 0), tok(GLA_DV, 0), tok(GLOW_PAD, 0),
                   whole((GLOW_PAD, GLA_DK)), whole((1, GLA_DK)), whole((1, GLA_HEAD_V))],
        out_shape=[jax.ShapeDtypeStruct((t, GLA_DK), BF16), jax.ShapeDtypeStruct((t, GLA_DK), BF16),
                   jax.ShapeDtypeStruct((t, GLA_DV), BF16), jax.ShapeDtypeStruct((t, GLA_DV), BF16),
                   jax.ShapeDtypeStruct((t, GLOW_PAD), BF16),
                   jax.ShapeDtypeStruct((GLOW_PAD, GLA_DK), F32), jax.ShapeDtypeStruct((1, GLA_DK), F32),
                   jax.ShapeDtypeStruct((1, GLA_HEAD_V), F32)],
        scratch_shapes=[pltpu.VMEM((GLA_HEADS, GLA_HEAD_V, GLA_HEAD_K), F32), pltpu.VMEM((c, GLA_DK), F32),
                        pltpu.VMEM((c, c), F32), pltpu.VMEM((c, GLA_DK), F32), pltpu.VMEM((c, GLA_DK), F32),
                        pltpu.VMEM((c, GLA_DK), F32), pltpu.VMEM((c, GLA_DK), F32)])
    args = (proj, proj, proj, proj, proj, wgu, bias, ng, o, a, states, states, dy)
    if comm is None:
        return pl.pallas_call(body, compiler_params=_cparams(("arbitrary", "arbitrary")), **call)(*args)
    return _call_with_comm(body, comm, n_in=13, n_out=8, args=args, **call)


DIL_STEPS = DIL_BLOCK
DIL_SCALE = DIL_HEAD_DIM ** -0.5
DIL_HEADS_PER_STEP = {1: 8, 4: 1, 16: 1}


def _dil_mask(i, nq):
    if nq == 1:
        return (lax.broadcasted_iota(jnp.int32, (DIL_BLOCK, DIL_BLOCK), 0)
                >= lax.broadcasted_iota(jnp.int32, (DIL_BLOCK, DIL_BLOCK), 1))
    rowi = lax.broadcasted_iota(jnp.int32, (DIL_BLOCK, 2 * DIL_BLOCK), 0)
    colj = lax.broadcasted_iota(jnp.int32, (DIL_BLOCK, 2 * DIL_BLOCK), 1)
    dist = rowi + DIL_BLOCK - colj
    band = jnp.logical_and(dist >= 0, dist <= DIL_STEPS)
    return jnp.logical_and(band, jnp.logical_or(i > 0, colj >= DIL_BLOCK))


def _dil_inputs(refs, nq):
    if nq == 1:
        q_ref, kc_ref, vc_ref = refs
        return q_ref, kc_ref, None, vc_ref, None
    return refs


def _dil_keys(prev_ref, cur_ref, rr, hc):
    if prev_ref is None:
        return cur_ref[rr, hc].astype(BF16)
    return jnp.concatenate([prev_ref[rr, hc], cur_ref[rr, hc]], axis=0).astype(BF16)


def _dil_geometry(t, gi):
    _, d = DIL_PATTERNS[gi]
    return d, SEQ // d // DIL_BLOCK, t // SEQ, DIL_BLOCK * d, DIL_HEADS_PER_STEP[d]


def _dil_specs(gi, d, nq, rows, hps, order):
    width = hps * DIL_HEAD_DIM
    per_part = DIL_WIDTH // width

    def named(f):
        return lambda *idx: f(**dict(zip(order, idx)))

    def block(i, prev):
        ic = jnp.minimum(i, nq - 1)
        return jnp.maximum(ic - 1, 0) if prev else ic

    def part(j, prev):
        return pl.BlockSpec((rows, width), named(lambda b, i, h: (b * nq + block(i, prev), (gi * 3 + j) * per_part + h)))

    cur = pl.BlockSpec((rows, width), named(lambda b, i, h: (b * nq + block(i, False), h)))
    done = pl.BlockSpec((rows, width), named(lambda b, i, h: (b * nq + jnp.maximum(i - 1, 0), h)))
    parts = [part(0, False), part(1, False), part(1, True), part(2, False), part(2, True)]
    return ([parts[0], parts[1], parts[3]] if nq == 1 else parts), cur, done


def _dil_rows(r, d):
    return pl.ds(r, DIL_BLOCK, stride=d) if d > 1 else pl.ds(0, DIL_BLOCK)


def _dil_fwd(proj, gi, *, name):
    t = proj.shape[0]
    d, nq, nb, rows, hps = _dil_geometry(t, gi)

    def body(*refs):
        (q_ref, kc_ref, kp_ref, vc_ref, vp_ref), (o_ref, lse_ref) = _dil_inputs(refs[:-2], nq), refs[-2:]
        mask = _dil_mask(pl.program_id(1), nq)
        for h in range(hps):
            hc = pl.ds(h * DIL_HEAD_DIM, DIL_HEAD_DIM)
            for r in range(d):
                rr = _dil_rows(r, d)
                qh = q_ref[rr, hc].astype(BF16)
                kcat = _dil_keys(kp_ref, kc_ref, rr, hc)
                vcat = _dil_keys(vp_ref, vc_ref, rr, hc)
                s = lax.dot_general(qh, kcat, _NT, preferred_element_type=F32) * DIL_SCALE
                s = jnp.where(mask, s, -jnp.inf)
                m = jnp.max(s, axis=-1, keepdims=True)
                p = jnp.exp(s - m)
                l = jnp.sum(p, axis=-1, keepdims=True)
                o_ref[rr, hc] = jnp.dot((p / l).astype(BF16), vcat, preferred_element_type=F32)
                lse_ref[rr, hc] = jnp.broadcast_to(m + jnp.log(l), (DIL_BLOCK, DIL_HEAD_DIM))

    parts, cur, _ = _dil_specs(gi, d, nq, rows, hps, "bih")
    return pl.pallas_call(
        body,
        name=name,
        grid=(nb, nq, DIL_HEADS // hps),
        in_specs=parts,
        out_specs=[cur, cur],
        out_shape=[jax.ShapeDtypeStruct((t, DIL_WIDTH), F32)] * 2,
        compiler_params=_cparams(("parallel", "parallel", "parallel")),
    )(*[proj] * len(parts))


def _dil_bwd(proj, gi, lse, do, delta, *, name):
    t = proj.shape[0]
    d, nq, nb, rows, hps = _dil_geometry(t, gi)

    def body(*refs):
        q_ref, kc_ref, kp_ref, vc_ref, vp_ref = _dil_inputs(refs[:-11], nq)
        lse_ref, do_ref, dl_ref, dq_ref, dk_ref, dv_ref, ck, cv, fk, fv, dq_s = refs[-11:]
        i = pl.program_id(2)

        @pl.when(i == 0)
        def _():
            ck[...] = jnp.zeros(ck.shape, F32)
            cv[...] = jnp.zeros(cv.shape, F32)

        @pl.when(i < nq)
        def _():
            mask = _dil_mask(i, nq)
            for h in range(hps):
                hc = pl.ds(h * DIL_HEAD_DIM, DIL_HEAD_DIM)
                h1 = pl.ds(h * DIL_HEAD_DIM, 1)
                for r in range(d):
                    rr = _dil_rows(r, d)
                    qh = q_ref[rr, hc].astype(BF16)
                    kcat = _dil_keys(kp_ref, kc_ref, rr, hc)
                    vcat = _dil_keys(vp_ref, vc_ref, rr, hc)
                    doh = do_ref[rr, hc].astype(BF16)
                    s = lax.dot_general(qh, kcat, _NT, preferred_element_type=F32) * DIL_SCALE
                    p = jnp.exp(jnp.where(mask, s, -jnp.inf) - lse_ref[rr, h1])
                    dp = lax.dot_general(doh, vcat, _NT, preferred_element_type=F32)
                    ds = (p * (dp + dl_ref[rr, h1]) * DIL_SCALE).astype(BF16)
                    dq_s[rr, hc] = jnp.dot(ds, kcat, preferred_element_type=F32)
                    dkcat = lax.dot_general(ds, qh, _TN, preferred_element_type=F32)
                    dvcat = lax.dot_general(p.astype(BF16), doh, _TN, preferred_element_type=F32)
                    if nq > 1:
                        fk[rr, hc] = ck[rr, hc] + dkcat[:DIL_BLOCK]
                        fv[rr, hc] = cv[rr, hc] + dvcat[:DIL_BLOCK]
                    ck[rr, hc] = dkcat[-DIL_BLOCK:]
                    cv[rr, hc] = dvcat[-DIL_BLOCK:]
            dq_ref[...] = dq_s[...].astype(BF16)

            @pl.when(i > 0)
            def _():
                dk_ref[...] = fk[...].astype(BF16)
                dv_ref[...] = fv[...].astype(BF16)

        @pl.when(i == nq)
        def _():
            dk_ref[...] = ck[...].astype(BF16)
            dv_ref[...] = cv[...].astype(BF16)

    parts, cur, done = _dil_specs(gi, d, nq, rows, hps, "bhi")
    shape = jax.ShapeDtypeStruct((t, DIL_WIDTH), BF16)
    tile = pltpu.VMEM((rows, hps * DIL_HEAD_DIM), F32)
    return pl.pallas_call(
        body,
        name=name,
        grid=(nb, DIL_HEADS // hps, nq + 1),
        in_specs=parts + [cur, cur, cur],
        out_specs=[cur, done, done],
        out_shape=[shape, shape, shape],
        scratch_shapes=[tile] * 5,
        compiler_params=_cparams(("parallel", "parallel", "arbitrary")),
    )(*[proj] * len(parts), lse, do, delta)


MIX_ROWS = 256


def _head_rowsum(x):
    parts = []
    for h in range(DIL_HEADS):
        s = jnp.sum(x[:, h * DIL_HEAD_DIM:(h + 1) * DIL_HEAD_DIM], axis=-1, keepdims=True)
        parts.append(jnp.broadcast_to(s, (x.shape[0], DIL_HEAD_DIM)))
    return jnp.concatenate(parts, axis=-1)


def _mix_weights(lse_refs):
    ls = [r[...] for r in lse_refs]
    m = jnp.maximum(jnp.maximum(ls[0], ls[1]), ls[2])
    es = [jnp.exp(l - m) for l in ls]
    inv = 1.0 / (es[0] + es[1] + es[2])
    return [e * inv for e in es]


def _dil_mix_fwd(os_, lses, *, name):
    t = os_[0].shape[0]

    def body(o0, o1, o2, l0, l1, l2, out_ref):
        w = _mix_weights((l0, l1, l2))
        out_ref[...] = (w[0] * o0[...] + w[1] * o1[...] + w[2] * o2[...]).astype(BF16)

    row = pl.BlockSpec((MIX_ROWS, DIL_WIDTH), lambda i: (i, 0))
    return pl.pallas_call(
        body, name=name, grid=(t // MIX_ROWS,), in_specs=[row] * 6, out_specs=row,
        out_shape=jax.ShapeDtypeStruct((t, DIL_WIDTH), BF16), compiler_params=_cparams(("parallel",)),
    )(*os_, *lses)


def _dil_mix_bwd(os_, lses, dout, *, name):
    t = os_[0].shape[0]

    def body(o0, o1, o2, l0, l1, l2, d_ref, do0, do1, do2, dl0, dl1, dl2):
        w = _mix_weights((l0, l1, l2))
        dv = d_ref[...]
        mix = w[0] * o0[...] + w[1] * o1[...] + w[2] * o2[...]
        bar = _head_rowsum(dv * mix)
        for wg, do_ref, dl_ref in zip(w, (do0, do1, do2), (dl0, dl1, dl2)):
            do_ref[...] = wg * dv
            dl_ref[...] = -wg * bar

    row = pl.BlockSpec((MIX_ROWS, DIL_WIDTH), lambda i: (i, 0))
    outs = pl.pallas_call(
        body, name=name, grid=(t // MIX_ROWS,), in_specs=[row] * 7, out_specs=[row] * 6,
        out_shape=[jax.ShapeDtypeStruct((t, DIL_WIDTH), F32)] * 6,
        compiler_params=_cparams(("parallel",)),
    )(*os_, *lses, dout)
    return outs[:3], outs[3:]


_MESH = pl.DeviceIdType.MESH
_ANY = pl.BlockSpec(memory_space=pl.ANY)


def _position():
    return lax.axis_index("x"), lax.axis_index("y"), lax.axis_index("c")


AG_COPIES = 7


def _run_comm(task, *, name):
    n_in, n_out = len(task.inputs), len(task.out_shapes)

    def body(*refs):
        parts = (refs[:n_in], refs[n_in:n_in + n_out], refs[n_in + n_out:])
        task.start(*parts)
        task.middle(*parts)
        task.finish(*parts)

    return pl.pallas_call(
        body, name=name, out_shape=task.out_shapes, in_specs=[_ANY] * n_in, out_specs=[_ANY] * n_out,
        scratch_shapes=task.sem_shapes,
    )(*task.inputs)


def _gather_task(shards):
    n = len(shards)

    def copies(x_refs, out_refs, sems):
        send_sems, recv_sems, local_sems = sems
        x, y, cc = _position()
        me, sibling = (x, y, cc), (x, y, 1 - cc)
        chips = [(1 - x, y), (x, 1 - y), (1 - x, 1 - y)]

        def copy(w, k, block, to, own=False):
            px, py, pc = block
            slot = out_refs[w].at[4 * px + 2 * py + pc]
            return pltpu.make_async_remote_copy(
                src_ref=x_refs[w] if own else slot, dst_ref=slot,
                send_sem=send_sems.at[AG_COPIES * w + k], recv_sem=recv_sems.at[AG_COPIES * w + k],
                device_id=to, device_id_type=_MESH)

        mine = [pltpu.make_async_copy(x_refs[w], out_refs[w].at[4 * x + 2 * y + cc], local_sems.at[w]) for w in range(n)]
        first = [[copy(w, 0, me, sibling, own=True)] + [copy(w, 1 + j, me, (*chip, cc), own=True) for j, chip in enumerate(chips)]
                 for w in range(n)]
        landed = [[copy(w, 1 + j, (*chip, cc), me) for j, chip in enumerate(chips)] for w in range(n)]
        passed = [[copy(w, 4 + j, (*chip, cc), sibling) for j, chip in enumerate(chips)] for w in range(n)]
        from_sibling = [[copy(w, 0, sibling, me)] + [copy(w, 4 + j, (*chip, 1 - cc), me) for j, chip in enumerate(chips)]
                        for w in range(n)]
        return mine, first, landed, passed, from_sibling

    def start(ins, outs, sems):
        mine, first, _, _, _ = copies(ins, outs, sems)
        for w in range(n):
            mine[w].start()
            for cp in first[w]:
                cp.start()

    def middle(ins, outs, sems):
        _, _, landed, passed, _ = copies(ins, outs, sems)
        for j in range(3):
            for w in range(n):
                landed[w][j].wait_recv()
                passed[w][j].start()

    def finish(ins, outs, sems):
        mine, first, _, passed, from_sibling = copies(ins, outs, sems)
        for w in range(n):
            for cp in from_sibling[w]:
                cp.wait_recv()
        for w in range(n):
            for cp in first[w] + passed[w]:
                cp.wait_send()
            mine[w].wait()

    return _Comm(shards, [jax.ShapeDtypeStruct((N_DEV,) + s.shape, s.dtype) for s in shards],
                 [pltpu.SemaphoreType.DMA((AG_COPIES * n,)), pltpu.SemaphoreType.DMA((AG_COPIES * n,)),
                  pltpu.SemaphoreType.DMA((n,))], start, middle, finish)


def _parity_half(ref, parity, half_rows):
    if half_rows is None:
        return ref.at[:, parity]
    return ref.at[:, pl.ds(parity * half_rows, half_rows), :]


def _exchange_task(make_copies, inputs, out_shapes, n_copies):
    def start(ins, outs, sems):
        for cp in make_copies(ins, outs, sems):
            cp.start()

    def finish(ins, outs, sems):
        for cp in make_copies(ins, outs, sems):
            cp.wait()

    return _Comm(inputs, out_shapes, [pltpu.SemaphoreType.DMA((n_copies,)), pltpu.SemaphoreType.DMA((n_copies,))],
                 start, lambda ins, outs, sems: None, finish)


def _sibling_task(gs, half_rows):
    n = len(gs)

    def make_copies(g_refs, out_refs, sems):
        x, y, cc = _position()
        return [pltpu.make_async_remote_copy(
            src_ref=_parity_half(g_refs[k], 1 - cc, half_rows[k]), dst_ref=out_refs[k],
            send_sem=sems[0].at[k], recv_sem=sems[1].at[k],
            device_id=(x, y, 1 - cc), device_id_type=_MESH) for k in range(n)]

    def out_shape(g, hr):
        return jax.ShapeDtypeStruct((4,) + (g.shape[2:] if hr is None else (hr, g.shape[2])), g.dtype)

    return _exchange_task(make_copies, gs, [out_shape(g, hr) for g, hr in zip(gs, half_rows)], n)


def _chips_task(ps):
    n = len(ps)

    def make_copies(p_refs, out_refs, sems):
        x, y, cc = _position()
        copies = []
        for w in range(n):
            for k in (1, 2, 3):
                px = 1 - x if k >> 1 else x
                py = 1 - y if k & 1 else y
                copies.append(pltpu.make_async_remote_copy(
                    src_ref=p_refs[w].at[2 * px + py], dst_ref=out_refs[w].at[k - 1],
                    send_sem=sems[0].at[3 * w + k - 1], recv_sem=sems[1].at[3 * w + k - 1],
                    device_id=(px, py, cc), device_id_type=_MESH))
        return copies

    return _exchange_task(make_copies, ps, [jax.ShapeDtypeStruct((3,) + p.shape[1:], p.dtype) for p in ps], 3 * n)


def _add_sibling(g, r1, place, half_rows, *, tr, tc, name):
    _, r, c = r1.shape
    if half_rows is None:
        g_spec = pl.BlockSpec((None, None, tr, tc), lambda i, j, k, pc: (k, pc[0], i, j))
    else:
        per_half = half_rows // tr
        g_spec = pl.BlockSpec((None, tr, tc), lambda i, j, k, pc: (k, pc[0] * per_half + i, j))

    def body(pc_ref, g_ref, r_ref, pb_ref, own_ref):
        s = g_ref[...] + r_ref[...]
        pb_ref[...] = s.astype(BF16)

        @pl.when(pl.program_id(2) == pc_ref[1])
        def _():
            own_ref[...] = s

    grid_spec = pltpu.PrefetchScalarGridSpec(
        num_scalar_prefetch=1,
        grid=(r // tr, c // tc, 4),
        in_specs=[g_spec, pl.BlockSpec((None, tr, tc), lambda i, j, k, pc: (k, i, j))],
        out_specs=[pl.BlockSpec((None, tr, tc), lambda i, j, k, pc: (k, i, j)),
                   pl.BlockSpec((tr, tc), lambda i, j, k, pc: (i, j))],
    )
    return pl.pallas_call(
        body, name=name, grid_spec=grid_spec,
        out_shape=[jax.ShapeDtypeStruct((4, r, c), BF16), jax.ShapeDtypeStruct((r, c), F32)],
        compiler_params=_cparams(("parallel", "parallel", "arbitrary")),
    )(place, g, r1)


def _adamw_math(g, w, m, v):
    m = ADAM_B1 * m + (1.0 - ADAM_B1) * g
    v = ADAM_B2 * v + (1.0 - ADAM_B2) * (g * g)
    m_hat = m / (1.0 - ADAM_B1 ** ADAM_STEP)
    v_hat = v / (1.0 - ADAM_B2 ** ADAM_STEP)
    delta = -ADAM_LR * (m_hat / (jnp.sqrt(v_hat) + ADAM_EPS) + ADAM_WD * w)
    return delta, m, v


def _adamw_big(own, r2, w, m, v, prev, layer, *, tr, tc, name):
    _, r, c = w.shape

    def body(p_ref, r2_ref, w_ref, m_ref, v_ref, a0, a1, a2, a3, g_ref, d_ref, mo_ref, vo_ref):
        g = ((p_ref[...] + r2_ref[0].astype(F32)) + r2_ref[1].astype(F32)) + r2_ref[2].astype(F32)
        delta, mn, vn = _adamw_math(g, w_ref[...], m_ref[...], v_ref[...])
        g_ref[...] = g
        d_ref[...] = delta
        mo_ref[...] = mn
        vo_ref[...] = vn

    lay = pl.BlockSpec((None, tr, tc), lambda i, j: (layer, i, j))
    return pl.pallas_call(
        body, name=name, grid=(r // tr, c // tc),
        in_specs=[pl.BlockSpec((tr, tc), lambda i, j: (i, j)), pl.BlockSpec((3, tr, tc), lambda i, j: (0, i, j)),
                  lay, lay, lay, _ANY, _ANY, _ANY, _ANY],
        out_specs=[lay, lay, lay, lay],
        out_shape=[jax.ShapeDtypeStruct(w.shape, F32)] * 4,
        input_output_aliases={5: 0, 6: 1, 7: 2, 8: 3},
        compiler_params=_cparams(("parallel", "parallel")),
    )(own, r2, w, m, v, *prev)


SMALL_COLS = 1024


def _sum_gathered(parts, *, name):
    _, r, c = parts.shape

    def body(p_ref, o_ref):
        acc = p_ref[0]
        for k in range(1, N_DEV):
            acc = acc + p_ref[k]
        o_ref[...] = acc

    return pl.pallas_call(
        body, name=name, grid=(1,), in_specs=[pl.BlockSpec((N_DEV, r, c), lambda i: (0, 0, 0))],
        out_specs=pl.BlockSpec((r, c), lambda i: (0, 0)), out_shape=jax.ShapeDtypeStruct((r, c), F32),
        compiler_params=_cparams(("arbitrary",)),
    )(parts)


def _adamw_small(g, w, m, v, *, name):
    r, c = g.shape

    def body(g_ref, w_ref, m_ref, v_ref, d_ref, mo_ref, vo_ref):
        delta, mn, vn = _adamw_math(g_ref[...], w_ref[...], m_ref[...], v_ref[...])
        d_ref[...] = delta
        mo_ref[...] = mn
        vo_ref[...] = vn

    spec = pl.BlockSpec((r, c), lambda i: (0, 0))
    return pl.pallas_call(
        body, name=name, grid=(1,), in_specs=[spec] * 4, out_specs=[spec] * 3,
        out_shape=[jax.ShapeDtypeStruct((r, c), F32)] * 3, compiler_params=_cparams(("arbitrary",)),
    )(g, w, m, v)


WEIGHT_NAMES = ("gla_w_in", "gla_w_gate_up", "gla_gate_bias", "gla_norm_g", "gla_w_out", "dil_w_in", "dil_w_out",
                "ffn_w_up", "ffn_conv_w", "ffn_conv_b", "ffn_w_down", "ln_g", "ln_b")
ADAM_TILES = {"gla_w_in": (256, 770), "gla_w_out": (128, 2048), "dil_w_in": (256, 1152), "dil_w_out": (512, 256),
              "ffn_w_up": (344, 1024), "ffn_w_down": (344, 1024)}
ADD_TILES = {**ADAM_TILES, "ffn_w_up": (352, 1024)}
FF_DOWN_SHARD = D_FF // N_DEV
VEC_COLS = 128


def _pad_axis(a, axis, to):
    pads = [(0, 0)] * a.ndim
    pads[axis] = (0, to - a.shape[axis])
    return jnp.pad(a, pads)


def _ff_cols(blocks):
    r = blocks.shape[1]
    return _pad_axis(blocks, 2, FF_SHARD_PAD).reshape(2, 4, r, FF_SHARD_PAD).transpose(0, 2, 1, 3).reshape(2, r, FF_HALF_PAD)


def _ff_cols_back(a):
    r = a.shape[1]
    return a.reshape(2, r, 4, FF_SHARD_PAD)[..., :FF_SHARD].transpose(1, 0, 2, 3).reshape(r, 2 * D_FF)


def _vec_parts(l, w):
    return [w["ffn_conv_w"][l], w["ln_g"][l], w["ln_b"][l]] + ([w["gla_w_gate_up"][l // 2]] if l % 2 == 0 else [])


def _layer_shards(l, w):
    j = l // 2
    gla = l % 2 == 0
    parts = _vec_parts(l, w)
    vec_rows = -(-sum(math.prod(a.shape) for a in parts) // (VEC_COLS * SUBLANES)) * SUBLANES
    mixer = [(w["gla_w_in"] if gla else w["dil_w_in"])[j].astype(BF16),
             (w["gla_w_out"] if gla else w["dil_w_out"])[j].astype(BF16), _pack_rows(parts, vec_rows, VEC_COLS)]
    w_up_t = _pad_axis(jnp.swapaxes(w["ffn_w_up"][l], 0, 1).astype(BF16), 0, FF_SHARD_PAD)
    return mixer, w_up_t, w["ffn_w_down"][l].astype(BF16)


def _w_down_layout(g_down):
    return _pad_axis(g_down.reshape(4, FF_SHARD, D_MODEL), 1, FF_SHARD_PAD).reshape(FF_HALF_PAD, D_MODEL)


def _layer_weights(l, w, g_in, g_out, g_vec):
    j = l // 2
    gla = l % 2 == 0
    vec_shapes = [a.shape for a in _vec_parts(l, w)]
    out = {}
    vec = [jnp.stack(p) for p in zip(*[_unpack_rows(g_vec[d], vec_shapes) for d in range(N_DEV)])]
    out["conv_w"] = _ff_cols(vec[0])
    out["conv_b"] = _pad_axis(w["ffn_conv_b"][l].reshape(N_DEV, FF_SHARD), 1, FF_SHARD_PAD).reshape(2, 1, FF_HALF_PAD)
    out["ln_g"] = vec[1].transpose(1, 0, 2).reshape(2, 1, D_MODEL)
    out["ln_b"] = vec[2].transpose(1, 0, 2).reshape(2, 1, D_MODEL)
    if gla:
        win = g_in.transpose(1, 0, 2).reshape(D_MODEL, GLA_IN)
        out["w_in"] = _pad_axis(win, 1, GLA_MAIN + GLOW_PAD)
        wgu = vec[3].transpose(1, 0, 2).reshape(GLA_GATE_RANK, GLA_DK).astype(BF16)
        out["w_gate_up"] = _pad_axis(wgu, 0, GLOW_PAD)
        out["w_out"] = g_out.reshape(GLA_DV, D_MODEL)
        out["gate_bias"] = w["gla_gate_bias"][j].reshape(1, GLA_DK)
        out["norm_g"] = w["gla_norm_g"][j].reshape(1, GLA_HEAD_V)
    else:
        out["w_in"] = g_in
        out["w_out"] = g_out.transpose(1, 0, 2).reshape(DIL_WIDTH, D_MODEL)
    return out


def _by_chip_parity(blocks):
    return blocks.reshape((4, 2) + blocks.shape[1:])


def _col_blocks(dw, width):
    r = dw.shape[0]
    return dw.reshape(r, N_DEV, width).transpose(1, 0, 2)


def _carry(call, task):
    if task is None:
        return call(None), []
    return call(task)


def _sibling_sum(l, n, g, half_rows, from_sibling, place):
    return _add_sibling(g, from_sibling, place, half_rows, tr=ADD_TILES[n][0], tc=ADD_TILES[n][1], name=f"l{l}_{n}_add")


def _ffn_fwd(l, yb, lw, task_up, task_down):
    h, got_up = _carry(lambda c: _ffn_hidden(yb, lw["w_up_t"], comm=c, name=f"l{l}_ffn_hidden"), task_up)
    act = _convgate_fwd(h, lw["conv_w"], lw["conv_b"], name=f"l{l}_convgate")
    ffn, got_down = _carry(lambda c: _matmul(act, lw["w_down"], tm=1024, tn=1024, tk=2816, comm=c, name=f"l{l}_ffn_down"), task_down)
    return ffn, (h, act), got_up, got_down


def _ffn_bwd(l, yb, dz, dzb, lw, saved, place, pending):
    h, act = saved
    t = yb.shape[0]
    task = _sibling_task(list(pending[1].values()), [None] * len(pending[1])) if pending else None
    dact, from_sibling = _carry(lambda c: _matmul(dzb, lw["w_down"], tb=True, tm=512, tn=2816, tk=D_MODEL, comm=c,
                                                  name=f"l{l}_ffn_dact"), task)
    pend_sums = {n: _sibling_sum(pending[0], n, g, None, r1, place)
                 for (n, g), r1 in zip(pending[1].items(), from_sibling)} if pending else {}
    d_down = _ffn_down_dw(act, dzb, name=f"l{l}_ffn_dwdown")
    dh, dcw, dcb = _convgate_bwd(h, dact, lw["conv_w"], lw["conv_b"], name=f"l{l}_convgate_bwd")
    dh = dh.reshape(2 * t, FF_HALF_PAD)
    tasks = [_sibling_task([d_down], [FF_DOWN_SHARD])] + ([_chips_task([s[0] for s in pend_sums.values()])] if pending else [])
    d_up_t, got = _ffn_hidden_dw(dh, yb, comm=_join_comm(tasks), name=f"l{l}_ffn_dwup")
    d_up = _by_chip_parity(d_up_t)
    dy, got_up = _ffn_hidden_dy(dh, lw["w_up_t"], dz, DEEPNORM_ALPHA, comm=_sibling_task([d_up], [None]), name=f"l{l}_ffn_dy")
    sums = {"ffn_w_down": _sibling_sum(l, "ffn_w_down", d_down, FF_DOWN_SHARD, got[0], place),
            "ffn_w_up": _sibling_sum(l, "ffn_w_up", d_up, None, got_up[0], place)}
    small = {"ffn_conv_w": _ff_cols_back(dcw), "ffn_conv_b": _ff_cols_back(dcb)[0]}
    return dy, sums, small, pend_sums, got[1:]


def _gla_layer_fwd(l, hb, lw, tasks):
    proj, got_proj = _carry(lambda c: _matmul(hb, lw["w_in"], tm=1024, tn=896, tk=D_MODEL, comm=c, name=f"l{l}_gla_proj"),
                            tasks["proj"])
    (y, o, a, st), got_gla = _carry(lambda c: _gla_fwd(proj, lw["w_gate_up"], lw["gate_bias"], lw["norm_g"], comm=c,
                                                       name=f"l{l}_gla"), tasks["gla"])
    mix, got_out = _carry(lambda c: _matmul(y, lw["w_out"], tm=1024, tn=1024, tk=GLA_DV, comm=c, name=f"l{l}_gla_out"),
                          tasks["out"])
    return mix, (proj, y, o, a, st), {"proj": got_proj, "gla": got_gla, "out": got_out}


def _gla_layer_bwd(l, hb, dz, dzb, lw, saved, ffn_sums):
    proj, y, o, a, st = saved
    dy = _matmul(dzb, lw["w_out"], tb=True, tm=1024, tn=1024, tk=D_MODEL, name=f"l{l}_gla_dy")
    d_out = _matmul(y, dzb, ta=True, tm=1024, tn=1024, tk=2048, name=f"l{l}_gla_dwout")
    names = list(ffn_sums)
    (dq, dk, dv, dr, dgl, dwgu, dbias, dng), got = _gla_bwd(
        proj, lw["w_gate_up"], lw["gate_bias"], lw["norm_g"], o, a, st, dy,
        comm=_chips_task([ffn_sums[n][0] for n in names]), name=f"l{l}_gla_bwd")
    dproj = jnp.concatenate([dq, dk, dv, dr, dgl], axis=-1)
    d_in = _matmul(hb, dproj, ta=True, tm=1024, tn=896, tk=2048, name=f"l{l}_gla_dwin")
    dx = _matmul(dproj, lw["w_in"], tb=True, tm=1024, tn=1024, tk=896, res=dz, res_scale=DEEPNORM_ALPHA, name=f"l{l}_gla_dx")
    big = {"gla_w_in": _by_chip_parity(_col_blocks(d_in[:, :GLA_IN], GLA_IN // N_DEV)),
           "gla_w_out": _by_chip_parity(d_out.reshape(N_DEV, GLA_DV // N_DEV, D_MODEL))}
    small = {"gla_w_gate_up": dwgu[:GLA_GATE_RANK], "gla_gate_bias": dbias[0], "gla_norm_g": dng[0]}
    return dx, big, small, dict(zip(names, got))


def _dil_layer_fwd(l, hb, lw, tasks):
    proj, got_proj = _carry(lambda c: _mm_colblocks(hb, lw["w_in"], comm=c, name=f"l{l}_dil_proj"), tasks["proj"])
    os_, lses = [], []
    for gi in range(len(DIL_PATTERNS)):
        o, lse = _dil_fwd(proj, gi, name=f"l{l}_dil_attn{gi}")
        os_.append(o)
        lses.append(lse)
    omix = _dil_mix_fwd(os_, lses, name=f"l{l}_dil_mix")
    mix, got_out = _carry(lambda c: _matmul(omix, lw["w_out"], tm=1024, tn=1024, tk=DIL_WIDTH, comm=c, name=f"l{l}_dil_out"),
                          tasks["out"])
    return mix, (proj, os_, lses, omix), {"proj": got_proj, "out": got_out}


def _dil_layer_bwd(l, hb, dz, dzb, lw, saved, ffn_sums):
    proj, os_, lses, omix = saved
    dout = _matmul(dzb, lw["w_out"], tb=True, tm=1024, tn=1024, tk=D_MODEL, name=f"l{l}_dil_dy")
    d_out = _matmul(omix, dzb, ta=True, tm=1024, tn=1024, tk=2048, name=f"l{l}_dil_dwout")
    dos, dls = _dil_mix_bwd(os_, lses, dout, name=f"l{l}_dil_mix_bwd")
    parts = []
    for gi in range(len(DIL_PATTERNS)):
        parts += list(_dil_bwd(proj, gi, lses[gi], dos[gi], dls[gi], name=f"l{l}_dil_attn_bwd{gi}"))
    dproj = jnp.concatenate(parts, axis=-1)
    d_in, got_down = _mm_grad_colblocks(hb, dproj, DIL_IN // N_DEV, comm=_chips_task([ffn_sums["ffn_w_down"][0]]),
                                        name=f"l{l}_dil_dwin")
    dx, got_up = _mm_colblocks_t(dproj, lw["w_in"], dz, DEEPNORM_ALPHA, comm=_chips_task([ffn_sums["ffn_w_up"][0]]),
                                 name=f"l{l}_dil_dx")
    big = {"dil_w_in": _by_chip_parity(d_in), "dil_w_out": _by_chip_parity(_col_blocks(d_out, D_MODEL // N_DEV))}
    return dx, big, {}, {"ffn_w_down": got_down[0], "ffn_w_up": got_up[0]}


def _pack_rows(arrays, rows, cols=SMALL_COLS):
    flat = [a.reshape(-1) for a in arrays]
    used = sum(f.shape[0] for f in flat)
    return jnp.concatenate(flat + [jnp.zeros((rows * cols - used,), F32)]).reshape(rows, cols)


def _unpack_rows(packed, shapes):
    flat, out, off = packed.reshape(-1), [], 0
    for s in shapes:
        n = math.prod(s)
        out.append(flat[off:off + n].reshape(s))
        off += n
    return out


def _rows_for(shapes):
    n = sum(math.prod(s) for s in shapes)
    return -(-n // (SMALL_COLS * SUBLANES)) * SUBLANES


def kernel(x, gla_w_in, gla_w_gate_up, gla_gate_bias, gla_norm_g, gla_w_out, dil_w_in, dil_w_out, ffn_w_up, ffn_conv_w, ffn_conv_b, ffn_w_down, ln_g, ln_b, loss_target, m_gla_w_in, m_gla_w_gate_up, m_gla_gate_bias, m_gla_norm_g, m_gla_w_out, m_dil_w_in, m_dil_w_out, m_ffn_w_up, m_ffn_conv_w, m_ffn_conv_b, m_ffn_w_down, m_ln_g, m_ln_b, v_gla_w_in, v_gla_w_gate_up, v_gla_gate_bias, v_gla_norm_g, v_gla_w_out, v_dil_w_in, v_dil_w_out, v_ffn_w_up, v_ffn_conv_w, v_ffn_conv_b, v_ffn_w_down, v_ln_g, v_ln_b):
    w = dict(zip(WEIGHT_NAMES, (gla_w_in, gla_w_gate_up, gla_gate_bias, gla_norm_g, gla_w_out, dil_w_in, dil_w_out,
                                ffn_w_up, ffn_conv_w, ffn_conv_b, ffn_w_down, ln_g, ln_b)))
    mom = dict(zip(WEIGHT_NAMES, (m_gla_w_in, m_gla_w_gate_up, m_gla_gate_bias, m_gla_norm_g, m_gla_w_out, m_dil_w_in,
                                  m_dil_w_out, m_ffn_w_up, m_ffn_conv_w, m_ffn_conv_b, m_ffn_w_down, m_ln_g, m_ln_b)))
    var = dict(zip(WEIGHT_NAMES, (v_gla_w_in, v_gla_w_gate_up, v_gla_gate_bias, v_gla_norm_g, v_gla_w_out, v_dil_w_in,
                                  v_dil_w_out, v_ffn_w_up, v_ffn_conv_w, v_ffn_conv_b, v_ffn_w_down, v_ln_g, v_ln_b)))
    xi, yi, ci = _position()
    dev = 4 * xi + 2 * yi + ci
    place = jnp.stack([ci, 2 * xi + yi]).astype(jnp.int32)
    t = x.shape[0] * x.shape[1]
    h = x.reshape(t, D_MODEL)
    hb = h.astype(BF16)
    target = loss_target.reshape(t, D_MODEL)

    shards = [_layer_shards(l, w) for l in range(DEPTH)]
    arrived = {0: dict(zip(("in", "out", "vec"), _run_comm(_gather_task(shards[0][0]), name="gather_l0")))}
    lws, saved = [], []
    for l in range(DEPTH):
        gla = l % 2 == 0
        nxt = l + 1 if l + 1 < DEPTH else None
        nxt_dil = nxt is not None and nxt % 2 == 1
        here = arrived[l]
        lw = _layer_weights(l, w, here["in"], here["out"], here["vec"])
        tasks = {"proj": _gather_task([shards[l][2]]),
                 "gla": _gather_task([shards[l][1]]) if gla else None,
                 "out": _gather_task(shards[nxt][0][1:]) if nxt else None}
        mix, mixer_saved, got = (_gla_layer_fwd if gla else _dil_layer_fwd)(l, hb, lw, tasks)
        lw["w_down"] = _w_down_layout(got["proj"][0])
        lw["w_up_t"] = got["gla"][0] if gla else here["up"]
        y1, y1b, xh1, rs1 = _ln_fwd(h, mix, lw["ln_g"][0], lw["ln_b"][0], name=f"l{l}_ln1")
        task_hidden = _gather_task([shards[nxt][1] if nxt_dil else shards[nxt][0][0]]) if nxt else None
        task_down = _gather_task([shards[nxt][0][0]]) if nxt_dil else None
        ffn, ffn_saved, got_hidden, got_down = _ffn_fwd(l, y1b, lw, task_hidden, task_down)
        y2, y2b, xh2, rs2 = _ln_fwd(y1, ffn, lw["ln_g"][1], lw["ln_b"][1], name=f"l{l}_ln2")
        saved.append((hb, mixer_saved, y1b, xh1, rs1, ffn_saved, xh2, rs2))
        lws.append(lw)
        h, hb = y2, y2b
        if nxt:
            arrived[nxt] = {"out": got["out"][0], "vec": got["out"][1]}
            if nxt_dil:
                arrived[nxt].update({"up": got_hidden[0], "in": got_down[0]})
            else:
                arrived[nxt]["in"] = got_hidden[0]
    loss_local, dy = _loss_fwd_bwd(h, target, name="loss")
    loss = lax.psum(loss_local[0, 0], ("x", "y", "c"))

    big_names = tuple(ADAM_TILES)
    as_updated = lambda n, a: jnp.swapaxes(a, 1, 2) if n == "ffn_w_up" else a
    wt, mt, vt = ({n: as_updated(n, d[n]) for n in big_names} for d in (w, mom, var))
    results = {n: [lax.empty(wt[n].shape, F32) for _ in range(4)] for n in big_names}
    small_grads = {n: [None] * w[n].shape[0] for n in WEIGHT_NAMES if n not in big_names}

    def adamw(l, n, own, from_chips):
        results[n] = _adamw_big(own, from_chips, wt[n], mt[n], vt[n], results[n], l if n.startswith("ffn") else l // 2,
                                tr=ADAM_TILES[n][0], tc=ADAM_TILES[n][1], name=f"l{l}_{n}_adamw")

    pending = None
    for l in reversed(range(DEPTH)):
        lw = lws[l]
        hb_in, mixer_saved, y1b, xh1, rs1, ffn_saved, xh2, rs2 = saved[l]
        dz2, dz2b, dg2, db2 = _ln_bwd(dy, xh2, rs2, lw["ln_g"][1], name=f"l{l}_ln2_bwd")
        dy1, ffn_sums, small_ffn, pend_sums, got_pending = _ffn_bwd(l, y1b, dz2, dz2b, lw, ffn_saved, place, pending)
        for (n, (_, own)), r2 in zip(pend_sums.items(), got_pending):
            adamw(pending[0], n, own, r2)
        dz1, dz1b, dg1, db1 = _ln_bwd(dy1, xh1, rs1, lw["ln_g"][0], name=f"l{l}_ln1_bwd")
        dy, big_mix, small_mix, got_ffn = (_gla_layer_bwd if l % 2 == 0 else _dil_layer_bwd)(
            l, hb_in, dz1, dz1b, lw, mixer_saved, ffn_sums)
        for n, (_, own) in ffn_sums.items():
            adamw(l, n, own, got_ffn[n])
        pending = (l, big_mix)
        small_grads["ln_g"][l] = jnp.concatenate([dg1, dg2], axis=0)
        small_grads["ln_b"][l] = jnp.concatenate([db1, db2], axis=0)
        for n, g in small_ffn.items():
            small_grads[n][l] = g
        for n, g in small_mix.items():
            small_grads[n][l // 2] = g
    from_sibling = _run_comm(_sibling_task(list(pending[1].values()), [None] * len(pending[1])), name="reduce_sibling_l0")
    last_sums = {n: _sibling_sum(0, n, g, None, r1, place) for (n, g), r1 in zip(pending[1].items(), from_sibling)}
    from_chips = _run_comm(_chips_task([s[0] for s in last_sums.values()]), name="reduce_chips_l0")
    for (n, (_, own)), r2 in zip(last_sums.items(), from_chips):
        adamw(0, n, own, r2)
    results = {n: [as_updated(n, a) for a in results[n]] for n in big_names}
    grad_x = dy.reshape(x.shape)

    small_names = [n for n in WEIGHT_NAMES if n not in big_names]
    full_shapes = {"gla_w_gate_up": (2, GLA_GATE_RANK, GLA_DK), "gla_gate_bias": (2, GLA_DK), "gla_norm_g": (2, GLA_HEAD_V),
                   "ffn_conv_w": (DEPTH, 3, 2 * D_FF), "ffn_conv_b": (DEPTH, 2 * D_FF),
                   "ln_g": (DEPTH, 2, D_MODEL), "ln_b": (DEPTH, 2, D_MODEL)}
    shapes = [full_shapes[n] for n in small_names]
    rows = _rows_for(shapes)
    packed = _pack_rows([jnp.stack(small_grads[n]) for n in small_names], rows)
    summed = _sum_gathered(_run_comm(_gather_task([packed]), name="gather_small_grads")[0], name="sum_small_grads")
    full = dict(zip(small_names, _unpack_rows(summed, shapes)))
    own = {n: (full[n] if w[n].shape == full[n].shape
               else lax.dynamic_slice_in_dim(full[n], dev * w[n].shape[-1], w[n].shape[-1], axis=full[n].ndim - 1))
           for n in small_names}
    own_shapes = [w[n].shape for n in small_names]
    rows = _rows_for(own_shapes)
    pk = lambda d: _pack_rows([d[n] for n in small_names], rows)
    outs = _adamw_small(pk(own), pk(w), pk(mom), pk(var), name="adamw_small")
    for n in small_names:
        results[n] = [own[n]]
    for k, packed_out in enumerate(outs):
        for n, a in zip(small_names, _unpack_rows(packed_out, own_shapes)):
            results[n].append(a)

    return (loss, grad_x) + tuple(results[n][k] for k in range(4) for n in WEIGHT_NAMES)
```

```python
import functools
import math

import jax
import jax.numpy as jnp
from jax import lax
from jax.experimental import pallas as pl
from jax.experimental.pallas import tpu as pltpu

F32 = jnp.float32
BF16 = jnp.bfloat16

D_MODEL = 2048
SEQ = 2048
DEPTH = 4
N_DEV = 8
GLA_HEADS = 4
GLA_DK = 1024
GLA_DV = 2048
GLA_HEAD_K = 256
GLA_HEAD_V = 512
GLA_GATE_RANK = 16
GLA_GATE_NORMALIZER = 16.0
GLA_CHUNK = 64
GLA_MAIN = 2 * GLA_DK + 2 * GLA_DV
GLA_IN = GLA_MAIN + GLA_GATE_RANK
DIL_PATTERNS = ((128, 1), (512, 4), (2048, 16))
DIL_HEADS = 8
DIL_HEAD_DIM = 128
DIL_WIDTH = DIL_HEADS * DIL_HEAD_DIM
DIL_BLOCK = 128
DIL_IN = 3 * len(DIL_PATTERNS) * DIL_WIDTH
D_FF = 5504
DEEPNORM_ALPHA = (2 * DEPTH) ** 0.25
LN_EPS = 1e-5
RMS_EPS = 1e-6
ADAM_LR = 0.001
ADAM_B1 = 0.9
ADAM_B2 = 0.999
ADAM_EPS = 1e-08
ADAM_WD = 0.01
ADAM_STEP = 10

LANES = 128
SUBLANES = 8
VMEM_LIMIT_BYTES = 56 * 1024 * 1024

FF_SHARD = 2 * D_FF // N_DEV
FF_SHARD_PAD = 1408
FF_HALF_PAD = 4 * FF_SHARD_PAD
GLOW_PAD = LANES


def _cparams(dims=None):
    return pltpu.CompilerParams(dimension_semantics=dims, vmem_limit_bytes=VMEM_LIMIT_BYTES)


class _Comm:
    def __init__(self, inputs, out_shapes, sem_shapes, start, middle, finish):
        self.inputs, self.out_shapes, self.sem_shapes = list(inputs), list(out_shapes), list(sem_shapes)
        self.start, self.middle, self.finish = start, middle, finish


def _join_comm(comms):
    def cut(refs, counts):
        out, off = [], 0
        for c in counts:
            out.append(refs[off:off + c])
            off += c
        return out

    n_in = [len(c.inputs) for c in comms]
    n_out = [len(c.out_shapes) for c in comms]
    n_sem = [len(c.sem_shapes) for c in comms]

    def hook(which):
        def run(ins, outs, sems):
            for c, i, o, s in zip(comms, cut(ins, n_in), cut(outs, n_out), cut(sems, n_sem)):
                getattr(c, which)(i, o, s)
        return run

    return _Comm([a for c in comms for a in c.inputs], [s for c in comms for s in c.out_shapes],
                 [s for c in comms for s in c.sem_shapes], hook("start"), hook("middle"), hook("finish"))


def _mm(a, b, *, grid, a_spec, b_spec, o_spec, out_shape, acc_shape, ta=False, tb=False, res=None, res_scale=1.0,
        comm=None, name):
    nk = grid[2]
    dims = (((0 if ta else 1,), (1 if tb else 0,)), ((), ()))
    has_res = res is not None
    n_in = 2 + has_res
    n_cin = len(comm.inputs) if comm else 0
    n_cout = len(comm.out_shapes) if comm else 0

    def body(*refs):
        a_ref, b_ref = refs[0], refs[1]
        res_ref = refs[2] if has_res else None
        o_ref = refs[n_in + n_cin]
        scratch = refs[n_in + n_cin + 1 + n_cout:]
        acc_ref = scratch[0] if nk > 1 else None
        if comm:
            task = (refs[n_in:n_in + n_cin], refs[n_in + n_cin + 1:n_in + n_cin + 1 + n_cout],
                    scratch[1:] if nk > 1 else scratch)
            step = (pl.program_id(0) * grid[1] + pl.program_id(1)) * nk + pl.program_id(2)
            steps = grid[0] * grid[1] * nk

            @pl.when(step == 0)
            def _():
                comm.start(*task)

        p = lax.dot_general(a_ref[...], b_ref[...], dims, preferred_element_type=F32)

        def finish(acc):
            if has_res:
                acc = acc + res_scale * res_ref[...]
            o_ref[...] = acc.astype(o_ref.dtype)

        if nk == 1:
            finish(p)
        else:
            kk = pl.program_id(2)

            @pl.when(kk == 0)
            def _():
                acc_ref[...] = p

            @pl.when(kk > 0)
            def _():
                acc_ref[...] += p

            @pl.when(kk == nk - 1)
            def _():
                finish(acc_ref[...])

        if comm:
            @pl.when(step == steps - 1)
            def _():
                comm.middle(*task)
                comm.finish(*task)

    in_specs = [a_spec, b_spec] + ([o_spec] if has_res else [])
    args = (a, b) + ((res,) if has_res else ())
    acc = [pltpu.VMEM(acc_shape, F32)] if nk > 1 else []
    if not comm:
        return pl.pallas_call(
            body, name=name, grid=grid, in_specs=in_specs, out_specs=o_spec, out_shape=out_shape, scratch_shapes=acc,
            compiler_params=_cparams(("parallel", "parallel", "arbitrary")),
        )(*args)
    outs = pl.pallas_call(
        body, name=name, grid=grid, in_specs=in_specs + [_ANY] * n_cin, out_specs=[o_spec] + [_ANY] * n_cout,
        out_shape=[out_shape] + comm.out_shapes, scratch_shapes=acc + comm.sem_shapes,
        compiler_params=_cparams(("arbitrary", "arbitrary", "arbitrary")),
    )(*args, *comm.inputs)
    return outs[0], list(outs[1:])


def _matmul(a, b, *, ta=False, tb=False, tm, tn, tk, out_dtype=F32, res=None, res_scale=1.0, comm=None, name):
    m, k = (a.shape[1], a.shape[0]) if ta else a.shape
    n = b.shape[0] if tb else b.shape[1]
    assert (b.shape[1] if tb else b.shape[0]) == k
    assert m % tm == 0 and n % tn == 0 and k % tk == 0, (m, n, k, tm, tn, tk)
    a_spec = pl.BlockSpec((tk, tm), lambda i, j, kk: (kk, i)) if ta else pl.BlockSpec((tm, tk), lambda i, j, kk: (i, kk))
    b_spec = pl.BlockSpec((tn, tk), lambda i, j, kk: (j, kk)) if tb else pl.BlockSpec((tk, tn), lambda i, j, kk: (kk, j))
    return _mm(a, b, grid=(m // tm, n // tn, k // tk), a_spec=a_spec, b_spec=b_spec,
               o_spec=pl.BlockSpec((tm, tn), lambda i, j, kk: (i, j)), out_shape=jax.ShapeDtypeStruct((m, n), out_dtype),
               acc_shape=(tm, tn), ta=ta, tb=tb, res=res, res_scale=res_scale, comm=comm, name=name)


MM_ROWS = 1024


def _mm_colblocks(a, wb, *, comm=None, name):
    m, k = a.shape
    nb, _, w = wb.shape
    return _mm(a, wb, grid=(m // MM_ROWS, nb, 1),
               a_spec=pl.BlockSpec((MM_ROWS, k), lambda i, j, kk: (i, 0)),
               b_spec=pl.BlockSpec((None, k, w), lambda i, j, kk: (j, 0, 0)),
               o_spec=pl.BlockSpec((MM_ROWS, w), lambda i, j, kk: (i, j)),
               out_shape=jax.ShapeDtypeStruct((m, nb * w), F32), acc_shape=(MM_ROWS, w), comm=comm, name=name)


def _mm_colblocks_t(a, wb, res, res_scale, *, comm=None, name):
    m = a.shape[0]
    nb, n, w = wb.shape
    tn = 1024
    return _mm(a, wb, grid=(m // MM_ROWS, n // tn, nb),
               a_spec=pl.BlockSpec((MM_ROWS, w), lambda i, j, kk: (i, kk)),
               b_spec=pl.BlockSpec((None, tn, w), lambda i, j, kk: (kk, j, 0)),
               o_spec=pl.BlockSpec((MM_ROWS, tn), lambda i, j, kk: (i, j)),
               out_shape=jax.ShapeDtypeStruct((m, n), F32), acc_shape=(MM_ROWS, tn), tb=True,
               res=res, res_scale=res_scale, comm=comm, name=name)


def _mm_grad_colblocks(x, dy, w, *, comm=None, name):
    t, k = x.shape
    nb = dy.shape[1] // w
    tm, tk = 1024, 2048
    return _mm(x, dy, grid=(k // tm, nb, t // tk),
               a_spec=pl.BlockSpec((tk, tm), lambda i, j, kk: (kk, i)),
               b_spec=pl.BlockSpec((tk, w), lambda i, j, kk: (kk, j)),
               o_spec=pl.BlockSpec((None, tm, w), lambda i, j, kk: (j, i, 0)),
               out_shape=jax.ShapeDtypeStruct((nb, k, w), F32), acc_shape=(tm, w), ta=True, comm=comm, name=name)


def _ffn_hidden(y, wt, *, comm=None, name):
    t, k = y.shape
    ni = t // MM_ROWS
    return _mm(y, wt, grid=(ni, N_DEV, 1),
               a_spec=pl.BlockSpec((MM_ROWS, k), lambda i, j, kk: (i, 0)),
               b_spec=pl.BlockSpec((None, FF_SHARD_PAD, k), lambda i, j, kk: (j, 0, 0)),
               o_spec=pl.BlockSpec((MM_ROWS, FF_SHARD_PAD), lambda i, j, kk: ((j // 4) * ni + i, j % 4)),
               out_shape=jax.ShapeDtypeStruct((2 * t, FF_HALF_PAD), F32), acc_shape=(MM_ROWS, FF_SHARD_PAD), tb=True,
               comm=comm, name=name)


def _ffn_hidden_dy(dh, wt, res, res_scale, *, comm=None, name):
    t = dh.shape[0] // 2
    ni, tn = t // MM_ROWS, 1024
    return _mm(dh, wt, grid=(ni, D_MODEL // tn, N_DEV),
               a_spec=pl.BlockSpec((MM_ROWS, FF_SHARD_PAD), lambda i, j, kk: ((kk // 4) * ni + i, kk % 4)),
               b_spec=pl.BlockSpec((None, FF_SHARD_PAD, tn), lambda i, j, kk: (kk, 0, j)),
               o_spec=pl.BlockSpec((MM_ROWS, tn), lambda i, j, kk: (i, j)),
               out_shape=jax.ShapeDtypeStruct((t, D_MODEL), F32), acc_shape=(MM_ROWS, tn),
               res=res, res_scale=res_scale, comm=comm, name=name)


def _ffn_hidden_dw(dh, y, *, comm=None, name):
    t, k = y.shape
    tk, tn = 2048, 1024
    nk = t // tk
    return _mm(dh, y, grid=(N_DEV, k // tn, nk),
               a_spec=pl.BlockSpec((tk, FF_SHARD_PAD), lambda i, j, kk: ((i // 4) * nk + kk, i % 4)),
               b_spec=pl.BlockSpec((tk, tn), lambda i, j, kk: (kk, j)),
               o_spec=pl.BlockSpec((None, FF_SHARD_PAD, tn), lambda i, j, kk: (i, 0, j)),
               out_shape=jax.ShapeDtypeStruct((N_DEV, FF_SHARD_PAD, k), F32), acc_shape=(FF_SHARD_PAD, tn), ta=True,
               comm=comm, name=name)


def _ffn_down_dw(act, dz, *, comm=None, name):
    t, k = dz.shape
    tk, tn = 2048, 1024
    return _mm(act, dz, grid=(4, k // tn, t // tk),
               a_spec=pl.BlockSpec((tk, FF_SHARD_PAD), lambda i, j, kk: (kk, i)),
               b_spec=pl.BlockSpec((tk, tn), lambda i, j, kk: (kk, j)),
               o_spec=pl.BlockSpec((None, FF_SHARD_PAD, tn), lambda i, j, kk: (i, 0, j)),
               out_shape=jax.ShapeDtypeStruct((4, FF_SHARD_PAD, k), F32), acc_shape=(FF_SHARD_PAD, tn), ta=True,
               comm=comm, name=name)


LN_ROWS = 256


def _ln_fwd(x, f, g, b, *, name):
    t, d = x.shape

    def body(x_ref, f_ref, g_ref, b_ref, y_ref, yb_ref, xh_ref, rs_ref):
        z = DEEPNORM_ALPHA * x_ref[...] + f_ref[...]
        mu = jnp.mean(z, axis=-1, keepdims=True)
        zc = z - mu
        var = jnp.mean(zc * zc, axis=-1, keepdims=True)
        rstd = lax.rsqrt(var + LN_EPS)
        xh = zc * rstd
        y = xh * g_ref[...] + b_ref[...]
        y_ref[...] = y
        yb_ref[...] = y.astype(BF16)
        xh_ref[...] = xh
        rs_ref[...] = rstd

    row = pl.BlockSpec((LN_ROWS, d), lambda i: (i, 0))
    vec = pl.BlockSpec((1, d), lambda i: (0, 0))
    return pl.pallas_call(
        body,
        name=name,
        grid=(t // LN_ROWS,),
        in_specs=[row, row, vec, vec],
        out_specs=[row, row, row, pl.BlockSpec((LN_ROWS, 1), lambda i: (i, 0))],
        out_shape=[jax.ShapeDtypeStruct((t, d), F32), jax.ShapeDtypeStruct((t, d), BF16),
                   jax.ShapeDtypeStruct((t, d), F32), jax.ShapeDtypeStruct((t, 1), F32)],
        compiler_params=_cparams(("parallel",)),
    )(x, f, g, b)


def _ln_bwd(dy, xhat, rstd, g, *, name):
    t, d = dy.shape

    def body(dy_ref, xh_ref, rs_ref, g_ref, dz_ref, dzb_ref, dg_ref, db_ref):
        dyv = dy_ref[...]
        xh = xh_ref[...]
        dyg = dyv * g_ref[...]
        m1 = jnp.mean(dyg, axis=-1, keepdims=True)
        m2 = jnp.mean(dyg * xh, axis=-1, keepdims=True)
        dz = rs_ref[...] * (dyg - m1 - xh * m2)
        dz_ref[...] = dz
        dzb_ref[...] = dz.astype(BF16)
        dg_part = jnp.sum(dyv * xh, axis=0, keepdims=True)
        db_part = jnp.sum(dyv, axis=0, keepdims=True)

        @pl.when(pl.program_id(0) == 0)
        def _():
            dg_ref[...] = dg_part
            db_ref[...] = db_part

        @pl.when(pl.program_id(0) > 0)
        def _():
            dg_ref[...] += dg_part
            db_ref[...] += db_part

    row = pl.BlockSpec((LN_ROWS, d), lambda i: (i, 0))
    vec = pl.BlockSpec((1, d), lambda i: (0, 0))
    return pl.pallas_call(
        body,
        name=name,
        grid=(t // LN_ROWS,),
        in_specs=[row, row, pl.BlockSpec((LN_ROWS, 1), lambda i: (i, 0)), vec],
        out_specs=[row, row, vec, vec],
        out_shape=[jax.ShapeDtypeStruct((t, d), F32), jax.ShapeDtypeStruct((t, d), BF16),
                   jax.ShapeDtypeStruct((1, d), F32), jax.ShapeDtypeStruct((1, d), F32)],
        compiler_params=_cparams(("arbitrary",)),
    )(dy, xhat, rstd, g)


def _loss_fwd_bwd(y, target, *, name):
    t, d = y.shape

    def body(y_ref, t_ref, loss_ref, dy_ref):
        err = y_ref[...] - t_ref[...]
        dy_ref[...] = err * (1.0 / d)
        part = 0.5 * jnp.sum(jnp.mean(err * err, axis=-1, keepdims=True), axis=0, keepdims=True)

        @pl.when(pl.program_id(0) == 0)
        def _():
            loss_ref[...] = part

        @pl.when(pl.program_id(0) > 0)
        def _():
            loss_ref[...] += part

    row = pl.BlockSpec((LN_ROWS, d), lambda i: (i, 0))
    return pl.pallas_call(
        body,
        name=name,
        grid=(t // LN_ROWS,),
        in_specs=[row, row],
        out_specs=[pl.BlockSpec((1, 1), lambda i: (0, 0)), row],
        out_shape=[jax.ShapeDtypeStruct((1, 1), F32), jax.ShapeDtypeStruct((t, d), F32)],
        compiler_params=_cparams(("arbitrary",)),
    )(y, target)


FFN_COLS = 256


def _shift_rows(h, s):
    rows = lax.broadcasted_iota(jnp.int32, h.shape, 0)
    return jnp.where(rows >= s, pltpu.roll(h, s, 0), 0.0)


def _shift_rows_up(h, s):
    n = h.shape[0]
    rows = lax.broadcasted_iota(jnp.int32, h.shape, 0)
    return jnp.where(rows < n - s, pltpu.roll(h, n - s, 0), 0.0)


def _causal_conv(h, w, b):
    return w[0:1, :] * _shift_rows(h, 2) + w[1:2, :] * _shift_rows(h, 1) + w[2:3, :] * h + b


def _sigmoid(x):
    return 1.0 / (1.0 + jnp.exp(-x))


def _convgate_fwd(h, cw, cb, *, name):
    t, n = h.shape[0] // 2, h.shape[1]
    nb = t // SEQ

    def body(hg_ref, hu_ref, wg_ref, wu_ref, bg_ref, bu_ref, a_ref):
        gate = _causal_conv(hg_ref[...], wg_ref[...], bg_ref[...])
        up = _causal_conv(hu_ref[...], wu_ref[...], bu_ref[...])
        a_ref[...] = (gate * _sigmoid(gate) * up).astype(BF16)

    def half(rows, k):
        return pl.BlockSpec((None, rows, FFN_COLS), lambda s, j: (k, 0, j))

    return pl.pallas_call(
        body,
        name=name,
        grid=(nb, n // FFN_COLS),
        in_specs=[pl.BlockSpec((SEQ, FFN_COLS), lambda s, j: (s, j)), pl.BlockSpec((SEQ, FFN_COLS), lambda s, j: (nb + s, j)),
                  half(3, 0), half(3, 1), half(1, 0), half(1, 1)],
        out_specs=pl.BlockSpec((SEQ, FFN_COLS), lambda s, j: (s, j)),
        out_shape=jax.ShapeDtypeStruct((t, n), BF16),
        compiler_params=_cparams(("parallel", "parallel")),
    )(h, h, cw, cw, cb, cb)


def _convgate_bwd(h, dact, cw, cb, *, name):
    t, n = h.shape[0] // 2, h.shape[1]
    nb = t // SEQ

    def body(hg_ref, hu_ref, da_ref, wg_ref, wu_ref, bg_ref, bu_ref, dh_ref, dw_ref, db_ref):
        dhg_ref, dhu_ref = dh_ref.at[0], dh_ref.at[1]
        dwg_ref, dwu_ref = dw_ref.at[0], dw_ref.at[1]
        dbg_ref, dbu_ref = db_ref.at[0], db_ref.at[1]
        hgv, huv = hg_ref[...], hu_ref[...]
        wgv, wuv = wg_ref[...], wu_ref[...]
        gate = _causal_conv(hgv, wgv, bg_ref[...])
        up = _causal_conv(huv, wuv, bu_ref[...])
        sg = _sigmoid(gate)
        da = da_ref[...]
        dgate = da * up * (sg * (1.0 + gate * (1.0 - sg)))
        dup = da * (gate * sg)

        def conv_bwd(dc, h, w, dh_ref, dw_ref, db_ref):
            dh = w[2:3, :] * dc + w[1:2, :] * _shift_rows_up(dc, 1) + w[0:1, :] * _shift_rows_up(dc, 2)
            dh_ref[...] = dh.astype(BF16)
            dws = [jnp.sum(dc * _shift_rows(h, 2), axis=0, keepdims=True),
                   jnp.sum(dc * _shift_rows(h, 1), axis=0, keepdims=True),
                   jnp.sum(dc * h, axis=0, keepdims=True)]
            db = jnp.sum(dc, axis=0, keepdims=True)

            @pl.when(pl.program_id(1) == 0)
            def _():
                for r in range(3):
                    dw_ref[r:r + 1, :] = dws[r]
                db_ref[...] = db

            @pl.when(pl.program_id(1) > 0)
            def _():
                for r in range(3):
                    dw_ref[r:r + 1, :] += dws[r]
                db_ref[...] += db

        conv_bwd(dgate, hgv, wgv, dhg_ref, dwg_ref, dbg_ref)
        conv_bwd(dup, huv, wuv, dhu_ref, dwu_ref, dbu_ref)

    def half(rows, k):
        return pl.BlockSpec((None, rows, FFN_COLS), lambda j, s: (k, 0, j))

    def both(rows):
        return pl.BlockSpec((2, rows, FFN_COLS), lambda j, s: (0, 0, j))

    return pl.pallas_call(
        body,
        name=name,
        grid=(n // FFN_COLS, nb),
        in_specs=[pl.BlockSpec((SEQ, FFN_COLS), lambda j, s: (s, j)), pl.BlockSpec((SEQ, FFN_COLS), lambda j, s: (nb + s, j)),
                  pl.BlockSpec((SEQ, FFN_COLS), lambda j, s: (s, j)), half(3, 0), half(3, 1), half(1, 0), half(1, 1)],
        out_specs=[pl.BlockSpec((2, SEQ, FFN_COLS), lambda j, s: (0, s, j)), both(3), both(1)],
        out_shape=[jax.ShapeDtypeStruct((2, t, n), BF16), jax.ShapeDtypeStruct((2, 3, n), F32),
                   jax.ShapeDtypeStruct((2, 1, n), F32)],
        compiler_params=_cparams(("parallel", "arbitrary")),
    )(h, h, dact, cw, cw, cb, cb)


GLA_Q_SCALE = GLA_HEAD_K ** -0.5
GLA_NC = SEQ // GLA_CHUNK
_NT = (((1,), (1,)), ((), ()))
_TN = (((0,), (0,)), ((), ()))


def _cumsum_rows(g):
    n = g.shape[0]
    rows = lax.broadcasted_iota(jnp.int32, g.shape, 0)
    s = 1
    while s < n:
        g = g + jnp.where(rows >= s, pltpu.roll(g, s, 0), 0.0)
        s *= 2
    return g


def _suffix_sum_rows(x):
    n = x.shape[0]
    rows = lax.broadcasted_iota(jnp.int32, x.shape, 0)
    s = 1
    while s < n:
        x = x + jnp.where(rows < n - s, pltpu.roll(x, n - s, 0), 0.0)
        s *= 2
    return x


def _gla_log_gate(gl_ref, wgu_ref, bias_ref):
    pre = jnp.dot(gl_ref[...].astype(BF16), wgu_ref[...], preferred_element_type=F32) + bias_ref[...]
    log_sig = jnp.minimum(pre, 0.0) - jnp.log(1.0 + jnp.exp(-jnp.abs(pre)))
    return pre, log_sig * (1.0 / GLA_GATE_NORMALIZER)


def _pair_rows(j):
    return (j // SUBLANES) * SUBLANES


def _gla_pair_fwd(q_scr, k_ref, b_scr, a_scr, h):
    c = GLA_CHUNK
    kc = pl.ds(h * GLA_HEAD_K, GLA_HEAD_K)
    a_scr[...] = jnp.zeros(a_scr.shape, F32)
    lane = lax.broadcasted_iota(jnp.int32, (1, c), 1)
    for j in range(c):
        r0 = _pair_rows(j)
        rs = pl.ds(r0, c - r0)
        rows = lax.broadcasted_iota(jnp.int32, (c - r0, 1), 0) + r0
        e = jnp.exp2(jnp.minimum(b_scr[rs, kc] - b_scr[pl.ds(j, 1), kc], 0.0))
        w = q_scr[rs, kc] * k_ref[pl.ds(j, 1), kc] * e
        col = jnp.where(rows >= j, jnp.sum(w, axis=-1, keepdims=True), 0.0)
        a_scr[rs, :] += col * (lane == j).astype(F32)


def _call_with_comm(body, comm, *, n_in, n_out, grid, in_specs, out_specs, out_shape, scratch_shapes, name, args):
    n_cin, n_cout, n_scr = len(comm.inputs), len(comm.out_shapes), len(scratch_shapes)

    def carrier(*refs):
        ins, cins = refs[:n_in], refs[n_in:n_in + n_cin]
        outs, couts = refs[n_in + n_cin:n_in + n_cin + n_out], refs[n_in + n_cin + n_out:n_in + n_cin + n_out + n_cout]
        scr = refs[n_in + n_cin + n_out + n_cout:]
        task = (cins, couts, scr[n_scr:])
        step, steps = 0, 1
        for axis, size in enumerate(grid):
            step = step * size + pl.program_id(axis)
            steps *= size

        @pl.when(step == 0)
        def _():
            comm.start(*task)

        body(*ins, *outs, *scr[:n_scr])

        @pl.when(step == steps - 1)
        def _():
            comm.middle(*task)
            comm.finish(*task)

    res = pl.pallas_call(
        carrier, name=name, grid=grid, in_specs=list(in_specs) + [_ANY] * n_cin, out_specs=list(out_specs) + [_ANY] * n_cout,
        out_shape=list(out_shape) + comm.out_shapes, scratch_shapes=list(scratch_shapes) + comm.sem_shapes,
        compiler_params=_cparams(("arbitrary",) * len(grid)),
    )(*args, *comm.inputs)
    return list(res[:n_out]), list(res[n_out:])


LOG2_E = 1.4426950408889634


def _gla_fwd(proj, wgu, bias, ng, *, comm=None, name):
    t = proj.shape[0]
    nb, nc, c = t // SEQ, GLA_NC, GLA_CHUNK

    def body(q_ref, k_ref, v_ref, r_ref, gl_ref, wgu_ref, bias_ref, ng_ref,
             y_ref, o_ref, a_ref, st_ref, state, b_scr, a_scr, q_scr):
        @pl.when(pl.program_id(1) == 0)
        def _():
            state[...] = jnp.zeros(state.shape, F32)

        _, g = _gla_log_gate(gl_ref, wgu_ref, bias_ref)
        b_scr[...] = _cumsum_rows(g) * LOG2_E
        q_scr[...] = q_ref[...] * GLA_Q_SCALE
        for h in range(GLA_HEADS):
            kc = pl.ds(h * GLA_HEAD_K, GLA_HEAD_K)
            vc = pl.ds(h * GLA_HEAD_V, GLA_HEAD_V)
            qh = q_scr[:, kc]
            kh = k_ref[:, kc]
            vh = v_ref[:, vc].astype(BF16)
            bh = b_scr[:, kc]
            blast = b_scr[pl.ds(c - 1, 1), kc]
            st = state[h]
            st_ref[h] = st
            o_inter = lax.dot_general((qh * jnp.exp2(bh)).astype(BF16), st.astype(BF16), _NT, preferred_element_type=F32)
            _gla_pair_fwd(q_scr, k_ref, b_scr, a_scr, h)
            a = a_scr[...]
            a_ref[h] = a
            o = o_inter + jnp.dot(a.astype(BF16), vh, preferred_element_type=F32)
            kd = (kh * jnp.exp2(blast - bh)).astype(BF16)
            state[h] = st * jnp.exp2(blast) + lax.dot_general(vh, kd, _TN, preferred_element_type=F32)
            o_ref[:, vc] = o
            rs = lax.rsqrt(jnp.mean(o * o, axis=-1, keepdims=True) + RMS_EPS)
            rh = r_ref[:, vc]
            y_ref[:, vc] = ((o * rs * ng_ref[...]) * (rh * _sigmoid(rh))).astype(BF16)

    def tok(width, col):
        return pl.BlockSpec((c, width), lambda b, i: (b * nc + i, col))

    whole = lambda shape: pl.BlockSpec(shape, lambda b, i: (0,) * len(shape))
    call = dict(
        grid=(nb, nc),
        in_specs=[tok(GLA_DK, 0), tok(GLA_DK, 1), tok(GLA_DV, 1), tok(GLA_DV, 2), tok(GLOW_PAD, GLA_MAIN // GLOW_PAD),
                  whole((GLOW_PAD, GLA_DK)), whole((1, GLA_DK)), whole((1, GLA_HEAD_V))],
        out_specs=[tok(GLA_DV, 0), tok(GLA_DV, 0),
                   pl.BlockSpec((GLA_HEADS, c, c), lambda b, i: (0, b * nc + i, 0)),
                   pl.BlockSpec((None, GLA_HEADS, GLA_HEAD_V, GLA_HEAD_K), lambda b, i: (b * nc + i, 0, 0, 0))],
        out_shape=[jax.ShapeDtypeStruct((t, GLA_DV), BF16), jax.ShapeDtypeStruct((t, GLA_DV), F32),
                   jax.ShapeDtypeStruct((GLA_HEADS, t, c), F32),
                   jax.ShapeDtypeStruct((t // c, GLA_HEADS, GLA_HEAD_V, GLA_HEAD_K), F32)],
        scratch_shapes=[pltpu.VMEM((GLA_HEADS, GLA_HEAD_V, GLA_HEAD_K), F32), pltpu.VMEM((c, GLA_DK), F32),
                        pltpu.VMEM((c, c), F32), pltpu.VMEM((c, GLA_DK), F32)],
        name=name)
    args = (proj, proj, proj, proj, proj, wgu, bias, ng)
    if comm is None:
        return pl.pallas_call(body, compiler_params=_cparams(("parallel", "arbitrary")), **call)(*args)
    return _call_with_comm(body, comm, n_in=8, n_out=4, args=args, **call)


def _gla_pair_bwd(q_scr, k_ref, b_scr, da_scr, dq_scr, dk_scr, h):
    c = GLA_CHUNK
    kc = pl.ds(h * GLA_HEAD_K, GLA_HEAD_K)
    lane = lax.broadcasted_iota(jnp.int32, (1, c), 1)
    for j in range(c):
        r0 = _pair_rows(j)
        rs = pl.ds(r0, c - r0)
        rows = lax.broadcasted_iota(jnp.int32, (c - r0, 1), 0) + r0
        e = jnp.exp2(jnp.minimum(b_scr[rs, kc] - b_scr[pl.ds(j, 1), kc], 0.0))
        dacol = jnp.sum(jnp.where(lane == j, da_scr[rs, :], 0.0), axis=-1, keepdims=True)
        t1 = jnp.where(rows >= j, dacol, 0.0) * e
        dq_scr[rs, kc] += t1 * k_ref[pl.ds(j, 1), kc]
        dk_scr[pl.ds(j, 1), kc] += jnp.sum(t1 * q_scr[rs, kc], axis=0, keepdims=True)


def _gla_bwd(proj, wgu, bias, ng, o, a, states, dy, *, comm=None, name):
    t = proj.shape[0]
    nb, nc, c = t // SEQ, GLA_NC, GLA_CHUNK

    def body(q_ref, k_ref, v_ref, r_ref, gl_ref, wgu_ref, bias_ref, ng_ref, o_ref, a_ref, stp_ref, stn_ref, dy_ref,
             dq_ref, dk_ref, dv_ref, dr_ref, dgl_ref, dwgu_ref, dbias_ref, dng_ref,
             dstate, b_scr, da_scr, dq_scr, dk_scr, dg_scr, q_scr):
        first = jnp.logical_and(pl.program_id(0) == 0, pl.program_id(1) == 0)

        @pl.when(first)
        def _():
            dwgu_ref[...] = jnp.zeros(dwgu_ref.shape, F32)
            dbias_ref[...] = jnp.zeros(dbias_ref.shape, F32)
            dng_ref[...] = jnp.zeros(dng_ref.shape, F32)

        @pl.when(pl.program_id(1) == 0)
        def _():
            dstate[...] = jnp.zeros(dstate.shape, F32)

        pre, g = _gla_log_gate(gl_ref, wgu_ref, bias_ref)
        b_scr[...] = _cumsum_rows(g) * LOG2_E
        q_scr[...] = q_ref[...] * GLA_Q_SCALE
        ngv = ng_ref[...]
        tri = lax.broadcasted_iota(jnp.int32, (c, c), 0) >= lax.broadcasted_iota(jnp.int32, (c, c), 1)
        for h in range(GLA_HEADS):
            kc = pl.ds(h * GLA_HEAD_K, GLA_HEAD_K)
            vc = pl.ds(h * GLA_HEAD_V, GLA_HEAD_V)
            oh = o_ref[:, vc]
            rh = r_ref[:, vc]
            dyh = dy_ref[:, vc]
            rs = lax.rsqrt(jnp.mean(oh * oh, axis=-1, keepdims=True) + RMS_EPS)
            u = oh * rs
            sg = _sigmoid(rh)
            sr = rh * sg
            dr_ref[:, vc] = (dyh * (u * ngv) * (sg * (1.0 + rh * (1.0 - sg)))).astype(BF16)
            dng_ref[...] += jnp.sum(dyh * sr * u, axis=0, keepdims=True)
            du = dyh * sr * ngv
            do = (rs * (du - u * jnp.mean(du * u, axis=-1, keepdims=True))).astype(BF16)
            qh = q_scr[:, kc]
            kh = k_ref[:, kc]
            vh = v_ref[:, vc].astype(BF16)
            bh = b_scr[:, kc]
            blast = b_scr[pl.ds(c - 1, 1), kc]
            eb = jnp.exp2(bh)
            ek = jnp.exp2(blast - bh)
            dst = dstate[h]
            dst_b = dst.astype(BF16)
            dg_carry = jnp.sum(dst * stn_ref[h], axis=0, keepdims=True)
            da = lax.dot_general(do, vh, _NT, preferred_element_type=F32)
            da_scr[...] = jnp.where(tri, da, 0.0)
            dv = lax.dot_general(a_ref[h].astype(BF16), do, _TN, preferred_element_type=F32)
            dv = dv + lax.dot_general((kh * ek).astype(BF16), dst_b, _NT, preferred_element_type=F32)
            dv_ref[:, vc] = dv.astype(BF16)
            dq_scr[:, kc] = jnp.dot(do, stp_ref[h].astype(BF16), preferred_element_type=F32) * eb
            dk_scr[:, kc] = jnp.dot(vh, dst_b, preferred_element_type=F32) * ek
            _gla_pair_bwd(q_scr, k_ref, b_scr, da_scr, dq_scr, dk_scr, h)
            dq = dq_scr[:, kc]
            dk = dk_scr[:, kc]
            dg_scr[:, kc] = _suffix_sum_rows(qh * dq - kh * dk) + dg_carry
            dstate[h] = dst * jnp.exp2(blast) + lax.dot_general(do, (qh * eb).astype(BF16), _TN, preferred_element_type=F32)
        dq_ref[...] = (dq_scr[...] * GLA_Q_SCALE).astype(BF16)
        dk_ref[...] = dk_scr[...].astype(BF16)
        dpre = dg_scr[...] * ((1.0 - _sigmoid(pre)) * (1.0 / GLA_GATE_NORMALIZER))
        dpre_b = dpre.astype(BF16)
        dbias_ref[...] += jnp.sum(dpre, axis=0, keepdims=True)
        dwgu_ref[...] += lax.dot_general(gl_ref[...].astype(BF16), dpre_b, _TN, preferred_element_type=F32)
        dgl_ref[...] = lax.dot_general(dpre_b, wgu_ref[...], _NT, preferred_element_type=F32).astype(BF16)

    def chunk(b, i):
        return b * nc + (nc - 1 - i)

    def tok(width, col):
        return pl.BlockSpec((c, width), lambda b, i: (chunk(b, i), col))

    whole = lambda shape: pl.BlockSpec(shape, lambda b, i: (0,) * len(shape))
    st_shape = (None, GLA_HEADS, GLA_HEAD_V, GLA_HEAD_K)
    call = dict(
        name=name,
        grid=(nb, nc),
        in_specs=[tok(GLA_DK, 0), tok(GLA_DK, 1), tok(GLA_DV, 1), tok(GLA_DV, 2), tok(GLOW_PAD, GLA_MAIN // GLOW_PAD),
                  whole((GLOW_PAD, GLA_DK)), whole((1, GLA_DK)), whole((1, GLA_HEAD_V)),
                  tok(GLA_DV, 0),
                  pl.BlockSpec((GLA_HEADS, c, c), lambda b, i: (0, chunk(b, i), 0)),
                  pl.BlockSpec(st_shape, lambda b, i: (chunk(b, i), 0, 0, 0)),
                  pl.BlockSpec(st_shape, lambda b, i: (b * nc + jnp.minimum(nc - i, nc - 1), 0, 0, 0)),
                  tok(GLA_DV, 0)],
        out_specs=[tok(GLA_DK, 0), tok(GLA_DK, 0), tok(GLA_DV, 0), tok(GLA_DV, 0), tok(GLOW_PAD, 0),
                   whole((GLOW_PAD, GLA_DK)), whole((1, GLA_DK)), whole((1, GLA_HEAD_V))],
        out_shape=[jax.ShapeDtypeStruct((t, GLA_DK), BF16), jax.ShapeDtypeStruct((t, GLA_DK), BF16),
                   jax.ShapeDtypeStruct((t, GLA_DV), BF16), jax.ShapeDtypeStruct((t, GLA_DV), BF16),
                   jax.ShapeDtypeStruct((t, GLOW_PAD), BF16),
                   jax.ShapeDtypeStruct((GLOW_PAD, GLA_DK), F32), jax.ShapeDtypeStruct((1, GLA_DK), F32),
                   jax.ShapeDtypeStruct((1, GLA_HEAD_V), F32)],
        scratch_shapes=[pltpu.VMEM((GLA_HEADS, GLA_HEAD_V, GLA_HEAD_K), F32), pltpu.VMEM((c, GLA_DK), F32),
                        pltpu.VMEM((c, c), F32), pltpu.VMEM((c, GLA_DK), F32), pltpu.VMEM((c, GLA_DK), F32),
                        pltpu.VMEM((c, GLA_DK), F32), pltpu.VMEM((c, GLA_DK), F32)])
    args = (proj, proj, proj, proj, proj, wgu, bias, ng, o, a, states, states, dy)
    if comm is None:
        return pl.pallas_call(body, compiler_params=_cparams(("arbitrary", "arbitrary")), **call)(*args)
    return _call_with_comm(body, comm, n_in=13, n_out=8, args=args, **call)


DIL_STEPS = DIL_BLOCK
DIL_SCALE = DIL_HEAD_DIM ** -0.5
DIL_HEADS_PER_STEP = {1: 8, 4: 1, 16: 1}


def _dil_mask(i, with_prev):
    if not with_prev:
        return (lax.broadcasted_iota(jnp.int32, (DIL_BLOCK, DIL_BLOCK), 0)
                >= lax.broadcasted_iota(jnp.int32, (DIL_BLOCK, DIL_BLOCK), 1))
    rowi = lax.broadcasted_iota(jnp.int32, (DIL_BLOCK, 2 * DIL_BLOCK), 0)
    colj = lax.broadcasted_iota(jnp.int32, (DIL_BLOCK, 2 * DIL_BLOCK), 1)
    dist = rowi + DIL_BLOCK - colj
    band = jnp.logical_and(dist >= 0, dist <= DIL_STEPS)
    return jnp.logical_and(band, jnp.logical_or(i > 0, colj >= DIL_BLOCK))


def _dil_inputs(refs, with_prev):
    if not with_prev:
        q_ref, kc_ref, vc_ref = refs
        return q_ref, kc_ref, None, vc_ref, None
    return refs


def _dil_keys(prev_ref, cur_ref, rr, hc):
    if prev_ref is None:
        return cur_ref[rr, hc].astype(BF16)
    return jnp.concatenate([prev_ref[rr, hc], cur_ref[rr, hc]], axis=0).astype(BF16)


def _dil_geometry(t, gi):
    _, d = DIL_PATTERNS[gi]
    return d, SEQ // d // DIL_BLOCK, t // SEQ, DIL_BLOCK * d, DIL_HEADS_PER_STEP[d]


def _dil_specs(gi, d, nq, rows, hps, order, with_prev):
    width = hps * DIL_HEAD_DIM
    per_part = DIL_WIDTH // width

    def named(f):
        return lambda *idx: f(**dict(zip(order, idx)))

    def block(i, prev):
        ic = jnp.minimum(i, nq - 1)
        return jnp.maximum(ic - 1, 0) if prev else ic

    def part(j, prev):
        return pl.BlockSpec((rows, width), named(lambda b, i, h: (b * nq + block(i, prev), (gi * 3 + j) * per_part + h)))

    cur = pl.BlockSpec((rows, width), named(lambda b, i, h: (b * nq + block(i, False), h)))
    done = pl.BlockSpec((rows, width), named(lambda b, i, h: (b * nq + jnp.maximum(i - 1, 0), h)))
    parts = [part(0, False), part(1, False), part(1, True), part(2, False), part(2, True)]
    return (parts if with_prev else [parts[0], parts[1], parts[3]]), cur, done


def _dil_rows(r, d):
    return pl.ds(r, DIL_BLOCK, stride=d) if d > 1 else pl.ds(0, DIL_BLOCK)


def _dil_fwd(proj, gi, *, name):
    t = proj.shape[0]
    d, nq, nb, rows, hps = _dil_geometry(t, gi)

    def body(*refs):
        q_ref, kc_ref, kp_ref, vc_ref, vp_ref, o_ref, lse_ref = refs
        mask = _dil_mask(pl.program_id(1), True)
        for h in range(hps):
            hc = pl.ds(h * DIL_HEAD_DIM, DIL_HEAD_DIM)
            for r in range(d):
                rr = _dil_rows(r, d)
                qh = q_ref[rr, hc].astype(BF16)
                kcat = _dil_keys(kp_ref, kc_ref, rr, hc)
                vcat = _dil_keys(vp_ref, vc_ref, rr, hc)
                s = lax.dot_general(qh, kcat, _NT, preferred_element_type=F32) * DIL_SCALE
                s = jnp.where(mask, s, -jnp.inf)
                m = jnp.max(s, axis=-1, keepdims=True)
                p = jnp.exp(s - m)
                l = jnp.sum(p, axis=-1, keepdims=True)
                o_ref[rr, hc] = jnp.dot((p / l).astype(BF16), vcat, preferred_element_type=F32)
                lse_ref[rr, hc] = jnp.broadcast_to(m + jnp.log(l), (DIL_BLOCK, DIL_HEAD_DIM))

    parts, cur, _ = _dil_specs(gi, d, nq, rows, hps, "bih", True)
    return pl.pallas_call(
        body,
        name=name,
        grid=(nb, nq, DIL_HEADS // hps),
        in_specs=parts,
        out_specs=[cur, cur],
        out_shape=[jax.ShapeDtypeStruct((t, DIL_WIDTH), F32)] * 2,
        compiler_params=_cparams(("parallel", "parallel", "parallel")),
    )(*[proj] * len(parts))


def _dil_bwd(proj, gi, lse, do, delta, *, name):
    t = proj.shape[0]
    d, nq, nb, rows, hps = _dil_geometry(t, gi)
    with_prev = nq > 1

    def body(*refs):
        q_ref, kc_ref, kp_ref, vc_ref, vp_ref = _dil_inputs(refs[:-11], with_prev)
        lse_ref, do_ref, dl_ref, dq_ref, dk_ref, dv_ref, ck, cv, fk, fv, dq_s = refs[-11:]
        i = pl.program_id(2)

        @pl.when(i == 0)
        def _():
            ck[...] = jnp.zeros(ck.shape, F32)
            cv[...] = jnp.zeros(cv.shape, F32)

        @pl.when(i < nq)
        def _():
            mask = _dil_mask(i, with_prev)
            for h in range(hps):
                hc = pl.ds(h * DIL_HEAD_DIM, DIL_HEAD_DIM)
                h1 = pl.ds(h * DIL_HEAD_DIM, 1)
                for r in range(d):
                    rr = _dil_rows(r, d)
                    qh = q_ref[rr, hc].astype(BF16)
                    kcat = _dil_keys(kp_ref, kc_ref, rr, hc)
                    vcat = _dil_keys(vp_ref, vc_ref, rr, hc)
                    doh = do_ref[rr, hc].astype(BF16)
                    s = lax.dot_general(qh, kcat, _NT, preferred_element_type=F32) * DIL_SCALE
                    p = jnp.exp(jnp.where(mask, s, -jnp.inf) - lse_ref[rr, h1])
                    dp = lax.dot_general(doh, vcat, _NT, preferred_element_type=F32)
                    ds = (p * (dp + dl_ref[rr, h1]) * DIL_SCALE).astype(BF16)
                    dq_s[rr, hc] = jnp.dot(ds, kcat, preferred_element_type=F32)
                    dkcat = lax.dot_general(ds, qh, _TN, preferred_element_type=F32)
                    dvcat = lax.dot_general(p.astype(BF16), doh, _TN, preferred_element_type=F32)
                    if with_prev:
                        fk[rr, hc] = ck[rr, hc] + dkcat[:DIL_BLOCK]
                        fv[rr, hc] = cv[rr, hc] + dvcat[:DIL_BLOCK]
                    ck[rr, hc] = dkcat[-DIL_BLOCK:]
                    cv[rr, hc] = dvcat[-DIL_BLOCK:]
            dq_ref[...] = dq_s[...].astype(BF16)

            @pl.when(i > 0)
            def _():
                dk_ref[...] = fk[...].astype(BF16)
                dv_ref[...] = fv[...].astype(BF16)

        @pl.when(i == nq)
        def _():
            dk_ref[...] = ck[...].astype(BF16)
            dv_ref[...] = cv[...].astype(BF16)

    parts, cur, done = _dil_specs(gi, d, nq, rows, hps, "bhi", with_prev)
    shape = jax.ShapeDtypeStruct((t, DIL_WIDTH), BF16)
    tile = pltpu.VMEM((rows, hps * DIL_HEAD_DIM), F32)
    return pl.pallas_call(
        body,
        name=name,
        grid=(nb, DIL_HEADS // hps, nq + 1),
        in_specs=parts + [cur, cur, cur],
        out_specs=[cur, done, done],
        out_shape=[shape, shape, shape],
        scratch_shapes=[tile] * 5,
        compiler_params=_cparams(("parallel", "parallel", "arbitrary")),
    )(*[proj] * len(parts), lse, do, delta)


MIX_ROWS = 256


def _head_rowsum(x):
    parts = []
    for h in range(DIL_HEADS):
        s = jnp.sum(x[:, h * DIL_HEAD_DIM:(h + 1) * DIL_HEAD_DIM], axis=-1, keepdims=True)
        parts.append(jnp.broadcast_to(s, (x.shape[0], DIL_HEAD_DIM)))
    return jnp.concatenate(parts, axis=-1)


def _mix_weights(lse_refs):
    ls = [r[...] for r in lse_refs]
    m = jnp.maximum(jnp.maximum(ls[0], ls[1]), ls[2])
    es = [jnp.exp(l - m) for l in ls]
    inv = 1.0 / (es[0] + es[1] + es[2])
    return [e * inv for e in es]


def _dil_mix_fwd(os_, lses, *, name):
    t = os_[0].shape[0]

    def body(o0, o1, o2, l0, l1, l2, out_ref):
        w = _mix_weights((l0, l1, l2))
        out_ref[...] = (w[0] * o0[...] + w[1] * o1[...] + w[2] * o2[...]).astype(BF16)

    row = pl.BlockSpec((MIX_ROWS, DIL_WIDTH), lambda i: (i, 0))
    return pl.pallas_call(
        body, name=name, grid=(t // MIX_ROWS,), in_specs=[row] * 6, out_specs=row,
        out_shape=jax.ShapeDtypeStruct((t, DIL_WIDTH), BF16), compiler_params=_cparams(("parallel",)),
    )(*os_, *lses)


def _dil_mix_bwd(os_, lses, dout, *, name):
    t = os_[0].shape[0]

    def body(o0, o1, o2, l0, l1, l2, d_ref, do0, do1, do2, dl0, dl1, dl2):
        w = _mix_weights((l0, l1, l2))
        dv = d_ref[...]
        mix = w[0] * o0[...] + w[1] * o1[...] + w[2] * o2[...]
        bar = _head_rowsum(dv * mix)
        for wg, do_ref, dl_ref in zip(w, (do0, do1, do2), (dl0, dl1, dl2)):
            do_ref[...] = wg * dv
            dl_ref[...] = -wg * bar

    row = pl.BlockSpec((MIX_ROWS, DIL_WIDTH), lambda i: (i, 0))
    outs = pl.pallas_call(
        body, name=name, grid=(t // MIX_ROWS,), in_specs=[row] * 7, out_specs=[row] * 6,
        out_shape=[jax.ShapeDtypeStruct((t, DIL_WIDTH), F32)] * 6,
        compiler_params=_cparams(("parallel",)),
    )(*os_, *lses, dout)
    return outs[:3], outs[3:]


_MESH = pl.DeviceIdType.MESH
_ANY = pl.BlockSpec(memory_space=pl.ANY)


def _position():
    return lax.axis_index("x"), lax.axis_index("y"), lax.axis_index("c")


AG_COPIES = 7


def _run_comm(task, *, name):
    n_in, n_out = len(task.inputs), len(task.out_shapes)

    def body(*refs):
        parts = (refs[:n_in], refs[n_in:n_in + n_out], refs[n_in + n_out:])
        task.start(*parts)
        task.middle(*parts)
        task.finish(*parts)

    return pl.pallas_call(
        body, name=name, out_shape=task.out_shapes, in_specs=[_ANY] * n_in, out_specs=[_ANY] * n_out,
        scratch_shapes=task.sem_shapes,
    )(*task.inputs)


def _gather_task(shards):
    n = len(shards)

    def copies(x_refs, out_refs, sems):
        send_sems, recv_sems, local_sems = sems
        x, y, cc = _position()
        me, sibling = (x, y, cc), (x, y, 1 - cc)
        chips = [(1 - x, y), (x, 1 - y), (1 - x, 1 - y)]

        def copy(w, k, block, to, own=False):
            px, py, pc = block
            slot = out_refs[w].at[4 * px + 2 * py + pc]
            return pltpu.make_async_remote_copy(
                src_ref=x_refs[w] if own else slot, dst_ref=slot,
                send_sem=send_sems.at[AG_COPIES * w + k], recv_sem=recv_sems.at[AG_COPIES * w + k],
                device_id=to, device_id_type=_MESH)

        mine = [pltpu.make_async_copy(x_refs[w], out_refs[w].at[4 * x + 2 * y + cc], local_sems.at[w]) for w in range(n)]
        first = [[copy(w, 0, me, sibling, own=True)] + [copy(w, 1 + j, me, (*chip, cc), own=True) for j, chip in enumerate(chips)]
                 for w in range(n)]
        landed = [[copy(w, 1 + j, (*chip, cc), me) for j, chip in enumerate(chips)] for w in range(n)]
        passed = [[copy(w, 4 + j, (*chip, cc), sibling) for j, chip in enumerate(chips)] for w in range(n)]
        from_sibling = [[copy(w, 0, sibling, me)] + [copy(w, 4 + j, (*chip, 1 - cc), me) for j, chip in enumerate(chips)]
                        for w in range(n)]
        return mine, first, landed, passed, from_sibling

    def start(ins, outs, sems):
        mine, first, _, _, _ = copies(ins, outs, sems)
        for w in range(n):
            mine[w].start()
            for cp in first[w]:
                cp.start()

    def middle(ins, outs, sems):
        _, _, landed, passed, _ = copies(ins, outs, sems)
        for j in range(3):
            for w in range(n):
                landed[w][j].wait_recv()
                passed[w][j].start()

    def finish(ins, outs, sems):
        mine, first, _, passed, from_sibling = copies(ins, outs, sems)
        for w in range(n):
            for cp in from_sibling[w]:
                cp.wait_recv()
        for w in range(n):
            for cp in first[w] + passed[w]:
                cp.wait_send()
            mine[w].wait()

    return _Comm(shards, [jax.ShapeDtypeStruct((N_DEV,) + s.shape, s.dtype) for s in shards],
                 [pltpu.SemaphoreType.DMA((AG_COPIES * n,)), pltpu.SemaphoreType.DMA((AG_COPIES * n,)),
                  pltpu.SemaphoreType.DMA((n,))], start, middle, finish)


def _parity_half(ref, parity, half_rows):
    if half_rows is None:
        return ref.at[:, parity]
    return ref.at[:, pl.ds(parity * half_rows, half_rows), :]


def _exchange_task(make_copies, inputs, out_shapes, n_copies):
    def start(ins, outs, sems):
        for cp in make_copies(ins, outs, sems):
            cp.start()

    def finish(ins, outs, sems):
        for cp in make_copies(ins, outs, sems):
            cp.wait()

    return _Comm(inputs, out_shapes, [pltpu.SemaphoreType.DMA((n_copies,)), pltpu.SemaphoreType.DMA((n_copies,))],
                 start, lambda ins, outs, sems: None, finish)


def _sibling_task(gs, half_rows):
    n = len(gs)

    def make_copies(g_refs, out_refs, sems):
        x, y, cc = _position()
        return [pltpu.make_async_remote_copy(
            src_ref=_parity_half(g_refs[k], 1 - cc, half_rows[k]), dst_ref=out_refs[k],
            send_sem=sems[0].at[k], recv_sem=sems[1].at[k],
            device_id=(x, y, 1 - cc), device_id_type=_MESH) for k in range(n)]

    def out_shape(g, hr):
        return jax.ShapeDtypeStruct((4,) + (g.shape[2:] if hr is None else (hr, g.shape[2])), g.dtype)

    return _exchange_task(make_copies, gs, [out_shape(g, hr) for g, hr in zip(gs, half_rows)], n)


def _chips_task(ps):
    n = len(ps)

    def make_copies(p_refs, out_refs, sems):
        x, y, cc = _position()
        copies = []
        for w in range(n):
            for k in (1, 2, 3):
                px = 1 - x if k >> 1 else x
                py = 1 - y if k & 1 else y
                copies.append(pltpu.make_async_remote_copy(
                    src_ref=p_refs[w].at[2 * px + py], dst_ref=out_refs[w].at[k - 1],
                    send_sem=sems[0].at[3 * w + k - 1], recv_sem=sems[1].at[3 * w + k - 1],
                    device_id=(px, py, cc), device_id_type=_MESH))
        return copies

    return _exchange_task(make_copies, ps, [jax.ShapeDtypeStruct((3,) + p.shape[1:], p.dtype) for p in ps], 3 * n)


def _add_sibling(g, r1, place, half_rows, *, tr, tc, name):
    _, r, c = r1.shape
    if half_rows is None:
        g_spec = pl.BlockSpec((None, None, tr, tc), lambda i, j, k, pc: (k, pc[0], i, j))
    else:
        per_half = half_rows // tr
        g_spec = pl.BlockSpec((None, tr, tc), lambda i, j, k, pc: (k, pc[0] * per_half + i, j))

    def body(pc_ref, g_ref, r_ref, pb_ref, own_ref):
        s = g_ref[...] + r_ref[...]
        pb_ref[...] = s.astype(BF16)

        @pl.when(pl.program_id(2) == pc_ref[1])
        def _():
            own_ref[...] = s

    grid_spec = pltpu.PrefetchScalarGridSpec(
        num_scalar_prefetch=1,
        grid=(r // tr, c // tc, 4),
        in_specs=[g_spec, pl.BlockSpec((None, tr, tc), lambda i, j, k, pc: (k, i, j))],
        out_specs=[pl.BlockSpec((None, tr, tc), lambda i, j, k, pc: (k, i, j)),
                   pl.BlockSpec((tr, tc), lambda i, j, k, pc: (i, j))],
    )
    return pl.pallas_call(
        body, name=name, grid_spec=grid_spec,
        out_shape=[jax.ShapeDtypeStruct((4, r, c), BF16), jax.ShapeDtypeStruct((r, c), F32)],
        compiler_params=_cparams(("parallel", "parallel", "arbitrary")),
    )(place, g, r1)


def _adamw_math(g, w, m, v):
    m = ADAM_B1 * m + (1.0 - ADAM_B1) * g
    v = ADAM_B2 * v + (1.0 - ADAM_B2) * (g * g)
    m_hat = m / (1.0 - ADAM_B1 ** ADAM_STEP)
    v_hat = v / (1.0 - ADAM_B2 ** ADAM_STEP)
    delta = -ADAM_LR * (m_hat / (jnp.sqrt(v_hat) + ADAM_EPS) + ADAM_WD * w)
    return delta, m, v


def _adamw_big(own, r2, w, m, v, prev, layer, *, tr, tc, name):
    _, r, c = w.shape

    def body(p_ref, r2_ref, w_ref, m_ref, v_ref, a0, a1, a2, a3, g_ref, d_ref, mo_ref, vo_ref):
        g = ((p_ref[...] + r2_ref[0].astype(F32)) + r2_ref[1].astype(F32)) + r2_ref[2].astype(F32)
        delta, mn, vn = _adamw_math(g, w_ref[...], m_ref[...], v_ref[...])
        g_ref[...] = g
        d_ref[...] = delta
        mo_ref[...] = mn
        vo_ref[...] = vn

    lay = pl.BlockSpec((None, tr, tc), lambda i, j: (layer, i, j))
    return pl.pallas_call(
        body, name=name, grid=(r // tr, c // tc),
        in_specs=[pl.BlockSpec((tr, tc), lambda i, j: (i, j)), pl.BlockSpec((3, tr, tc), lambda i, j: (0, i, j)),
                  lay, lay, lay, _ANY, _ANY, _ANY, _ANY],
        out_specs=[lay, lay, lay, lay],
        out_shape=[jax.ShapeDtypeStruct(w.shape, F32)] * 4,
        input_output_aliases={5: 0, 6: 1, 7: 2, 8: 3},
        compiler_params=_cparams(("parallel", "parallel")),
    )(own, r2, w, m, v, *prev)


SMALL_COLS = 1024


def _sum_gathered(parts, *, name):
    _, r, c = parts.shape

    def body(p_ref, o_ref):
        acc = p_ref[0]
        for k in range(1, N_DEV):
            acc = acc + p_ref[k]
        o_ref[...] = acc

    return pl.pallas_call(
        body, name=name, grid=(1,), in_specs=[pl.BlockSpec((N_DEV, r, c), lambda i: (0, 0, 0))],
        out_specs=pl.BlockSpec((r, c), lambda i: (0, 0)), out_shape=jax.ShapeDtypeStruct((r, c), F32),
        compiler_params=_cparams(("arbitrary",)),
    )(parts)


def _adamw_small(g, w, m, v, *, name):
    r, c = g.shape

    def body(g_ref, w_ref, m_ref, v_ref, d_ref, mo_ref, vo_ref):
        delta, mn, vn = _adamw_math(g_ref[...], w_ref[...], m_ref[...], v_ref[...])
        d_ref[...] = delta
        mo_ref[...] = mn
        vo_ref[...] = vn

    spec = pl.BlockSpec((r, c), lambda i: (0, 0))
    return pl.pallas_call(
        body, name=name, grid=(1,), in_specs=[spec] * 4, out_specs=[spec] * 3,
        out_shape=[jax.ShapeDtypeStruct((r, c), F32)] * 3, compiler_params=_cparams(("arbitrary",)),
    )(g, w, m, v)


WEIGHT_NAMES = ("gla_w_in", "gla_w_gate_up", "gla_gate_bias", "gla_norm_g", "gla_w_out", "dil_w_in", "dil_w_out",
                "ffn_w_up", "ffn_conv_w", "ffn_conv_b", "ffn_w_down", "ln_g", "ln_b")
ADAM_TILES = {"gla_w_in": (256, 770), "gla_w_out": (128, 2048), "dil_w_in": (256, 1152), "dil_w_out": (512, 256),
              "ffn_w_up": (344, 1024), "ffn_w_down": (344, 1024)}
ADD_TILES = {**ADAM_TILES, "ffn_w_up": (352, 1024)}
FF_DOWN_SHARD = D_FF // N_DEV
VEC_COLS = 128


def _pad_axis(a, axis, to):
    pads = [(0, 0)] * a.ndim
    pads[axis] = (0, to - a.shape[axis])
    return jnp.pad(a, pads)


def _ff_cols(blocks):
    r = blocks.shape[1]
    return _pad_axis(blocks, 2, FF_SHARD_PAD).reshape(2, 4, r, FF_SHARD_PAD).transpose(0, 2, 1, 3).reshape(2, r, FF_HALF_PAD)


def _ff_cols_back(a):
    r = a.shape[1]
    return a.reshape(2, r, 4, FF_SHARD_PAD)[..., :FF_SHARD].transpose(1, 0, 2, 3).reshape(r, 2 * D_FF)


def _vec_parts(l, w):
    return [w["ffn_conv_w"][l], w["ln_g"][l], w["ln_b"][l]] + ([w["gla_w_gate_up"][l // 2]] if l % 2 == 0 else [])


def _layer_shards(l, w):
    j = l // 2
    gla = l % 2 == 0
    parts = _vec_parts(l, w)
    vec_rows = -(-sum(math.prod(a.shape) for a in parts) // (VEC_COLS * SUBLANES)) * SUBLANES
    mixer = [(w["gla_w_in"] if gla else w["dil_w_in"])[j].astype(BF16),
             (w["gla_w_out"] if gla else w["dil_w_out"])[j].astype(BF16), _pack_rows(parts, vec_rows, VEC_COLS)]
    w_up_t = _pad_axis(jnp.swapaxes(w["ffn_w_up"][l], 0, 1).astype(BF16), 0, FF_SHARD_PAD)
    return mixer, w_up_t, w["ffn_w_down"][l].astype(BF16)


def _w_down_layout(g_down):
    return _pad_axis(g_down.reshape(4, FF_SHARD, D_MODEL), 1, FF_SHARD_PAD).reshape(FF_HALF_PAD, D_MODEL)


def _layer_weights(l, w, g_in, g_out, g_vec):
    j = l // 2
    gla = l % 2 == 0
    vec_shapes = [a.shape for a in _vec_parts(l, w)]
    out = {}
    vec = [jnp.stack(p) for p in zip(*[_unpack_rows(g_vec[d], vec_shapes) for d in range(N_DEV)])]
    out["conv_w"] = _ff_cols(vec[0])
    out["conv_b"] = _pad_axis(w["ffn_conv_b"][l].reshape(N_DEV, FF_SHARD), 1, FF_SHARD_PAD).reshape(2, 1, FF_HALF_PAD)
    out["ln_g"] = vec[1].transpose(1, 0, 2).reshape(2, 1, D_MODEL)
    out["ln_b"] = vec[2].transpose(1, 0, 2).reshape(2, 1, D_MODEL)
    if gla:
        win = g_in.transpose(1, 0, 2).reshape(D_MODEL, GLA_IN)
        out["w_in"] = _pad_axis(win, 1, GLA_MAIN + GLOW_PAD)
        wgu = vec[3].transpose(1, 0, 2).reshape(GLA_GATE_RANK, GLA_DK).astype(BF16)
        out["w_gate_up"] = _pad_axis(wgu, 0, GLOW_PAD)
        out["w_out"] = g_out.reshape(GLA_DV, D_MODEL)
        out["gate_bias"] = w["gla_gate_bias"][j].reshape(1, GLA_DK)
        out["norm_g"] = w["gla_norm_g"][j].reshape(1, GLA_HEAD_V)
    else:
        out["w_in"] = g_in
        out["w_out"] = g_out.transpose(1, 0, 2).reshape(DIL_WIDTH, D_MODEL)
    return out


def _by_chip_parity(blocks):
    return blocks.reshape((4, 2) + blocks.shape[1:])


def _col_blocks(dw, width):
    r = dw.shape[0]
    return dw.reshape(r, N_DEV, width).transpose(1, 0, 2)


def _carry(call, task):
    if task is None:
        return call(None), []
    return call(task)


def _sibling_sum(l, n, g, half_rows, from_sibling, place):
    return _add_sibling(g, from_sibling, place, half_rows, tr=ADD_TILES[n][0], tc=ADD_TILES[n][1], name=f"l{l}_{n}_add")


def _ffn_fwd(l, yb, lw, task_up, task_down):
    h, got_up = _carry(lambda c: _ffn_hidden(yb, lw["w_up_t"], comm=c, name=f"l{l}_ffn_hidden"), task_up)
    act = _convgate_fwd(h, lw["conv_w"], lw["conv_b"], name=f"l{l}_convgate")
    ffn, got_down = _carry(lambda c: _matmul(act, lw["w_down"], tm=1024, tn=1024, tk=2816, comm=c, name=f"l{l}_ffn_down"), task_down)
    return ffn, (h, act), got_up, got_down


def _ffn_bwd(l, yb, dz, dzb, lw, saved, place, pending):
    h, act = saved
    t = yb.shape[0]
    task = _sibling_task(list(pending[1].values()), [None] * len(pending[1])) if pending else None
    dact, from_sibling = _carry(lambda c: _matmul(dzb, lw["w_down"], tb=True, tm=512, tn=2816, tk=D_MODEL, comm=c,
                                                  name=f"l{l}_ffn_dact"), task)
    pend_sums = {n: _sibling_sum(pending[0], n, g, None, r1, place)
                 for (n, g), r1 in zip(pending[1].items(), from_sibling)} if pending else {}
    d_down = _ffn_down_dw(act, dzb, name=f"l{l}_ffn_dwdown")
    dh, dcw, dcb = _convgate_bwd(h, dact, lw["conv_w"], lw["conv_b"], name=f"l{l}_convgate_bwd")
    dh = dh.reshape(2 * t, FF_HALF_PAD)
    tasks = [_sibling_task([d_down], [FF_DOWN_SHARD])] + ([_chips_task([s[0] for s in pend_sums.values()])] if pending else [])
    d_up_t, got = _ffn_hidden_dw(dh, yb, comm=_join_comm(tasks), name=f"l{l}_ffn_dwup")
    d_up = _by_chip_parity(d_up_t)
    dy, got_up = _ffn_hidden_dy(dh, lw["w_up_t"], dz, DEEPNORM_ALPHA, comm=_sibling_task([d_up], [None]), name=f"l{l}_ffn_dy")
    sums = {"ffn_w_down": _sibling_sum(l, "ffn_w_down", d_down, FF_DOWN_SHARD, got[0], place),
            "ffn_w_up": _sibling_sum(l, "ffn_w_up", d_up, None, got_up[0], place)}
    small = {"ffn_conv_w": _ff_cols_back(dcw), "ffn_conv_b": _ff_cols_back(dcb)[0]}
    return dy, sums, small, pend_sums, got[1:]


def _gla_layer_fwd(l, hb, lw, tasks):
    proj, got_proj = _carry(lambda c: _matmul(hb, lw["w_in"], tm=1024, tn=896, tk=D_MODEL, comm=c, name=f"l{l}_gla_proj"),
                            tasks["proj"])
    (y, o, a, st), got_gla = _carry(lambda c: _gla_fwd(proj, lw["w_gate_up"], lw["gate_bias"], lw["norm_g"], comm=c,
                                                       name=f"l{l}_gla"), tasks["gla"])
    mix, got_out = _carry(lambda c: _matmul(y, lw["w_out"], tm=1024, tn=1024, tk=GLA_DV, comm=c, name=f"l{l}_gla_out"),
                          tasks["out"])
    return mix, (proj, y, o, a, st), {"proj": got_proj, "gla": got_gla, "out": got_out}


def _gla_layer_bwd(l, hb, dz, dzb, lw, saved, ffn_sums, place):
    proj, y, o, a, st = saved
    dy = _matmul(dzb, lw["w_out"], tb=True, tm=1024, tn=1024, tk=D_MODEL, name=f"l{l}_gla_dy")
    d_out = _by_chip_parity(_matmul(y, dzb, ta=True, tm=1024, tn=1024, tk=2048, name=f"l{l}_gla_dwout")
                            .reshape(N_DEV, GLA_DV // N_DEV, D_MODEL))
    names = list(ffn_sums)
    (dq, dk, dv, dr, dgl, dwgu, dbias, dng), got = _gla_bwd(
        proj, lw["w_gate_up"], lw["gate_bias"], lw["norm_g"], o, a, st, dy,
        comm=_join_comm([_chips_task([ffn_sums[n][0] for n in names]), _sibling_task([d_out], [None])]), name=f"l{l}_gla_bwd")
    out_blocks, out_own = _sibling_sum(l, "gla_w_out", d_out, None, got[len(names)], place)
    dproj = jnp.concatenate([dq, dk, dv, dr, dgl], axis=-1)
    d_in, got_out = _matmul(hb, dproj, ta=True, tm=1024, tn=896, tk=2048, comm=_chips_task([out_blocks]), name=f"l{l}_gla_dwin")
    dx = _matmul(dproj, lw["w_in"], tb=True, tm=1024, tn=1024, tk=896, res=dz, res_scale=DEEPNORM_ALPHA, name=f"l{l}_gla_dx")
    big = {"gla_w_in": _by_chip_parity(_col_blocks(d_in[:, :GLA_IN], GLA_IN // N_DEV))}
    small = {"gla_w_gate_up": dwgu[:GLA_GATE_RANK], "gla_gate_bias": dbias[0], "gla_norm_g": dng[0]}
    return dx, big, small, dict(zip(names, got)), {"gla_w_out": (out_own, got_out[0])}


def _dil_layer_fwd(l, hb, lw, tasks):
    proj, got_proj = _carry(lambda c: _mm_colblocks(hb, lw["w_in"], comm=c, name=f"l{l}_dil_proj"), tasks["proj"])
    os_, lses = [], []
    for gi in range(len(DIL_PATTERNS)):
        o, lse = _dil_fwd(proj, gi, name=f"l{l}_dil_attn{gi}")
        os_.append(o)
        lses.append(lse)
    omix = _dil_mix_fwd(os_, lses, name=f"l{l}_dil_mix")
    mix, got_out = _carry(lambda c: _matmul(omix, lw["w_out"], tm=1024, tn=1024, tk=DIL_WIDTH, comm=c, name=f"l{l}_dil_out"),
                          tasks["out"])
    return mix, (proj, os_, lses, omix), {"proj": got_proj, "out": got_out}


def _dil_layer_bwd(l, hb, dz, dzb, lw, saved, ffn_sums, place):
    proj, os_, lses, omix = saved
    dout = _matmul(dzb, lw["w_out"], tb=True, tm=1024, tn=1024, tk=D_MODEL, name=f"l{l}_dil_dy")
    d_out = _by_chip_parity(_col_blocks(_matmul(omix, dzb, ta=True, tm=1024, tn=1024, tk=2048, name=f"l{l}_dil_dwout"),
                                        D_MODEL // N_DEV))
    dos, dls = _dil_mix_bwd(os_, lses, dout, name=f"l{l}_dil_mix_bwd")
    parts = []
    for gi in range(len(DIL_PATTERNS)):
        parts += list(_dil_bwd(proj, gi, lses[gi], dos[gi], dls[gi], name=f"l{l}_dil_attn_bwd{gi}"))
    dproj = jnp.concatenate(parts, axis=-1)
    d_in, got = _mm_grad_colblocks(hb, dproj, DIL_IN // N_DEV, name=f"l{l}_dil_dwin",
                                   comm=_join_comm([_chips_task([ffn_sums["ffn_w_down"][0]]), _sibling_task([d_out], [None])]))
    out_blocks, out_own = _sibling_sum(l, "dil_w_out", d_out, None, got[1], place)
    dx, got2 = _mm_colblocks_t(dproj, lw["w_in"], dz, DEEPNORM_ALPHA, name=f"l{l}_dil_dx",
                               comm=_join_comm([_chips_task([ffn_sums["ffn_w_up"][0]]), _chips_task([out_blocks])]))
    big = {"dil_w_in": _by_chip_parity(d_in)}
    return dx, big, {}, {"ffn_w_down": got[0], "ffn_w_up": got2[0]}, {"dil_w_out": (out_own, got2[1])}


def _pack_rows(arrays, rows, cols=SMALL_COLS):
    flat = [a.reshape(-1) for a in arrays]
    used = sum(f.shape[0] for f in flat)
    return jnp.concatenate(flat + [jnp.zeros((rows * cols - used,), F32)]).reshape(rows, cols)


def _unpack_rows(packed, shapes):
    flat, out, off = packed.reshape(-1), [], 0
    for s in shapes:
        n = math.prod(s)
        out.append(flat[off:off + n].reshape(s))
        off += n
    return out


def _rows_for(shapes):
    n = sum(math.prod(s) for s in shapes)
    return -(-n // (SMALL_COLS * SUBLANES)) * SUBLANES


def kernel(x, gla_w_in, gla_w_gate_up, gla_gate_bias, gla_norm_g, gla_w_out, dil_w_in, dil_w_out, ffn_w_up, ffn_conv_w, ffn_conv_b, ffn_w_down, ln_g, ln_b, loss_target, m_gla_w_in, m_gla_w_gate_up, m_gla_gate_bias, m_gla_norm_g, m_gla_w_out, m_dil_w_in, m_dil_w_out, m_ffn_w_up, m_ffn_conv_w, m_ffn_conv_b, m_ffn_w_down, m_ln_g, m_ln_b, v_gla_w_in, v_gla_w_gate_up, v_gla_gate_bias, v_gla_norm_g, v_gla_w_out, v_dil_w_in, v_dil_w_out, v_ffn_w_up, v_ffn_conv_w, v_ffn_conv_b, v_ffn_w_down, v_ln_g, v_ln_b):
    w = dict(zip(WEIGHT_NAMES, (gla_w_in, gla_w_gate_up, gla_gate_bias, gla_norm_g, gla_w_out, dil_w_in, dil_w_out,
                                ffn_w_up, ffn_conv_w, ffn_conv_b, ffn_w_down, ln_g, ln_b)))
    mom = dict(zip(WEIGHT_NAMES, (m_gla_w_in, m_gla_w_gate_up, m_gla_gate_bias, m_gla_norm_g, m_gla_w_out, m_dil_w_in,
                                  m_dil_w_out, m_ffn_w_up, m_ffn_conv_w, m_ffn_conv_b, m_ffn_w_down, m_ln_g, m_ln_b)))
    var = dict(zip(WEIGHT_NAMES, (v_gla_w_in, v_gla_w_gate_up, v_gla_gate_bias, v_gla_norm_g, v_gla_w_out, v_dil_w_in,
                                  v_dil_w_out, v_ffn_w_up, v_ffn_conv_w, v_ffn_conv_b, v_ffn_w_down, v_ln_g, v_ln_b)))
    xi, yi, ci = _position()
    dev = 4 * xi + 2 * yi + ci
    place = jnp.stack([ci, 2 * xi + yi]).astype(jnp.int32)
    t = x.shape[0] * x.shape[1]
    h = x.reshape(t, D_MODEL)
    hb = h.astype(BF16)
    target = loss_target.reshape(t, D_MODEL)

    shards = [_layer_shards(l, w) for l in range(DEPTH)]
    arrived = {0: dict(zip(("in", "out", "vec"), _run_comm(_gather_task(shards[0][0]), name="gather_l0")))}
    lws, saved = [], []
    for l in range(DEPTH):
        gla = l % 2 == 0
        nxt = l + 1 if l + 1 < DEPTH else None
        nxt_dil = nxt is not None and nxt % 2 == 1
        here = arrived[l]
        lw = _layer_weights(l, w, here["in"], here["out"], here["vec"])
        tasks = {"proj": _gather_task([shards[l][2]]),
                 "gla": _gather_task([shards[l][1]]) if gla else None,
                 "out": _gather_task(shards[nxt][0][1:]) if nxt else None}
        mix, mixer_saved, got = (_gla_layer_fwd if gla else _dil_layer_fwd)(l, hb, lw, tasks)
        lw["w_down"] = _w_down_layout(got["proj"][0])
        lw["w_up_t"] = got["gla"][0] if gla else here["up"]
        y1, y1b, xh1, rs1 = _ln_fwd(h, mix, lw["ln_g"][0], lw["ln_b"][0], name=f"l{l}_ln1")
        task_hidden = _gather_task([shards[nxt][1] if nxt_dil else shards[nxt][0][0]]) if nxt else None
        task_down = _gather_task([shards[nxt][0][0]]) if nxt_dil else None
        ffn, ffn_saved, got_hidden, got_down = _ffn_fwd(l, y1b, lw, task_hidden, task_down)
        y2, y2b, xh2, rs2 = _ln_fwd(y1, ffn, lw["ln_g"][1], lw["ln_b"][1], name=f"l{l}_ln2")
        saved.append((hb, mixer_saved, y1b, xh1, rs1, ffn_saved, xh2, rs2))
        lws.append(lw)
        h, hb = y2, y2b
        if nxt:
            arrived[nxt] = {"out": got["out"][0], "vec": got["out"][1]}
            if nxt_dil:
                arrived[nxt].update({"up": got_hidden[0], "in": got_down[0]})
            else:
                arrived[nxt]["in"] = got_hidden[0]
    loss_local, dy = _loss_fwd_bwd(h, target, name="loss")
    loss = lax.psum(loss_local[0, 0], ("x", "y", "c"))

    big_names = tuple(ADAM_TILES)
    as_updated = lambda n, a: jnp.swapaxes(a, 1, 2) if n == "ffn_w_up" else a
    wt, mt, vt = ({n: as_updated(n, d[n]) for n in big_names} for d in (w, mom, var))
    results = {n: [lax.empty(wt[n].shape, F32) for _ in range(4)] for n in big_names}
    small_grads = {n: [None] * w[n].shape[0] for n in WEIGHT_NAMES if n not in big_names}

    def adamw(l, n, own, from_chips):
        results[n] = _adamw_big(own, from_chips, wt[n], mt[n], vt[n], results[n], l if n.startswith("ffn") else l // 2,
                                tr=ADAM_TILES[n][0], tc=ADAM_TILES[n][1], name=f"l{l}_{n}_adamw")

    pending = None
    for l in reversed(range(DEPTH)):
        lw = lws[l]
        hb_in, mixer_saved, y1b, xh1, rs1, ffn_saved, xh2, rs2 = saved[l]
        dz2, dz2b, dg2, db2 = _ln_bwd(dy, xh2, rs2, lw["ln_g"][1], name=f"l{l}_ln2_bwd")
        dy1, ffn_sums, small_ffn, pend_sums, got_pending = _ffn_bwd(l, y1b, dz2, dz2b, lw, ffn_saved, place, pending)
        for (n, (_, own)), r2 in zip(pend_sums.items(), got_pending):
            adamw(pending[0], n, own, r2)
        dz1, dz1b, dg1, db1 = _ln_bwd(dy1, xh1, rs1, lw["ln_g"][0], name=f"l{l}_ln1_bwd")
        dy, big_mix, small_mix, got_ffn, reduced = (_gla_layer_bwd if l % 2 == 0 else _dil_layer_bwd)(
            l, hb_in, dz1, dz1b, lw, mixer_saved, ffn_sums, place)
        for n, (_, own) in ffn_sums.items():
            adamw(l, n, own, got_ffn[n])
        for n, (own, r2) in reduced.items():
            adamw(l, n, own, r2)
        pending = (l, big_mix)
        small_grads["ln_g"][l] = jnp.concatenate([dg1, dg2], axis=0)
        small_grads["ln_b"][l] = jnp.concatenate([db1, db2], axis=0)
        for n, g in small_ffn.items():
            small_grads[n][l] = g
        for n, g in small_mix.items():
            small_grads[n][l // 2] = g
    from_sibling = _run_comm(_sibling_task(list(pending[1].values()), [None] * len(pending[1])), name="reduce_sibling_l0")
    last_sums = {n: _sibling_sum(0, n, g, None, r1, place) for (n, g), r1 in zip(pending[1].items(), from_sibling)}
    from_chips = _run_comm(_chips_task([s[0] for s in last_sums.values()]), name="reduce_chips_l0")
    for (n, (_, own)), r2 in zip(last_sums.items(), from_chips):
        adamw(0, n, own, r2)
    results = {n: [as_updated(n, a) for a in results[n]] for n in big_names}
    grad_x = dy.reshape(x.shape)

    small_names = [n for n in WEIGHT_NAMES if n not in big_names]
    full_shapes = {"gla_w_gate_up": (2, GLA_GATE_RANK, GLA_DK), "gla_gate_bias": (2, GLA_DK), "gla_norm_g": (2, GLA_HEAD_V),
                   "ffn_conv_w": (DEPTH, 3, 2 * D_FF), "ffn_conv_b": (DEPTH, 2 * D_FF),
                   "ln_g": (DEPTH, 2, D_MODEL), "ln_b": (DEPTH, 2, D_MODEL)}
    shapes = [full_shapes[n] for n in small_names]
    rows = _rows_for(shapes)
    packed = _pack_rows([jnp.stack(small_grads[n]) for n in small_names], rows)
    summed = _sum_gathered(_run_comm(_gather_task([packed]), name="gather_small_grads")[0], name="sum_small_grads")
    full = dict(zip(small_names, _unpack_rows(summed, shapes)))
    own = {n: (full[n] if w[n].shape == full[n].shape
               else lax.dynamic_slice_in_dim(full[n], dev * w[n].shape[-1], w[n].shape[-1], axis=full[n].ndim - 1))
           for n in small_names}
    own_shapes = [w[n].shape for n in small_names]
    rows = _rows_for(own_shapes)
    pk = lambda d: _pack_rows([d[n] for n in small_names], rows)
    outs = _adamw_small(pk(own), pk(w), pk(mom), pk(var), name="adamw_small")
    for n in small_names:
        results[n] = [own[n]]
    for k, packed_out in enumerate(outs):
        for n, a in zip(small_names, _unpack_rows(packed_out, own_shapes)):
            results[n].append(a)

    return (loss, grad_x) + tuple(results[n][k] for k in range(4) for n in WEIGHT_NAMES)
```

```python
import functools
import math

import jax
import jax.numpy as jnp
from jax import lax
from jax.experimental import pallas as pl
from jax.experimental.pallas import tpu as pltpu

F32 = jnp.float32
BF16 = jnp.bfloat16

D_MODEL = 2048
SEQ = 2048
DEPTH = 4
N_DEV = 8
GLA_HEADS = 4
GLA_DK = 1024
GLA_DV = 2048
GLA_HEAD_K = 256
GLA_HEAD_V = 512
GLA_GATE_RANK = 16
GLA_GATE_NORMALIZER = 16.0
GLA_CHUNK = 64
GLA_MAIN = 2 * GLA_DK + 2 * GLA_DV
GLA_IN = GLA_MAIN + GLA_GATE_RANK
DIL_PATTERNS = ((128, 1), (512, 4), (2048, 16))
DIL_HEADS = 8
DIL_HEAD_DIM = 128
DIL_WIDTH = DIL_HEADS * DIL_HEAD_DIM
DIL_BLOCK = 128
DIL_IN = 3 * len(DIL_PATTERNS) * DIL_WIDTH
D_FF = 5504
DEEPNORM_ALPHA = (2 * DEPTH) ** 0.25
LN_EPS = 1e-5
RMS_EPS = 1e-6
ADAM_LR = 0.001
ADAM_B1 = 0.9
ADAM_B2 = 0.999
ADAM_EPS = 1e-08
ADAM_WD = 0.01
ADAM_STEP = 10

LANES = 128
SUBLANES = 8
VMEM_LIMIT_BYTES = 56 * 1024 * 1024

FF_SHARD = 2 * D_FF // N_DEV
FF_SHARD_PAD = 1408
FF_HALF_PAD = 4 * FF_SHARD_PAD
GLOW_PAD = LANES


def _cparams(dims=None):
    return pltpu.CompilerParams(dimension_semantics=dims, vmem_limit_bytes=VMEM_LIMIT_BYTES)


class _Comm:
    def __init__(self, inputs, out_shapes, sem_shapes, start, middle, finish):
        self.inputs, self.out_shapes, self.sem_shapes = list(inputs), list(out_shapes), list(sem_shapes)
        self.start, self.middle, self.finish = start, middle, finish


def _join_comm(comms):
    def cut(refs, counts):
        out, off = [], 0
        for c in counts:
            out.append(refs[off:off + c])
            off += c
        return out

    n_in = [len(c.inputs) for c in comms]
    n_out = [len(c.out_shapes) for c in comms]
    n_sem = [len(c.sem_shapes) for c in comms]

    def hook(which):
        def run(ins, outs, sems):
            for c, i, o, s in zip(comms, cut(ins, n_in), cut(outs, n_out), cut(sems, n_sem)):
                getattr(c, which)(i, o, s)
        return run

    return _Comm([a for c in comms for a in c.inputs], [s for c in comms for s in c.out_shapes],
                 [s for c in comms for s in c.sem_shapes], hook("start"), hook("middle"), hook("finish"))


def _mm(a, b, *, grid, a_spec, b_spec, o_spec, out_shape, acc_shape, ta=False, tb=False, res=None, res_scale=1.0,
        comm=None, name):
    nk = grid[2]
    dims = (((0 if ta else 1,), (1 if tb else 0,)), ((), ()))
    has_res = res is not None
    n_in = 2 + has_res
    n_cin = len(comm.inputs) if comm else 0
    n_cout = len(comm.out_shapes) if comm else 0

    def body(*refs):
        a_ref, b_ref = refs[0], refs[1]
        res_ref = refs[2] if has_res else None
        o_ref = refs[n_in + n_cin]
        scratch = refs[n_in + n_cin + 1 + n_cout:]
        acc_ref = scratch[0] if nk > 1 else None
        if comm:
            task = (refs[n_in:n_in + n_cin], refs[n_in + n_cin + 1:n_in + n_cin + 1 + n_cout],
                    scratch[1:] if nk > 1 else scratch)
            step = (pl.program_id(0) * grid[1] + pl.program_id(1)) * nk + pl.program_id(2)
            steps = grid[0] * grid[1] * nk

            @pl.when(step == 0)
            def _():
                comm.start(*task)

        p = lax.dot_general(a_ref[...], b_ref[...], dims, preferred_element_type=F32)

        def finish(acc):
            if has_res:
                acc = acc + res_scale * res_ref[...]
            o_ref[...] = acc.astype(o_ref.dtype)

        if nk == 1:
            finish(p)
        else:
            kk = pl.program_id(2)

            @pl.when(kk == 0)
            def _():
                acc_ref[...] = p

            @pl.when(kk > 0)
            def _():
                acc_ref[...] += p

            @pl.when(kk == nk - 1)
            def _():
                finish(acc_ref[...])

        if comm:
            @pl.when(step == steps - 1)
            def _():
                comm.middle(*task)
                comm.finish(*task)

    in_specs = [a_spec, b_spec] + ([o_spec] if has_res else [])
    args = (a, b) + ((res,) if has_res else ())
    acc = [pltpu.VMEM(acc_shape, F32)] if nk > 1 else []
    if not comm:
        return pl.pallas_call(
            body, name=name, grid=grid, in_specs=in_specs, out_specs=o_spec, out_shape=out_shape, scratch_shapes=acc,
            compiler_params=_cparams(("parallel", "parallel", "arbitrary")),
        )(*args)
    outs = pl.pallas_call(
        body, name=name, grid=grid, in_specs=in_specs + [_ANY] * n_cin, out_specs=[o_spec] + [_ANY] * n_cout,
        out_shape=[out_shape] + comm.out_shapes, scratch_shapes=acc + comm.sem_shapes,
        compiler_params=_cparams(("arbitrary", "arbitrary", "arbitrary")),
    )(*args, *comm.inputs)
    return outs[0], list(outs[1:])


def _matmul(a, b, *, ta=False, tb=False, tm, tn, tk, out_dtype=F32, res=None, res_scale=1.0, comm=None, name):
    m, k = (a.shape[1], a.shape[0]) if ta else a.shape
    n = b.shape[0] if tb else b.shape[1]
    assert (b.shape[1] if tb else b.shape[0]) == k
    assert m % tm == 0 and n % tn == 0 and k % tk == 0, (m, n, k, tm, tn, tk)
    a_spec = pl.BlockSpec((tk, tm), lambda i, j, kk: (kk, i)) if ta else pl.BlockSpec((tm, tk), lambda i, j, kk: (i, kk))
    b_spec = pl.BlockSpec((tn, tk), lambda i, j, kk: (j, kk)) if tb else pl.BlockSpec((tk, tn), lambda i, j, kk: (kk, j))
    return _mm(a, b, grid=(m // tm, n // tn, k // tk), a_spec=a_spec, b_spec=b_spec,
               o_spec=pl.BlockSpec((tm, tn), lambda i, j, kk: (i, j)), out_shape=jax.ShapeDtypeStruct((m, n), out_dtype),
               acc_shape=(tm, tn), ta=ta, tb=tb, res=res, res_scale=res_scale, comm=comm, name=name)


MM_ROWS = 1024


def _mm_colblocks(a, wb, *, comm=None, name):
    m, k = a.shape
    nb, _, w = wb.shape
    return _mm(a, wb, grid=(m // MM_ROWS, nb, 1),
               a_spec=pl.BlockSpec((MM_ROWS, k), lambda i, j, kk: (i, 0)),
               b_spec=pl.BlockSpec((None, k, w), lambda i, j, kk: (j, 0, 0)),
               o_spec=pl.BlockSpec((MM_ROWS, w), lambda i, j, kk: (i, j)),
               out_shape=jax.ShapeDtypeStruct((m, nb * w), F32), acc_shape=(MM_ROWS, w), comm=comm, name=name)


def _mm_colblocks_t(a, wb, res, res_scale, *, comm=None, name):
    m = a.shape[0]
    nb, n, w = wb.shape
    tn = 1024
    return _mm(a, wb, grid=(m // MM_ROWS, n // tn, nb),
               a_spec=pl.BlockSpec((MM_ROWS, w), lambda i, j, kk: (i, kk)),
               b_spec=pl.BlockSpec((None, tn, w), lambda i, j, kk: (kk, j, 0)),
               o_spec=pl.BlockSpec((MM_ROWS, tn), lambda i, j, kk: (i, j)),
               out_shape=jax.ShapeDtypeStruct((m, n), F32), acc_shape=(MM_ROWS, tn), tb=True,
               res=res, res_scale=res_scale, comm=comm, name=name)


def _mm_grad_colblocks(x, dy, w, *, comm=None, name):
    t, k = x.shape
    nb = dy.shape[1] // w
    tm, tk = 1024, 2048
    return _mm(x, dy, grid=(k // tm, nb, t // tk),
               a_spec=pl.BlockSpec((tk, tm), lambda i, j, kk: (kk, i)),
               b_spec=pl.BlockSpec((tk, w), lambda i, j, kk: (kk, j)),
               o_spec=pl.BlockSpec((None, tm, w), lambda i, j, kk: (j, i, 0)),
               out_shape=jax.ShapeDtypeStruct((nb, k, w), F32), acc_shape=(tm, w), ta=True, comm=comm, name=name)


def _ffn_hidden(y, wt, *, comm=None, name):
    t, k = y.shape
    ni = t // MM_ROWS
    return _mm(y, wt, grid=(ni, N_DEV, 1),
               a_spec=pl.BlockSpec((MM_ROWS, k), lambda i, j, kk: (i, 0)),
               b_spec=pl.BlockSpec((None, FF_SHARD_PAD, k), lambda i, j, kk: (j, 0, 0)),
               o_spec=pl.BlockSpec((MM_ROWS, FF_SHARD_PAD), lambda i, j, kk: ((j // 4) * ni + i, j % 4)),
               out_shape=jax.ShapeDtypeStruct((2 * t, FF_HALF_PAD), F32), acc_shape=(MM_ROWS, FF_SHARD_PAD), tb=True,
               comm=comm, name=name)


def _ffn_hidden_dy(dh, wt, res, res_scale, *, comm=None, name):
    t = dh.shape[0] // 2
    ni, tn = t // MM_ROWS, 1024
    return _mm(dh, wt, grid=(ni, D_MODEL // tn, N_DEV),
               a_spec=pl.BlockSpec((MM_ROWS, FF_SHARD_PAD), lambda i, j, kk: ((kk // 4) * ni + i, kk % 4)),
               b_spec=pl.BlockSpec((None, FF_SHARD_PAD, tn), lambda i, j, kk: (kk, 0, j)),
               o_spec=pl.BlockSpec((MM_ROWS, tn), lambda i, j, kk: (i, j)),
               out_shape=jax.ShapeDtypeStruct((t, D_MODEL), F32), acc_shape=(MM_ROWS, tn),
               res=res, res_scale=res_scale, comm=comm, name=name)


def _ffn_hidden_dw(dh, y, *, comm=None, name):
    t, k = y.shape
    tk, tn = 2048, 1024
    nk = t // tk
    return _mm(dh, y, grid=(N_DEV, k // tn, nk),
               a_spec=pl.BlockSpec((tk, FF_SHARD_PAD), lambda i, j, kk: ((i // 4) * nk + kk, i % 4)),
               b_spec=pl.BlockSpec((tk, tn), lambda i, j, kk: (kk, j)),
               o_spec=pl.BlockSpec((None, FF_SHARD_PAD, tn), lambda i, j, kk: (i, 0, j)),
               out_shape=jax.ShapeDtypeStruct((N_DEV, FF_SHARD_PAD, k), F32), acc_shape=(FF_SHARD_PAD, tn), ta=True,
               comm=comm, name=name)


def _ffn_down_dw(act, dz, *, comm=None, name):
    t, k = dz.shape
    tk, tn = 2048, 1024
    return _mm(act, dz, grid=(4, k // tn, t // tk),
               a_spec=pl.BlockSpec((tk, FF_SHARD_PAD), lambda i, j, kk: (kk, i)),
               b_spec=pl.BlockSpec((tk, tn), lambda i, j, kk: (kk, j)),
               o_spec=pl.BlockSpec((None, FF_SHARD_PAD, tn), lambda i, j, kk: (i, 0, j)),
               out_shape=jax.ShapeDtypeStruct((4, FF_SHARD_PAD, k), F32), acc_shape=(FF_SHARD_PAD, tn), ta=True,
               comm=comm, name=name)


LN_ROWS = 256


def _ln_fwd(x, f, g, b, *, name):
    t, d = x.shape

    def body(x_ref, f_ref, g_ref, b_ref, y_ref, yb_ref, xh_ref, rs_ref):
        z = DEEPNORM_ALPHA * x_ref[...] + f_ref[...]
        mu = jnp.mean(z, axis=-1, keepdims=True)
        zc = z - mu
        var = jnp.mean(zc * zc, axis=-1, keepdims=True)
        rstd = lax.rsqrt(var + LN_EPS)
        xh = zc * rstd
        y = xh * g_ref[...] + b_ref[...]
        y_ref[...] = y
        yb_ref[...] = y.astype(BF16)
        xh_ref[...] = xh
        rs_ref[...] = rstd

    row = pl.BlockSpec((LN_ROWS, d), lambda i: (i, 0))
    vec = pl.BlockSpec((1, d), lambda i: (0, 0))
    return pl.pallas_call(
        body,
        name=name,
        grid=(t // LN_ROWS,),
        in_specs=[row, row, vec, vec],
        out_specs=[row, row, row, pl.BlockSpec((LN_ROWS, 1), lambda i: (i, 0))],
        out_shape=[jax.ShapeDtypeStruct((t, d), F32), jax.ShapeDtypeStruct((t, d), BF16),
                   jax.ShapeDtypeStruct((t, d), F32), jax.ShapeDtypeStruct((t, 1), F32)],
        compiler_params=_cparams(("parallel",)),
    )(x, f, g, b)


def _ln_bwd(dy, xhat, rstd, g, *, name):
    t, d = dy.shape

    def body(dy_ref, xh_ref, rs_ref, g_ref, dz_ref, dzb_ref, dg_ref, db_ref):
        dyv = dy_ref[...]
        xh = xh_ref[...]
        dyg = dyv * g_ref[...]
        m1 = jnp.mean(dyg, axis=-1, keepdims=True)
        m2 = jnp.mean(dyg * xh, axis=-1, keepdims=True)
        dz = rs_ref[...] * (dyg - m1 - xh * m2)
        dz_ref[...] = dz
        dzb_ref[...] = dz.astype(BF16)
        dg_part = jnp.sum(dyv * xh, axis=0, keepdims=True)
        db_part = jnp.sum(dyv, axis=0, keepdims=True)

        @pl.when(pl.program_id(0) == 0)
        def _():
            dg_ref[...] = dg_part
            db_ref[...] = db_part

        @pl.when(pl.program_id(0) > 0)
        def _():
            dg_ref[...] += dg_part
            db_ref[...] += db_part

    row = pl.BlockSpec((LN_ROWS, d), lambda i: (i, 0))
    vec = pl.BlockSpec((1, d), lambda i: (0, 0))
    return pl.pallas_call(
        body,
        name=name,
        grid=(t // LN_ROWS,),
        in_specs=[row, row, pl.BlockSpec((LN_ROWS, 1), lambda i: (i, 0)), vec],
        out_specs=[row, row, vec, vec],
        out_shape=[jax.ShapeDtypeStruct((t, d), F32), jax.ShapeDtypeStruct((t, d), BF16),
                   jax.ShapeDtypeStruct((1, d), F32), jax.ShapeDtypeStruct((1, d), F32)],
        compiler_params=_cparams(("arbitrary",)),
    )(dy, xhat, rstd, g)


def _loss_fwd_bwd(y, target, *, name):
    t, d = y.shape

    def body(y_ref, t_ref, loss_ref, dy_ref):
        err = y_ref[...] - t_ref[...]
        dy_ref[...] = err * (1.0 / d)
        part = 0.5 * jnp.sum(jnp.mean(err * err, axis=-1, keepdims=True), axis=0, keepdims=True)

        @pl.when(pl.program_id(0) == 0)
        def _():
            loss_ref[...] = part

        @pl.when(pl.program_id(0) > 0)
        def _():
            loss_ref[...] += part

    row = pl.BlockSpec((LN_ROWS, d), lambda i: (i, 0))
    return pl.pallas_call(
        body,
        name=name,
        grid=(t // LN_ROWS,),
        in_specs=[row, row],
        out_specs=[pl.BlockSpec((1, 1), lambda i: (0, 0)), row],
        out_shape=[jax.ShapeDtypeStruct((1, 1), F32), jax.ShapeDtypeStruct((t, d), F32)],
        compiler_params=_cparams(("arbitrary",)),
    )(y, target)


FFN_COLS = 256


def _shift_rows(h, s):
    rows = lax.broadcasted_iota(jnp.int32, h.shape, 0)
    return jnp.where(rows >= s, pltpu.roll(h, s, 0), 0.0)


def _shift_rows_up(h, s):
    n = h.shape[0]
    rows = lax.broadcasted_iota(jnp.int32, h.shape, 0)
    return jnp.where(rows < n - s, pltpu.roll(h, n - s, 0), 0.0)


def _causal_conv(h, w, b):
    return w[0:1, :] * _shift_rows(h, 2) + w[1:2, :] * _shift_rows(h, 1) + w[2:3, :] * h + b


def _sigmoid(x):
    return 1.0 / (1.0 + jnp.exp(-x))


def _convgate_fwd(h, cw, cb, *, name):
    t, n = h.shape[0] // 2, h.shape[1]
    nb = t // SEQ

    def body(hg_ref, hu_ref, wg_ref, wu_ref, bg_ref, bu_ref, a_ref):
        gate = _causal_conv(hg_ref[...], wg_ref[...], bg_ref[...])
        up = _causal_conv(hu_ref[...], wu_ref[...], bu_ref[...])
        a_ref[...] = (gate * _sigmoid(gate) * up).astype(BF16)

    def half(rows, k):
        return pl.BlockSpec((None, rows, FFN_COLS), lambda s, j: (k, 0, j))

    return pl.pallas_call(
        body,
        name=name,
        grid=(nb, n // FFN_COLS),
        in_specs=[pl.BlockSpec((SEQ, FFN_COLS), lambda s, j: (s, j)), pl.BlockSpec((SEQ, FFN_COLS), lambda s, j: (nb + s, j)),
                  half(3, 0), half(3, 1), half(1, 0), half(1, 1)],
        out_specs=pl.BlockSpec((SEQ, FFN_COLS), lambda s, j: (s, j)),
        out_shape=jax.ShapeDtypeStruct((t, n), BF16),
        compiler_params=_cparams(("parallel", "parallel")),
    )(h, h, cw, cw, cb, cb)


def _convgate_bwd(h, dact, cw, cb, *, name):
    t, n = h.shape[0] // 2, h.shape[1]
    nb = t // SEQ

    def body(hg_ref, hu_ref, da_ref, wg_ref, wu_ref, bg_ref, bu_ref, dh_ref, dw_ref, db_ref):
        dhg_ref, dhu_ref = dh_ref.at[0], dh_ref.at[1]
        dwg_ref, dwu_ref = dw_ref.at[0], dw_ref.at[1]
        dbg_ref, dbu_ref = db_ref.at[0], db_ref.at[1]
        hgv, huv = hg_ref[...], hu_ref[...]
        wgv, wuv = wg_ref[...], wu_ref[...]
        gate = _causal_conv(hgv, wgv, bg_ref[...])
        up = _causal_conv(huv, wuv, bu_ref[...])
        sg = _sigmoid(gate)
        da = da_ref[...]
        dgate = da * up * (sg * (1.0 + gate * (1.0 - sg)))
        dup = da * (gate * sg)

        def conv_bwd(dc, h, w, dh_ref, dw_ref, db_ref):
            dh = w[2:3, :] * dc + w[1:2, :] * _shift_rows_up(dc, 1) + w[0:1, :] * _shift_rows_up(dc, 2)
            dh_ref[...] = dh.astype(BF16)
            dws = [jnp.sum(dc * _shift_rows(h, 2), axis=0, keepdims=True),
                   jnp.sum(dc * _shift_rows(h, 1), axis=0, keepdims=True),
                   jnp.sum(dc * h, axis=0, keepdims=True)]
            db = jnp.sum(dc, axis=0, keepdims=True)

            @pl.when(pl.program_id(1) == 0)
            def _():
                for r in range(3):
                    dw_ref[r:r + 1, :] = dws[r]
                db_ref[...] = db

            @pl.when(pl.program_id(1) > 0)
            def _():
                for r in range(3):
                    dw_ref[r:r + 1, :] += dws[r]
                db_ref[...] += db

        conv_bwd(dgate, hgv, wgv, dhg_ref, dwg_ref, dbg_ref)
        conv_bwd(dup, huv, wuv, dhu_ref, dwu_ref, dbu_ref)

    def half(rows, k):
        return pl.BlockSpec((None, rows, FFN_COLS), lambda j, s: (k, 0, j))

    def both(rows):
        return pl.BlockSpec((2, rows, FFN_COLS), lambda j, s: (0, 0, j))

    return pl.pallas_call(
        body,
        name=name,
        grid=(n // FFN_COLS, nb),
        in_specs=[pl.BlockSpec((SEQ, FFN_COLS), lambda j, s: (s, j)), pl.BlockSpec((SEQ, FFN_COLS), lambda j, s: (nb + s, j)),
                  pl.BlockSpec((SEQ, FFN_COLS), lambda j, s: (s, j)), half(3, 0), half(3, 1), half(1, 0), half(1, 1)],
        out_specs=[pl.BlockSpec((2, SEQ, FFN_COLS), lambda j, s: (0, s, j)), both(3), both(1)],
        out_shape=[jax.ShapeDtypeStruct((2, t, n), BF16), jax.ShapeDtypeStruct((2, 3, n), F32),
                   jax.ShapeDtypeStruct((2, 1, n), F32)],
        compiler_params=_cparams(("parallel", "arbitrary")),
    )(h, h, dact, cw, cw, cb, cb)


GLA_Q_SCALE = GLA_HEAD_K ** -0.5
GLA_NC = SEQ // GLA_CHUNK
_NT = (((1,), (1,)), ((), ()))
_TN = (((0,), (0,)), ((), ()))


def _cumsum_rows(g):
    n = g.shape[0]
    rows = lax.broadcasted_iota(jnp.int32, g.shape, 0)
    s = 1
    while s < n:
        g = g + jnp.where(rows >= s, pltpu.roll(g, s, 0), 0.0)
        s *= 2
    return g


def _suffix_sum_rows(x):
    n = x.shape[0]
    rows = lax.broadcasted_iota(jnp.int32, x.shape, 0)
    s = 1
    while s < n:
        x = x + jnp.where(rows < n - s, pltpu.roll(x, n - s, 0), 0.0)
        s *= 2
    return x


def _gla_log_gate(gl_ref, wgu_ref, bias_ref):
    pre = jnp.dot(gl_ref[...].astype(BF16), wgu_ref[...], preferred_element_type=F32) + bias_ref[...]
    log_sig = jnp.minimum(pre, 0.0) - jnp.log(1.0 + jnp.exp(-jnp.abs(pre)))
    return pre, log_sig * (1.0 / GLA_GATE_NORMALIZER)


def _pair_rows(j):
    return (j // SUBLANES) * SUBLANES


def _gla_pair_fwd(q_scr, k_ref, b_scr, a_scr, h):
    c = GLA_CHUNK
    kc = pl.ds(h * GLA_HEAD_K, GLA_HEAD_K)
    a_scr[...] = jnp.zeros(a_scr.shape, F32)
    lane = lax.broadcasted_iota(jnp.int32, (1, c), 1)
    for j in range(c):
        r0 = _pair_rows(j)
        rs = pl.ds(r0, c - r0)
        rows = lax.broadcasted_iota(jnp.int32, (c - r0, 1), 0) + r0
        e = jnp.exp2(jnp.minimum(b_scr[rs, kc] - b_scr[pl.ds(j, 1), kc], 0.0))
        w = q_scr[rs, kc] * k_ref[pl.ds(j, 1), kc] * e
        col = jnp.where(rows >= j, jnp.sum(w, axis=-1, keepdims=True), 0.0)
        a_scr[rs, :] += col * (lane == j).astype(F32)


def _call_with_comm(body, comm, *, n_in, n_out, grid, in_specs, out_specs, out_shape, scratch_shapes, name, args):
    n_cin, n_cout, n_scr = len(comm.inputs), len(comm.out_shapes), len(scratch_shapes)

    def carrier(*refs):
        ins, cins = refs[:n_in], refs[n_in:n_in + n_cin]
        outs, couts = refs[n_in + n_cin:n_in + n_cin + n_out], refs[n_in + n_cin + n_out:n_in + n_cin + n_out + n_cout]
        scr = refs[n_in + n_cin + n_out + n_cout:]
        task = (cins, couts, scr[n_scr:])
        step, steps = 0, 1
        for axis, size in enumerate(grid):
            step = step * size + pl.program_id(axis)
            steps *= size

        @pl.when(step == 0)
        def _():
            comm.start(*task)

        body(*ins, *outs, *scr[:n_scr])

        @pl.when(step == steps - 1)
        def _():
            comm.middle(*task)
            comm.finish(*task)

    res = pl.pallas_call(
        carrier, name=name, grid=grid, in_specs=list(in_specs) + [_ANY] * n_cin, out_specs=list(out_specs) + [_ANY] * n_cout,
        out_shape=list(out_shape) + comm.out_shapes, scratch_shapes=list(scratch_shapes) + comm.sem_shapes,
        compiler_params=_cparams(("arbitrary",) * len(grid)),
    )(*args, *comm.inputs)
    return list(res[:n_out]), list(res[n_out:])


LOG2_E = 1.4426950408889634


def _gla_fwd(proj, wgu, bias, ng, *, comm=None, name):
    t = proj.shape[0]
    nb, nc, c = t // SEQ, GLA_NC, GLA_CHUNK

    def body(q_ref, k_ref, v_ref, r_ref, gl_ref, wgu_ref, bias_ref, ng_ref,
             y_ref, o_ref, a_ref, st_ref, state, b_scr, a_scr, q_scr):
        @pl.when(pl.program_id(1) == 0)
        def _():
            state[...] = jnp.zeros(state.shape, F32)

        _, g = _gla_log_gate(gl_ref, wgu_ref, bias_ref)
        b_scr[...] = _cumsum_rows(g) * LOG2_E
        q_scr[...] = q_ref[...] * GLA_Q_SCALE
        for h in range(GLA_HEADS):
            kc = pl.ds(h * GLA_HEAD_K, GLA_HEAD_K)
            vc = pl.ds(h * GLA_HEAD_V, GLA_HEAD_V)
            qh = q_scr[:, kc]
            kh = k_ref[:, kc]
            vh = v_ref[:, vc].astype(BF16)
            bh = b_scr[:, kc]
            blast = b_scr[pl.ds(c - 1, 1), kc]
            st = state[h]
            st_ref[h] = st
            o_inter = lax.dot_general((qh * jnp.exp2(bh)).astype(BF16), st.astype(BF16), _NT, preferred_element_type=F32)
            _gla_pair_fwd(q_scr, k_ref, b_scr, a_scr, h)
            a = a_scr[...]
            a_ref[h] = a
            o = o_inter + jnp.dot(a.astype(BF16), vh, preferred_element_type=F32)
            kd = (kh * jnp.exp2(blast - bh)).astype(BF16)
            state[h] = st * jnp.exp2(blast) + lax.dot_general(vh, kd, _TN, preferred_element_type=F32)
            o_ref[:, vc] = o
            rs = lax.rsqrt(jnp.mean(o * o, axis=-1, keepdims=True) + RMS_EPS)
            rh = r_ref[:, vc]
            y_ref[:, vc] = ((o * rs * ng_ref[...]) * (rh * _sigmoid(rh))).astype(BF16)

    def tok(width, col):
        return pl.BlockSpec((c, width), lambda b, i: (b * nc + i, col))

    whole = lambda shape: pl.BlockSpec(shape, lambda b, i: (0,) * len(shape))
    call = dict(
        grid=(nb, nc),
        in_specs=[tok(GLA_DK, 0), tok(GLA_DK, 1), tok(GLA_DV, 1), tok(GLA_DV, 2), tok(GLOW_PAD, GLA_MAIN // GLOW_PAD),
                  whole((GLOW_PAD, GLA_DK)), whole((1, GLA_DK)), whole((1, GLA_HEAD_V))],
        out_specs=[tok(GLA_DV, 0), tok(GLA_DV, 0),
                   pl.BlockSpec((GLA_HEADS, c, c), lambda b, i: (0, b * nc + i, 0)),
                   pl.BlockSpec((None, GLA_HEADS, GLA_HEAD_V, GLA_HEAD_K), lambda b, i: (b * nc + i, 0, 0, 0))],
        out_shape=[jax.ShapeDtypeStruct((t, GLA_DV), BF16), jax.ShapeDtypeStruct((t, GLA_DV), F32),
                   jax.ShapeDtypeStruct((GLA_HEADS, t, c), F32),
                   jax.ShapeDtypeStruct((t // c, GLA_HEADS, GLA_HEAD_V, GLA_HEAD_K), F32)],
        scratch_shapes=[pltpu.VMEM((GLA_HEADS, GLA_HEAD_V, GLA_HEAD_K), F32), pltpu.VMEM((c, GLA_DK), F32),
                        pltpu.VMEM((c, c), F32), pltpu.VMEM((c, GLA_DK), F32)],
        name=name)
    args = (proj, proj, proj, proj, proj, wgu, bias, ng)
    if comm is None:
        return pl.pallas_call(body, compiler_params=_cparams(("parallel", "arbitrary")), **call)(*args)
    return _call_with_comm(body, comm, n_in=8, n_out=4, args=args, **call)


def _gla_pair_bwd(q_scr, k_ref, b_scr, da_scr, dq_scr, dk_scr, h):
    c = GLA_CHUNK
    kc = pl.ds(h * GLA_HEAD_K, GLA_HEAD_K)
    lane = lax.broadcasted_iota(jnp.int32, (1, c), 1)
    for j in range(c):
        r0 = _pair_rows(j)
        rs = pl.ds(r0, c - r0)
        rows = lax.broadcasted_iota(jnp.int32, (c - r0, 1), 0) + r0
        e = jnp.exp2(jnp.minimum(b_scr[rs, kc] - b_scr[pl.ds(j, 1), kc], 0.0))
        dacol = jnp.sum(jnp.where(lane == j, da_scr[rs, :], 0.0), axis=-1, keepdims=True)
        t1 = jnp.where(rows >= j, dacol, 0.0) * e
        dq_scr[rs, kc] += t1 * k_ref[pl.ds(j, 1), kc]
        dk_scr[pl.ds(j, 1), kc] += jnp.sum(t1 * q_scr[rs, kc], axis=0, keepdims=True)


def _gla_bwd(proj, wgu, bias, ng, o, a, states, dy, *, comm=None, name):
    t = proj.shape[0]
    nb, nc, c = t // SEQ, GLA_NC, GLA_CHUNK

    def body(q_ref, k_ref, v_ref, r_ref, gl_ref, wgu_ref, bias_ref, ng_ref, o_ref, a_ref, stp_ref, stn_ref, dy_ref,
             dq_ref, dk_ref, dv_ref, dr_ref, dgl_ref, dwgu_ref, dbias_ref, dng_ref,
             dstate, b_scr, da_scr, dq_scr, dk_scr, dg_scr, q_scr):
        first = jnp.logical_and(pl.program_id(0) == 0, pl.program_id(1) == 0)

        @pl.when(first)
        def _():
            dwgu_ref[...] = jnp.zeros(dwgu_ref.shape, F32)
            dbias_ref[...] = jnp.zeros(dbias_ref.shape, F32)
            dng_ref[...] = jnp.zeros(dng_ref.shape, F32)

        @pl.when(pl.program_id(1) == 0)
        def _():
            dstate[...] = jnp.zeros(dstate.shape, F32)

        pre, g = _gla_log_gate(gl_ref, wgu_ref, bias_ref)
        b_scr[...] = _cumsum_rows(g) * LOG2_E
        q_scr[...] = q_ref[...] * GLA_Q_SCALE
        ngv = ng_ref[...]
        tri = lax.broadcasted_iota(jnp.int32, (c, c), 0) >= lax.broadcasted_iota(jnp.int32, (c, c), 1)
        for h in range(GLA_HEADS):
            kc = pl.ds(h * GLA_HEAD_K, GLA_HEAD_K)
            vc = pl.ds(h * GLA_HEAD_V, GLA_HEAD_V)
            oh = o_ref[:, vc]
            rh = r_ref[:, vc]
            dyh = dy_ref[:, vc]
            rs = lax.rsqrt(jnp.mean(oh * oh, axis=-1, keepdims=True) + RMS_EPS)
            u = oh * rs
            sg = _sigmoid(rh)
            sr = rh * sg
            dr_ref[:, vc] = (dyh * (u * ngv) * (sg * (1.0 + rh * (1.0 - sg)))).astype(BF16)
            dng_ref[...] += jnp.sum(dyh * sr * u, axis=0, keepdims=True)
            du = dyh * sr * ngv
            do = (rs * (du - u * jnp.mean(du * u, axis=-1, keepdims=True))).astype(BF16)
            qh = q_scr[:, kc]
            kh = k_ref[:, kc]
            vh = v_ref[:, vc].astype(BF16)
            bh = b_scr[:, kc]
            blast = b_scr[pl.ds(c - 1, 1), kc]
            eb = jnp.exp2(bh)
            ek = jnp.exp2(blast - bh)
            dst = dstate[h]
            dst_b = dst.astype(BF16)
            dg_carry = jnp.sum(dst * stn_ref[h], axis=0, keepdims=True)
            da = lax.dot_general(do, vh, _NT, preferred_element_type=F32)
            da_scr[...] = jnp.where(tri, da, 0.0)
            dv = lax.dot_general(a_ref[h].astype(BF16), do, _TN, preferred_element_type=F32)
            dv = dv + lax.dot_general((kh * ek).astype(BF16), dst_b, _NT, preferred_element_type=F32)
            dv_ref[:, vc] = dv.astype(BF16)
            dq_scr[:, kc] = jnp.dot(do, stp_ref[h].astype(BF16), preferred_element_type=F32) * eb
            dk_scr[:, kc] = jnp.dot(vh, dst_b, preferred_element_type=F32) * ek
            _gla_pair_bwd(q_scr, k_ref, b_scr, da_scr, dq_scr, dk_scr, h)
            dq = dq_scr[:, kc]
            dk = dk_scr[:, kc]
            dg_scr[:, kc] = _suffix_sum_rows(qh * dq - kh * dk) + dg_carry
            dstate[h] = dst * jnp.exp2(blast) + lax.dot_general(do, (qh * eb).astype(BF16), _TN, preferred_element_type=F32)
        dq_ref[...] = (dq_scr[...] * GLA_Q_SCALE).astype(BF16)
        dk_ref[...] = dk_scr[...].astype(BF16)
        dpre = dg_scr[...] * ((1.0 - _sigmoid(pre)) * (1.0 / GLA_GATE_NORMALIZER))
        dpre_b = dpre.astype(BF16)
        dbias_ref[...] += jnp.sum(dpre, axis=0, keepdims=True)
        dwgu_ref[...] += lax.dot_general(gl_ref[...].astype(BF16), dpre_b, _TN, preferred_element_type=F32)
        dgl_ref[...] = lax.dot_general(dpre_b, wgu_ref[...], _NT, preferred_element_type=F32).astype(BF16)

    def chunk(b, i):
        return b * nc + (nc - 1 - i)

    def tok(width, col):
        return pl.BlockSpec((c, width), lambda b, i: (chunk(b, i), col))

    whole = lambda shape: pl.BlockSpec(shape, lambda b, i: (0,) * len(shape))
    st_shape = (None, GLA_HEADS, GLA_HEAD_V, GLA_HEAD_K)
    call = dict(
        name=name,
        grid=(nb, nc),
        in_specs=[tok(GLA_DK, 0), tok(GLA_DK, 1), tok(GLA_DV, 1), tok(GLA_DV, 2), tok(GLOW_PAD, GLA_MAIN // GLOW_PAD),
                  whole((GLOW_PAD, GLA_DK)), whole((1, GLA_DK)), whole((1, GLA_HEAD_V)),
                  tok(GLA_DV, 0),
                  pl.BlockSpec((GLA_HEADS, c, c), lambda b, i: (0, chunk(b, i), 0)),
                  pl.BlockSpec(st_shape, lambda b, i: (chunk(b, i), 0, 0, 0)),
                  pl.BlockSpec(st_shape, lambda b, i: (b * nc + jnp.minimum(nc - i, nc - 1), 0, 0, 0)),
                  tok(GLA_DV, 0)],
        out_specs=[tok(GLA_DK, 0), tok(GLA_DK, 0), tok(GLA_DV, 0), tok(GLA_DV, 0), tok(GLOW_PAD, 0),
                   whole((GLOW_PAD, GLA_DK)), whole((1, GLA_DK)), whole((1, GLA_HEAD_V))],
        out_shape=[jax.ShapeDtypeStruct((t, GLA_DK), BF16), jax.ShapeDtypeStruct((t, GLA_DK), BF16),
                   jax.ShapeDtypeStruct((t, GLA_DV), BF16), jax.ShapeDtypeStruct((t, GLA_DV), BF16),
                   jax.ShapeDtypeStruct((t, GLOW_PAD), BF16),
                   jax.ShapeDtypeStruct((GLOW_PAD, GLA_DK), F32), jax.ShapeDtypeStruct((1, GLA_DK), F32),
                   jax.ShapeDtypeStruct((1, GLA_HEAD_V), F32)],
        scratch_shapes=[pltpu.VMEM((GLA_HEADS, GLA_HEAD_V, GLA_HEAD_K), F32), pltpu.VMEM((c, GLA_DK), F32),
                        pltpu.VMEM((c, c), F32), pltpu.VMEM((c, GLA_DK), F32), pltpu.VMEM((c, GLA_DK), F32),
                        pltpu.VMEM((c, GLA_DK), F32), pltpu.VMEM((c, GLA_DK), F32)])
    args = (proj, proj, proj, proj, proj, wgu, bias, ng, o, a, states, states, dy)
    if comm is None:
        return pl.pallas_call(body, compiler_params=_cparams(("arbitrary", "arbitrary")), **call)(*args)
    return _call_with_comm(body, comm, n_in=13, n_out=8, args=args, **call)


DIL_STEPS = DIL_BLOCK
DIL_SCALE = DIL_HEAD_DIM ** -0.5
DIL_HEADS_PER_STEP = {1: 8, 4: 1, 16: 1}


def _dil_mask(i, with_prev):
    if not with_prev:
        return (lax.broadcasted_iota(jnp.int32, (DIL_BLOCK, DIL_BLOCK), 0)
                >= lax.broadcasted_iota(jnp.int32, (DIL_BLOCK, DIL_BLOCK), 1))
    rowi = lax.broadcasted_iota(jnp.int32, (DIL_BLOCK, 2 * DIL_BLOCK), 0)
    colj = lax.broadcasted_iota(jnp.int32, (DIL_BLOCK, 2 * DIL_BLOCK), 1)
    dist = rowi + DIL_BLOCK - colj
    band = jnp.logical_and(dist >= 0, dist <= DIL_STEPS)
    return jnp.logical_and(band, jnp.logical_or(i > 0, colj >= DIL_BLOCK))


def _dil_inputs(refs, with_prev):
    if not with_prev:
        q_ref, kc_ref, vc_ref = refs
        return q_ref, kc_ref, None, vc_ref, None
    return refs


def _dil_keys(prev_ref, cur_ref, rr, hc):
    if prev_ref is None:
        return cur_ref[rr, hc].astype(BF16)
    return jnp.concatenate([prev_ref[rr, hc], cur_ref[rr, hc]], axis=0).astype(BF16)


def _dil_geometry(t, gi):
    _, d = DIL_PATTERNS[gi]
    return d, SEQ // d // DIL_BLOCK, t // SEQ, DIL_BLOCK * d, DIL_HEADS_PER_STEP[d]


def _dil_specs(gi, d, nq, rows, hps, order, with_prev):
    width = hps * DIL_HEAD_DIM
    per_part = DIL_WIDTH // width

    def named(f):
        return lambda *idx: f(**dict(zip(order, idx)))

    def block(i, prev):
        ic = jnp.minimum(i, nq - 1)
        return jnp.maximum(ic - 1, 0) if prev else ic

    def part(j, prev):
        return pl.BlockSpec((rows, width), named(lambda b, i, h: (b * nq + block(i, prev), (gi * 3 + j) * per_part + h)))

    cur = pl.BlockSpec((rows, width), named(lambda b, i, h: (b * nq + block(i, False), h)))
    done = pl.BlockSpec((rows, width), named(lambda b, i, h: (b * nq + jnp.maximum(i - 1, 0), h)))
    parts = [part(0, False), part(1, False), part(1, True), part(2, False), part(2, True)]
    return (parts if with_prev else [parts[0], parts[1], parts[3]]), cur, done


def _dil_rows(r, d):
    return pl.ds(r, DIL_BLOCK, stride=d) if d > 1 else pl.ds(0, DIL_BLOCK)


def _dil_fwd(proj, gi, *, name):
    t = proj.shape[0]
    d, nq, nb, rows, hps = _dil_geometry(t, gi)

    def body(*refs):
        q_ref, kc_ref, kp_ref, vc_ref, vp_ref, o_ref, lse_ref = refs
        mask = _dil_mask(pl.program_id(1), True)
        for h in range(hps):
            hc = pl.ds(h * DIL_HEAD_DIM, DIL_HEAD_DIM)
            for r in range(d):
                rr = _dil_rows(r, d)
                qh = q_ref[rr, hc].astype(BF16)
                kcat = _dil_keys(kp_ref, kc_ref, rr, hc)
                vcat = _dil_keys(vp_ref, vc_ref, rr, hc)
                s = lax.dot_general(qh, kcat, _NT, preferred_element_type=F32) * DIL_SCALE
                s = jnp.where(mask, s, -jnp.inf)
                m = jnp.max(s, axis=-1, keepdims=True)
                p = jnp.exp(s - m)
                l = jnp.sum(p, axis=-1, keepdims=True)
                o_ref[rr, hc] = jnp.dot((p / l).astype(BF16), vcat, preferred_element_type=F32)
                lse_ref[rr, hc] = jnp.broadcast_to(m + jnp.log(l), (DIL_BLOCK, DIL_HEAD_DIM))

    parts, cur, _ = _dil_specs(gi, d, nq, rows, hps, "bih", True)
    return pl.pallas_call(
        body,
        name=name,
        grid=(nb, nq, DIL_HEADS // hps),
        in_specs=parts,
        out_specs=[cur, cur],
        out_shape=[jax.ShapeDtypeStruct((t, DIL_WIDTH), F32)] * 2,
        compiler_params=_cparams(("parallel", "parallel", "parallel")),
    )(*[proj] * len(parts))


def _dil_bwd(proj, gi, lse, do, delta, *, name):
    t = proj.shape[0]
    d, nq, nb, rows, hps = _dil_geometry(t, gi)
    with_prev = nq > 1

    def body(*refs):
        q_ref, kc_ref, kp_ref, vc_ref, vp_ref = _dil_inputs(refs[:-11], with_prev)
        lse_ref, do_ref, dl_ref, dq_ref, dk_ref, dv_ref, ck, cv, fk, fv, dq_s = refs[-11:]
        i = pl.program_id(2)

        @pl.when(i == 0)
        def _():
            ck[...] = jnp.zeros(ck.shape, F32)
            cv[...] = jnp.zeros(cv.shape, F32)

        @pl.when(i < nq)
        def _():
            mask = _dil_mask(i, with_prev)
            for h in range(hps):
                hc = pl.ds(h * DIL_HEAD_DIM, DIL_HEAD_DIM)
                h1 = pl.ds(h * DIL_HEAD_DIM, 1)
                for r in range(d):
                    rr = _dil_rows(r, d)
                    qh = q_ref[rr, hc].astype(BF16)
                    kcat = _dil_keys(kp_ref, kc_ref, rr, hc)
                    vcat = _dil_keys(vp_ref, vc_ref, rr, hc)
                    doh = do_ref[rr, hc].astype(BF16)
                    s = lax.dot_general(qh, kcat, _NT, preferred_element_type=F32) * DIL_SCALE
                    p = jnp.exp(jnp.where(mask, s, -jnp.inf) - lse_ref[rr, h1])
                    dp = lax.dot_general(doh, vcat, _NT, preferred_element_type=F32)
                    ds = (p * (dp + dl_ref[rr, h1]) * DIL_SCALE).astype(BF16)
                    dq_s[rr, hc] = jnp.dot(ds, kcat, preferred_element_type=F32)
                    dkcat = lax.dot_general(ds, qh, _TN, preferred_element_type=F32)
                    dvcat = lax.dot_general(p.astype(BF16), doh, _TN, preferred_element_type=F32)
                    if with_prev:
                        fk[rr, hc] = ck[rr, hc] + dkcat[:DIL_BLOCK]
                        fv[rr, hc] = cv[rr, hc] + dvcat[:DIL_BLOCK]
                    ck[rr, hc] = dkcat[-DIL_BLOCK:]
                    cv[rr, hc] = dvcat[-DIL_BLOCK:]
            dq_ref[...] = dq_s[...].astype(BF16)

            @pl.when(i > 0)
            def _():
                dk_ref[...] = fk[...].astype(BF16)
                dv_ref[...] = fv[...].astype(BF16)

        @pl.when(i == nq)
        def _():
            dk_ref[...] = ck[...].astype(BF16)
            dv_ref[...] = cv[...].astype(BF16)

    parts, cur, done = _dil_specs(gi, d, nq, rows, hps, "bhi", with_prev)
    shape = jax.ShapeDtypeStruct((t, DIL_WIDTH), BF16)
    tile = pltpu.VMEM((rows, hps * DIL_HEAD_DIM), F32)
    return pl.pallas_call(
        body,
        name=name,
        grid=(nb, DIL_HEADS // hps, nq + 1),
        in_specs=parts + [cur, cur, cur],
        out_specs=[cur, done, done],
        out_shape=[shape, shape, shape],
        scratch_shapes=[tile] * 5,
        compiler_params=_cparams(("parallel", "parallel", "arbitrary")),
    )(*[proj] * len(parts), lse, do, delta)


MIX_ROWS = 256


def _head_rowsum(x):
    parts = []
    for h in range(DIL_HEADS):
        s = jnp.sum(x[:, h * DIL_HEAD_DIM:(h + 1) * DIL_HEAD_DIM], axis=-1, keepdims=True)
        parts.append(jnp.broadcast_to(s, (x.shape[0], DIL_HEAD_DIM)))
    return jnp.concatenate(parts, axis=-1)


def _mix_weights(lse_refs):
    ls = [r[...] for r in lse_refs]
    m = jnp.maximum(jnp.maximum(ls[0], ls[1]), ls[2])
    es = [jnp.exp(l - m) for l in ls]
    inv = 1.0 / (es[0] + es[1] + es[2])
    return [e * inv for e in es]


def _dil_mix_fwd(os_, lses, *, name):
    t = os_[0].shape[0]

    def body(o0, o1, o2, l0, l1, l2, out_ref):
        w = _mix_weights((l0, l1, l2))
        out_ref[...] = (w[0] * o0[...] + w[1] * o1[...] + w[2] * o2[...]).astype(BF16)

    row = pl.BlockSpec((MIX_ROWS, DIL_WIDTH), lambda i: (i, 0))
    return pl.pallas_call(
        body, name=name, grid=(t // MIX_ROWS,), in_specs=[row] * 6, out_specs=row,
        out_shape=jax.ShapeDtypeStruct((t, DIL_WIDTH), BF16), compiler_params=_cparams(("parallel",)),
    )(*os_, *lses)


def _dil_mix_bwd(os_, lses, dout, *, name):
    t = os_[0].shape[0]

    def body(o0, o1, o2, l0, l1, l2, d_ref, do0, do1, do2, dl0, dl1, dl2):
        w = _mix_weights((l0, l1, l2))
        dv = d_ref[...]
        mix = w[0] * o0[...] + w[1] * o1[...] + w[2] * o2[...]
        bar = _head_rowsum(dv * mix)
        for wg, do_ref, dl_ref in zip(w, (do0, do1, do2), (dl0, dl1, dl2)):
            do_ref[...] = wg * dv
            dl_ref[...] = -wg * bar

    row = pl.BlockSpec((MIX_ROWS, DIL_WIDTH), lambda i: (i, 0))
    outs = pl.pallas_call(
        body, name=name, grid=(t // MIX_ROWS,), in_specs=[row] * 7, out_specs=[row] * 6,
        out_shape=[jax.ShapeDtypeStruct((t, DIL_WIDTH), F32)] * 6,
        compiler_params=_cparams(("parallel",)),
    )(*os_, *lses, dout)
    return outs[:3], outs[3:]


_MESH = pl.DeviceIdType.MESH
_ANY = pl.BlockSpec(memory_space=pl.ANY)


def _position():
    return lax.axis_index("x"), lax.axis_index("y"), lax.axis_index("c")


AG_COPIES = 7


def _run_comm(task, *, name):
    n_in, n_out = len(task.inputs), len(task.out_shapes)

    def body(*refs):
        parts = (refs[:n_in], refs[n_in:n_in + n_out], refs[n_in + n_out:])
        task.start(*parts)
        task.middle(*parts)
        task.finish(*parts)

    return pl.pallas_call(
        body, name=name, out_shape=task.out_shapes, in_specs=[_ANY] * n_in, out_specs=[_ANY] * n_out,
        scratch_shapes=task.sem_shapes,
    )(*task.inputs)


def _gather_task(shards):
    n = len(shards)

    def copies(x_refs, out_refs, sems):
        send_sems, recv_sems, local_sems = sems
        x, y, cc = _position()
        me, sibling = (x, y, cc), (x, y, 1 - cc)
        chips = [(1 - x, y), (x, 1 - y), (1 - x, 1 - y)]

        def copy(w, k, block, to, own=False):
            px, py, pc = block
            slot = out_refs[w].at[4 * px + 2 * py + pc]
            return pltpu.make_async_remote_copy(
                src_ref=x_refs[w] if own else slot, dst_ref=slot,
                send_sem=send_sems.at[AG_COPIES * w + k], recv_sem=recv_sems.at[AG_COPIES * w + k],
                device_id=to, device_id_type=_MESH)

        mine = [pltpu.make_async_copy(x_refs[w], out_refs[w].at[4 * x + 2 * y + cc], local_sems.at[w]) for w in range(n)]
        first = [[copy(w, 0, me, sibling, own=True)] + [copy(w, 1 + j, me, (*chip, cc), own=True) for j, chip in enumerate(chips)]
                 for w in range(n)]
        landed = [[copy(w, 1 + j, (*chip, cc), me) for j, chip in enumerate(chips)] for w in range(n)]
        passed = [[copy(w, 4 + j, (*chip, cc), sibling) for j, chip in enumerate(chips)] for w in range(n)]
        from_sibling = [[copy(w, 0, sibling, me)] + [copy(w, 4 + j, (*chip, 1 - cc), me) for j, chip in enumerate(chips)]
                        for w in range(n)]
        return mine, first, landed, passed, from_sibling

    def start(ins, outs, sems):
        mine, first, _, _, _ = copies(ins, outs, sems)
        for w in range(n):
            mine[w].start()
            for cp in first[w]:
                cp.start()

    def middle(ins, outs, sems):
        _, _, landed, passed, _ = copies(ins, outs, sems)
        for j in range(3):
            for w in range(n):
                landed[w][j].wait_recv()
                passed[w][j].start()

    def finish(ins, outs, sems):
        mine, first, _, passed, from_sibling = copies(ins, outs, sems)
        for w in range(n):
            for cp in from_sibling[w]:
                cp.wait_recv()
        for w in range(n):
            for cp in first[w] + passed[w]:
                cp.wait_send()
            mine[w].wait()

    return _Comm(shards, [jax.ShapeDtypeStruct((N_DEV,) + s.shape, s.dtype) for s in shards],
                 [pltpu.SemaphoreType.DMA((AG_COPIES * n,)), pltpu.SemaphoreType.DMA((AG_COPIES * n,)),
                  pltpu.SemaphoreType.DMA((n,))], start, middle, finish)


def _parity_half(ref, parity, half_rows):
    if half_rows is None:
        return ref.at[:, parity]
    return ref.at[:, pl.ds(parity * half_rows, half_rows), :]


def _exchange_task(make_copies, inputs, out_shapes, n_copies):
    def start(ins, outs, sems):
        for cp in make_copies(ins, outs, sems):
            cp.start()

    def finish(ins, outs, sems):
        for cp in make_copies(ins, outs, sems):
            cp.wait()

    return _Comm(inputs, out_shapes, [pltpu.SemaphoreType.DMA((n_copies,)), pltpu.SemaphoreType.DMA((n_copies,))],
                 start, lambda ins, outs, sems: None, finish)


def _sibling_task(gs, half_rows):
    n = len(gs)

    def make_copies(g_refs, out_refs, sems):
        x, y, cc = _position()
        return [pltpu.make_async_remote_copy(
            src_ref=_parity_half(g_refs[k], 1 - cc, half_rows[k]), dst_ref=out_refs[k],
            send_sem=sems[0].at[k], recv_sem=sems[1].at[k],
            device_id=(x, y, 1 - cc), device_id_type=_MESH) for k in range(n)]

    def out_shape(g, hr):
        return jax.ShapeDtypeStruct((4,) + (g.shape[2:] if hr is None else (hr, g.shape[2])), g.dtype)

    return _exchange_task(make_copies, gs, [out_shape(g, hr) for g, hr in zip(gs, half_rows)], n)


def _chips_task(ps):
    n = len(ps)

    def make_copies(p_refs, out_refs, sems):
        x, y, cc = _position()
        copies = []
        for w in range(n):
            for k in (1, 2, 3):
                px = 1 - x if k >> 1 else x
                py = 1 - y if k & 1 else y
                copies.append(pltpu.make_async_remote_copy(
                    src_ref=p_refs[w].at[2 * px + py], dst_ref=out_refs[w].at[k - 1],
                    send_sem=sems[0].at[3 * w + k - 1], recv_sem=sems[1].at[3 * w + k - 1],
                    device_id=(px, py, cc), device_id_type=_MESH))
        return copies

    return _exchange_task(make_copies, ps, [jax.ShapeDtypeStruct((3,) + p.shape[1:], p.dtype) for p in ps], 3 * n)


def _add_sibling(g, r1, place, half_rows, *, tr, tc, name):
    _, r, c = r1.shape
    if half_rows is None:
        g_spec = pl.BlockSpec((None, None, tr, tc), lambda i, j, k, pc: (k, pc[0], i, j))
    else:
        per_half = half_rows // tr
        g_spec = pl.BlockSpec((None, tr, tc), lambda i, j, k, pc: (k, pc[0] * per_half + i, j))

    def body(pc_ref, g_ref, r_ref, pb_ref, own_ref):
        s = g_ref[...] + r_ref[...]
        pb_ref[...] = s.astype(BF16)

        @pl.when(pl.program_id(2) == pc_ref[1])
        def _():
            own_ref[...] = s

    grid_spec = pltpu.PrefetchScalarGridSpec(
        num_scalar_prefetch=1,
        grid=(r // tr, c // tc, 4),
        in_specs=[g_spec, pl.BlockSpec((None, tr, tc), lambda i, j, k, pc: (k, i, j))],
        out_specs=[pl.BlockSpec((None, tr, tc), lambda i, j, k, pc: (k, i, j)),
                   pl.BlockSpec((tr, tc), lambda i, j, k, pc: (i, j))],
    )
    return pl.pallas_call(
        body, name=name, grid_spec=grid_spec,
        out_shape=[jax.ShapeDtypeStruct((4, r, c), BF16), jax.ShapeDtypeStruct((r, c), F32)],
        compiler_params=_cparams(("parallel", "parallel", "arbitrary")),
    )(place, g, r1)


def _adamw_math(g, w, m, v):
    m = ADAM_B1 * m + (1.0 - ADAM_B1) * g
    v = ADAM_B2 * v + (1.0 - ADAM_B2) * (g * g)
    m_hat = m / (1.0 - ADAM_B1 ** ADAM_STEP)
    v_hat = v / (1.0 - ADAM_B2 ** ADAM_STEP)
    delta = -ADAM_LR * (m_hat / (jnp.sqrt(v_hat) + ADAM_EPS) + ADAM_WD * w)
    return delta, m, v


def _adamw_big(own, r2, w, m, v, prev, layer, *, tr, tc, name):
    _, r, c = w.shape

    def body(p_ref, r2_ref, w_ref, m_ref, v_ref, a0, a1, a2, a3, g_ref, d_ref, mo_ref, vo_ref):
        g = ((p_ref[...] + r2_ref[0].astype(F32)) + r2_ref[1].astype(F32)) + r2_ref[2].astype(F32)
        delta, mn, vn = _adamw_math(g, w_ref[...], m_ref[...], v_ref[...])
        g_ref[...] = g
        d_ref[...] = delta
        mo_ref[...] = mn
        vo_ref[...] = vn

    lay = pl.BlockSpec((None, tr, tc), lambda i, j: (layer, i, j))
    return pl.pallas_call(
        body, name=name, grid=(r // tr, c // tc),
        in_specs=[pl.BlockSpec((tr, tc), lambda i, j: (i, j)), pl.BlockSpec((3, tr, tc), lambda i, j: (0, i, j)),
                  lay, lay, lay, _ANY, _ANY, _ANY, _ANY],
        out_specs=[lay, lay, lay, lay],
        out_shape=[jax.ShapeDtypeStruct(w.shape, F32)] * 4,
        input_output_aliases={5: 0, 6: 1, 7: 2, 8: 3},
        compiler_params=_cparams(("parallel", "parallel")),
    )(own, r2, w, m, v, *prev)


SMALL_COLS = 1024


def _sum_gathered(parts, *, name):
    _, r, c = parts.shape

    def body(p_ref, o_ref):
        acc = p_ref[0]
        for k in range(1, N_DEV):
            acc = acc + p_ref[k]
        o_ref[...] = acc

    return pl.pallas_call(
        body, name=name, grid=(1,), in_specs=[pl.BlockSpec((N_DEV, r, c), lambda i: (0, 0, 0))],
        out_specs=pl.BlockSpec((r, c), lambda i: (0, 0)), out_shape=jax.ShapeDtypeStruct((r, c), F32),
        compiler_params=_cparams(("arbitrary",)),
    )(parts)


def _adamw_small(g, w, m, v, *, name):
    r, c = g.shape

    def body(g_ref, w_ref, m_ref, v_ref, d_ref, mo_ref, vo_ref):
        delta, mn, vn = _adamw_math(g_ref[...], w_ref[...], m_ref[...], v_ref[...])
        d_ref[...] = delta
        mo_ref[...] = mn
        vo_ref[...] = vn

    spec = pl.BlockSpec((r, c), lambda i: (0, 0))
    return pl.pallas_call(
        body, name=name, grid=(1,), in_specs=[spec] * 4, out_specs=[spec] * 3,
        out_shape=[jax.ShapeDtypeStruct((r, c), F32)] * 3, compiler_params=_cparams(("arbitrary",)),
    )(g, w, m, v)


WEIGHT_NAMES = ("gla_w_in", "gla_w_gate_up", "gla_gate_bias", "gla_norm_g", "gla_w_out", "dil_w_in", "dil_w_out",
                "ffn_w_up", "ffn_conv_w", "ffn_conv_b", "ffn_w_down", "ln_g", "ln_b")
ADAM_TILES = {"gla_w_in": (256, 770), "gla_w_out": (128, 2048), "dil_w_in": (256, 1152), "dil_w_out": (512, 256),
              "ffn_w_up": (344, 1024), "ffn_w_down": (344, 1024)}
ADD_TILES = {**ADAM_TILES, "ffn_w_up": (352, 1024)}
FF_DOWN_SHARD = D_FF // N_DEV
VEC_COLS = 128


def _pad_axis(a, axis, to):
    pads = [(0, 0)] * a.ndim
    pads[axis] = (0, to - a.shape[axis])
    return jnp.pad(a, pads)


def _ff_cols(blocks):
    r = blocks.shape[1]
    return _pad_axis(blocks, 2, FF_SHARD_PAD).reshape(2, 4, r, FF_SHARD_PAD).transpose(0, 2, 1, 3).reshape(2, r, FF_HALF_PAD)


def _ff_cols_back(a):
    r = a.shape[1]
    return a.reshape(2, r, 4, FF_SHARD_PAD)[..., :FF_SHARD].transpose(1, 0, 2, 3).reshape(r, 2 * D_FF)


def _vec_parts(l, w):
    return [w["ffn_conv_w"][l], w["ln_g"][l], w["ln_b"][l]] + ([w["gla_w_gate_up"][l // 2]] if l % 2 == 0 else [])


def _layer_shards(l, w):
    j = l // 2
    gla = l % 2 == 0
    parts = _vec_parts(l, w)
    vec_rows = -(-sum(math.prod(a.shape) for a in parts) // (VEC_COLS * SUBLANES)) * SUBLANES
    mixer = [(w["gla_w_in"] if gla else w["dil_w_in"])[j].astype(BF16),
             (w["gla_w_out"] if gla else w["dil_w_out"])[j].astype(BF16), _pack_rows(parts, vec_rows, VEC_COLS)]
    w_up_t = _pad_axis(jnp.swapaxes(w["ffn_w_up"][l], 0, 1).astype(BF16), 0, FF_SHARD_PAD)
    return mixer, w_up_t, w["ffn_w_down"][l].astype(BF16)


def _w_down_layout(g_down):
    return _pad_axis(g_down.reshape(4, FF_SHARD, D_MODEL), 1, FF_SHARD_PAD).reshape(FF_HALF_PAD, D_MODEL)


def _layer_weights(l, w, g_in, g_out, g_vec):
    j = l // 2
    gla = l % 2 == 0
    vec_shapes = [a.shape for a in _vec_parts(l, w)]
    out = {}
    flat, vec, off = g_vec.reshape(N_DEV, -1), [], 0
    for s in vec_shapes:
        vec.append(flat[:, off:off + math.prod(s)].reshape((N_DEV,) + s))
        off += math.prod(s)
    out["conv_w"] = _ff_cols(vec[0])
    out["conv_b"] = _pad_axis(w["ffn_conv_b"][l].reshape(N_DEV, FF_SHARD), 1, FF_SHARD_PAD).reshape(2, 1, FF_HALF_PAD)
    out["ln_g"] = vec[1].transpose(1, 0, 2).reshape(2, 1, D_MODEL)
    out["ln_b"] = vec[2].transpose(1, 0, 2).reshape(2, 1, D_MODEL)
    if gla:
        win = g_in.transpose(1, 0, 2).reshape(D_MODEL, GLA_IN)
        out["w_in"] = _pad_axis(win, 1, GLA_MAIN + GLOW_PAD)
        wgu = vec[3].transpose(1, 0, 2).reshape(GLA_GATE_RANK, GLA_DK).astype(BF16)
        out["w_gate_up"] = _pad_axis(wgu, 0, GLOW_PAD)
        out["w_out"] = g_out.reshape(GLA_DV, D_MODEL)
        out["gate_bias"] = w["gla_gate_bias"][j].reshape(1, GLA_DK)
        out["norm_g"] = w["gla_norm_g"][j].reshape(1, GLA_HEAD_V)
    else:
        out["w_in"] = g_in
        out["w_out"] = g_out.transpose(1, 0, 2).reshape(DIL_WIDTH, D_MODEL)
    return out


def _by_chip_parity(blocks):
    return blocks.reshape((4, 2) + blocks.shape[1:])


def _col_blocks(dw, width):
    r = dw.shape[0]
    return dw.reshape(r, N_DEV, width).transpose(1, 0, 2)


def _carry(call, task):
    if task is None:
        return call(None), []
    return call(task)


def _sibling_sum(l, n, g, half_rows, from_sibling, place):
    return _add_sibling(g, from_sibling, place, half_rows, tr=ADD_TILES[n][0], tc=ADD_TILES[n][1], name=f"l{l}_{n}_add")


def _ffn_fwd(l, yb, lw, task_up, task_down):
    h, got_up = _carry(lambda c: _ffn_hidden(yb, lw["w_up_t"], comm=c, name=f"l{l}_ffn_hidden"), task_up)
    act = _convgate_fwd(h, lw["conv_w"], lw["conv_b"], name=f"l{l}_convgate")
    ffn, got_down = _carry(lambda c: _matmul(act, lw["w_down"], tm=1024, tn=1024, tk=2816, comm=c, name=f"l{l}_ffn_down"), task_down)
    return ffn, (h, act), got_up, got_down


def _ffn_bwd(l, yb, dz, dzb, lw, saved, place, pending):
    h, act = saved
    t = yb.shape[0]
    task = _sibling_task(list(pending[1].values()), [None] * len(pending[1])) if pending else None
    dact, from_sibling = _carry(lambda c: _matmul(dzb, lw["w_down"], tb=True, tm=512, tn=2816, tk=D_MODEL, comm=c,
                                                  name=f"l{l}_ffn_dact"), task)
    pend_sums = {n: _sibling_sum(pending[0], n, g, None, r1, place)
                 for (n, g), r1 in zip(pending[1].items(), from_sibling)} if pending else {}
    d_down = _ffn_down_dw(act, dzb, name=f"l{l}_ffn_dwdown")
    dh, dcw, dcb = _convgate_bwd(h, dact, lw["conv_w"], lw["conv_b"], name=f"l{l}_convgate_bwd")
    dh = dh.reshape(2 * t, FF_HALF_PAD)
    tasks = [_sibling_task([d_down], [FF_DOWN_SHARD])] + ([_chips_task([s[0] for s in pend_sums.values()])] if pending else [])
    d_up_t, got = _ffn_hidden_dw(dh, yb, comm=_join_comm(tasks), name=f"l{l}_ffn_dwup")
    d_up = _by_chip_parity(d_up_t)
    dy, got_up = _ffn_hidden_dy(dh, lw["w_up_t"], dz, DEEPNORM_ALPHA, comm=_sibling_task([d_up], [None]), name=f"l{l}_ffn_dy")
    sums = {"ffn_w_down": _sibling_sum(l, "ffn_w_down", d_down, FF_DOWN_SHARD, got[0], place),
            "ffn_w_up": _sibling_sum(l, "ffn_w_up", d_up, None, got_up[0], place)}
    small = {"ffn_conv_w": _ff_cols_back(dcw), "ffn_conv_b": _ff_cols_back(dcb)[0]}
    return dy, sums, small, pend_sums, got[1:]


def _gla_layer_fwd(l, hb, lw, tasks):
    proj, got_proj = _carry(lambda c: _matmul(hb, lw["w_in"], tm=1024, tn=896, tk=D_MODEL, comm=c, name=f"l{l}_gla_proj"),
                            tasks["proj"])
    (y, o, a, st), got_gla = _carry(lambda c: _gla_fwd(proj, lw["w_gate_up"], lw["gate_bias"], lw["norm_g"], comm=c,
                                                       name=f"l{l}_gla"), tasks["gla"])
    mix, got_out = _carry(lambda c: _matmul(y, lw["w_out"], tm=1024, tn=1024, tk=GLA_DV, comm=c, name=f"l{l}_gla_out"),
                          tasks["out"])
    return mix, (proj, y, o, a, st), {"proj": got_proj, "gla": got_gla, "out": got_out}


def _gla_layer_bwd(l, hb, dz, dzb, lw, saved, ffn_sums, place):
    proj, y, o, a, st = saved
    dy = _matmul(dzb, lw["w_out"], tb=True, tm=1024, tn=1024, tk=D_MODEL, name=f"l{l}_gla_dy")
    d_out = _by_chip_parity(_matmul(y, dzb, ta=True, tm=1024, tn=1024, tk=2048, name=f"l{l}_gla_dwout")
                            .reshape(N_DEV, GLA_DV // N_DEV, D_MODEL))
    names = list(ffn_sums)
    (dq, dk, dv, dr, dgl, dwgu, dbias, dng), got = _gla_bwd(
        proj, lw["w_gate_up"], lw["gate_bias"], lw["norm_g"], o, a, st, dy,
        comm=_join_comm([_chips_task([ffn_sums[n][0] for n in names]), _sibling_task([d_out], [None])]), name=f"l{l}_gla_bwd")
    out_blocks, out_own = _sibling_sum(l, "gla_w_out", d_out, None, got[len(names)], place)
    dproj = jnp.concatenate([dq, dk, dv, dr, dgl], axis=-1)
    d_in, got_out = _matmul(hb, dproj, ta=True, tm=1024, tn=896, tk=2048, comm=_chips_task([out_blocks]), name=f"l{l}_gla_dwin")
    dx = _matmul(dproj, lw["w_in"], tb=True, tm=1024, tn=1024, tk=896, res=dz, res_scale=DEEPNORM_ALPHA, name=f"l{l}_gla_dx")
    big = {"gla_w_in": _by_chip_parity(_col_blocks(d_in[:, :GLA_IN], GLA_IN // N_DEV))}
    small = {"gla_w_gate_up": dwgu[:GLA_GATE_RANK], "gla_gate_bias": dbias[0], "gla_norm_g": dng[0]}
    return dx, big, small, dict(zip(names, got)), {"gla_w_out": (out_own, got_out[0])}


def _dil_layer_fwd(l, hb, lw, tasks):
    proj, got_proj = _carry(lambda c: _mm_colblocks(hb, lw["w_in"], comm=c, name=f"l{l}_dil_proj"), tasks["proj"])
    os_, lses = [], []
    for gi in range(len(DIL_PATTERNS)):
        o, lse = _dil_fwd(proj, gi, name=f"l{l}_dil_attn{gi}")
        os_.append(o)
        lses.append(lse)
    omix = _dil_mix_fwd(os_, lses, name=f"l{l}_dil_mix")
    mix, got_out = _carry(lambda c: _matmul(omix, lw["w_out"], tm=1024, tn=1024, tk=DIL_WIDTH, comm=c, name=f"l{l}_dil_out"),
                          tasks["out"])
    return mix, (proj, os_, lses, omix), {"proj": got_proj, "out": got_out}


def _dil_layer_bwd(l, hb, dz, dzb, lw, saved, ffn_sums, place):
    proj, os_, lses, omix = saved
    dout = _matmul(dzb, lw["w_out"], tb=True, tm=1024, tn=1024, tk=D_MODEL, name=f"l{l}_dil_dy")
    d_out = _by_chip_parity(_col_blocks(_matmul(omix, dzb, ta=True, tm=1024, tn=1024, tk=2048, name=f"l{l}_dil_dwout"),
                                        D_MODEL // N_DEV))
    dos, dls = _dil_mix_bwd(os_, lses, dout, name=f"l{l}_dil_mix_bwd")
    parts = []
    for gi in range(len(DIL_PATTERNS)):
        parts += list(_dil_bwd(proj, gi, lses[gi], dos[gi], dls[gi], name=f"l{l}_dil_attn_bwd{gi}"))
    dproj = jnp.concatenate(parts, axis=-1)
    d_in, got = _mm_grad_colblocks(hb, dproj, DIL_IN // N_DEV, name=f"l{l}_dil_dwin",
                                   comm=_join_comm([_chips_task([ffn_sums["ffn_w_down"][0]]), _sibling_task([d_out], [None])]))
    out_blocks, out_own = _sibling_sum(l, "dil_w_out", d_out, None, got[1], place)
    dx, got2 = _mm_colblocks_t(dproj, lw["w_in"], dz, DEEPNORM_ALPHA, name=f"l{l}_dil_dx",
                               comm=_join_comm([_chips_task([ffn_sums["ffn_w_up"][0]]), _chips_task([out_blocks])]))
    big = {"dil_w_in": _by_chip_parity(d_in)}
    return dx, big, {}, {"ffn_w_down": got[0], "ffn_w_up": got2[0]}, {"dil_w_out": (out_own, got2[1])}


def _pack_rows(arrays, rows, cols=SMALL_COLS):
    flat = [a.reshape(-1) for a in arrays]
    used = sum(f.shape[0] for f in flat)
    return jnp.concatenate(flat + [jnp.zeros((rows * cols - used,), F32)]).reshape(rows, cols)


def _unpack_rows(packed, shapes):
    flat, out, off = packed.reshape(-1), [], 0
    for s in shapes:
        n = math.prod(s)
        out.append(flat[off:off + n].reshape(s))
        off += n
    return out


def _rows_for(shapes):
    n = sum(math.prod(s) for s in shapes)
    return -(-n // (SMALL_COLS * SUBLANES)) * SUBLANES


def kernel(x, gla_w_in, gla_w_gate_up, gla_gate_bias, gla_norm_g, gla_w_out, dil_w_in, dil_w_out, ffn_w_up, ffn_conv_w, ffn_conv_b, ffn_w_down, ln_g, ln_b, loss_target, m_gla_w_in, m_gla_w_gate_up, m_gla_gate_bias, m_gla_norm_g, m_gla_w_out, m_dil_w_in, m_dil_w_out, m_ffn_w_up, m_ffn_conv_w, m_ffn_conv_b, m_ffn_w_down, m_ln_g, m_ln_b, v_gla_w_in, v_gla_w_gate_up, v_gla_gate_bias, v_gla_norm_g, v_gla_w_out, v_dil_w_in, v_dil_w_out, v_ffn_w_up, v_ffn_conv_w, v_ffn_conv_b, v_ffn_w_down, v_ln_g, v_ln_b):
    w = dict(zip(WEIGHT_NAMES, (gla_w_in, gla_w_gate_up, gla_gate_bias, gla_norm_g, gla_w_out, dil_w_in, dil_w_out,
                                ffn_w_up, ffn_conv_w, ffn_conv_b, ffn_w_down, ln_g, ln_b)))
    mom = dict(zip(WEIGHT_NAMES, (m_gla_w_in, m_gla_w_gate_up, m_gla_gate_bias, m_gla_norm_g, m_gla_w_out, m_dil_w_in,
                                  m_dil_w_out, m_ffn_w_up, m_ffn_conv_w, m_ffn_conv_b, m_ffn_w_down, m_ln_g, m_ln_b)))
    var = dict(zip(WEIGHT_NAMES, (v_gla_w_in, v_gla_w_gate_up, v_gla_gate_bias, v_gla_norm_g, v_gla_w_out, v_dil_w_in,
                                  v_dil_w_out, v_ffn_w_up, v_ffn_conv_w, v_ffn_conv_b, v_ffn_w_down, v_ln_g, v_ln_b)))
    xi, yi, ci = _position()
    dev = 4 * xi + 2 * yi + ci
    place = jnp.stack([ci, 2 * xi + yi]).astype(jnp.int32)
    t = x.shape[0] * x.shape[1]
    h = x.reshape(t, D_MODEL)
    hb = h.astype(BF16)
    target = loss_target.reshape(t, D_MODEL)

    shards = [_layer_shards(l, w) for l in range(DEPTH)]
    arrived = {0: dict(zip(("in", "out", "vec"), _run_comm(_gather_task(shards[0][0]), name="gather_l0")))}
    lws, saved = [], []
    for l in range(DEPTH):
        gla = l % 2 == 0
        nxt = l + 1 if l + 1 < DEPTH else None
        nxt_dil = nxt is not None and nxt % 2 == 1
        here = arrived[l]
        lw = _layer_weights(l, w, here["in"], here["out"], here["vec"])
        tasks = {"proj": _gather_task([shards[l][2]]),
                 "gla": _gather_task([shards[l][1]]) if gla else None,
                 "out": _gather_task(shards[nxt][0][1:]) if nxt else None}
        mix, mixer_saved, got = (_gla_layer_fwd if gla else _dil_layer_fwd)(l, hb, lw, tasks)
        lw["w_down"] = _w_down_layout(got["proj"][0])
        lw["w_up_t"] = got["gla"][0] if gla else here["up"]
        y1, y1b, xh1, rs1 = _ln_fwd(h, mix, lw["ln_g"][0], lw["ln_b"][0], name=f"l{l}_ln1")
        task_hidden = _gather_task([shards[nxt][1] if nxt_dil else shards[nxt][0][0]]) if nxt else None
        task_down = _gather_task([shards[nxt][0][0]]) if nxt_dil else None
        ffn, ffn_saved, got_hidden, got_down = _ffn_fwd(l, y1b, lw, task_hidden, task_down)
        y2, y2b, xh2, rs2 = _ln_fwd(y1, ffn, lw["ln_g"][1], lw["ln_b"][1], name=f"l{l}_ln2")
        saved.append((hb, mixer_saved, y1b, xh1, rs1, ffn_saved, xh2, rs2))
        lws.append(lw)
        h, hb = y2, y2b
        if nxt:
            arrived[nxt] = {"out": got["out"][0], "vec": got["out"][1]}
            if nxt_dil:
                arrived[nxt].update({"up": got_hidden[0], "in": got_down[0]})
            else:
                arrived[nxt]["in"] = got_hidden[0]
    loss_local, dy = _loss_fwd_bwd(h, target, name="loss")
    loss = lax.psum(loss_local[0, 0], ("x", "y", "c"))

    big_names = tuple(ADAM_TILES)
    as_updated = lambda n, a: jnp.swapaxes(a, 1, 2) if n == "ffn_w_up" else a
    wt, mt, vt = ({n: as_updated(n, d[n]) for n in big_names} for d in (w, mom, var))
    results = {n: [lax.empty(wt[n].shape, F32) for _ in range(4)] for n in big_names}
    small_grads = {n: [None] * w[n].shape[0] for n in WEIGHT_NAMES if n not in big_names}

    def adamw(l, n, own, from_chips):
        results[n] = _adamw_big(own, from_chips, wt[n], mt[n], vt[n], results[n], l if n.startswith("ffn") else l // 2,
                                tr=ADAM_TILES[n][0], tc=ADAM_TILES[n][1], name=f"l{l}_{n}_adamw")

    pending = None
    for l in reversed(range(DEPTH)):
        lw = lws[l]
        hb_in, mixer_saved, y1b, xh1, rs1, ffn_saved, xh2, rs2 = saved[l]
        dz2, dz2b, dg2, db2 = _ln_bwd(dy, xh2, rs2, lw["ln_g"][1], name=f"l{l}_ln2_bwd")
        dy1, ffn_sums, small_ffn, pend_sums, got_pending = _ffn_bwd(l, y1b, dz2, dz2b, lw, ffn_saved, place, pending)
        for (n, (_, own)), r2 in zip(pend_sums.items(), got_pending):
            adamw(pending[0], n, own, r2)
        dz1, dz1b, dg1, db1 = _ln_bwd(dy1, xh1, rs1, lw["ln_g"][0], name=f"l{l}_ln1_bwd")
        dy, big_mix, small_mix, got_ffn, reduced = (_gla_layer_bwd if l % 2 == 0 else _dil_layer_bwd)(
            l, hb_in, dz1, dz1b, lw, mixer_saved, ffn_sums, place)
        for n, (_, own) in ffn_sums.items():
            adamw(l, n, own, got_ffn[n])
        for n, (own, r2) in reduced.items():
            adamw(l, n, own, r2)
        pending = (l, big_mix)
        small_grads["ln_g"][l] = jnp.concatenate([dg1, dg2], axis=0)
        small_grads["ln_b"][l] = jnp.concatenate([db1, db2], axis=0)
        for n, g in small_ffn.items():
            small_grads[n][l] = g
        for n, g in small_mix.items():
            small_grads[n][l // 2] = g
    from_sibling = _run_comm(_sibling_task(list(pending[1].values()), [None] * len(pending[1])), name="reduce_sibling_l0")
    last_sums = {n: _sibling_sum(0, n, g, None, r1, place) for (n, g), r1 in zip(pending[1].items(), from_sibling)}
    from_chips = _run_comm(_chips_task([s[0] for s in last_sums.values()]), name="reduce_chips_l0")
    for (n, (_, own)), r2 in zip(last_sums.items(), from_chips):
        adamw(0, n, own, r2)
    results = {n: [as_updated(n, a) for a in results[n]] for n in big_names}
    grad_x = dy.reshape(x.shape)

    small_names = [n for n in WEIGHT_NAMES if n not in big_names]
    full_shapes = {"gla_w_gate_up": (2, GLA_GATE_RANK, GLA_DK), "gla_gate_bias": (2, GLA_DK), "gla_norm_g": (2, GLA_HEAD_V),
                   "ffn_conv_w": (DEPTH, 3, 2 * D_FF), "ffn_conv_b": (DEPTH, 2 * D_FF),
                   "ln_g": (DEPTH, 2, D_MODEL), "ln_b": (DEPTH, 2, D_MODEL)}
    shapes = [full_shapes[n] for n in small_names]
    rows = _rows_for(shapes)
    packed = _pack_rows([jnp.stack(small_grads[n]) for n in small_names], rows)
    summed = _sum_gathered(_run_comm(_gather_task([packed]), name="gather_small_grads")[0], name="sum_small_grads")
    full = dict(zip(small_names, _unpack_rows(summed, shapes)))
    own = {n: (full[n] if w[n].shape == full[n].shape
               else lax.dynamic_slice_in_dim(full[n], dev * w[n].shape[-1], w[n].shape[-1], axis=full[n].ndim - 1))
           for n in small_names}
    own_shapes = [w[n].shape for n in small_names]
    rows = _rows_for(own_shapes)
    pk = lambda d: _pack_rows([d[n] for n in small_names], rows)
    outs = _adamw_small(pk(own), pk(w), pk(mom), pk(var), name="adamw_small")
    for n in small_names:
        results[n] = [own[n]]
    for k, packed_out in enumerate(outs):
        for n, a in zip(small_names, _unpack_rows(packed_out, own_shapes)):
            results[n].append(a)

    return (loss, grad_x) + tuple(results[n][k] for k in range(4) for n in WEIGHT_NAMES)
```

```python
import functools
import math

import jax
import jax.numpy as jnp
from jax import lax
from jax.experimental import pallas as pl
from jax.experimental.pallas import tpu as pltpu

F32 = jnp.float32
BF16 = jnp.bfloat16

D_MODEL = 2048
SEQ = 2048
DEPTH = 4
N_DEV = 8
GLA_HEADS = 4
GLA_DK = 1024
GLA_DV = 2048
GLA_HEAD_K = 256
GLA_HEAD_V = 512
GLA_GATE_RANK = 16
GLA_GATE_NORMALIZER = 16.0
GLA_CHUNK = 64
GLA_MAIN = 2 * GLA_DK + 2 * GLA_DV
GLA_IN = GLA_MAIN + GLA_GATE_RANK
DIL_PATTERNS = ((128, 1), (512, 4), (2048, 16))
DIL_HEADS = 8
DIL_HEAD_DIM = 128
DIL_WIDTH = DIL_HEADS * DIL_HEAD_DIM
DIL_BLOCK = 128
DIL_IN = 3 * len(DIL_PATTERNS) * DIL_WIDTH
D_FF = 5504
DEEPNORM_ALPHA = (2 * DEPTH) ** 0.25
LN_EPS = 1e-5
RMS_EPS = 1e-6
ADAM_LR = 0.001
ADAM_B1 = 0.9
ADAM_B2 = 0.999
ADAM_EPS = 1e-08
ADAM_WD = 0.01
ADAM_STEP = 10

LANES = 128
SUBLANES = 8
VMEM_LIMIT_BYTES = 56 * 1024 * 1024

FF_SHARD = 2 * D_FF // N_DEV
FF_SHARD_PAD = 1408
FF_HALF_PAD = 4 * FF_SHARD_PAD
GLOW_PAD = LANES


def _cparams(dims=None):
    return pltpu.CompilerParams(dimension_semantics=dims, vmem_limit_bytes=VMEM_LIMIT_BYTES)


class _Comm:
    def __init__(self, inputs, out_shapes, sem_shapes, start, middle, finish):
        self.inputs, self.out_shapes, self.sem_shapes = list(inputs), list(out_shapes), list(sem_shapes)
        self.start, self.middle, self.finish = start, middle, finish


def _join_comm(comms):
    def cut(refs, counts):
        out, off = [], 0
        for c in counts:
            out.append(refs[off:off + c])
            off += c
        return out

    n_in = [len(c.inputs) for c in comms]
    n_out = [len(c.out_shapes) for c in comms]
    n_sem = [len(c.sem_shapes) for c in comms]

    def hook(which):
        def run(ins, outs, sems):
            for c, i, o, s in zip(comms, cut(ins, n_in), cut(outs, n_out), cut(sems, n_sem)):
                getattr(c, which)(i, o, s)
        return run

    return _Comm([a for c in comms for a in c.inputs], [s for c in comms for s in c.out_shapes],
                 [s for c in comms for s in c.sem_shapes], hook("start"), hook("middle"), hook("finish"))


def _mm(a, b, *, grid, a_spec, b_spec, o_spec, out_shape, acc_shape, ta=False, tb=False, res=None, res_scale=1.0,
        comm=None, name):
    nk = grid[2]
    dims = (((0 if ta else 1,), (1 if tb else 0,)), ((), ()))
    has_res = res is not None
    n_in = 2 + has_res
    n_cin = len(comm.inputs) if comm else 0
    n_cout = len(comm.out_shapes) if comm else 0

    def body(*refs):
        a_ref, b_ref = refs[0], refs[1]
        res_ref = refs[2] if has_res else None
        o_ref = refs[n_in + n_cin]
        scratch = refs[n_in + n_cin + 1 + n_cout:]
        acc_ref = scratch[0] if nk > 1 else None
        if comm:
            task = (refs[n_in:n_in + n_cin], refs[n_in + n_cin + 1:n_in + n_cin + 1 + n_cout],
                    scratch[1:] if nk > 1 else scratch)
            step = (pl.program_id(0) * grid[1] + pl.program_id(1)) * nk + pl.program_id(2)
            steps = grid[0] * grid[1] * nk

            @pl.when(step == 0)
            def _():
                comm.start(*task)

        p = lax.dot_general(a_ref[...], b_ref[...], dims, preferred_element_type=F32)

        def finish(acc):
            if has_res:
                acc = acc + res_scale * res_ref[...]
            o_ref[...] = acc.astype(o_ref.dtype)

        if nk == 1:
            finish(p)
        else:
            kk = pl.program_id(2)

            @pl.when(kk == 0)
            def _():
                acc_ref[...] = p

            @pl.when(kk > 0)
            def _():
                acc_ref[...] += p

            @pl.when(kk == nk - 1)
            def _():
                finish(acc_ref[...])

        if comm:
            @pl.when(step == steps - 1)
            def _():
                comm.middle(*task)
                comm.finish(*task)

    in_specs = [a_spec, b_spec] + ([o_spec] if has_res else [])
    args = (a, b) + ((res,) if has_res else ())
    acc = [pltpu.VMEM(acc_shape, F32)] if nk > 1 else []
    if not comm:
        return pl.pallas_call(
            body, name=name, grid=grid, in_specs=in_specs, out_specs=o_spec, out_shape=out_shape, scratch_shapes=acc,
            compiler_params=_cparams(("parallel", "parallel", "arbitrary")),
        )(*args)
    outs = pl.pallas_call(
        body, name=name, grid=grid, in_specs=in_specs + [_ANY] * n_cin, out_specs=[o_spec] + [_ANY] * n_cout,
        out_shape=[out_shape] + comm.out_shapes, scratch_shapes=acc + comm.sem_shapes,
        compiler_params=_cparams(("arbitrary", "arbitrary", "arbitrary")),
    )(*args, *comm.inputs)
    return outs[0], list(outs[1:])


def _matmul(a, b, *, ta=False, tb=False, tm, tn, tk, out_dtype=F32, res=None, res_scale=1.0, comm=None, name):
    m, k = (a.shape[1], a.shape[0]) if ta else a.shape
    n = b.shape[0] if tb else b.shape[1]
    assert (b.shape[1] if tb else b.shape[0]) == k
    assert m % tm == 0 and n % tn == 0 and k % tk == 0, (m, n, k, tm, tn, tk)
    a_spec = pl.BlockSpec((tk, tm), lambda i, j, kk: (kk, i)) if ta else pl.BlockSpec((tm, tk), lambda i, j, kk: (i, kk))
    b_spec = pl.BlockSpec((tn, tk), lambda i, j, kk: (j, kk)) if tb else pl.BlockSpec((tk, tn), lambda i, j, kk: (kk, j))
    return _mm(a, b, grid=(m // tm, n // tn, k // tk), a_spec=a_spec, b_spec=b_spec,
               o_spec=pl.BlockSpec((tm, tn), lambda i, j, kk: (i, j)), out_shape=jax.ShapeDtypeStruct((m, n), out_dtype),
               acc_shape=(tm, tn), ta=ta, tb=tb, res=res, res_scale=res_scale, comm=comm, name=name)


MM_ROWS = 1024


def _mm_colblocks(a, wb, *, comm=None, name):
    m, k = a.shape
    nb, _, w = wb.shape
    return _mm(a, wb, grid=(m // MM_ROWS, nb, 1),
               a_spec=pl.BlockSpec((MM_ROWS, k), lambda i, j, kk: (i, 0)),
               b_spec=pl.BlockSpec((None, k, w), lambda i, j, kk: (j, 0, 0)),
               o_spec=pl.BlockSpec((MM_ROWS, w), lambda i, j, kk: (i, j)),
               out_shape=jax.ShapeDtypeStruct((m, nb * w), F32), acc_shape=(MM_ROWS, w), comm=comm, name=name)


def _mm_colblocks_t(a, wb, res, res_scale, *, comm=None, name):
    m = a.shape[0]
    nb, n, w = wb.shape
    tn = 1024
    return _mm(a, wb, grid=(m // MM_ROWS, n // tn, nb),
               a_spec=pl.BlockSpec((MM_ROWS, w), lambda i, j, kk: (i, kk)),
               b_spec=pl.BlockSpec((None, tn, w), lambda i, j, kk: (kk, j, 0)),
               o_spec=pl.BlockSpec((MM_ROWS, tn), lambda i, j, kk: (i, j)),
               out_shape=jax.ShapeDtypeStruct((m, n), F32), acc_shape=(MM_ROWS, tn), tb=True,
               res=res, res_scale=res_scale, comm=comm, name=name)


def _mm_grad_colblocks(x, dy, w, *, comm=None, name):
    t, k = x.shape
    nb = dy.shape[1] // w
    tm, tk = 1024, 2048
    return _mm(x, dy, grid=(k // tm, nb, t // tk),
               a_spec=pl.BlockSpec((tk, tm), lambda i, j, kk: (kk, i)),
               b_spec=pl.BlockSpec((tk, w), lambda i, j, kk: (kk, j)),
               o_spec=pl.BlockSpec((None, tm, w), lambda i, j, kk: (j, i, 0)),
               out_shape=jax.ShapeDtypeStruct((nb, k, w), F32), acc_shape=(tm, w), ta=True, comm=comm, name=name)


def _ffn_hidden(y, wt, *, comm=None, name):
    t, k = y.shape
    ni = t // MM_ROWS
    return _mm(y, wt, grid=(ni, N_DEV, 1),
               a_spec=pl.BlockSpec((MM_ROWS, k), lambda i, j, kk: (i, 0)),
               b_spec=pl.BlockSpec((None, FF_SHARD_PAD, k), lambda i, j, kk: (j, 0, 0)),
               o_spec=pl.BlockSpec((MM_ROWS, FF_SHARD_PAD), lambda i, j, kk: ((j // 4) * ni + i, j % 4)),
               out_shape=jax.ShapeDtypeStruct((2 * t, FF_HALF_PAD), F32), acc_shape=(MM_ROWS, FF_SHARD_PAD), tb=True,
               comm=comm, name=name)


def _ffn_hidden_dy(dh, wt, res, res_scale, *, comm=None, name):
    t = dh.shape[0] // 2
    tm, tn = 2048, 512
    ni = t // tm
    return _mm(dh, wt, grid=(ni, D_MODEL // tn, N_DEV),
               a_spec=pl.BlockSpec((tm, FF_SHARD_PAD), lambda i, j, kk: ((kk // 4) * ni + i, kk % 4)),
               b_spec=pl.BlockSpec((None, FF_SHARD_PAD, tn), lambda i, j, kk: (kk, 0, j)),
               o_spec=pl.BlockSpec((tm, tn), lambda i, j, kk: (i, j)),
               out_shape=jax.ShapeDtypeStruct((t, D_MODEL), F32), acc_shape=(tm, tn),
               res=res, res_scale=res_scale, comm=comm, name=name)


def _ffn_hidden_dw(dh, y, *, comm=None, name):
    t, k = y.shape
    tk, tn = 2048, 1024
    nk = t // tk
    return _mm(dh, y, grid=(N_DEV, k // tn, nk),
               a_spec=pl.BlockSpec((tk, FF_SHARD_PAD), lambda i, j, kk: ((i // 4) * nk + kk, i % 4)),
               b_spec=pl.BlockSpec((tk, tn), lambda i, j, kk: (kk, j)),
               o_spec=pl.BlockSpec((None, FF_SHARD_PAD, tn), lambda i, j, kk: (i, 0, j)),
               out_shape=jax.ShapeDtypeStruct((N_DEV, FF_SHARD_PAD, k), F32), acc_shape=(FF_SHARD_PAD, tn), ta=True,
               comm=comm, name=name)


def _ffn_down_dw(act, dz, *, comm=None, name):
    t, k = dz.shape
    tk, tn = 2048, 1024
    return _mm(act, dz, grid=(4, k // tn, t // tk),
               a_spec=pl.BlockSpec((tk, FF_SHARD_PAD), lambda i, j, kk: (kk, i)),
               b_spec=pl.BlockSpec((tk, tn), lambda i, j, kk: (kk, j)),
               o_spec=pl.BlockSpec((None, FF_SHARD_PAD, tn), lambda i, j, kk: (i, 0, j)),
               out_shape=jax.ShapeDtypeStruct((4, FF_SHARD_PAD, k), F32), acc_shape=(FF_SHARD_PAD, tn), ta=True,
               comm=comm, name=name)


LN_ROWS = 256


def _ln_fwd(x, f, g, b, *, name):
    t, d = x.shape

    def body(x_ref, f_ref, g_ref, b_ref, y_ref, yb_ref, xh_ref, rs_ref):
        z = DEEPNORM_ALPHA * x_ref[...] + f_ref[...]
        mu = jnp.mean(z, axis=-1, keepdims=True)
        zc = z - mu
        var = jnp.mean(zc * zc, axis=-1, keepdims=True)
        rstd = lax.rsqrt(var + LN_EPS)
        xh = zc * rstd
        y = xh * g_ref[...] + b_ref[...]
        y_ref[...] = y
        yb_ref[...] = y.astype(BF16)
        xh_ref[...] = xh
        rs_ref[...] = rstd

    row = pl.BlockSpec((LN_ROWS, d), lambda i: (i, 0))
    vec = pl.BlockSpec((1, d), lambda i: (0, 0))
    return pl.pallas_call(
        body,
        name=name,
        grid=(t // LN_ROWS,),
        in_specs=[row, row, vec, vec],
        out_specs=[row, row, row, pl.BlockSpec((LN_ROWS, 1), lambda i: (i, 0))],
        out_shape=[jax.ShapeDtypeStruct((t, d), F32), jax.ShapeDtypeStruct((t, d), BF16),
                   jax.ShapeDtypeStruct((t, d), F32), jax.ShapeDtypeStruct((t, 1), F32)],
        compiler_params=_cparams(("parallel",)),
    )(x, f, g, b)


def _ln_bwd(dy, xhat, rstd, g, *, name):
    t, d = dy.shape

    def body(dy_ref, xh_ref, rs_ref, g_ref, dz_ref, dzb_ref, dg_ref, db_ref):
        dyv = dy_ref[...]
        xh = xh_ref[...]
        dyg = dyv * g_ref[...]
        m1 = jnp.mean(dyg, axis=-1, keepdims=True)
        m2 = jnp.mean(dyg * xh, axis=-1, keepdims=True)
        dz = rs_ref[...] * (dyg - m1 - xh * m2)
        dz_ref[...] = dz
        dzb_ref[...] = dz.astype(BF16)
        dg_part = jnp.sum(dyv * xh, axis=0, keepdims=True)
        db_part = jnp.sum(dyv, axis=0, keepdims=True)

        @pl.when(pl.program_id(0) == 0)
        def _():
            dg_ref[...] = dg_part
            db_ref[...] = db_part

        @pl.when(pl.program_id(0) > 0)
        def _():
            dg_ref[...] += dg_part
            db_ref[...] += db_part

    row = pl.BlockSpec((LN_ROWS, d), lambda i: (i, 0))
    vec = pl.BlockSpec((1, d), lambda i: (0, 0))
    return pl.pallas_call(
        body,
        name=name,
        grid=(t // LN_ROWS,),
        in_specs=[row, row, pl.BlockSpec((LN_ROWS, 1), lambda i: (i, 0)), vec],
        out_specs=[row, row, vec, vec],
        out_shape=[jax.ShapeDtypeStruct((t, d), F32), jax.ShapeDtypeStruct((t, d), BF16),
                   jax.ShapeDtypeStruct((1, d), F32), jax.ShapeDtypeStruct((1, d), F32)],
        compiler_params=_cparams(("arbitrary",)),
    )(dy, xhat, rstd, g)


def _loss_fwd_bwd(y, target, *, name):
    t, d = y.shape

    def body(y_ref, t_ref, loss_ref, dy_ref):
        err = y_ref[...] - t_ref[...]
        dy_ref[...] = err * (1.0 / d)
        part = 0.5 * jnp.sum(jnp.mean(err * err, axis=-1, keepdims=True), axis=0, keepdims=True)

        @pl.when(pl.program_id(0) == 0)
        def _():
            loss_ref[...] = part

        @pl.when(pl.program_id(0) > 0)
        def _():
            loss_ref[...] += part

    row = pl.BlockSpec((LN_ROWS, d), lambda i: (i, 0))
    return pl.pallas_call(
        body,
        name=name,
        grid=(t // LN_ROWS,),
        in_specs=[row, row],
        out_specs=[pl.BlockSpec((1, 1), lambda i: (0, 0)), row],
        out_shape=[jax.ShapeDtypeStruct((1, 1), F32), jax.ShapeDtypeStruct((t, d), F32)],
        compiler_params=_cparams(("arbitrary",)),
    )(y, target)


FFN_COLS = 256


def _shift_rows(h, s):
    rows = lax.broadcasted_iota(jnp.int32, h.shape, 0)
    return jnp.where(rows >= s, pltpu.roll(h, s, 0), 0.0)


def _shift_rows_up(h, s):
    n = h.shape[0]
    rows = lax.broadcasted_iota(jnp.int32, h.shape, 0)
    return jnp.where(rows < n - s, pltpu.roll(h, n - s, 0), 0.0)


def _causal_conv(h, w, b):
    return w[0:1, :] * _shift_rows(h, 2) + w[1:2, :] * _shift_rows(h, 1) + w[2:3, :] * h + b


def _sigmoid(x):
    return 1.0 / (1.0 + jnp.exp(-x))


def _convgate_fwd(h, cw, cb, *, name):
    t, n = h.shape[0] // 2, h.shape[1]
    nb = t // SEQ

    def body(hg_ref, hu_ref, wg_ref, wu_ref, bg_ref, bu_ref, a_ref):
        gate = _causal_conv(hg_ref[...], wg_ref[...], bg_ref[...])
        up = _causal_conv(hu_ref[...], wu_ref[...], bu_ref[...])
        a_ref[...] = (gate * _sigmoid(gate) * up).astype(BF16)

    def half(rows, k):
        return pl.BlockSpec((None, rows, FFN_COLS), lambda s, j: (k, 0, j))

    return pl.pallas_call(
        body,
        name=name,
        grid=(nb, n // FFN_COLS),
        in_specs=[pl.BlockSpec((SEQ, FFN_COLS), lambda s, j: (s, j)), pl.BlockSpec((SEQ, FFN_COLS), lambda s, j: (nb + s, j)),
                  half(3, 0), half(3, 1), half(1, 0), half(1, 1)],
        out_specs=pl.BlockSpec((SEQ, FFN_COLS), lambda s, j: (s, j)),
        out_shape=jax.ShapeDtypeStruct((t, n), BF16),
        compiler_params=_cparams(("parallel", "parallel")),
    )(h, h, cw, cw, cb, cb)


def _convgate_bwd(h, dact, cw, cb, *, name):
    t, n = h.shape[0] // 2, h.shape[1]
    nb = t // SEQ

    def body(hg_ref, hu_ref, da_ref, wg_ref, wu_ref, bg_ref, bu_ref, dh_ref, dw_ref, db_ref):
        dhg_ref, dhu_ref = dh_ref.at[0], dh_ref.at[1]
        dwg_ref, dwu_ref = dw_ref.at[0], dw_ref.at[1]
        dbg_ref, dbu_ref = db_ref.at[0], db_ref.at[1]
        hgv, huv = hg_ref[...], hu_ref[...]
        wgv, wuv = wg_ref[...], wu_ref[...]
        gate = _causal_conv(hgv, wgv, bg_ref[...])
        up = _causal_conv(huv, wuv, bu_ref[...])
        sg = _sigmoid(gate)
        da = da_ref[...]
        dgate = da * up * (sg * (1.0 + gate * (1.0 - sg)))
        dup = da * (gate * sg)

        def conv_bwd(dc, h, w, dh_ref, dw_ref, db_ref):
            dh = w[2:3, :] * dc + w[1:2, :] * _shift_rows_up(dc, 1) + w[0:1, :] * _shift_rows_up(dc, 2)
            dh_ref[...] = dh.astype(BF16)
            dws = [jnp.sum(dc * _shift_rows(h, 2), axis=0, keepdims=True),
                   jnp.sum(dc * _shift_rows(h, 1), axis=0, keepdims=True),
                   jnp.sum(dc * h, axis=0, keepdims=True)]
            db = jnp.sum(dc, axis=0, keepdims=True)

            @pl.when(pl.program_id(1) == 0)
            def _():
                for r in range(3):
                    dw_ref[r:r + 1, :] = dws[r]
                db_ref[...] = db

            @pl.when(pl.program_id(1) > 0)
            def _():
                for r in range(3):
                    dw_ref[r:r + 1, :] += dws[r]
                db_ref[...] += db

        conv_bwd(dgate, hgv, wgv, dhg_ref, dwg_ref, dbg_ref)
        conv_bwd(dup, huv, wuv, dhu_ref, dwu_ref, dbu_ref)

    def half(rows, k):
        return pl.BlockSpec((None, rows, FFN_COLS), lambda j, s: (k, 0, j))

    def both(rows):
        return pl.BlockSpec((2, rows, FFN_COLS), lambda j, s: (0, 0, j))

    return pl.pallas_call(
        body,
        name=name,
        grid=(n // FFN_COLS, nb),
        in_specs=[pl.BlockSpec((SEQ, FFN_COLS), lambda j, s: (s, j)), pl.BlockSpec((SEQ, FFN_COLS), lambda j, s: (nb + s, j)),
                  pl.BlockSpec((SEQ, FFN_COLS), lambda j, s: (s, j)), half(3, 0), half(3, 1), half(1, 0), half(1, 1)],
        out_specs=[pl.BlockSpec((2, SEQ, FFN_COLS), lambda j, s: (0, s, j)), both(3), both(1)],
        out_shape=[jax.ShapeDtypeStruct((2, t, n), BF16), jax.ShapeDtypeStruct((2, 3, n), F32),
                   jax.ShapeDtypeStruct((2, 1, n), F32)],
        compiler_params=_cparams(("parallel", "arbitrary")),
    )(h, h, dact, cw, cw, cb, cb)


GLA_Q_SCALE = GLA_HEAD_K ** -0.5
GLA_NC = SEQ // GLA_CHUNK
_NT = (((1,), (1,)), ((), ()))
_TN = (((0,), (0,)), ((), ()))


def _cumsum_rows(g):
    n = g.shape[0]
    rows = lax.broadcasted_iota(jnp.int32, g.shape, 0)
    s = 1
    while s < n:
        g = g + jnp.where(rows >= s, pltpu.roll(g, s, 0), 0.0)
        s *= 2
    return g


def _suffix_sum_rows(x):
    n = x.shape[0]
    rows = lax.broadcasted_iota(jnp.int32, x.shape, 0)
    s = 1
    while s < n:
        x = x + jnp.where(rows < n - s, pltpu.roll(x, n - s, 0), 0.0)
        s *= 2
    return x


def _gla_log_gate(gl_ref, wgu_ref, bias_ref):
    pre = jnp.dot(gl_ref[...].astype(BF16), wgu_ref[...], preferred_element_type=F32) + bias_ref[...]
    log_sig = jnp.minimum(pre, 0.0) - jnp.log(1.0 + jnp.exp(-jnp.abs(pre)))
    return pre, log_sig * (1.0 / GLA_GATE_NORMALIZER)


def _pair_rows(j):
    return (j // SUBLANES) * SUBLANES


def _gla_pair_fwd(q_scr, k_ref, b_scr, a_scr, h):
    c = GLA_CHUNK
    kc = pl.ds(h * GLA_HEAD_K, GLA_HEAD_K)
    a_scr[...] = jnp.zeros(a_scr.shape, F32)
    lane = lax.broadcasted_iota(jnp.int32, (1, c), 1)
    for j in range(c):
        r0 = _pair_rows(j)
        rs = pl.ds(r0, c - r0)
        rows = lax.broadcasted_iota(jnp.int32, (c - r0, 1), 0) + r0
        e = jnp.exp2(jnp.minimum(b_scr[rs, kc] - b_scr[pl.ds(j, 1), kc], 0.0))
        w = q_scr[rs, kc] * k_ref[pl.ds(j, 1), kc] * e
        col = jnp.where(rows >= j, jnp.sum(w, axis=-1, keepdims=True), 0.0)
        a_scr[rs, :] += col * (lane == j).astype(F32)


def _call_with_comm(body, comm, *, n_in, n_out, grid, in_specs, out_specs, out_shape, scratch_shapes, name, args):
    n_cin, n_cout, n_scr = len(comm.inputs), len(comm.out_shapes), len(scratch_shapes)

    def carrier(*refs):
        ins, cins = refs[:n_in], refs[n_in:n_in + n_cin]
        outs, couts = refs[n_in + n_cin:n_in + n_cin + n_out], refs[n_in + n_cin + n_out:n_in + n_cin + n_out + n_cout]
        scr = refs[n_in + n_cin + n_out + n_cout:]
        task = (cins, couts, scr[n_scr:])
        step, steps = 0, 1
        for axis, size in enumerate(grid):
            step = step * size + pl.program_id(axis)
            steps *= size

        @pl.when(step == 0)
        def _():
            comm.start(*task)

        body(*ins, *outs, *scr[:n_scr])

        @pl.when(step == steps - 1)
        def _():
            comm.middle(*task)
            comm.finish(*task)

    res = pl.pallas_call(
        carrier, name=name, grid=grid, in_specs=list(in_specs) + [_ANY] * n_cin, out_specs=list(out_specs) + [_ANY] * n_cout,
        out_shape=list(out_shape) + comm.out_shapes, scratch_shapes=list(scratch_shapes) + comm.sem_shapes,
        compiler_params=_cparams(("arbitrary",) * len(grid)),
    )(*args, *comm.inputs)
    return list(res[:n_out]), list(res[n_out:])


LOG2_E = 1.4426950408889634


def _gla_fwd(proj, wgu, bias, ng, *, comm=None, name):
    t = proj.shape[0]
    nb, nc, c = t // SEQ, GLA_NC, GLA_CHUNK

    def body(q_ref, k_ref, v_ref, r_ref, gl_ref, wgu_ref, bias_ref, ng_ref,
             y_ref, o_ref, a_ref, st_ref, state, b_scr, a_scr, q_scr):
        @pl.when(pl.program_id(1) == 0)
        def _():
            state[...] = jnp.zeros(state.shape, F32)

        _, g = _gla_log_gate(gl_ref, wgu_ref, bias_ref)
        b_scr[...] = _cumsum_rows(g) * LOG2_E
        q_scr[...] = q_ref[...] * GLA_Q_SCALE
        for h in range(GLA_HEADS):
            kc = pl.ds(h * GLA_HEAD_K, GLA_HEAD_K)
            vc = pl.ds(h * GLA_HEAD_V, GLA_HEAD_V)
            qh = q_scr[:, kc]
            kh = k_ref[:, kc]
            vh = v_ref[:, vc].astype(BF16)
            bh = b_scr[:, kc]
            blast = b_scr[pl.ds(c - 1, 1), kc]
            st = state[h]
            st_ref[h] = st
            o_inter = lax.dot_general((qh * jnp.exp2(bh)).astype(BF16), st.astype(BF16), _NT, preferred_element_type=F32)
            _gla_pair_fwd(q_scr, k_ref, b_scr, a_scr, h)
            a = a_scr[...]
            a_ref[h] = a
            o = o_inter + jnp.dot(a.astype(BF16), vh, preferred_element_type=F32)
            kd = (kh * jnp.exp2(blast - bh)).astype(BF16)
            state[h] = st * jnp.exp2(blast) + lax.dot_general(vh, kd, _TN, preferred_element_type=F32)
            o_ref[:, vc] = o
            rs = lax.rsqrt(jnp.mean(o * o, axis=-1, keepdims=True) + RMS_EPS)
            rh = r_ref[:, vc]
            y_ref[:, vc] = ((o * rs * ng_ref[...]) * (rh * _sigmoid(rh))).astype(BF16)

    def tok(width, col):
        return pl.BlockSpec((c, width), lambda b, i: (b * nc + i, col))

    whole = lambda shape: pl.BlockSpec(shape, lambda b, i: (0,) * len(shape))
    call = dict(
        grid=(nb, nc),
        in_specs=[tok(GLA_DK, 0), tok(GLA_DK, 1), tok(GLA_DV, 1), tok(GLA_DV, 2), tok(GLOW_PAD, GLA_MAIN // GLOW_PAD),
                  whole((GLOW_PAD, GLA_DK)), whole((1, GLA_DK)), whole((1, GLA_HEAD_V))],
        out_specs=[tok(GLA_DV, 0), tok(GLA_DV, 0),
                   pl.BlockSpec((GLA_HEADS, c, c), lambda b, i: (0, b * nc + i, 0)),
                   pl.BlockSpec((None, GLA_HEADS, GLA_HEAD_V, GLA_HEAD_K), lambda b, i: (b * nc + i, 0, 0, 0))],
        out_shape=[jax.ShapeDtypeStruct((t, GLA_DV), BF16), jax.ShapeDtypeStruct((t, GLA_DV), F32),
                   jax.ShapeDtypeStruct((GLA_HEADS, t, c), F32),
                   jax.ShapeDtypeStruct((t // c, GLA_HEADS, GLA_HEAD_V, GLA_HEAD_K), F32)],
        scratch_shapes=[pltpu.VMEM((GLA_HEADS, GLA_HEAD_V, GLA_HEAD_K), F32), pltpu.VMEM((c, GLA_DK), F32),
                        pltpu.VMEM((c, c), F32), pltpu.VMEM((c, GLA_DK), F32)],
        name=name)
    args = (proj, proj, proj, proj, proj, wgu, bias, ng)
    if comm is None:
        return pl.pallas_call(body, compiler_params=_cparams(("parallel", "arbitrary")), **call)(*args)
    return _call_with_comm(body, comm, n_in=8, n_out=4, args=args, **call)


def _gla_pair_bwd(q_scr, k_ref, b_scr, da_scr, dq_scr, dk_scr, h):
    c = GLA_CHUNK
    kc = pl.ds(h * GLA_HEAD_K, GLA_HEAD_K)
    lane = lax.broadcasted_iota(jnp.int32, (1, c), 1)
    for j in range(c):
        r0 = _pair_rows(j)
        rs = pl.ds(r0, c - r0)
        rows = lax.broadcasted_iota(jnp.int32, (c - r0, 1), 0) + r0
        e = jnp.exp2(jnp.minimum(b_scr[rs, kc] - b_scr[pl.ds(j, 1), kc], 0.0))
        dacol = jnp.sum(jnp.where(lane == j, da_scr[rs, :], 0.0), axis=-1, keepdims=True)
        t1 = jnp.where(rows >= j, dacol, 0.0) * e
        dq_scr[rs, kc] += t1 * k_ref[pl.ds(j, 1), kc]
        dk_scr[pl.ds(j, 1), kc] += jnp.sum(t1 * q_scr[rs, kc], axis=0, keepdims=True)


def _gla_bwd(proj, wgu, bias, ng, o, a, states, dy, *, comm=None, name):
    t = proj.shape[0]
    nb, nc, c = t // SEQ, GLA_NC, GLA_CHUNK

    def body(q_ref, k_ref, v_ref, r_ref, gl_ref, wgu_ref, bias_ref, ng_ref, o_ref, a_ref, stp_ref, stn_ref, dy_ref,
             dq_ref, dk_ref, dv_ref, dr_ref, dgl_ref, dwgu_ref, dbias_ref, dng_ref,
             dstate, b_scr, da_scr, dq_scr, dk_scr, dg_scr, q_scr):
        first = jnp.logical_and(pl.program_id(0) == 0, pl.program_id(1) == 0)

        @pl.when(first)
        def _():
            dwgu_ref[...] = jnp.zeros(dwgu_ref.shape, F32)
            dbias_ref[...] = jnp.zeros(dbias_ref.shape, F32)
            dng_ref[...] = jnp.zeros(dng_ref.shape, F32)

        @pl.when(pl.program_id(1) == 0)
        def _():
            dstate[...] = jnp.zeros(dstate.shape, F32)

        pre, g = _gla_log_gate(gl_ref, wgu_ref, bias_ref)
        b_scr[...] = _cumsum_rows(g) * LOG2_E
        q_scr[...] = q_ref[...] * GLA_Q_SCALE
        ngv = ng_ref[...]
        tri = lax.broadcasted_iota(jnp.int32, (c, c), 0) >= lax.broadcasted_iota(jnp.int32, (c, c), 1)
        for h in range(GLA_HEADS):
            kc = pl.ds(h * GLA_HEAD_K, GLA_HEAD_K)
            vc = pl.ds(h * GLA_HEAD_V, GLA_HEAD_V)
            oh = o_ref[:, vc]
            rh = r_ref[:, vc]
            dyh = dy_ref[:, vc]
            rs = lax.rsqrt(jnp.mean(oh * oh, axis=-1, keepdims=True) + RMS_EPS)
            u = oh * rs
            sg = _sigmoid(rh)
            sr = rh * sg
            dr_ref[:, vc] = (dyh * (u * ngv) * (sg * (1.0 + rh * (1.0 - sg)))).astype(BF16)
            dng_ref[...] += jnp.sum(dyh * sr * u, axis=0, keepdims=True)
            du = dyh * sr * ngv
            do = (rs * (du - u * jnp.mean(du * u, axis=-1, keepdims=True))).astype(BF16)
            qh = q_scr[:, kc]
            kh = k_ref[:, kc]
            vh = v_ref[:, vc].astype(BF16)
            bh = b_scr[:, kc]
            blast = b_scr[pl.ds(c - 1, 1), kc]
            eb = jnp.exp2(bh)
            ek = jnp.exp2(blast - bh)
            dst = dstate[h]
            dst_b = dst.astype(BF16)
            dg_carry = jnp.sum(dst * stn_ref[h], axis=0, keepdims=True)
            da = lax.dot_general(do, vh, _NT, preferred_element_type=F32)
            da_scr[...] = jnp.where(tri, da, 0.0)
            dv = lax.dot_general(a_ref[h].astype(BF16), do, _TN, preferred_element_type=F32)
            dv = dv + lax.dot_general((kh * ek).astype(BF16), dst_b, _NT, preferred_element_type=F32)
            dv_ref[:, vc] = dv.astype(BF16)
            dq_scr[:, kc] = jnp.dot(do, stp_ref[h].astype(BF16), preferred_element_type=F32) * eb
            dk_scr[:, kc] = jnp.dot(vh, dst_b, preferred_element_type=F32) * ek
            _gla_pair_bwd(q_scr, k_ref, b_scr, da_scr, dq_scr, dk_scr, h)
            dq = dq_scr[:, kc]
            dk = dk_scr[:, kc]
            dg_scr[:, kc] = _suffix_sum_rows(qh * dq - kh * dk) + dg_carry
            dstate[h] = dst * jnp.exp2(blast) + lax.dot_general(do, (qh * eb).astype(BF16), _TN, preferred_element_type=F32)
        dq_ref[...] = (dq_scr[...] * GLA_Q_SCALE).astype(BF16)
        dk_ref[...] = dk_scr[...].astype(BF16)
        dpre = dg_scr[...] * ((1.0 - _sigmoid(pre)) * (1.0 / GLA_GATE_NORMALIZER))
        dpre_b = dpre.astype(BF16)
        dbias_ref[...] += jnp.sum(dpre, axis=0, keepdims=True)
        dwgu_ref[...] += lax.dot_general(gl_ref[...].astype(BF16), dpre_b, _TN, preferred_element_type=F32)
        dgl_ref[...] = lax.dot_general(dpre_b, wgu_ref[...], _NT, preferred_element_type=F32).astype(BF16)

    def chunk(b, i):
        return b * nc + (nc - 1 - i)

    def tok(width, col):
        return pl.BlockSpec((c, width), lambda b, i: (chunk(b, i), col))

    whole = lambda shape: pl.BlockSpec(shape, lambda b, i: (0,) * len(shape))
    st_shape = (None, GLA_HEADS, GLA_HEAD_V, GLA_HEAD_K)
    call = dict(
        name=name,
        grid=(nb, nc),
        in_specs=[tok(GLA_DK, 0), tok(GLA_DK, 1), tok(GLA_DV, 1), tok(GLA_DV, 2), tok(GLOW_PAD, GLA_MAIN // GLOW_PAD),
                  whole((GLOW_PAD, GLA_DK)), whole((1, GLA_DK)), whole((1, GLA_HEAD_V)),
                  tok(GLA_DV, 0),
                  pl.BlockSpec((GLA_HEADS, c, c), lambda b, i: (0, chunk(b, i), 0)),
                  pl.BlockSpec(st_shape, lambda b, i: (chunk(b, i), 0, 0, 0)),
                  pl.BlockSpec(st_shape, lambda b, i: (b * nc + jnp.minimum(nc - i, nc - 1), 0, 0, 0)),
                  tok(GLA_DV, 0)],
        out_specs=[tok(GLA_DK, 0), tok(GLA_DK, 0), tok(GLA_DV, 0), tok(GLA_DV, 0), tok(GLOW_PAD, 0),
                   whole((GLOW_PAD, GLA_DK)), whole((1, GLA_DK)), whole((1, GLA_HEAD_V))],
        out_shape=[jax.ShapeDtypeStruct((t, GLA_DK), BF16), jax.ShapeDtypeStruct((t, GLA_DK), BF16),
                   jax.ShapeDtypeStruct((t, GLA_DV), BF16), jax.ShapeDtypeStruct((t, GLA_DV), BF16),
                   jax.ShapeDtypeStruct((t, GLOW_PAD), BF16),
                   jax.ShapeDtypeStruct((GLOW_PAD, GLA_DK), F32), jax.ShapeDtypeStruct((1, GLA_DK), F32),
                   jax.ShapeDtypeStruct((1, GLA_HEAD_V), F32)],
        scratch_shapes=[pltpu.VMEM((GLA_HEADS, GLA_HEAD_V, GLA_HEAD_K), F32), pltpu.VMEM((c, GLA_DK), F32),
                        pltpu.VMEM((c, c), F32), pltpu.VMEM((c, GLA_DK), F32), pltpu.VMEM((c, GLA_DK), F32),
                        pltpu.VMEM((c, GLA_DK), F32), pltpu.VMEM((c, GLA_DK), F32)])
    args = (proj, proj, proj, proj, proj, wgu, bias, ng, o, a, states, states, dy)
    if comm is None:
        return pl.pallas_call(body, compiler_params=_cparams(("arbitrary", "arbitrary")), **call)(*args)
    return _call_with_comm(body, comm, n_in=13, n_out=8, args=args, **call)


DIL_STEPS = DIL_BLOCK
DIL_SCALE = DIL_HEAD_DIM ** -0.5
DIL_HEADS_PER_STEP = {1: 8, 4: 1, 16: 1}


def _dil_mask(i, with_prev):
    if not with_prev:
        return (lax.broadcasted_iota(jnp.int32, (DIL_BLOCK, DIL_BLOCK), 0)
                >= lax.broadcasted_iota(jnp.int32, (DIL_BLOCK, DIL_BLOCK), 1))
    rowi = lax.broadcasted_iota(jnp.int32, (DIL_BLOCK, 2 * DIL_BLOCK), 0)
    colj = lax.broadcasted_iota(jnp.int32, (DIL_BLOCK, 2 * DIL_BLOCK), 1)
    dist = rowi + DIL_BLOCK - colj
    band = jnp.logical_and(dist >= 0, dist <= DIL_STEPS)
    return jnp.logical_and(band, jnp.logical_or(i > 0, colj >= DIL_BLOCK))


def _dil_inputs(refs, with_prev):
    if not with_prev:
        q_ref, kc_ref, vc_ref = refs
        return q_ref, kc_ref, None, vc_ref, None
    return refs


def _dil_keys(prev_ref, cur_ref, rr, hc):
    if prev_ref is None:
        return cur_ref[rr, hc].astype(BF16)
    return jnp.concatenate([prev_ref[rr, hc], cur_ref[rr, hc]], axis=0).astype(BF16)


def _dil_geometry(t, gi):
    _, d = DIL_PATTERNS[gi]
    return d, SEQ // d // DIL_BLOCK, t // SEQ, DIL_BLOCK * d, DIL_HEADS_PER_STEP[d]


def _dil_specs(gi, d, nq, rows, hps, order, with_prev):
    width = hps * DIL_HEAD_DIM
    per_part = DIL_WIDTH // width

    def named(f):
        return lambda *idx: f(**dict(zip(order, idx)))

    def block(i, prev):
        ic = jnp.minimum(i, nq - 1)
        return jnp.maximum(ic - 1, 0) if prev else ic

    def part(j, prev):
        return pl.BlockSpec((rows, width), named(lambda b, i, h: (b * nq + block(i, prev), (gi * 3 + j) * per_part + h)))

    cur = pl.BlockSpec((rows, width), named(lambda b, i, h: (b * nq + block(i, False), h)))
    done = pl.BlockSpec((rows, width), named(lambda b, i, h: (b * nq + jnp.maximum(i - 1, 0), h)))
    parts = [part(0, False), part(1, False), part(1, True), part(2, False), part(2, True)]
    return (parts if with_prev else [parts[0], parts[1], parts[3]]), cur, done


def _dil_rows(r, d):
    return pl.ds(r, DIL_BLOCK, stride=d) if d > 1 else pl.ds(0, DIL_BLOCK)


def _dil_fwd(proj, gi, *, name):
    t = proj.shape[0]
    d, nq, nb, rows, hps = _dil_geometry(t, gi)

    def body(*refs):
        q_ref, kc_ref, kp_ref, vc_ref, vp_ref, o_ref, lse_ref = refs
        mask = _dil_mask(pl.program_id(1), True)
        for h in range(hps):
            hc = pl.ds(h * DIL_HEAD_DIM, DIL_HEAD_DIM)
            for r in range(d):
                rr = _dil_rows(r, d)
                qh = q_ref[rr, hc].astype(BF16)
                kcat = _dil_keys(kp_ref, kc_ref, rr, hc)
                vcat = _dil_keys(vp_ref, vc_ref, rr, hc)
                s = lax.dot_general(qh, kcat, _NT, preferred_element_type=F32) * DIL_SCALE
                s = jnp.where(mask, s, -jnp.inf)
                m = jnp.max(s, axis=-1, keepdims=True)
                p = jnp.exp(s - m)
                l = jnp.sum(p, axis=-1, keepdims=True)
                o_ref[rr, hc] = jnp.dot((p / l).astype(BF16), vcat, preferred_element_type=F32)
                lse_ref[rr, hc] = jnp.broadcast_to(m + jnp.log(l), (DIL_BLOCK, DIL_HEAD_DIM))

    parts, cur, _ = _dil_specs(gi, d, nq, rows, hps, "bih", True)
    return pl.pallas_call(
        body,
        name=name,
        grid=(nb, nq, DIL_HEADS // hps),
        in_specs=parts,
        out_specs=[cur, cur],
        out_shape=[jax.ShapeDtypeStruct((t, DIL_WIDTH), F32)] * 2,
        compiler_params=_cparams(("parallel", "parallel", "parallel")),
    )(*[proj] * len(parts))


def _dil_bwd(proj, gi, lse, do, delta, *, name):
    t = proj.shape[0]
    d, nq, nb, rows, hps = _dil_geometry(t, gi)
    with_prev = nq > 1

    def body(*refs):
        q_ref, kc_ref, kp_ref, vc_ref, vp_ref = _dil_inputs(refs[:-11], with_prev)
        lse_ref, do_ref, dl_ref, dq_ref, dk_ref, dv_ref, ck, cv, fk, fv, dq_s = refs[-11:]
        i = pl.program_id(2)

        @pl.when(i == 0)
        def _():
            ck[...] = jnp.zeros(ck.shape, F32)
            cv[...] = jnp.zeros(cv.shape, F32)

        @pl.when(i < nq)
        def _():
            mask = _dil_mask(i, with_prev)
            for h in range(hps):
                hc = pl.ds(h * DIL_HEAD_DIM, DIL_HEAD_DIM)
                h1 = pl.ds(h * DIL_HEAD_DIM, 1)
                for r in range(d):
                    rr = _dil_rows(r, d)
                    qh = q_ref[rr, hc].astype(BF16)
                    kcat = _dil_keys(kp_ref, kc_ref, rr, hc)
                    vcat = _dil_keys(vp_ref, vc_ref, rr, hc)
                    doh = do_ref[rr, hc].astype(BF16)
                    s = lax.dot_general(qh, kcat, _NT, preferred_element_type=F32) * DIL_SCALE
                    p = jnp.exp(jnp.where(mask, s, -jnp.inf) - lse_ref[rr, h1])
                    dp = lax.dot_general(doh, vcat, _NT, preferred_element_type=F32)
                    ds = (p * (dp + dl_ref[rr, h1]) * DIL_SCALE).astype(BF16)
                    dq_s[rr, hc] = jnp.dot(ds, kcat, preferred_element_type=F32)
                    dkcat = lax.dot_general(ds, qh, _TN, preferred_element_type=F32)
                    dvcat = lax.dot_general(p.astype(BF16), doh, _TN, preferred_element_type=F32)
                    if with_prev:
                        fk[rr, hc] = ck[rr, hc] + dkcat[:DIL_BLOCK]
                        fv[rr, hc] = cv[rr, hc] + dvcat[:DIL_BLOCK]
                    ck[rr, hc] = dkcat[-DIL_BLOCK:]
                    cv[rr, hc] = dvcat[-DIL_BLOCK:]
            dq_ref[...] = dq_s[...].astype(BF16)

            @pl.when(i > 0)
            def _():
                dk_ref[...] = fk[...].astype(BF16)
                dv_ref[...] = fv[...].astype(BF16)

        @pl.when(i == nq)
        def _():
            dk_ref[...] = ck[...].astype(BF16)
            dv_ref[...] = cv[...].astype(BF16)

    parts, cur, done = _dil_specs(gi, d, nq, rows, hps, "bhi", with_prev)
    shape = jax.ShapeDtypeStruct((t, DIL_WIDTH), BF16)
    tile = pltpu.VMEM((rows, hps * DIL_HEAD_DIM), F32)
    return pl.pallas_call(
        body,
        name=name,
        grid=(nb, DIL_HEADS // hps, nq + 1),
        in_specs=parts + [cur, cur, cur],
        out_specs=[cur, done, done],
        out_shape=[shape, shape, shape],
        scratch_shapes=[tile] * 5,
        compiler_params=_cparams(("parallel", "parallel", "arbitrary")),
    )(*[proj] * len(parts), lse, do, delta)


MIX_ROWS = 256


def _head_rowsum(x):
    parts = []
    for h in range(DIL_HEADS):
        s = jnp.sum(x[:, h * DIL_HEAD_DIM:(h + 1) * DIL_HEAD_DIM], axis=-1, keepdims=True)
        parts.append(jnp.broadcast_to(s, (x.shape[0], DIL_HEAD_DIM)))
    return jnp.concatenate(parts, axis=-1)


def _mix_weights(lse_refs):
    ls = [r[...] for r in lse_refs]
    m = jnp.maximum(jnp.maximum(ls[0], ls[1]), ls[2])
    es = [jnp.exp(l - m) for l in ls]
    inv = 1.0 / (es[0] + es[1] + es[2])
    return [e * inv for e in es]


def _dil_mix_fwd(os_, lses, *, name):
    t = os_[0].shape[0]

    def body(o0, o1, o2, l0, l1, l2, out_ref):
        w = _mix_weights((l0, l1, l2))
        out_ref[...] = (w[0] * o0[...] + w[1] * o1[...] + w[2] * o2[...]).astype(BF16)

    row = pl.BlockSpec((MIX_ROWS, DIL_WIDTH), lambda i: (i, 0))
    return pl.pallas_call(
        body, name=name, grid=(t // MIX_ROWS,), in_specs=[row] * 6, out_specs=row,
        out_shape=jax.ShapeDtypeStruct((t, DIL_WIDTH), BF16), compiler_params=_cparams(("parallel",)),
    )(*os_, *lses)


def _dil_mix_bwd(os_, lses, dout, *, name):
    t = os_[0].shape[0]

    def body(o0, o1, o2, l0, l1, l2, d_ref, do0, do1, do2, dl0, dl1, dl2):
        w = _mix_weights((l0, l1, l2))
        dv = d_ref[...]
        mix = w[0] * o0[...] + w[1] * o1[...] + w[2] * o2[...]
        bar = _head_rowsum(dv * mix)
        for wg, do_ref, dl_ref in zip(w, (do0, do1, do2), (dl0, dl1, dl2)):
            do_ref[...] = wg * dv
            dl_ref[...] = -wg * bar

    row = pl.BlockSpec((MIX_ROWS, DIL_WIDTH), lambda i: (i, 0))
    outs = pl.pallas_call(
        body, name=name, grid=(t // MIX_ROWS,), in_specs=[row] * 7, out_specs=[row] * 6,
        out_shape=[jax.ShapeDtypeStruct((t, DIL_WIDTH), F32)] * 6,
        compiler_params=_cparams(("parallel",)),
    )(*os_, *lses, dout)
    return outs[:3], outs[3:]


_MESH = pl.DeviceIdType.MESH
_ANY = pl.BlockSpec(memory_space=pl.ANY)


def _position():
    return lax.axis_index("x"), lax.axis_index("y"), lax.axis_index("c")


AG_COPIES = 7


def _run_comm(task, *, name):
    n_in, n_out = len(task.inputs), len(task.out_shapes)

    def body(*refs):
        parts = (refs[:n_in], refs[n_in:n_in + n_out], refs[n_in + n_out:])
        task.start(*parts)
        task.middle(*parts)
        task.finish(*parts)

    return pl.pallas_call(
        body, name=name, out_shape=task.out_shapes, in_specs=[_ANY] * n_in, out_specs=[_ANY] * n_out,
        scratch_shapes=task.sem_shapes,
    )(*task.inputs)


def _gather_task(shards):
    n = len(shards)

    def copies(x_refs, out_refs, sems):
        send_sems, recv_sems, local_sems = sems
        x, y, cc = _position()
        me, sibling = (x, y, cc), (x, y, 1 - cc)
        chips = [(1 - x, y), (x, 1 - y), (1 - x, 1 - y)]

        def copy(w, k, block, to, own=False):
            px, py, pc = block
            slot = out_refs[w].at[4 * px + 2 * py + pc]
            return pltpu.make_async_remote_copy(
                src_ref=x_refs[w] if own else slot, dst_ref=slot,
                send_sem=send_sems.at[AG_COPIES * w + k], recv_sem=recv_sems.at[AG_COPIES * w + k],
                device_id=to, device_id_type=_MESH)

        mine = [pltpu.make_async_copy(x_refs[w], out_refs[w].at[4 * x + 2 * y + cc], local_sems.at[w]) for w in range(n)]
        first = [[copy(w, 0, me, sibling, own=True)] + [copy(w, 1 + j, me, (*chip, cc), own=True) for j, chip in enumerate(chips)]
                 for w in range(n)]
        landed = [[copy(w, 1 + j, (*chip, cc), me) for j, chip in enumerate(chips)] for w in range(n)]
        passed = [[copy(w, 4 + j, (*chip, cc), sibling) for j, chip in enumerate(chips)] for w in range(n)]
        from_sibling = [[copy(w, 0, sibling, me)] + [copy(w, 4 + j, (*chip, 1 - cc), me) for j, chip in enumerate(chips)]
                        for w in range(n)]
        return mine, first, landed, passed, from_sibling

    def start(ins, outs, sems):
        mine, first, _, _, _ = copies(ins, outs, sems)
        for w in range(n):
            mine[w].start()
            for cp in first[w]:
                cp.start()

    def middle(ins, outs, sems):
        _, _, landed, passed, _ = copies(ins, outs, sems)
        for j in range(3):
            for w in range(n):
                landed[w][j].wait_recv()
                passed[w][j].start()

    def finish(ins, outs, sems):
        mine, first, _, passed, from_sibling = copies(ins, outs, sems)
        for w in range(n):
            for cp in from_sibling[w]:
                cp.wait_recv()
        for w in range(n):
            for cp in first[w] + passed[w]:
                cp.wait_send()
            mine[w].wait()

    return _Comm(shards, [jax.ShapeDtypeStruct((N_DEV,) + s.shape, s.dtype) for s in shards],
                 [pltpu.SemaphoreType.DMA((AG_COPIES * n,)), pltpu.SemaphoreType.DMA((AG_COPIES * n,)),
                  pltpu.SemaphoreType.DMA((n,))], start, middle, finish)


def _parity_half(ref, parity, half_rows):
    if half_rows is None:
        return ref.at[:, parity]
    return ref.at[:, pl.ds(parity * half_rows, half_rows), :]


def _exchange_task(make_copies, inputs, out_shapes, n_copies):
    def start(ins, outs, sems):
        for cp in make_copies(ins, outs, sems):
            cp.start()

    def finish(ins, outs, sems):
        for cp in make_copies(ins, outs, sems):
            cp.wait()

    return _Comm(inputs, out_shapes, [pltpu.SemaphoreType.DMA((n_copies,)), pltpu.SemaphoreType.DMA((n_copies,))],
                 start, lambda ins, outs, sems: None, finish)


def _sibling_task(gs, half_rows):
    n = len(gs)

    def make_copies(g_refs, out_refs, sems):
        x, y, cc = _position()
        return [pltpu.make_async_remote_copy(
            src_ref=_parity_half(g_refs[k], 1 - cc, half_rows[k]), dst_ref=out_refs[k],
            send_sem=sems[0].at[k], recv_sem=sems[1].at[k],
            device_id=(x, y, 1 - cc), device_id_type=_MESH) for k in range(n)]

    def out_shape(g, hr):
        return jax.ShapeDtypeStruct((4,) + (g.shape[2:] if hr is None else (hr, g.shape[2])), g.dtype)

    return _exchange_task(make_copies, gs, [out_shape(g, hr) for g, hr in zip(gs, half_rows)], n)


def _chips_task(ps):
    n = len(ps)

    def make_copies(p_refs, out_refs, sems):
        x, y, cc = _position()
        copies = []
        for w in range(n):
            for k in (1, 2, 3):
                px = 1 - x if k >> 1 else x
                py = 1 - y if k & 1 else y
                copies.append(pltpu.make_async_remote_copy(
                    src_ref=p_refs[w].at[2 * px + py], dst_ref=out_refs[w].at[k - 1],
                    send_sem=sems[0].at[3 * w + k - 1], recv_sem=sems[1].at[3 * w + k - 1],
                    device_id=(px, py, cc), device_id_type=_MESH))
        return copies

    return _exchange_task(make_copies, ps, [jax.ShapeDtypeStruct((3,) + p.shape[1:], p.dtype) for p in ps], 3 * n)


def _add_sibling(g, r1, place, half_rows, *, tr, tc, name):
    _, r, c = r1.shape
    if half_rows is None:
        g_spec = pl.BlockSpec((None, None, tr, tc), lambda i, j, k, pc: (k, pc[0], i, j))
    else:
        per_half = half_rows // tr
        g_spec = pl.BlockSpec((None, tr, tc), lambda i, j, k, pc: (k, pc[0] * per_half + i, j))

    def body(pc_ref, g_ref, r_ref, pb_ref, own_ref):
        s = g_ref[...] + r_ref[...]
        pb_ref[...] = s.astype(BF16)

        @pl.when(pl.program_id(2) == pc_ref[1])
        def _():
            own_ref[...] = s

    grid_spec = pltpu.PrefetchScalarGridSpec(
        num_scalar_prefetch=1,
        grid=(r // tr, c // tc, 4),
        in_specs=[g_spec, pl.BlockSpec((None, tr, tc), lambda i, j, k, pc: (k, i, j))],
        out_specs=[pl.BlockSpec((None, tr, tc), lambda i, j, k, pc: (k, i, j)),
                   pl.BlockSpec((tr, tc), lambda i, j, k, pc: (i, j))],
    )
    return pl.pallas_call(
        body, name=name, grid_spec=grid_spec,
        out_shape=[jax.ShapeDtypeStruct((4, r, c), BF16), jax.ShapeDtypeStruct((r, c), F32)],
        compiler_params=_cparams(("parallel", "parallel", "arbitrary")),
    )(place, g, r1)


def _adamw_math(g, w, m, v):
    m = ADAM_B1 * m + (1.0 - ADAM_B1) * g
    v = ADAM_B2 * v + (1.0 - ADAM_B2) * (g * g)
    m_hat = m / (1.0 - ADAM_B1 ** ADAM_STEP)
    v_hat = v / (1.0 - ADAM_B2 ** ADAM_STEP)
    delta = -ADAM_LR * (m_hat / (jnp.sqrt(v_hat) + ADAM_EPS) + ADAM_WD * w)
    return delta, m, v


def _adamw_big(own, r2, w, m, v, prev, layer, *, tr, tc, name):
    _, r, c = w.shape

    def body(p_ref, r2_ref, w_ref, m_ref, v_ref, a0, a1, a2, a3, g_ref, d_ref, mo_ref, vo_ref):
        g = ((p_ref[...] + r2_ref[0].astype(F32)) + r2_ref[1].astype(F32)) + r2_ref[2].astype(F32)
        delta, mn, vn = _adamw_math(g, w_ref[...], m_ref[...], v_ref[...])
        g_ref[...] = g
        d_ref[...] = delta
        mo_ref[...] = mn
        vo_ref[...] = vn

    lay = pl.BlockSpec((None, tr, tc), lambda i, j: (layer, i, j))
    return pl.pallas_call(
        body, name=name, grid=(r // tr, c // tc),
        in_specs=[pl.BlockSpec((tr, tc), lambda i, j: (i, j)), pl.BlockSpec((3, tr, tc), lambda i, j: (0, i, j)),
                  lay, lay, lay, _ANY, _ANY, _ANY, _ANY],
        out_specs=[lay, lay, lay, lay],
        out_shape=[jax.ShapeDtypeStruct(w.shape, F32)] * 4,
        input_output_aliases={5: 0, 6: 1, 7: 2, 8: 3},
        compiler_params=_cparams(("parallel", "parallel")),
    )(own, r2, w, m, v, *prev)


SMALL_COLS = 1024


def _sum_gathered(parts, *, name):
    _, r, c = parts.shape

    def body(p_ref, o_ref):
        acc = p_ref[0]
        for k in range(1, N_DEV):
            acc = acc + p_ref[k]
        o_ref[...] = acc

    return pl.pallas_call(
        body, name=name, grid=(1,), in_specs=[pl.BlockSpec((N_DEV, r, c), lambda i: (0, 0, 0))],
        out_specs=pl.BlockSpec((r, c), lambda i: (0, 0)), out_shape=jax.ShapeDtypeStruct((r, c), F32),
        compiler_params=_cparams(("arbitrary",)),
    )(parts)


def _adamw_small(g, w, m, v, *, name):
    r, c = g.shape

    def body(g_ref, w_ref, m_ref, v_ref, d_ref, mo_ref, vo_ref):
        delta, mn, vn = _adamw_math(g_ref[...], w_ref[...], m_ref[...], v_ref[...])
        d_ref[...] = delta
        mo_ref[...] = mn
        vo_ref[...] = vn

    spec = pl.BlockSpec((r, c), lambda i: (0, 0))
    return pl.pallas_call(
        body, name=name, grid=(1,), in_specs=[spec] * 4, out_specs=[spec] * 3,
        out_shape=[jax.ShapeDtypeStruct((r, c), F32)] * 3, compiler_params=_cparams(("arbitrary",)),
    )(g, w, m, v)


WEIGHT_NAMES = ("gla_w_in", "gla_w_gate_up", "gla_gate_bias", "gla_norm_g", "gla_w_out", "dil_w_in", "dil_w_out",
                "ffn_w_up", "ffn_conv_w", "ffn_conv_b", "ffn_w_down", "ln_g", "ln_b")
ADAM_TILES = {"gla_w_in": (256, 770), "gla_w_out": (128, 2048), "dil_w_in": (256, 1152), "dil_w_out": (512, 256),
              "ffn_w_up": (344, 1024), "ffn_w_down": (344, 1024)}
ADD_TILES = {**ADAM_TILES, "ffn_w_up": (352, 1024)}
FF_DOWN_SHARD = D_FF // N_DEV
VEC_COLS = 128


def _pad_axis(a, axis, to):
    pads = [(0, 0)] * a.ndim
    pads[axis] = (0, to - a.shape[axis])
    return jnp.pad(a, pads)


def _ff_cols(blocks):
    r = blocks.shape[1]
    return _pad_axis(blocks, 2, FF_SHARD_PAD).reshape(2, 4, r, FF_SHARD_PAD).transpose(0, 2, 1, 3).reshape(2, r, FF_HALF_PAD)


def _ff_cols_back(a):
    r = a.shape[1]
    return a.reshape(2, r, 4, FF_SHARD_PAD)[..., :FF_SHARD].transpose(1, 0, 2, 3).reshape(r, 2 * D_FF)


def _vec_parts(l, w):
    return [w["ffn_conv_w"][l], w["ln_g"][l], w["ln_b"][l]] + ([w["gla_w_gate_up"][l // 2]] if l % 2 == 0 else [])


def _layer_shards(l, w):
    j = l // 2
    gla = l % 2 == 0
    parts = _vec_parts(l, w)
    vec_rows = -(-sum(math.prod(a.shape) for a in parts) // (VEC_COLS * SUBLANES)) * SUBLANES
    mixer = [(w["gla_w_in"] if gla else w["dil_w_in"])[j].astype(BF16),
             (w["gla_w_out"] if gla else w["dil_w_out"])[j].astype(BF16), _pack_rows(parts, vec_rows, VEC_COLS)]
    w_up_t = _pad_axis(jnp.swapaxes(w["ffn_w_up"][l], 0, 1).astype(BF16), 0, FF_SHARD_PAD)
    return mixer, w_up_t, w["ffn_w_down"][l].astype(BF16)


def _w_down_layout(g_down):
    return _pad_axis(g_down.reshape(4, FF_SHARD, D_MODEL), 1, FF_SHARD_PAD).reshape(FF_HALF_PAD, D_MODEL)


def _layer_weights(l, w, g_in, g_out, g_vec):
    j = l // 2
    gla = l % 2 == 0
    vec_shapes = [a.shape for a in _vec_parts(l, w)]
    out = {}
    flat, vec, off = g_vec.reshape(N_DEV, -1), [], 0
    for s in vec_shapes:
        vec.append(flat[:, off:off + math.prod(s)].reshape((N_DEV,) + s))
        off += math.prod(s)
    out["conv_w"] = _ff_cols(vec[0])
    out["conv_b"] = _pad_axis(w["ffn_conv_b"][l].reshape(N_DEV, FF_SHARD), 1, FF_SHARD_PAD).reshape(2, 1, FF_HALF_PAD)
    out["ln_g"] = vec[1].transpose(1, 0, 2).reshape(2, 1, D_MODEL)
    out["ln_b"] = vec[2].transpose(1, 0, 2).reshape(2, 1, D_MODEL)
    if gla:
        win = g_in.transpose(1, 0, 2).reshape(D_MODEL, GLA_IN)
        out["w_in"] = _pad_axis(win, 1, GLA_MAIN + GLOW_PAD)
        wgu = vec[3].transpose(1, 0, 2).reshape(GLA_GATE_RANK, GLA_DK).astype(BF16)
        out["w_gate_up"] = _pad_axis(wgu, 0, GLOW_PAD)
        out["w_out"] = g_out.reshape(GLA_DV, D_MODEL)
        out["gate_bias"] = w["gla_gate_bias"][j].reshape(1, GLA_DK)
        out["norm_g"] = w["gla_norm_g"][j].reshape(1, GLA_HEAD_V)
    else:
        out["w_in"] = g_in
        out["w_out"] = g_out.transpose(1, 0, 2).reshape(DIL_WIDTH, D_MODEL)
    return out


def _by_chip_parity(blocks):
    return blocks.reshape((4, 2) + blocks.shape[1:])


def _col_blocks(dw, width):
    r = dw.shape[0]
    return dw.reshape(r, N_DEV, width).transpose(1, 0, 2)


def _carry(call, task):
    if task is None:
        return call(None), []
    return call(task)


def _sibling_sum(l, n, g, half_rows, from_sibling, place):
    return _add_sibling(g, from_sibling, place, half_rows, tr=ADD_TILES[n][0], tc=ADD_TILES[n][1], name=f"l{l}_{n}_add")


def _ffn_fwd(l, yb, lw, task_up, task_down):
    h, got_up = _carry(lambda c: _ffn_hidden(yb, lw["w_up_t"], comm=c, name=f"l{l}_ffn_hidden"), task_up)
    act = _convgate_fwd(h, lw["conv_w"], lw["conv_b"], name=f"l{l}_convgate")
    ffn, got_down = _carry(lambda c: _matmul(act, lw["w_down"], tm=1024, tn=1024, tk=2816, comm=c, name=f"l{l}_ffn_down"), task_down)
    return ffn, (h, act), got_up, got_down


def _ffn_bwd(l, yb, dz, dzb, lw, saved, place, pending):
    h, act = saved
    t = yb.shape[0]
    task = _sibling_task(list(pending[1].values()), [None] * len(pending[1])) if pending else None
    dact, from_sibling = _carry(lambda c: _matmul(dzb, lw["w_down"], tb=True, tm=512, tn=2816, tk=D_MODEL, comm=c,
                                                  name=f"l{l}_ffn_dact"), task)
    pend_sums = {n: _sibling_sum(pending[0], n, g, None, r1, place)
                 for (n, g), r1 in zip(pending[1].items(), from_sibling)} if pending else {}
    d_down = _ffn_down_dw(act, dzb, name=f"l{l}_ffn_dwdown")
    dh, dcw, dcb = _convgate_bwd(h, dact, lw["conv_w"], lw["conv_b"], name=f"l{l}_convgate_bwd")
    dh = dh.reshape(2 * t, FF_HALF_PAD)
    tasks = [_sibling_task([d_down], [FF_DOWN_SHARD])] + ([_chips_task([s[0] for s in pend_sums.values()])] if pending else [])
    d_up_t, got = _ffn_hidden_dw(dh, yb, comm=_join_comm(tasks), name=f"l{l}_ffn_dwup")
    d_up = _by_chip_parity(d_up_t)
    dy, got_up = _ffn_hidden_dy(dh, lw["w_up_t"], dz, DEEPNORM_ALPHA, comm=_sibling_task([d_up], [None]), name=f"l{l}_ffn_dy")
    sums = {"ffn_w_down": _sibling_sum(l, "ffn_w_down", d_down, FF_DOWN_SHARD, got[0], place),
            "ffn_w_up": _sibling_sum(l, "ffn_w_up", d_up, None, got_up[0], place)}
    small = {"ffn_conv_w": _ff_cols_back(dcw), "ffn_conv_b": _ff_cols_back(dcb)[0]}
    return dy, sums, small, pend_sums, got[1:]


def _gla_layer_fwd(l, hb, lw, tasks):
    proj, got_proj = _carry(lambda c: _matmul(hb, lw["w_in"], tm=1024, tn=896, tk=D_MODEL, comm=c, name=f"l{l}_gla_proj"),
                            tasks["proj"])
    (y, o, a, st), got_gla = _carry(lambda c: _gla_fwd(proj, lw["w_gate_up"], lw["gate_bias"], lw["norm_g"], comm=c,
                                                       name=f"l{l}_gla"), tasks["gla"])
    mix, got_out = _carry(lambda c: _matmul(y, lw["w_out"], tm=1024, tn=1024, tk=GLA_DV, comm=c, name=f"l{l}_gla_out"),
                          tasks["out"])
    return mix, (proj, y, o, a, st), {"proj": got_proj, "gla": got_gla, "out": got_out}


def _gla_layer_bwd(l, hb, dz, dzb, lw, saved, ffn_sums, place):
    proj, y, o, a, st = saved
    dy = _matmul(dzb, lw["w_out"], tb=True, tm=1024, tn=1024, tk=D_MODEL, name=f"l{l}_gla_dy")
    d_out = _by_chip_parity(_matmul(y, dzb, ta=True, tm=1024, tn=1024, tk=2048, name=f"l{l}_gla_dwout")
                            .reshape(N_DEV, GLA_DV // N_DEV, D_MODEL))
    names = list(ffn_sums)
    (dq, dk, dv, dr, dgl, dwgu, dbias, dng), got = _gla_bwd(
        proj, lw["w_gate_up"], lw["gate_bias"], lw["norm_g"], o, a, st, dy,
        comm=_join_comm([_chips_task([ffn_sums[n][0] for n in names]), _sibling_task([d_out], [None])]), name=f"l{l}_gla_bwd")
    out_blocks, out_own = _sibling_sum(l, "gla_w_out", d_out, None, got[len(names)], place)
    dproj = jnp.concatenate([dq, dk, dv, dr, dgl], axis=-1)
    d_in, got_out = _matmul(hb, dproj, ta=True, tm=1024, tn=896, tk=2048, comm=_chips_task([out_blocks]), name=f"l{l}_gla_dwin")
    dx = _matmul(dproj, lw["w_in"], tb=True, tm=1024, tn=1024, tk=896, res=dz, res_scale=DEEPNORM_ALPHA, name=f"l{l}_gla_dx")
    big = {"gla_w_in": _by_chip_parity(_col_blocks(d_in[:, :GLA_IN], GLA_IN // N_DEV))}
    small = {"gla_w_gate_up": dwgu[:GLA_GATE_RANK], "gla_gate_bias": dbias[0], "gla_norm_g": dng[0]}
    return dx, big, small, dict(zip(names, got)), {"gla_w_out": (out_own, got_out[0])}


def _dil_layer_fwd(l, hb, lw, tasks):
    proj, got_proj = _carry(lambda c: _mm_colblocks(hb, lw["w_in"], comm=c, name=f"l{l}_dil_proj"), tasks["proj"])
    os_, lses = [], []
    for gi in range(len(DIL_PATTERNS)):
        o, lse = _dil_fwd(proj, gi, name=f"l{l}_dil_attn{gi}")
        os_.append(o)
        lses.append(lse)
    omix = _dil_mix_fwd(os_, lses, name=f"l{l}_dil_mix")
    mix, got_out = _carry(lambda c: _matmul(omix, lw["w_out"], tm=1024, tn=1024, tk=DIL_WIDTH, comm=c, name=f"l{l}_dil_out"),
                          tasks["out"])
    return mix, (proj, os_, lses, omix), {"proj": got_proj, "out": got_out}


def _dil_layer_bwd(l, hb, dz, dzb, lw, saved, ffn_sums, place):
    proj, os_, lses, omix = saved
    dout = _matmul(dzb, lw["w_out"], tb=True, tm=1024, tn=1024, tk=D_MODEL, name=f"l{l}_dil_dy")
    d_out = _by_chip_parity(_col_blocks(_matmul(omix, dzb, ta=True, tm=1024, tn=1024, tk=2048, name=f"l{l}_dil_dwout"),
                                        D_MODEL // N_DEV))
    dos, dls = _dil_mix_bwd(os_, lses, dout, name=f"l{l}_dil_mix_bwd")
    parts = []
    for gi in range(len(DIL_PATTERNS)):
        parts += list(_dil_bwd(proj, gi, lses[gi], dos[gi], dls[gi], name=f"l{l}_dil_attn_bwd{gi}"))
    dproj = jnp.concatenate(parts, axis=-1)
    d_in, got = _mm_grad_colblocks(hb, dproj, DIL_IN // N_DEV, name=f"l{l}_dil_dwin",
                                   comm=_join_comm([_chips_task([ffn_sums["ffn_w_down"][0]]), _sibling_task([d_out], [None])]))
    out_blocks, out_own = _sibling_sum(l, "dil_w_out", d_out, None, got[1], place)
    dx, got2 = _mm_colblocks_t(dproj, lw["w_in"], dz, DEEPNORM_ALPHA, name=f"l{l}_dil_dx",
                               comm=_join_comm([_chips_task([ffn_sums["ffn_w_up"][0]]), _chips_task([out_blocks])]))
    big = {"dil_w_in": _by_chip_parity(d_in)}
    return dx, big, {}, {"ffn_w_down": got[0], "ffn_w_up": got2[0]}, {"dil_w_out": (out_own, got2[1])}


def _pack_rows(arrays, rows, cols=SMALL_COLS):
    flat = [a.reshape(-1) for a in arrays]
    used = sum(f.shape[0] for f in flat)
    return jnp.concatenate(flat + [jnp.zeros((rows * cols - used,), F32)]).reshape(rows, cols)


def _unpack_rows(packed, shapes):
    flat, out, off = packed.reshape(-1), [], 0
    for s in shapes:
        n = math.prod(s)
        out.append(flat[off:off + n].reshape(s))
        off += n
    return out


def _rows_for(shapes):
    n = sum(math.prod(s) for s in shapes)
    return -(-n // (SMALL_COLS * SUBLANES)) * SUBLANES


def kernel(x, gla_w_in, gla_w_gate_up, gla_gate_bias, gla_norm_g, gla_w_out, dil_w_in, dil_w_out, ffn_w_up, ffn_conv_w, ffn_conv_b, ffn_w_down, ln_g, ln_b, loss_target, m_gla_w_in, m_gla_w_gate_up, m_gla_gate_bias, m_gla_norm_g, m_gla_w_out, m_dil_w_in, m_dil_w_out, m_ffn_w_up, m_ffn_conv_w, m_ffn_conv_b, m_ffn_w_down, m_ln_g, m_ln_b, v_gla_w_in, v_gla_w_gate_up, v_gla_gate_bias, v_gla_norm_g, v_gla_w_out, v_dil_w_in, v_dil_w_out, v_ffn_w_up, v_ffn_conv_w, v_ffn_conv_b, v_ffn_w_down, v_ln_g, v_ln_b):
    w = dict(zip(WEIGHT_NAMES, (gla_w_in, gla_w_gate_up, gla_gate_bias, gla_norm_g, gla_w_out, dil_w_in, dil_w_out,
                                ffn_w_up, ffn_conv_w, ffn_conv_b, ffn_w_down, ln_g, ln_b)))
    mom = dict(zip(WEIGHT_NAMES, (m_gla_w_in, m_gla_w_gate_up, m_gla_gate_bias, m_gla_norm_g, m_gla_w_out, m_dil_w_in,
                                  m_dil_w_out, m_ffn_w_up, m_ffn_conv_w, m_ffn_conv_b, m_ffn_w_down, m_ln_g, m_ln_b)))
    var = dict(zip(WEIGHT_NAMES, (v_gla_w_in, v_gla_w_gate_up, v_gla_gate_bias, v_gla_norm_g, v_gla_w_out, v_dil_w_in,
                                  v_dil_w_out, v_ffn_w_up, v_ffn_conv_w, v_ffn_conv_b, v_ffn_w_down, v_ln_g, v_ln_b)))
    xi, yi, ci = _position()
    dev = 4 * xi + 2 * yi + ci
    place = jnp.stack([ci, 2 * xi + yi]).astype(jnp.int32)
    t = x.shape[0] * x.shape[1]
    h = x.reshape(t, D_MODEL)
    hb = h.astype(BF16)
    target = loss_target.reshape(t, D_MODEL)

    shards = [_layer_shards(l, w) for l in range(DEPTH)]
    arrived = {0: dict(zip(("in", "out", "vec"), _run_comm(_gather_task(shards[0][0]), name="gather_l0")))}
    lws, saved = [], []
    for l in range(DEPTH):
        gla = l % 2 == 0
        nxt = l + 1 if l + 1 < DEPTH else None
        nxt_dil = nxt is not None and nxt % 2 == 1
        here = arrived[l]
        lw = _layer_weights(l, w, here["in"], here["out"], here["vec"])
        tasks = {"proj": _gather_task([shards[l][2]]),
                 "gla": _gather_task([shards[l][1]]) if gla else None,
                 "out": _gather_task(shards[nxt][0][1:]) if nxt else None}
        mix, mixer_saved, got = (_gla_layer_fwd if gla else _dil_layer_fwd)(l, hb, lw, tasks)
        lw["w_down"] = _w_down_layout(got["proj"][0])
        lw["w_up_t"] = got["gla"][0] if gla else here["up"]
        y1, y1b, xh1, rs1 = _ln_fwd(h, mix, lw["ln_g"][0], lw["ln_b"][0], name=f"l{l}_ln1")
        task_hidden = _gather_task([shards[nxt][1] if nxt_dil else shards[nxt][0][0]]) if nxt else None
        task_down = _gather_task([shards[nxt][0][0]]) if nxt_dil else None
        ffn, ffn_saved, got_hidden, got_down = _ffn_fwd(l, y1b, lw, task_hidden, task_down)
        y2, y2b, xh2, rs2 = _ln_fwd(y1, ffn, lw["ln_g"][1], lw["ln_b"][1], name=f"l{l}_ln2")
        saved.append((hb, mixer_saved, y1b, xh1, rs1, ffn_saved, xh2, rs2))
        lws.append(lw)
        h, hb = y2, y2b
        if nxt:
            arrived[nxt] = {"out": got["out"][0], "vec": got["out"][1]}
            if nxt_dil:
                arrived[nxt].update({"up": got_hidden[0], "in": got_down[0]})
            else:
                arrived[nxt]["in"] = got_hidden[0]
    loss_local, dy = _loss_fwd_bwd(h, target, name="loss")
    loss = lax.psum(loss_local[0, 0], ("x", "y", "c"))

    big_names = tuple(ADAM_TILES)
    as_updated = lambda n, a: jnp.swapaxes(a, 1, 2) if n == "ffn_w_up" else a
    wt, mt, vt = ({n: as_updated(n, d[n]) for n in big_names} for d in (w, mom, var))
    results = {n: [lax.empty(wt[n].shape, F32) for _ in range(4)] for n in big_names}
    small_grads = {n: [None] * w[n].shape[0] for n in WEIGHT_NAMES if n not in big_names}

    def adamw(l, n, own, from_chips):
        results[n] = _adamw_big(own, from_chips, wt[n], mt[n], vt[n], results[n], l if n.startswith("ffn") else l // 2,
                                tr=ADAM_TILES[n][0], tc=ADAM_TILES[n][1], name=f"l{l}_{n}_adamw")

    pending = None
    for l in reversed(range(DEPTH)):
        lw = lws[l]
        hb_in, mixer_saved, y1b, xh1, rs1, ffn_saved, xh2, rs2 = saved[l]
        dz2, dz2b, dg2, db2 = _ln_bwd(dy, xh2, rs2, lw["ln_g"][1], name=f"l{l}_ln2_bwd")
        dy1, ffn_sums, small_ffn, pend_sums, got_pending = _ffn_bwd(l, y1b, dz2, dz2b, lw, ffn_saved, place, pending)
        for (n, (_, own)), r2 in zip(pend_sums.items(), got_pending):
            adamw(pending[0], n, own, r2)
        dz1, dz1b, dg1, db1 = _ln_bwd(dy1, xh1, rs1, lw["ln_g"][0], name=f"l{l}_ln1_bwd")
        dy, big_mix, small_mix, got_ffn, reduced = (_gla_layer_bwd if l % 2 == 0 else _dil_layer_bwd)(
            l, hb_in, dz1, dz1b, lw, mixer_saved, ffn_sums, place)
        for n, (_, own) in ffn_sums.items():
            adamw(l, n, own, got_ffn[n])
        for n, (own, r2) in reduced.items():
            adamw(l, n, own, r2)
        pending = (l, big_mix)
        small_grads["ln_g"][l] = jnp.concatenate([dg1, dg2], axis=0)
        small_grads["ln_b"][l] = jnp.concatenate([db1, db2], axis=0)
        for n, g in small_ffn.items():
            small_grads[n][l] = g
        for n, g in small_mix.items():
            small_grads[n][l // 2] = g
    from_sibling = _run_comm(_sibling_task(list(pending[1].values()), [None] * len(pending[1])), name="reduce_sibling_l0")
    last_sums = {n: _sibling_sum(0, n, g, None, r1, place) for (n, g), r1 in zip(pending[1].items(), from_sibling)}
    from_chips = _run_comm(_chips_task([s[0] for s in last_sums.values()]), name="reduce_chips_l0")
    for (n, (_, own)), r2 in zip(last_sums.items(), from_chips):
        adamw(0, n, own, r2)
    results = {n: [as_updated(n, a) for a in results[n]] for n in big_names}
    grad_x = dy.reshape(x.shape)

    small_names = [n for n in WEIGHT_NAMES if n not in big_names]
    full_shapes = {"gla_w_gate_up": (2, GLA_GATE_RANK, GLA_DK), "gla_gate_bias": (2, GLA_DK), "gla_norm_g": (2, GLA_HEAD_V),
                   "ffn_conv_w": (DEPTH, 3, 2 * D_FF), "ffn_conv_b": (DEPTH, 2 * D_FF),
                   "ln_g": (DEPTH, 2, D_MODEL), "ln_b": (DEPTH, 2, D_MODEL)}
    shapes = [full_shapes[n] for n in small_names]
    rows = _rows_for(shapes)
    packed = _pack_rows([jnp.stack(small_grads[n]) for n in small_names], rows)
    summed = _sum_gathered(_run_comm(_gather_task([packed]), name="gather_small_grads")[0], name="sum_small_grads")
    full = dict(zip(small_names, _unpack_rows(summed, shapes)))
    own = {n: (full[n] if w[n].shape == full[n].shape
               else lax.dynamic_slice_in_dim(full[n], dev * w[n].shape[-1], w[n].shape[-1], axis=full[n].ndim - 1))
           for n in small_names}
    own_shapes = [w[n].shape for n in small_names]
    rows = _rows_for(own_shapes)
    pk = lambda d: _pack_rows([d[n] for n in small_names], rows)
    outs = _adamw_small(pk(own), pk(w), pk(mom), pk(var), name="adamw_small")
    for n in small_names:
        results[n] = [own[n]]
    for k, packed_out in enumerate(outs):
        for n, a in zip(small_names, _unpack_rows(packed_out, own_shapes)):
            results[n].append(a)

    return (loss, grad_x) + tuple(results[n][k] for k in range(4) for n in WEIGHT_NAMES)
```
